```python
import math
import jax, jax.numpy as jnp
from jax import lax
import numpy as np

D_MODEL = 1024
BATCH = 8
SEQ = 2048
DEPTH = 2
DEC_BATCH = 128
DEC_SEQ = 1
PAST_LEN = 16384
PAGE_SIZE = 128

CHUNK = 128
CONV_K = 4
EPS = 1e-6

RET_H = 4
RET_DK = D_MODEL // 8
RET_DV = D_MODEL // 4
RET_QK = RET_H * RET_DK
RET_V = RET_H * RET_DV
ROPE_THETA = 10000.0
ML_H = 4
ML_DH = D_MODEL // 4
ML_W = ML_H * ML_DH
SSM_H = 16
SSM_P = D_MODEL // 16
SSM_W = SSM_H * SSM_P
SSM_G = 4
SSM_N = 128
SSM_CONV_DIM = SSM_W + 2 * SSM_G * SSM_N
MEM_LEN = 256
MEM_H = 4
MEM_DH = D_MODEL // MEM_H
D_FF = 4 * D_MODEL
N_RET = 2 * RET_QK + 2 * RET_V
N_ML = 3 * ML_W + 2 * ML_H
N_SSM = SSM_W + SSM_CONV_DIM + SSM_H
N_GATE = 3 * D_MODEL
N_IN = N_RET + N_ML + N_SSM + N_GATE

kernel_name = 'hybrid_retention_mlstm_ssd_decoder_step'


def rmsnorm(x, w):
    xf = x.astype(jnp.float32)
    y = xf * lax.rsqrt(jnp.mean(xf * xf, axis=-1, keepdims=True) + EPS)
    return (y * w.astype(jnp.float32)).astype(x.dtype)


def headnorm(x, w, n_groups, center):
    shp = x.shape
    xf = x.astype(jnp.float32).reshape(shp[:-1] + (n_groups, shp[-1] // n_groups))
    if center:
        xf = xf - jnp.mean(xf, axis=-1, keepdims=True)
    y = xf * lax.rsqrt(jnp.mean(xf * xf, axis=-1, keepdims=True) + EPS)
    return (y.reshape(shp) * w.astype(jnp.float32)).astype(x.dtype)


def rope(x, pos):
    half = x.shape[-1] // 2
    freqs = ROPE_THETA ** (-jnp.arange(half, dtype=jnp.float32) / half)
    ang = pos.astype(jnp.float32)[:, None] * freqs[None, :]
    cos = jnp.cos(ang)[None, :, None, :]
    sin = jnp.sin(ang)[None, :, None, :]
    xf = x.astype(jnp.float32)
    x1, x2 = xf[..., :half], xf[..., half:]
    return jnp.concatenate([x1 * cos - x2 * sin, x1 * sin + x2 * cos], axis=-1)


def causal_conv(x, buf, w, b):
    L = x.shape[1]
    xp = jnp.concatenate([buf.astype(x.dtype), x], axis=1)
    out = b
    for j in range(CONV_K):
        out = out + xp[:, j:j + L] * w[j]
    return jax.nn.silu(out), xp[:, -(CONV_K - 1):]


def chunk_len(L):
    return CHUNK if L % CHUNK == 0 else L


def to_chunks(x, c):
    b, L = x.shape[0], x.shape[1]
    return jnp.moveaxis(x.reshape((b, L // c, c) + x.shape[2:]), 1, 0)


def from_chunks(y):
    y = jnp.moveaxis(y, 0, 1)
    return y.reshape((y.shape[0], y.shape[1] * y.shape[2]) + y.shape[3:])


def retention(q, k, v, S0):
    c = chunk_len(q.shape[1])
    out_dtype = v.dtype
    log_g = jnp.log1p(-jnp.exp2(-5.0 - jnp.arange(RET_H, dtype=jnp.float32)))
    idx = jnp.arange(c, dtype=jnp.float32)
    diff = idx[:, None] - idx[None, :]
    causal = diff >= 0
    decay = jnp.where(causal, jnp.exp(log_g[:, None, None] * jnp.where(causal, diff, 0.0)), 0.0)
    q_decay = jnp.exp(log_g[None, :] * (idx[:, None] + 1.0))
    k_decay = jnp.exp(log_g[None, :] * (c - 1.0 - idx[:, None]))
    c_decay = jnp.exp(log_g * c)

    def step(S, blk):
        qc, kc, vc = blk
        sc = jnp.einsum('bthd,bshd->bhts', qc, kc) * decay
        y = (jnp.einsum('bhts,bshe->bthe', sc, vc)
             + jnp.einsum('bthd,bhde->bthe', qc, S) * q_decay[None, :, :, None])
        S = S * c_decay[None, :, None, None] + jnp.einsum('bshd,bshe,sh->bhde', kc, vc, k_decay)
        return S, y

    S, y = lax.scan(step, S0.astype(jnp.float32),
                    (to_chunks(q, c), to_chunks(k, c), to_chunks(v.astype(jnp.float32), c)))
    return from_chunks(y).astype(out_dtype), S.astype(S0.dtype)


def mlstm(q, k, v, i_pre, log_f, C0, n0, m0):
    c = chunk_len(q.shape[1])
    out_dtype = v.dtype
    f32 = jnp.float32
    causal = jnp.tril(jnp.ones((c, c), dtype=bool))

    def step(carry, blk):
        C, n, m = carry
        qc, kc, vc, ic, fc = blk
        b = jnp.swapaxes(jnp.cumsum(fc, axis=1), 1, 2)
        it = jnp.swapaxes(ic, 1, 2)
        logw = jnp.where(causal, b[..., :, None] - b[..., None, :] + it[..., None, :], -jnp.inf)
        inter = b + m[..., None]
        m_t = jnp.maximum(inter, jnp.max(logw, axis=-1))
        sc = jnp.einsum('bthd,bshd->bhts', qc, kc) * jnp.exp(logw - m_t[..., None])
        w_int = jnp.swapaxes(jnp.exp(inter - m_t), 1, 2)[..., None]
        num = jnp.einsum('bhts,bshe->bthe', sc, vc) + jnp.einsum('bthd,bhde->bthe', qc, C) * w_int
        den = jnp.swapaxes(jnp.sum(sc, axis=-1), 1, 2) + jnp.einsum('bthd,bhd->bth', qc, n) * w_int[..., 0]
        den = jnp.maximum(jnp.abs(den), jnp.swapaxes(jnp.exp(-m_t), 1, 2))
        h = num / den[..., None]
        b_end = b[..., -1]
        logw_s = b_end[..., None] - b + it
        m_new = jnp.maximum(b_end + m, jnp.max(logw_s, axis=-1))
        w_s = jnp.exp(logw_s - m_new[..., None])
        w_prev = jnp.exp(b_end + m - m_new)
        C = C * w_prev[..., None, None] + jnp.einsum('bshd,bshe,bhs->bhde', kc, vc, w_s)
        n = n * w_prev[..., None] + jnp.einsum('bshd,bhs->bhd', kc, w_s)
        return (C, n, m_new), h

    xs = (to_chunks(q.astype(f32), c), to_chunks(k.astype(f32), c), to_chunks(v.astype(f32), c),
          to_chunks(i_pre.astype(f32), c), to_chunks(log_f.astype(f32), c))
    (C, n, m), h = lax.scan(step, (C0.astype(f32), n0.astype(f32), m0.astype(f32)), xs)
    return (from_chunks(h).astype(out_dtype), C.astype(C0.dtype), n.astype(n0.dtype), m.astype(m0.dtype))


def ssd(x, dt, A, Bm, Cm, S0):
    b_, L = x.shape[0], x.shape[1]
    c = chunk_len(L)
    out_dtype = x.dtype
    f32 = jnp.float32
    R = SSM_H // SSM_G
    causal = jnp.tril(jnp.ones((c, c), dtype=bool))
    A_gr = A.reshape(SSM_G, R)

    def step(S, blk):
        xc, dtc, Bc, Cc = blk
        cum = jnp.cumsum(dtc * A_gr, axis=1)
        cum_t = jnp.moveaxis(cum, 1, -1)
        dt_t = jnp.moveaxis(dtc, 1, -1)
        seg = jnp.where(causal, jnp.exp(jnp.where(causal, cum_t[..., :, None] - cum_t[..., None, :], 0.0)), 0.0)
        M = jnp.einsum('btgn,bsgn->bgts', Cc, Bc)[:, :, None] * seg * dt_t[..., None, :]
        y = (jnp.einsum('bgrts,bsgrp->btgrp', M, xc)
             + jnp.einsum('btgn,bgrpn->btgrp', Cc, S) * jnp.exp(cum)[..., None])
        w_s = jnp.exp(cum_t[..., -1:] - cum_t) * dt_t
        S = S * jnp.exp(cum_t[..., -1])[..., None, None] + jnp.einsum('bgrs,bsgrp,bsgn->bgrpn', w_s, xc, Bc)
        return S, y

    xs = (to_chunks(x.astype(f32).reshape(b_, L, SSM_G, R, SSM_P), c),
          to_chunks(dt.astype(f32).reshape(b_, L, SSM_G, R), c),
          to_chunks(Bm.astype(f32), c), to_chunks(Cm.astype(f32), c))
    S, y = lax.scan(step, S0.astype(f32).reshape(b_, SSM_G, R, SSM_P, SSM_N), xs)
    return (from_chunks(y).reshape(b_, L, SSM_H, SSM_P).astype(out_dtype),
            S.reshape(b_, SSM_H, SSM_P, SSM_N).astype(S0.dtype))


def mem_attention(h, mk, mv, wq, wo):
    b, L = h.shape[0], h.shape[1]
    q = (h @ wq).reshape(b, L, MEM_H, MEM_DH)
    s = jnp.einsum('blhd,bmhd->bhlm', q, mk).astype(jnp.float32) * (MEM_DH ** -0.5)
    p = jax.nn.softmax(s, axis=-1).astype(h.dtype)
    o = jnp.einsum('bhlm,bmhd->blhd', p, mv).reshape(b, L, MEM_H * MEM_DH)
    return o @ wo


def split_cols(proj):
    sizes = (RET_QK, RET_QK, RET_V, RET_V, ML_W, ML_W, ML_W, ML_H, ML_H,
             SSM_W, SSM_CONV_DIM, SSM_H, D_MODEL, D_MODEL, D_MODEL)
    offs = np.cumsum(sizes)[:-1].tolist()
    return jnp.split(proj, offs, axis=-1)


def zero_states(nb, dtype):
    return (jnp.zeros((nb, RET_H, RET_DK, RET_DV), dtype),
            jnp.zeros((nb, ML_H, ML_DH, ML_DH), dtype),
            jnp.zeros((nb, ML_H, ML_DH), dtype),
            jnp.zeros((nb, ML_H), dtype),
            jnp.zeros((nb, CONV_K - 1, ML_W), dtype),
            jnp.zeros((nb, SSM_H, SSM_P, SSM_N), dtype),
            jnp.zeros((nb, CONV_K - 1, SSM_CONV_DIM), dtype))


def layer(l, x, pos, st, mem_k, mem_v, W):
    ret_S, ml_C, ml_n, ml_m, ml_buf, ssm_S, ssm_buf = st
    b, L = x.shape[0], x.shape[1]
    h = rmsnorm(x, W['norm_mix_w'][l])
    (q, k, v, g, u, vm, om, ig, fg, z, xbc, dt, gate_r, gate_m, gate_s) = split_cols(h @ W['w_in'][l])

    qr = rope(q.reshape(b, L, RET_H, RET_DK), pos)
    kr = rope(k.reshape(b, L, RET_H, RET_DK), pos) * (RET_DK ** -0.5)
    y_r, ret_S = retention(qr, kr, v.reshape(b, L, RET_H, RET_DV), ret_S)
    y_r = headnorm(y_r.reshape(b, L, RET_V), W['ret_norm_w'][l], RET_H, True) * jax.nn.silu(g)

    uc, ml_buf = causal_conv(u, ml_buf, W['ml_conv_w'][l], W['ml_conv_b'][l])
    uh = uc.reshape(b, L, ML_H, ML_DH)
    qm = jnp.einsum('blhd,hde->blhe', uh, W['ml_wq'][l])
    km = jnp.einsum('blhd,hde->blhe', uh, W['ml_wk'][l]) * (ML_DH ** -0.5)
    gbias = W['ml_gate_b'][l]
    i_pre = ig + gbias[:ML_H]
    log_f = jax.nn.log_sigmoid((fg + gbias[ML_H:]).astype(jnp.float32))
    h_m, ml_C, ml_n, ml_m = mlstm(qm, km, vm.reshape(b, L, ML_H, ML_DH), i_pre, log_f, ml_C, ml_n, ml_m)
    y_m = headnorm(h_m.reshape(b, L, ML_W), W['ml_norm_w'][l], ML_H, True) * jax.nn.sigmoid(om)

    xbc_c, ssm_buf = causal_conv(xbc, ssm_buf, W['ssm_conv_w'][l], W['ssm_conv_b'][l])
    xs, Bm, Cm = jnp.split(xbc_c, [SSM_W, SSM_W + SSM_G * SSM_N], axis=-1)
    delta = jax.nn.softplus((dt + W['ssm_dt_bias'][l]).astype(jnp.float32))
    A = -jnp.exp(W['ssm_A_log'][l].astype(jnp.float32))
    xh = xs.reshape(b, L, SSM_H, SSM_P)
    y_s, ssm_S = ssd(xh, delta, A, Bm.reshape(b, L, SSM_G, SSM_N), Cm.reshape(b, L, SSM_G, SSM_N), ssm_S)
    y_s = (y_s + W['ssm_D'][l][:, None] * xh).reshape(b, L, SSM_W)
    y_s = headnorm(y_s * jax.nn.silu(z), W['ssm_norm_w'][l], SSM_G, False)

    merged = (jax.nn.sigmoid(gate_r) * (y_r @ W['w_br_ret'][l])
              + jax.nn.sigmoid(gate_m) * (y_m @ W['w_br_ml'][l])
              + jax.nn.sigmoid(gate_s) * (y_s @ W['w_br_ssm'][l]))
    x = x + merged @ W['w_out_mix'][l]

    h = rmsnorm(x, W['norm_mem_w'][l])
    x = x + mem_attention(h, mem_k, mem_v, W['mem_wq'][l], W['mem_wo'][l])

    h = rmsnorm(x, W['norm_mlp_w'][l])
    x = x + jnp.square(jax.nn.relu(h @ W['mlp_w1'][l])) @ W['mlp_w2'][l]
    return x, (ret_S, ml_C, ml_n, ml_m, ml_buf, ssm_S, ssm_buf)


def trunk(x, pos, states, mem_ks, mem_vs, W):
    new = []
    for l in range(DEPTH):
        x, st = layer(l, x, pos, states[l], mem_ks[l], mem_vs[l], W)
        new.append(st)
    y = rmsnorm(x, W['norm_f_w'])
    stacked = [jnp.stack([new[l][i] for l in range(DEPTH)]) for i in range(7)]
    return y, stacked


def setup_inputs(seed: int = 0) -> dict:
    key = jax.random.key(seed)
    keys = jax.random.split(key, 64)
    counter = [0]

    def nk():
        counter[0] += 1
        return keys[counter[0] - 1]

    def nrm(shape, scale=1.0):
        return jax.random.normal(nk(), shape, jnp.float32) * scale

    def gain(shape):
        return 1.0 + nrm(shape, 0.02)

    dt0 = jnp.exp(jax.random.uniform(nk(), (DEPTH, SSM_H), jnp.float32, math.log(1e-3), math.log(1e-1)))
    dt_bias = dt0 + jnp.log(-jnp.expm1(-dt0))
    a_log = jnp.log(jax.random.uniform(nk(), (DEPTH, SSM_H), jnp.float32, 1.0, 16.0))
    f_bias = jnp.linspace(3.0, 6.0, ML_H, dtype=jnp.float32)[None, :] + nrm((DEPTH, ML_H), 0.01)
    i_bias = nrm((DEPTH, ML_H), 0.1)
    return {
        'x_prompt': nrm((BATCH, SEQ, D_MODEL)),
        'x_sample': nrm((DEC_BATCH, DEC_SEQ, D_MODEL)),
        'mem_prompt': nrm((BATCH, MEM_LEN, D_MODEL)),
        'state_ret': nrm((DEPTH, DEC_BATCH, RET_H, RET_DK, RET_DV), 0.1),
        'state_mlstm_C': nrm((DEPTH, DEC_BATCH, ML_H, ML_DH, ML_DH), 0.1),
        'state_mlstm_n': nrm((DEPTH, DEC_BATCH, ML_H, ML_DH), 0.5),
        'state_mlstm_m': nrm((DEPTH, DEC_BATCH, ML_H), 0.5),
        'state_mlstm_conv': nrm((DEPTH, DEC_BATCH, CONV_K - 1, ML_W)),
        'state_ssm': nrm((DEPTH, DEC_BATCH, SSM_H, SSM_P, SSM_N), 0.1),
        'state_ssm_conv': nrm((DEPTH, DEC_BATCH, CONV_K - 1, SSM_CONV_DIM)),
        'cache_mem_k': nrm((DEPTH, DEC_BATCH, MEM_LEN, MEM_H, MEM_DH)),
        'cache_mem_v': nrm((DEPTH, DEC_BATCH, MEM_LEN, MEM_H, MEM_DH)),
        'norm_mix_w': gain((DEPTH, D_MODEL)),
        'w_in': nrm((DEPTH, D_MODEL, N_IN), D_MODEL ** -0.5),
        'ret_norm_w': gain((DEPTH, RET_V)),
        'ml_conv_w': nrm((DEPTH, CONV_K, ML_W), CONV_K ** -0.5),
        'ml_conv_b': nrm((DEPTH, ML_W), 0.02),
        'ml_wq': nrm((DEPTH, ML_H, ML_DH, ML_DH), ML_DH ** -0.5),
        'ml_wk': nrm((DEPTH, ML_H, ML_DH, ML_DH), ML_DH ** -0.5),
        'ml_gate_b': jnp.concatenate([i_bias, f_bias], axis=-1),
        'ml_norm_w': gain((DEPTH, ML_W)),
        'ssm_conv_w': nrm((DEPTH, CONV_K, SSM_CONV_DIM), CONV_K ** -0.5),
        'ssm_conv_b': nrm((DEPTH, SSM_CONV_DIM), 0.02),
        'ssm_dt_bias': dt_bias,
        'ssm_A_log': a_log,
        'ssm_D': gain((DEPTH, SSM_H)),
        'ssm_norm_w': gain((DEPTH, SSM_W)),
        'w_br_ret': nrm((DEPTH, RET_V, D_MODEL), RET_V ** -0.5),
        'w_br_ml': nrm((DEPTH, ML_W, D_MODEL), ML_W ** -0.5),
        'w_br_ssm': nrm((DEPTH, SSM_W, D_MODEL), SSM_W ** -0.5),
        'w_out_mix': nrm((DEPTH, D_MODEL, D_MODEL), D_MODEL ** -0.5),
        'norm_mem_w': gain((DEPTH, D_MODEL)),
        'mem_wq': nrm((DEPTH, D_MODEL, MEM_H * MEM_DH), D_MODEL ** -0.5),
        'mem_wk': nrm((DEPTH, D_MODEL, MEM_H * MEM_DH), D_MODEL ** -0.5),
        'mem_wv': nrm((DEPTH, D_MODEL, MEM_H * MEM_DH), D_MODEL ** -0.5),
        'mem_wo': nrm((DEPTH, MEM_H * MEM_DH, D_MODEL), (MEM_H * MEM_DH) ** -0.5),
        'norm_mlp_w': gain((DEPTH, D_MODEL)),
        'mlp_w1': nrm((DEPTH, D_MODEL, D_FF), D_MODEL ** -0.5),
        'mlp_w2': nrm((DEPTH, D_FF, D_MODEL), D_FF ** -0.5),
        'norm_f_w': gain((D_MODEL,)),
    }


def reference(x_prompt, x_sample, mem_prompt, state_ret, state_mlstm_C, state_mlstm_n, state_mlstm_m,
              state_mlstm_conv, state_ssm, state_ssm_conv, cache_mem_k, cache_mem_v,
              norm_mix_w, w_in, ret_norm_w, ml_conv_w, ml_conv_b, ml_wq, ml_wk, ml_gate_b, ml_norm_w,
              ssm_conv_w, ssm_conv_b, ssm_dt_bias, ssm_A_log, ssm_D, ssm_norm_w,
              w_br_ret, w_br_ml, w_br_ssm, w_out_mix, norm_mem_w, mem_wq, mem_wk, mem_wv, mem_wo,
              norm_mlp_w, mlp_w1, mlp_w2, norm_f_w):
    W = dict(norm_mix_w=norm_mix_w, w_in=w_in, ret_norm_w=ret_norm_w, ml_conv_w=ml_conv_w,
             ml_conv_b=ml_conv_b, ml_wq=ml_wq, ml_wk=ml_wk, ml_gate_b=ml_gate_b, ml_norm_w=ml_norm_w,
             ssm_conv_w=ssm_conv_w, ssm_conv_b=ssm_conv_b, ssm_dt_bias=ssm_dt_bias, ssm_A_log=ssm_A_log,
             ssm_D=ssm_D, ssm_norm_w=ssm_norm_w, w_br_ret=w_br_ret, w_br_ml=w_br_ml, w_br_ssm=w_br_ssm,
             w_out_mix=w_out_mix, norm_mem_w=norm_mem_w, mem_wq=mem_wq, mem_wo=mem_wo,
             norm_mlp_w=norm_mlp_w, mlp_w1=mlp_w1, mlp_w2=mlp_w2, norm_f_w=norm_f_w)

    nb = x_prompt.shape[0]
    pos_p = jnp.arange(x_prompt.shape[1], dtype=jnp.int32)
    init_p = zero_states(nb, x_prompt.dtype)
    mk_p = [(mem_prompt @ mem_wk[l]).reshape(nb, MEM_LEN, MEM_H, MEM_DH) for l in range(DEPTH)]
    mv_p = [(mem_prompt @ mem_wv[l]).reshape(nb, MEM_LEN, MEM_H, MEM_DH) for l in range(DEPTH)]
    y_prompt, sp = trunk(x_prompt, pos_p, [init_p] * DEPTH, mk_p, mv_p, W)
    ret_p, mlC_p, mln_p, mlm_p, mlconv_p, ssm_p, ssmconv_p = sp
    memk_p = jnp.stack(mk_p)
    memv_p = jnp.stack(mv_p)

    pos_s = PAST_LEN + jnp.arange(x_sample.shape[1], dtype=jnp.int32)
    init_s = [(state_ret[l], state_mlstm_C[l], state_mlstm_n[l], state_mlstm_m[l], state_mlstm_conv[l],
               state_ssm[l], state_ssm_conv[l]) for l in range(DEPTH)]
    y_sample, ss = trunk(x_sample, pos_s, init_s, [cache_mem_k[l] for l in range(DEPTH)],
                         [cache_mem_v[l] for l in range(DEPTH)], W)
    ret_s, mlC_s, mln_s, mlm_s, mlconv_s, ssm_s, ssmconv_s = ss

    return (y_prompt, y_sample, ret_p, mlC_p, mln_p, mlm_p, mlconv_p, ssm_p, ssmconv_p, memk_p, memv_p,
            ret_s, mlC_s, mln_s, mlm_s, mlconv_s, ssm_s, ssmconv_s)
```

```python
import functools
import math

import jax
import jax.numpy as jnp
from jax import lax
from jax.experimental import pallas as pl
from jax.experimental.pallas import tpu as pltpu

F32 = jnp.float32
BF16 = jnp.bfloat16

D_MODEL = 1024
DEPTH = 2
PAST_LEN = 16384
CHUNK = 128
CONV_K = 4
EPS = 1e-6
RET_H, RET_DK, RET_DV = 4, 128, 256
ROPE_THETA = 10000.0
ML_H, ML_DH = 4, 256
SSM_H, SSM_P, SSM_G, SSM_N = 16, 64, 4, 128
SSM_R = SSM_H // SSM_G
MEM_LEN, MEM_H, MEM_DH = 256, 4, 256
D_FF = 4 * D_MODEL

C_Q, C_K, C_V, C_G = 0, 512, 1024, 2048
C_U, C_VM, C_OM = 3072, 4096, 5120
C_Z, C_XS, C_B, C_C = 6144, 7168, 8192, 8704
C_GR, C_GM, C_GS = 9216, 10240, 11264
N_MAIN = 12288
L_IG, L_FG, L_DT = 0, 4, 8
N_SMALL = 128

VMEM_LIMIT = 56 * 1024 * 1024
NT_DIMS = (((1,), (1,)), ((), ()))

_LOG_G = [math.log1p(-(2.0 ** (-5.0 - h))) for h in range(RET_H)]


def _cparams(n_axes):
    return pltpu.CompilerParams(dimension_semantics=("arbitrary",) * n_axes,
                                vmem_limit_bytes=VMEM_LIMIT)


def _dot(a, b):
    return jnp.dot(a, b, preferred_element_type=F32)


def _dot_nt(a, b):
    return lax.dot_general(a, b, NT_DIMS, preferred_element_type=F32)


def _rmsnorm(x, w):
    ms = jnp.mean(x * x, axis=-1, keepdims=True)
    return x * lax.rsqrt(ms + EPS) * w


def _groupnorm(x, w, center):
    if center:
        x = x - jnp.mean(x, axis=-1, keepdims=True)
    ms = jnp.mean(x * x, axis=-1, keepdims=True)
    return x * lax.rsqrt(ms + EPS) * w


def _sigmoid(x):
    return 1.0 / (1.0 + jnp.exp(-x))


def _silu(x):
    return x * _sigmoid(x)


def _softplus(x):
    return jnp.maximum(x, 0.0) + jnp.log1p(jnp.exp(-jnp.abs(x)))


def _log_sigmoid(x):
    return -_softplus(-x)


def _cumsum_rows(x, tril_b):
    hi = x.astype(BF16)
    r1 = x - hi.astype(F32)
    mid = r1.astype(BF16)
    lo = (r1 - mid.astype(F32)).astype(BF16)
    return _dot(tril_b, hi) + _dot(tril_b, mid) + _dot(tril_b, lo)


def _lane_pick(x, lane):
    idx = lax.broadcasted_iota(jnp.int32, x.shape, 1)
    return jnp.sum(jnp.where(idx == lane, x, 0.0), axis=1, keepdims=True)


def _pad_rows_t(x):
    n = x.shape[1]
    xp = jnp.concatenate([x, jnp.zeros((128 - x.shape[0], n), F32)], axis=0)
    return xp.T


def _inproj_kernel(x_ref, nw_ref, w_ref, ws_ref, o_ref, os_ref, xn_ref):
    @pl.when(pl.program_id(1) == 0)
    def _():
        xn = _rmsnorm(x_ref[...], nw_ref[...]).astype(BF16)
        xn_ref[...] = xn
        os_ref[...] = _dot(xn, ws_ref[...])

    o_ref[...] = _dot(xn_ref[...], w_ref[...]).astype(o_ref.dtype)


def _inproj(x, nw, w_main, w_small, *, tm, tn, out_dtype):
    m = x.shape[0]
    return pl.pallas_call(
        _inproj_kernel,
        grid=(m // tm, N_MAIN // tn),
        in_specs=[pl.BlockSpec((tm, D_MODEL), lambda i, j: (i, 0)),
                  pl.BlockSpec((1, D_MODEL), lambda i, j: (0, 0)),
                  pl.BlockSpec((D_MODEL, tn), lambda i, j: (0, j)),
                  pl.BlockSpec((D_MODEL, N_SMALL), lambda i, j: (0, 0))],
        out_specs=[pl.BlockSpec((tm, tn), lambda i, j: (i, j)),
                   pl.BlockSpec((tm, N_SMALL), lambda i, j: (i, 0))],
        out_shape=[jax.ShapeDtypeStruct((m, N_MAIN), out_dtype),
                   jax.ShapeDtypeStruct((m, N_SMALL), F32)],
        scratch_shapes=[pltpu.VMEM((tm, D_MODEL), BF16)],
        compiler_params=_cparams(2),
        name="inproj",
    )(x, nw, w_main, w_small)


def _mm_kernel(*refs, has_res):
    if has_res:
        a_ref, w_ref, r_ref, o_ref = refs
    else:
        a_ref, w_ref, o_ref = refs
    acc = _dot(a_ref[...].astype(BF16), w_ref[...])
    if has_res:
        acc = r_ref[...] + acc
    o_ref[...] = acc.astype(o_ref.dtype)


def _mm(a, w, res=None, *, tm, tn, out_dtype=F32):
    m, k = a.shape
    n = w.shape[1]
    in_specs = [pl.BlockSpec((tm, k), lambda i, j: (i, 0)),
                pl.BlockSpec((k, tn), lambda i, j: (0, j))]
    args = [a, w]
    if res is not None:
        in_specs.append(pl.BlockSpec((tm, tn), lambda i, j: (i, j)))
        args.append(res)
    return pl.pallas_call(
        functools.partial(_mm_kernel, has_res=res is not None),
        grid=(m // tm, n // tn),
        in_specs=in_specs,
        out_specs=pl.BlockSpec((tm, tn), lambda i, j: (i, j)),
        out_shape=jax.ShapeDtypeStruct((m, n), out_dtype),
        compiler_params=_cparams(2),
        name="mm",
    )(*args)


def _memkv_kernel(a_ref, wk_ref, wv_ref, ok_ref, ov_ref):
    a = a_ref[...].astype(BF16)
    ok_ref[0] = _dot(a, wk_ref[0])
    ov_ref[0] = _dot(a, wv_ref[0])


def _memkv(a, wk, wv, *, tm):
    m = a.shape[0]
    n = wk.shape[2]
    return pl.pallas_call(
        _memkv_kernel,
        grid=(DEPTH, m // tm),
        in_specs=[pl.BlockSpec((tm, D_MODEL), lambda l, i: (i, 0)),
                  pl.BlockSpec((1, D_MODEL, n), lambda l, i: (l, 0, 0)),
                  pl.BlockSpec((1, D_MODEL, n), lambda l, i: (l, 0, 0))],
        out_specs=[pl.BlockSpec((1, tm, n), lambda l, i: (l, i, 0)),
                   pl.BlockSpec((1, tm, n), lambda l, i: (l, i, 0))],
        out_shape=[jax.ShapeDtypeStruct((DEPTH, m, n), F32)] * 2,
        compiler_params=_cparams(2),
        name="memkv",
    )(a, wk, wv)


def _ret_prefill_kernel(q_ref, k_ref, v_ref, g_ref, cos_ref, sin_ref, nw_ref, y_ref, s_ref):
    @pl.when(pl.program_id(1) == 0)
    def _():
        s_ref[...] = jnp.zeros_like(s_ref)

    c = CHUNK
    cosf = cos_ref[...]
    sinf = sin_ref[...]
    row = lax.broadcasted_iota(jnp.int32, (c, c), 0)
    col = lax.broadcasted_iota(jnp.int32, (c, c), 1)
    causal = row >= col
    diff = jnp.where(causal, (row - col).astype(F32), 0.0)
    rowf = row.astype(F32)
    rowf_v = lax.broadcasted_iota(jnp.int32, (c, RET_DV), 0).astype(F32)
    for h in range(RET_H):
        lg = _LOG_G[h]
        decay = jnp.where(causal, jnp.exp(lg * diff), 0.0)
        q_decay = jnp.exp(lg * (rowf_v + 1.0))
        k_decay = jnp.exp(lg * (c - 1.0 - rowf))
        c_decay = math.exp(lg * c)
        q = q_ref[0, :, h * RET_DK:(h + 1) * RET_DK].astype(F32)
        k = k_ref[0, :, h * RET_DK:(h + 1) * RET_DK].astype(F32)
        qr = q * cosf + pltpu.roll(q, RET_DK // 2, 1) * sinf
        kr = (k * cosf + pltpu.roll(k, RET_DK // 2, 1) * sinf) * (RET_DK ** -0.5)
        v = v_ref[0, :, h * RET_DV:(h + 1) * RET_DV]
        qb = qr.astype(BF16)
        kb = kr.astype(BF16)
        sc = _dot_nt(qb, kb) * decay
        s_old = s_ref[0, h]
        y = _dot(sc.astype(BF16), v) + _dot(qb, s_old.astype(BF16)) * q_decay
        kt = (kr * k_decay).T.astype(BF16)
        s_ref[0, h] = s_old * c_decay + _dot(kt, v)
        g = g_ref[0, :, h * RET_DV:(h + 1) * RET_DV].astype(F32)
        yn = _groupnorm(y, nw_ref[:, h * RET_DV:(h + 1) * RET_DV], True) * _silu(g)
        y_ref[0, :, h * RET_DV:(h + 1) * RET_DV] = yn.astype(y_ref.dtype)


def _ret_prefill(proj, cosf, sinf, nw, *, nb, seq):
    c = CHUNK
    p3 = proj.reshape(nb, seq, N_MAIN)
    return pl.pallas_call(
        _ret_prefill_kernel,
        grid=(nb, seq // c),
        in_specs=[pl.BlockSpec((1, c, 512), lambda b, t: (b, t, C_Q // 512)),
                  pl.BlockSpec((1, c, 512), lambda b, t: (b, t, C_K // 512)),
                  pl.BlockSpec((1, c, 1024), lambda b, t: (b, t, C_V // 1024)),
                  pl.BlockSpec((1, c, 1024), lambda b, t: (b, t, C_G // 1024)),
                  pl.BlockSpec((c, RET_DK), lambda b, t: (t, 0)),
                  pl.BlockSpec((c, RET_DK), lambda b, t: (t, 0)),
                  pl.BlockSpec((1, 1024), lambda b, t: (0, 0))],
        out_specs=[pl.BlockSpec((1, c, 1024), lambda b, t: (b, t, 0)),
                   pl.BlockSpec((1, RET_H, RET_DK, RET_DV), lambda b, t: (b, 0, 0, 0))],
        out_shape=[jax.ShapeDtypeStruct((nb, seq, 1024), BF16),
                   jax.ShapeDtypeStruct((nb, RET_H, RET_DK, RET_DV), F32)],
        compiler_params=_cparams(2),
        name="ret_prefill",
    )(p3, p3, p3, p3, cosf, sinf, nw)


def _ml_prefill_kernel(u_ref, v_ref, o_ref, sm_ref, cw_ref, cb_ref, wq_ref, wk_ref, gb_ref, nw_ref,
                       y_ref, c_ref, n_ref, m_ref, cv_ref, xbuf):
    t = pl.program_id(1)
    c = CHUNK

    @pl.when(t == 0)
    def _():
        c_ref[...] = jnp.zeros_like(c_ref)
        n_ref[...] = jnp.zeros_like(n_ref)
        m_ref[...] = jnp.zeros_like(m_ref)
        xbuf[0:8, :] = jnp.zeros((8, xbuf.shape[1]), F32)

    xbuf[8:8 + c, :] = u_ref[0].astype(F32)
    acc = cb_ref[...]
    for j in range(CONV_K):
        acc = acc + xbuf[5 + j:5 + j + c, :] * cw_ref[j:j + 1, :]
    uc = _silu(acc).astype(BF16)
    cv_ref[0] = xbuf[c + 5:c + 8, :]
    xbuf[0:8, :] = xbuf[c:c + 8, :]

    row = lax.broadcasted_iota(jnp.int32, (c, c), 0)
    col = lax.broadcasted_iota(jnp.int32, (c, c), 1)
    causal = row >= col
    tril_b = jnp.where(causal, 1.0, 0.0).astype(BF16)
    lane_m = lax.broadcasted_iota(jnp.int32, (1, 128), 1)

    gates = sm_ref[0] + gb_ref[...]
    b_all = _cumsum_rows(_log_sigmoid(gates), tril_b)
    b_all_t = b_all.T
    gates_t = gates.T
    m_row = m_ref[0]
    m_row_new = m_row
    for h in range(ML_H):
        sl = slice(h * ML_DH, (h + 1) * ML_DH)
        b_col = b_all[:, L_FG + h:L_FG + h + 1]
        b_row = b_all_t[L_FG + h:L_FG + h + 1, :]
        it_col = gates[:, L_IG + h:L_IG + h + 1]
        it_row = gates_t[L_IG + h:L_IG + h + 1, :]
        m_prev = m_row[:, h:h + 1]
        qm = _dot(uc[:, sl], wq_ref[h])
        km = _dot(uc[:, sl], wk_ref[h]) * (ML_DH ** -0.5)
        v = v_ref[0, :, sl]
        qb = qm.astype(BF16)
        kb = km.astype(BF16)

        logw = jnp.where(causal, b_col - b_row + it_row, -jnp.inf)
        inter = b_col + m_prev
        m_t = jnp.maximum(inter, jnp.max(logw, axis=-1, keepdims=True))
        sc = _dot_nt(qb, kb) * jnp.exp(logw - m_t)
        w_int = jnp.exp(inter - m_t)
        c_old = c_ref[0, h]
        n_old = n_ref[0, h:h + 1, :]
        num = _dot(sc.astype(BF16), v) + _dot(qb, c_old.astype(BF16)) * w_int
        den = jnp.sum(sc, axis=-1, keepdims=True) + jnp.sum(qm * n_old, axis=-1, keepdims=True) * w_int
        den = jnp.maximum(jnp.abs(den), jnp.exp(-m_t))
        hh = num / den

        b_end = b_col[c - 1:c, :]
        logw_s = b_end - b_col + it_col
        m_new = jnp.maximum(b_end + m_prev, jnp.max(logw_s, axis=0, keepdims=True))
        w_s = jnp.exp(logw_s - m_new)
        w_prev = jnp.exp(b_end + m_prev - m_new)
        kw = km * w_s
        c_ref[0, h] = c_old * w_prev + _dot(kw.T.astype(BF16), v)
        n_ref[0, h:h + 1, :] = n_old * w_prev + jnp.sum(kw, axis=0, keepdims=True)
        m_row_new = jnp.where(lane_m == h, m_new, m_row_new)

        og = o_ref[0, :, sl].astype(F32)
        yn = _groupnorm(hh, nw_ref[:, sl], True) * _sigmoid(og)
        y_ref[0, :, sl] = yn.astype(y_ref.dtype)
    m_ref[0] = m_row_new


def _ml_prefill(proj, small, cw, cb, wq, wk, gb, nw, *, nb, seq):
    c = CHUNK
    p3 = proj.reshape(nb, seq, N_MAIN)
    s3 = small.reshape(nb, seq, N_SMALL)
    w = ML_H * ML_DH
    full2 = lambda b, t: (0, 0)
    return pl.pallas_call(
        _ml_prefill_kernel,
        grid=(nb, seq // c),
        in_specs=[pl.BlockSpec((1, c, w), lambda b, t: (b, t, C_U // w)),
                  pl.BlockSpec((1, c, w), lambda b, t: (b, t, C_VM // w)),
                  pl.BlockSpec((1, c, w), lambda b, t: (b, t, C_OM // w)),
                  pl.BlockSpec((1, c, N_SMALL), lambda b, t: (b, t, 0)),
                  pl.BlockSpec((CONV_K, w), full2),
                  pl.BlockSpec((1, w), full2),
                  pl.BlockSpec((ML_H, ML_DH, ML_DH), lambda b, t: (0, 0, 0)),
                  pl.BlockSpec((ML_H, ML_DH, ML_DH), lambda b, t: (0, 0, 0)),
                  pl.BlockSpec((1, N_SMALL), full2),
                  pl.BlockSpec((1, w), full2)],
        out_specs=[pl.BlockSpec((1, c, w), lambda b, t: (b, t, 0)),
                   pl.BlockSpec((1, ML_H, ML_DH, ML_DH), lambda b, t: (b, 0, 0, 0)),
                   pl.BlockSpec((1, ML_H, ML_DH), lambda b, t: (b, 0, 0)),
                   pl.BlockSpec((1, 1, 128), lambda b, t: (b, 0, 0)),
                   pl.BlockSpec((1, CONV_K - 1, w), lambda b, t: (b, 0, 0))],
        out_shape=[jax.ShapeDtypeStruct((nb, seq, w), BF16),
                   jax.ShapeDtypeStruct((nb, ML_H, ML_DH, ML_DH), F32),
                   jax.ShapeDtypeStruct((nb, ML_H, ML_DH), F32),
                   jax.ShapeDtypeStruct((nb, 1, 128), F32),
                   jax.ShapeDtypeStruct((nb, CONV_K - 1, w), F32)],
        scratch_shapes=[pltpu.VMEM((c + 8, w), F32)],
        compiler_params=_cparams(2),
        name="ml_prefill",
    )(p3, p3, p3, s3, cw, cb, wq, wk, gb, nw)


def _block_bcast(tile, lanes):
    c = tile.shape[0]
    lane = lax.broadcasted_iota(jnp.int32, (c, 128), 1)
    cols = [jnp.broadcast_to(tile[:, l:l + 1], (c, 128)) for l in lanes]
    left = jnp.where(lane < SSM_P, cols[0], cols[1])
    right = jnp.where(lane < SSM_P, cols[2], cols[3])
    return jnp.concatenate([left, right], axis=1)


def _ssd_prefill_kernel(z_ref, xs_ref, bc_ref, sm_ref, cw_ref, cb_ref, dtb_ref, alog_ref, dv_ref, nw_ref,
                        y_ref, s_ref, cv_ref, xbuf):
    t = pl.program_id(1)
    c = CHUNK
    wx = SSM_H * SSM_P

    @pl.when(t == 0)
    def _():
        s_ref[...] = jnp.zeros_like(s_ref)
        xbuf[0:8, :] = jnp.zeros((8, xbuf.shape[1]), F32)

    xbuf[8:8 + c, 0:wx] = xs_ref[0].astype(F32)
    xbuf[8:8 + c, wx:2 * wx] = bc_ref[0].astype(F32)
    acc = cb_ref[...]
    for j in range(CONV_K):
        acc = acc + xbuf[5 + j:5 + j + c, :] * cw_ref[j:j + 1, :]
    xc = _silu(acc)
    cv_ref[0] = xbuf[c + 5:c + 8, :]
    xbuf[0:8, :] = xbuf[c:c + 8, :]

    row = lax.broadcasted_iota(jnp.int32, (c, c), 0)
    col = lax.broadcasted_iota(jnp.int32, (c, c), 1)
    causal = row >= col
    tril_b = jnp.where(causal, 1.0, 0.0).astype(BF16)
    lane_blk = lax.shift_right_logical(lax.broadcasted_iota(jnp.int32, (c, SSM_R * SSM_P), 1), 6)

    delta = _softplus(sm_ref[0] + dtb_ref[...])
    a_row = -jnp.exp(alog_ref[...])
    cum = _cumsum_rows(delta * a_row, tril_b)
    cum_t = cum.T
    delta_t = delta.T
    ecum = jnp.exp(cum)
    cum_end = cum[c - 1:c, :]
    w_state = jnp.exp(cum_end - cum) * delta
    dec_row = jnp.exp(cum_end)

    for g in range(SSM_G):
        gs = slice(g * SSM_R * SSM_P, (g + 1) * SSM_R * SSM_P)
        bg = xc[:, wx + g * SSM_N:wx + (g + 1) * SSM_N].astype(BF16)
        cg = xc[:, wx + SSM_G * SSM_N + g * SSM_N:wx + SSM_G * SSM_N + (g + 1) * SSM_N].astype(BF16)
        xg = xc[:, gs]
        xgb = xg.astype(BF16)
        cb_mat = _dot_nt(cg, bg)
        lanes = [L_DT + g * SSM_R + r for r in range(SSM_R)]
        y = jnp.zeros((c, SSM_R * SSM_P), F32)
        for r in range(SSM_R):
            ln = lanes[r]
            seg_arg = jnp.where(causal, cum[:, ln:ln + 1] - cum_t[ln:ln + 1, :], 0.0)
            seg = jnp.where(causal, jnp.exp(seg_arg), 0.0)
            m_mat = (cb_mat * seg * delta_t[ln:ln + 1, :]).astype(BF16)
            y = jnp.where(lane_blk == r, _dot(m_mat, xgb), y)
        s_old = s_ref[0, g * SSM_R:(g + 1) * SSM_R].reshape(SSM_R * SSM_P, SSM_N)
        y = y + _dot_nt(cg, s_old.astype(BF16)) * _block_bcast(ecum, lanes)
        xw = xg * _block_bcast(w_state, lanes)
        upd = _dot(xw.T.astype(BF16), bg)
        for r in range(SSM_R):
            ln = lanes[r]
            s_ref[0, g * SSM_R + r] = (s_old[r * SSM_P:(r + 1) * SSM_P] * dec_row[:, ln:ln + 1]
                                       + upd[r * SSM_P:(r + 1) * SSM_P])
        y = y + dv_ref[:, gs] * xg
        z = z_ref[0, :, gs].astype(F32)
        yn = _groupnorm(y * _silu(z), nw_ref[:, gs], False)
        y_ref[0, :, gs] = yn.astype(y_ref.dtype)


def _ssd_prefill(proj, small, cw, cb, dtb, alog, dvec, nw, *, nb, seq):
    c = CHUNK
    wx = SSM_H * SSM_P
    p3 = proj.reshape(nb, seq, N_MAIN)
    s3 = small.reshape(nb, seq, N_SMALL)
    full2 = lambda b, t: (0, 0)
    return pl.pallas_call(
        _ssd_prefill_kernel,
        grid=(nb, seq // c),
        in_specs=[pl.BlockSpec((1, c, wx), lambda b, t: (b, t, C_Z // wx)),
                  pl.BlockSpec((1, c, wx), lambda b, t: (b, t, C_XS // wx)),
                  pl.BlockSpec((1, c, wx), lambda b, t: (b, t, C_B // wx)),
                  pl.BlockSpec((1, c, N_SMALL), lambda b, t: (b, t, 0)),
                  pl.BlockSpec((CONV_K, 2 * wx), full2),
                  pl.BlockSpec((1, 2 * wx), full2),
                  pl.BlockSpec((1, N_SMALL), full2),
                  pl.BlockSpec((1, N_SMALL), full2),
                  pl.BlockSpec((1, wx), full2),
                  pl.BlockSpec((1, wx), full2)],
        out_specs=[pl.BlockSpec((1, c, wx), lambda b, t: (b, t, 0)),
                   pl.BlockSpec((1, SSM_H, SSM_P, SSM_N), lambda b, t: (b, 0, 0, 0)),
                   pl.BlockSpec((1, CONV_K - 1, 2 * wx), lambda b, t: (b, 0, 0))],
        out_shape=[jax.ShapeDtypeStruct((nb, seq, wx), BF16),
                   jax.ShapeDtypeStruct((nb, SSM_H, SSM_P, SSM_N), F32),
                   jax.ShapeDtypeStruct((nb, CONV_K - 1, 2 * wx), F32)],
        scratch_shapes=[pltpu.VMEM((c + 8, 2 * wx), F32)],
        compiler_params=_cparams(2),
        name="ssd_prefill",
    )(p3, p3, p3, s3, cw, cb, dtb, alog, dvec, nw)


def _merge_kernel(yr_ref, ym_ref, ys_ref, gr_ref, gm_ref, gs_ref, x_ref, wr_ref, wm_ref, ws_ref,
                  wo_ref, nw_ref, wq_ref, xo_ref, qo_ref):
    def branch(y_ref, g_ref, w_ref):
        return _sigmoid(g_ref[...].astype(F32)) * _dot(y_ref[...].astype(BF16), w_ref[...])

    merged = branch(yr_ref, gr_ref, wr_ref) + branch(ym_ref, gm_ref, wm_ref) + branch(ys_ref, gs_ref, ws_ref)
    xn = x_ref[...] + _dot(merged.astype(BF16), wo_ref[...])
    xo_ref[...] = xn
    hq = _rmsnorm(xn, nw_ref[...]).astype(BF16)
    qo_ref[...] = _dot(hq, wq_ref[...]).astype(qo_ref.dtype)


def _merge(yr, ym, ys, proj, x, wr, wm, ws, wo, nw, wq, *, tm, q_dtype):
    m = x.shape[0]
    d = D_MODEL
    row = lambda i: (i, 0)
    wspec = pl.BlockSpec((d, d), lambda i: (0, 0))
    return pl.pallas_call(
        _merge_kernel,
        grid=(m // tm,),
        in_specs=[pl.BlockSpec((tm, d), row), pl.BlockSpec((tm, d), row), pl.BlockSpec((tm, d), row),
                  pl.BlockSpec((tm, d), lambda i: (i, C_GR // d)),
                  pl.BlockSpec((tm, d), lambda i: (i, C_GM // d)),
                  pl.BlockSpec((tm, d), lambda i: (i, C_GS // d)),
                  pl.BlockSpec((tm, d), row),
                  wspec, wspec, wspec, wspec,
                  pl.BlockSpec((1, d), lambda i: (0, 0)),
                  wspec],
        out_specs=[pl.BlockSpec((tm, d), row), pl.BlockSpec((tm, d), row)],
        out_shape=[jax.ShapeDtypeStruct((m, d), F32), jax.ShapeDtypeStruct((m, d), q_dtype)],
        compiler_params=_cparams(1),
        name="merge",
    )(yr, ym, ys, proj, proj, proj, x, wr, wm, ws, wo, nw, wq)


def _attn_prefill_kernel(q_ref, k_ref, v_ref, x_ref, wo_ref, o_ref):
    outs = []
    for h in range(MEM_H):
        sl = slice(h * MEM_DH, (h + 1) * MEM_DH)
        q = q_ref[0, :, sl]
        k = k_ref[0, :, sl].astype(BF16)
        v = v_ref[0, :, sl].astype(BF16)
        s = _dot_nt(q, k) * (MEM_DH ** -0.5)
        p = jnp.exp(s - jnp.max(s, axis=-1, keepdims=True))
        p = p / jnp.sum(p, axis=-1, keepdims=True)
        outs.append(_dot(p.astype(BF16), v).astype(BF16))
    o_all = jnp.concatenate(outs, axis=1)
    o_ref[0] = x_ref[0] + _dot(o_all, wo_ref[...])


def _attn_prefill(q, mk, mv, x, wo, *, layer, nb, seq, tq):
    d = D_MODEL
    q3 = q.reshape(nb, seq, d)
    x3 = x.reshape(nb, seq, d)
    out = pl.pallas_call(
        _attn_prefill_kernel,
        grid=(nb, seq // tq),
        in_specs=[pl.BlockSpec((1, tq, d), lambda b, t: (b, t, 0)),
                  pl.BlockSpec((1, MEM_LEN, d), lambda b, t: (layer * nb + b, 0, 0)),
                  pl.BlockSpec((1, MEM_LEN, d), lambda b, t: (layer * nb + b, 0, 0)),
                  pl.BlockSpec((1, tq, d), lambda b, t: (b, t, 0)),
                  pl.BlockSpec((d, d), lambda b, t: (0, 0))],
        out_specs=pl.BlockSpec((1, tq, d), lambda b, t: (b, t, 0)),
        out_shape=jax.ShapeDtypeStruct((nb, seq, d), F32),
        compiler_params=_cparams(2),
        name="attn_prefill",
    )(q3, mk, mv, x3, wo)
    return out.reshape(nb * seq, d)


def _attn_decode_kernel(q_ref, k_ref, v_ref, o_ref):
    nbk = q_ref.shape[0]
    for j in range(nbk):
        prod = k_ref[j] * q_ref[j:j + 1, :]
        for h in range(MEM_H):
            sl = slice(h * MEM_DH, (h + 1) * MEM_DH)
            s = jnp.sum(prod[:, sl], axis=1, keepdims=True) * (MEM_DH ** -0.5)
            p = jnp.exp(s - jnp.max(s, axis=0, keepdims=True))
            p = p / jnp.sum(p, axis=0, keepdims=True)
            o_ref[j:j + 1, sl] = jnp.sum(p * v_ref[j, :, sl], axis=0, keepdims=True)


def _attn_decode(q, ck, cv, *, layer, bb):
    nb, d = q.shape
    return pl.pallas_call(
        _attn_decode_kernel,
        grid=(nb // bb,),
        in_specs=[pl.BlockSpec((bb, d), lambda i: (i, 0)),
                  pl.BlockSpec((bb, MEM_LEN, d), lambda i: (layer * (nb // bb) + i, 0, 0)),
                  pl.BlockSpec((bb, MEM_LEN, d), lambda i: (layer * (nb // bb) + i, 0, 0))],
        out_specs=pl.BlockSpec((bb, d), lambda i: (i, 0)),
        out_shape=jax.ShapeDtypeStruct((nb, d), F32),
        compiler_params=_cparams(1),
        name="attn_decode",
    )(q, ck, cv)


def _mlp_kernel(x_ref, nw_ref, w1_ref, w2_ref, nf_ref, o_ref, xn_ref, acc_ref, *, final_norm):
    j = pl.program_id(1)

    @pl.when(j == 0)
    def _():
        xn_ref[...] = _rmsnorm(x_ref[...], nw_ref[...]).astype(BF16)
        acc_ref[...] = jnp.zeros_like(acc_ref)

    hid = jnp.square(jnp.maximum(_dot(xn_ref[...], w1_ref[...]), 0.0))
    acc_ref[...] += _dot(hid.astype(BF16), w2_ref[...])

    @pl.when(j == pl.num_programs(1) - 1)
    def _():
        y = x_ref[...] + acc_ref[...]
        if final_norm:
            y = _rmsnorm(y, nf_ref[...])
        o_ref[...] = y


def _mlp(x, nw, w1, w2, nf, *, tm, tf, final_norm):
    m = x.shape[0]
    d = D_MODEL
    return pl.pallas_call(
        functools.partial(_mlp_kernel, final_norm=final_norm),
        grid=(m // tm, D_FF // tf),
        in_specs=[pl.BlockSpec((tm, d), lambda i, j: (i, 0)),
                  pl.BlockSpec((1, d), lambda i, j: (0, 0)),
                  pl.BlockSpec((d, tf), lambda i, j: (0, j)),
                  pl.BlockSpec((tf, d), lambda i, j: (j, 0)),
                  pl.BlockSpec((1, d), lambda i, j: (0, 0))],
        out_specs=pl.BlockSpec((tm, d), lambda i, j: (i, 0)),
        out_shape=jax.ShapeDtypeStruct((m, d), F32),
        scratch_shapes=[pltpu.VMEM((tm, d), BF16), pltpu.VMEM((tm, d), F32)],
        compiler_params=_cparams(2),
        name="mlp",
    )(x, nw, w1, w2, nf)


DEC_BB = 8


def _ret_decode_kernel(dec_ref, q_ref, k_ref, v_ref, g_ref, cos_ref, sin_ref, nw_ref, s_ref,
                       y_ref, so_ref, ybuf):
    h = pl.program_id(1)
    g_dec = dec_ref[h]
    cosf = cos_ref[...]
    sinf = sin_ref[...]
    q = q_ref[...]
    k = k_ref[...]
    qr = q * cosf + pltpu.roll(q, RET_DK // 2, 1) * sinf
    kr = (k * cosf + pltpu.roll(k, RET_DK // 2, 1) * sinf) * (RET_DK ** -0.5)
    q_t = _pad_rows_t(qr)
    k_t = _pad_rows_t(kr)
    v = v_ref[...]
    for j in range(DEC_BB):
        s_new = s_ref[j, 0] * g_dec + k_t[:, j:j + 1] * v[j:j + 1, :]
        so_ref[j, 0] = s_new
        ybuf[j:j + 1, :] = jnp.sum(q_t[:, j:j + 1] * s_new, axis=0, keepdims=True)
    y_ref[...] = _groupnorm(ybuf[...], nw_ref[...], True) * _silu(g_ref[...])


def _ret_decode(dec, proj, cosf, sinf, nw, state, *, layer):
    nb = proj.shape[0]
    bb = DEC_BB
    return pl.pallas_call(
        _ret_decode_kernel,
        grid=(nb // bb, RET_H),
        in_specs=[pl.BlockSpec(memory_space=pltpu.SMEM),
                  pl.BlockSpec((bb, RET_DK), lambda i, h: (i, C_Q // RET_DK + h)),
                  pl.BlockSpec((bb, RET_DK), lambda i, h: (i, C_K // RET_DK + h)),
                  pl.BlockSpec((bb, RET_DV), lambda i, h: (i, C_V // RET_DV + h)),
                  pl.BlockSpec((bb, RET_DV), lambda i, h: (i, C_G // RET_DV + h)),
                  pl.BlockSpec((1, RET_DK), lambda i, h: (0, 0)),
                  pl.BlockSpec((1, RET_DK), lambda i, h: (0, 0)),
                  pl.BlockSpec((1, RET_DV), lambda i, h: (0, h)),
                  pl.BlockSpec((bb, 1, RET_DK, RET_DV), lambda i, h: (layer * (nb // bb) + i, h, 0, 0))],
        out_specs=[pl.BlockSpec((bb, RET_DV), lambda i, h: (i, h)),
                   pl.BlockSpec((bb, 1, RET_DK, RET_DV), lambda i, h: (i, h, 0, 0))],
        out_shape=[jax.ShapeDtypeStruct((nb, RET_H * RET_DV), F32),
                   jax.ShapeDtypeStruct((nb, RET_H, RET_DK, RET_DV), F32)],
        scratch_shapes=[pltpu.VMEM((bb, RET_DV), F32)],
        compiler_params=_cparams(2),
        name="ret_decode",
    )(dec, proj, proj, proj, proj, cosf, sinf, nw, state)


def _ml_decode_kernel(u_ref, v_ref, o_ref, sm_ref, buf_ref, cw_ref, cb_ref, wq_ref, wk_ref, gb_ref, nw_ref,
                      c_ref, n_ref, m_ref,
                      y_ref, co_ref, no_ref, mo_ref, bo_ref, ybuf):
    h = pl.program_id(1)
    u = u_ref[...]
    acc = cb_ref[...]
    for j in range(CONV_K - 1):
        acc = acc + buf_ref[j] * cw_ref[j:j + 1, :]
    acc = acc + u * cw_ref[CONV_K - 1:CONV_K, :]
    uc = _silu(acc).astype(BF16)
    bo_ref[0] = buf_ref[1]
    bo_ref[1] = buf_ref[2]
    bo_ref[2] = u

    qm = _dot(uc, wq_ref[0])
    km = _dot(uc, wk_ref[0]) * (ML_DH ** -0.5)
    gates = sm_ref[...] + gb_ref[...]
    i_pre = _lane_pick(gates, L_IG + h)
    log_f = _log_sigmoid(_lane_pick(gates, L_FG + h))
    lane4 = lax.broadcasted_iota(jnp.int32, m_ref.shape, 1)
    m_prev = jnp.sum(jnp.where(lane4 == h, m_ref[...], 0.0), axis=1, keepdims=True)
    inter = log_f + m_prev
    m_new = jnp.maximum(inter, i_pre)
    w_s = jnp.exp(i_pre - m_new)
    w_prev = jnp.exp(inter - m_new)
    kw = km * w_s
    n_new = n_ref[...] * w_prev + kw
    no_ref[...] = n_new

    @pl.when(h == 0)
    def _():
        mo_ref[...] = jnp.zeros_like(mo_ref)

    mo_ref[...] = jnp.where(lane4 == h, m_new, mo_ref[...])

    q_t = _pad_rows_t(qm)
    k_t = _pad_rows_t(kw)
    v = v_ref[...]
    for j in range(DEC_BB):
        c_new = c_ref[j, 0] * w_prev[j:j + 1, :] + k_t[:, j:j + 1] * v[j:j + 1, :]
        co_ref[j, 0] = c_new
        ybuf[j:j + 1, :] = jnp.sum(q_t[:, j:j + 1] * c_new, axis=0, keepdims=True)
    den = jnp.sum(qm * n_new, axis=-1, keepdims=True)
    den = jnp.maximum(jnp.abs(den), jnp.exp(-m_new))
    hh = ybuf[...] / den
    y_ref[...] = _groupnorm(hh, nw_ref[...], True) * _sigmoid(o_ref[...])


def _ml_decode(proj, small, buf, cw, cb, wq, wk, gb, nw, c_state, n_state, m_state, *, layer):
    nb = proj.shape[0]
    bb = DEC_BB
    nblk = nb // bb
    dh = ML_DH
    w = ML_H * ML_DH
    return pl.pallas_call(
        _ml_decode_kernel,
        grid=(nblk, ML_H),
        in_specs=[pl.BlockSpec((bb, dh), lambda i, h: (i, C_U // dh + h)),
                  pl.BlockSpec((bb, dh), lambda i, h: (i, C_VM // dh + h)),
                  pl.BlockSpec((bb, dh), lambda i, h: (i, C_OM // dh + h)),
                  pl.BlockSpec((bb, N_SMALL), lambda i, h: (i, 0)),
                  pl.BlockSpec((CONV_K - 1, bb, dh), lambda i, h: (0, i, h)),
                  pl.BlockSpec((CONV_K, dh), lambda i, h: (0, h)),
                  pl.BlockSpec((1, dh), lambda i, h: (0, h)),
                  pl.BlockSpec((1, dh, dh), lambda i, h: (h, 0, 0)),
                  pl.BlockSpec((1, dh, dh), lambda i, h: (h, 0, 0)),
                  pl.BlockSpec((1, N_SMALL), lambda i, h: (0, 0)),
                  pl.BlockSpec((1, dh), lambda i, h: (0, h)),
                  pl.BlockSpec((bb, 1, dh, dh), lambda i, h: (layer * nblk + i, h, 0, 0)),
                  pl.BlockSpec((bb, dh), lambda i, h: (layer * nblk + i, h)),
                  pl.BlockSpec((bb, ML_H), lambda i, h: (layer * nblk + i, 0))],
        out_specs=[pl.BlockSpec((bb, dh), lambda i, h: (i, h)),
                   pl.BlockSpec((bb, 1, dh, dh), lambda i, h: (i, h, 0, 0)),
                   pl.BlockSpec((bb, dh), lambda i, h: (i, h)),
                   pl.BlockSpec((bb, ML_H), lambda i, h: (i, 0)),
                   pl.BlockSpec((CONV_K - 1, bb, dh), lambda i, h: (0, i, h))],
        out_shape=[jax.ShapeDtypeStruct((nb, w), F32),
                   jax.ShapeDtypeStruct((nb, ML_H, dh, dh), F32),
                   jax.ShapeDtypeStruct((nb, w), F32),
                   jax.ShapeDtypeStruct((nb, ML_H), F32),
                   jax.ShapeDtypeStruct((CONV_K - 1, nb, w), F32)],
        scratch_shapes=[pltpu.VMEM((bb, dh), F32)],
        compiler_params=_cparams(2),
        name="ml_decode",
    )(proj, proj, proj, small, buf, cw, cb, wq, wk, gb, nw, c_state, n_state, m_state)


def _ssd_decode_kernel(z_ref, xs_ref, b_ref, c_ref, sm_ref, bufx_ref, bufb_ref, bufc_ref,
                       cwx_ref, cwb_ref, cwc_ref, cbx_ref, cbb_ref, cbc_ref,
                       dtb_ref, alog_ref, dv_ref, nw_ref, s_ref,
                       y_ref, so_ref, box_ref, bob_ref, boc_ref):
    g = pl.program_id(1)

    def conv(x_ref, buf_ref, cw_ref, cb_ref, bo_ref):
        x = x_ref[...]
        acc = cb_ref[...]
        for j in range(CONV_K - 1):
            acc = acc + buf_ref[j] * cw_ref[j:j + 1, :]
        acc = acc + x * cw_ref[CONV_K - 1:CONV_K, :]
        bo_ref[0] = buf_ref[1]
        bo_ref[1] = buf_ref[2]
        bo_ref[2] = x
        return _silu(acc)

    xg = conv(xs_ref, bufx_ref, cwx_ref, cbx_ref, box_ref)
    bm = conv(b_ref, bufb_ref, cwb_ref, cbb_ref, bob_ref)
    cm = conv(c_ref, bufc_ref, cwc_ref, cbc_ref, boc_ref)

    delta = _softplus(sm_ref[...] + dtb_ref[...])
    d_a = jnp.exp(delta * (-jnp.exp(alog_ref[...])))
    lane = lax.shift_right_logical(lax.broadcasted_iota(jnp.int32, (DEC_BB, SSM_R * SSM_P), 1), 6)
    dt_blk = jnp.zeros((DEC_BB, SSM_R * SSM_P), F32)
    d_cols = []
    for r in range(SSM_R):
        ln = L_DT + g * SSM_R + r
        dt_blk = jnp.where(lane == r, _lane_pick(delta, ln), dt_blk)
        d_cols.append(_lane_pick(d_a, ln))
    x_t = _pad_rows_t(xg * dt_blk)

    lane_j = lax.broadcasted_iota(jnp.int32, (SSM_R * SSM_P, 128), 1)
    y_cols = jnp.zeros((SSM_R * SSM_P, 128), F32)
    for j in range(DEC_BB):
        parts = []
        for r in range(SSM_R):
            s_new = (s_ref[j, r] * d_cols[r][j:j + 1, :]
                     + x_t[r * SSM_P:(r + 1) * SSM_P, j:j + 1] * bm[j:j + 1, :])
            so_ref[j, r] = s_new
            parts.append(jnp.sum(s_new * cm[j:j + 1, :], axis=1, keepdims=True))
        y_cols = jnp.where(lane_j == j, jnp.concatenate(parts, axis=0), y_cols)
    y = y_cols.T[0:DEC_BB, :]
    y = y + dv_ref[...] * xg
    y_ref[...] = _groupnorm(y * _silu(z_ref[...]), nw_ref[...], False)


def _ssd_decode(proj, small, buf, cw, cb, dtb, alog, dvec, nw, state, *, layer):
    nb = proj.shape[0]
    bb = DEC_BB
    nblk = nb // bb
    gw = SSM_R * SSM_P
    wx = SSM_H * SSM_P
    nbx = wx // SSM_N
    return pl.pallas_call(
        _ssd_decode_kernel,
        grid=(nblk, SSM_G),
        in_specs=[pl.BlockSpec((bb, gw), lambda i, g: (i, C_Z // gw + g)),
                  pl.BlockSpec((bb, gw), lambda i, g: (i, C_XS // gw + g)),
                  pl.BlockSpec((bb, SSM_N), lambda i, g: (i, C_B // SSM_N + g)),
                  pl.BlockSpec((bb, SSM_N), lambda i, g: (i, C_C // SSM_N + g)),
                  pl.BlockSpec((bb, N_SMALL), lambda i, g: (i, 0)),
                  pl.BlockSpec((CONV_K - 1, bb, gw), lambda i, g: (0, i, g)),
                  pl.BlockSpec((CONV_K - 1, bb, SSM_N), lambda i, g: (0, i, nbx + g)),
                  pl.BlockSpec((CONV_K - 1, bb, SSM_N), lambda i, g: (0, i, nbx + SSM_G + g)),
                  pl.BlockSpec((CONV_K, gw), lambda i, g: (0, g)),
                  pl.BlockSpec((CONV_K, SSM_N), lambda i, g: (0, nbx + g)),
                  pl.BlockSpec((CONV_K, SSM_N), lambda i, g: (0, nbx + SSM_G + g)),
                  pl.BlockSpec((1, gw), lambda i, g: (0, g)),
                  pl.BlockSpec((1, SSM_N), lambda i, g: (0, nbx + g)),
                  pl.BlockSpec((1, SSM_N), lambda i, g: (0, nbx + SSM_G + g)),
                  pl.BlockSpec((1, N_SMALL), lambda i, g: (0, 0)),
                  pl.BlockSpec((1, N_SMALL), lambda i, g: (0, 0)),
                  pl.BlockSpec((1, gw), lambda i, g: (0, g)),
                  pl.BlockSpec((1, gw), lambda i, g: (0, g)),
                  pl.BlockSpec((bb, SSM_R, SSM_P, SSM_N), lambda i, g: (layer * nblk + i, g, 0, 0))],
        out_specs=[pl.BlockSpec((bb, gw), lambda i, g: (i, g)),
                   pl.BlockSpec((bb, SSM_R, SSM_P, SSM_N), lambda i, g: (i, g, 0, 0)),
                   pl.BlockSpec((CONV_K - 1, bb, gw), lambda i, g: (0, i, g)),
                   pl.BlockSpec((CONV_K - 1, bb, SSM_N), lambda i, g: (0, i, g)),
                   pl.BlockSpec((CONV_K - 1, bb, SSM_N), lambda i, g: (0, i, g))],
        out_shape=[jax.ShapeDtypeStruct((nb, wx), F32),
                   jax.ShapeDtypeStruct((nb, SSM_H, SSM_P, SSM_N), F32),
                   jax.ShapeDtypeStruct((CONV_K - 1, nb, wx), F32),
                   jax.ShapeDtypeStruct((CONV_K - 1, nb, SSM_G * SSM_N), F32),
                   jax.ShapeDtypeStruct((CONV_K - 1, nb, SSM_G * SSM_N), F32)],
        compiler_params=_cparams(2),
        name="ssd_decode",
    )(proj, proj, proj, proj, small, buf, buf, buf, cw, cw, cw, cb, cb, cb, dtb, alog, dvec, nw, state)


def _rope_tables(pos):
    half = RET_DK // 2
    freqs = ROPE_THETA ** (-jnp.arange(half, dtype=F32) / half)
    ang = pos.astype(F32)[:, None] * freqs[None, :]
    cos = jnp.cos(ang)
    sin = jnp.sin(ang)
    return jnp.concatenate([cos, cos], axis=1), jnp.concatenate([-sin, sin], axis=1)


def _pad_lanes(v, offset):
    return jnp.zeros((1, N_SMALL), F32).at[0, offset:offset + v.shape[0]].set(v.astype(F32))


def _layer_weights(l, W):
    w_in = W['w_in'][l]
    n_ml0 = 2 * 512 + 2 * 1024
    n_ig = n_ml0 + 3 * 1024
    n_ssm0 = n_ig + 2 * ML_H
    n_dt = n_ssm0 + 1024 + 2048
    n_gate = n_dt + SSM_H
    w_main = jnp.concatenate([w_in[:, :n_ig], w_in[:, n_ssm0:n_dt], w_in[:, n_gate:]], axis=1).astype(BF16)
    w_small = jnp.concatenate([w_in[:, n_ig:n_ssm0], w_in[:, n_dt:n_gate],
                               jnp.zeros((D_MODEL, N_SMALL - 2 * ML_H - SSM_H), F32)], axis=1).astype(BF16)
    row = lambda v: v.reshape(1, -1).astype(F32)
    return dict(
        w_main=w_main, w_small=w_small,
        norm_mix=row(W['norm_mix_w'][l]),
        ret_norm=row(W['ret_norm_w'][l]),
        ml_cw=W['ml_conv_w'][l], ml_cb=row(W['ml_conv_b'][l]),
        ml_wq=W['ml_wq'][l].astype(BF16), ml_wk=W['ml_wk'][l].astype(BF16),
        ml_gb=_pad_lanes(W['ml_gate_b'][l], L_IG),
        ml_norm=row(W['ml_norm_w'][l]),
        ssm_cw=W['ssm_conv_w'][l], ssm_cb=row(W['ssm_conv_b'][l]),
        ssm_dtb=_pad_lanes(W['ssm_dt_bias'][l], L_DT),
        ssm_alog=_pad_lanes(W['ssm_A_log'][l], L_DT),
        ssm_dvec=row(jnp.repeat(W['ssm_D'][l], SSM_P)),
        ssm_norm=row(W['ssm_norm_w'][l]),
        w_br_ret=W['w_br_ret'][l].astype(BF16), w_br_ml=W['w_br_ml'][l].astype(BF16),
        w_br_ssm=W['w_br_ssm'][l].astype(BF16), w_out=W['w_out_mix'][l].astype(BF16),
        norm_mem=row(W['norm_mem_w'][l]),
        mem_wq=W['mem_wq'][l].astype(BF16), mem_wo=W['mem_wo'][l].astype(BF16),
        norm_mlp=row(W['norm_mlp_w'][l]),
        mlp_w1=W['mlp_w1'][l].astype(BF16), mlp_w2=W['mlp_w2'][l].astype(BF16),
    )


def kernel(x_prompt, x_sample, mem_prompt, state_ret, state_mlstm_C, state_mlstm_n, state_mlstm_m,
           state_mlstm_conv, state_ssm, state_ssm_conv, cache_mem_k, cache_mem_v,
           norm_mix_w, w_in, ret_norm_w, ml_conv_w, ml_conv_b, ml_wq, ml_wk, ml_gate_b, ml_norm_w,
           ssm_conv_w, ssm_conv_b, ssm_dt_bias, ssm_A_log, ssm_D, ssm_norm_w,
           w_br_ret, w_br_ml, w_br_ssm, w_out_mix, norm_mem_w, mem_wq, mem_wk, mem_wv, mem_wo,
           norm_mlp_w, mlp_w1, mlp_w2, norm_f_w):
    W = dict(norm_mix_w=norm_mix_w, w_in=w_in, ret_norm_w=ret_norm_w, ml_conv_w=ml_conv_w,
             ml_conv_b=ml_conv_b, ml_wq=ml_wq, ml_wk=ml_wk, ml_gate_b=ml_gate_b, ml_norm_w=ml_norm_w,
             ssm_conv_w=ssm_conv_w, ssm_conv_b=ssm_conv_b, ssm_dt_bias=ssm_dt_bias, ssm_A_log=ssm_A_log,
             ssm_D=ssm_D, ssm_norm_w=ssm_norm_w, w_br_ret=w_br_ret, w_br_ml=w_br_ml, w_br_ssm=w_br_ssm,
             w_out_mix=w_out_mix, norm_mem_w=norm_mem_w, mem_wq=mem_wq, mem_wo=mem_wo,
             norm_mlp_w=norm_mlp_w, mlp_w1=mlp_w1, mlp_w2=mlp_w2)
    LW = [_layer_weights(l, W) for l in range(DEPTH)]
    norm_f = norm_f_w.reshape(1, -1).astype(F32)
    d = D_MODEL

    nb, seq = x_prompt.shape[0], x_prompt.shape[1]
    cos_p, sin_p = _rope_tables(jnp.arange(seq, dtype=jnp.int32))
    memk, memv = _memkv(mem_prompt.reshape(nb * MEM_LEN, d), mem_wk.astype(BF16), mem_wv.astype(BF16), tm=1024)
    memk3 = memk.reshape(DEPTH * nb, MEM_LEN, d)
    memv3 = memv.reshape(DEPTH * nb, MEM_LEN, d)

    x = x_prompt.reshape(nb * seq, d)
    st_p = []
    for l in range(DEPTH):
        lw = LW[l]
        proj, small = _inproj(x, lw['norm_mix'], lw['w_main'], lw['w_small'], tm=1024, tn=1024, out_dtype=BF16)
        y_r, ret_s = _ret_prefill(proj, cos_p, sin_p, lw['ret_norm'], nb=nb, seq=seq)
        y_m, ml_c, ml_n, ml_m, ml_cv = _ml_prefill(proj, small, lw['ml_cw'], lw['ml_cb'], lw['ml_wq'], lw['ml_wk'],
                                                   lw['ml_gb'], lw['ml_norm'], nb=nb, seq=seq)
        y_s, ssm_s, ssm_cv = _ssd_prefill(proj, small, lw['ssm_cw'], lw['ssm_cb'], lw['ssm_dtb'], lw['ssm_alog'],
                                          lw['ssm_dvec'], lw['ssm_norm'], nb=nb, seq=seq)
        x, qm = _merge(y_r.reshape(nb * seq, d), y_m.reshape(nb * seq, d), y_s.reshape(nb * seq, d), proj, x,
                       lw['w_br_ret'], lw['w_br_ml'], lw['w_br_ssm'], lw['w_out'], lw['norm_mem'], lw['mem_wq'],
                       tm=256, q_dtype=BF16)
        x = _attn_prefill(qm, memk3, memv3, x, lw['mem_wo'], layer=l, nb=nb, seq=seq, tq=512)
        x = _mlp(x, lw['norm_mlp'], lw['mlp_w1'], lw['mlp_w2'], norm_f, tm=1024, tf=1024,
                 final_norm=(l == DEPTH - 1))
        st_p.append((ret_s, ml_c, ml_n, ml_m[:, 0, :ML_H], ml_cv, ssm_s, ssm_cv))
    y_prompt = x.reshape(nb, seq, d)
    prompt_states = [jnp.stack([st_p[l][i] for l in range(DEPTH)]) for i in range(7)]
    memk_p = memk.reshape(DEPTH, nb, MEM_LEN, MEM_H, MEM_DH)
    memv_p = memv.reshape(DEPTH, nb, MEM_LEN, MEM_H, MEM_DH)

    ns = x_sample.shape[0]
    cos_s, sin_s = _rope_tables(PAST_LEN + jnp.arange(1, dtype=jnp.int32))
    ret_dec = jnp.exp(jnp.log1p(-jnp.exp2(-5.0 - jnp.arange(RET_H, dtype=F32))))
    ret_state = state_ret.reshape(DEPTH * ns, RET_H, RET_DK, RET_DV)
    mlc_state = state_mlstm_C.reshape(DEPTH * ns, ML_H, ML_DH, ML_DH)
    mln_state = state_mlstm_n.reshape(DEPTH * ns, ML_H * ML_DH)
    mlm_state = state_mlstm_m.reshape(DEPTH * ns, ML_H)
    ssm_st = state_ssm.reshape(DEPTH * ns, SSM_H, SSM_P, SSM_N)
    ck = cache_mem_k.reshape(DEPTH * ns, MEM_LEN, d)
    cv = cache_mem_v.reshape(DEPTH * ns, MEM_LEN, d)

    x = x_sample.reshape(ns, d)
    st_s = []
    for l in range(DEPTH):
        lw = LW[l]
        proj, small = _inproj(x, lw['norm_mix'], lw['w_main'], lw['w_small'], tm=ns, tn=1024, out_dtype=F32)
        y_r, ret_s = _ret_decode(ret_dec, proj, cos_s, sin_s, lw['ret_norm'], ret_state, layer=l)
        ml_buf = jnp.swapaxes(state_mlstm_conv[l], 0, 1)
        y_m, ml_c, ml_n, ml_m, ml_bo = _ml_decode(proj, small, ml_buf, lw['ml_cw'], lw['ml_cb'], lw['ml_wq'],
                                                  lw['ml_wk'], lw['ml_gb'], lw['ml_norm'],
                                                  mlc_state, mln_state, mlm_state, layer=l)
        ssm_buf = jnp.swapaxes(state_ssm_conv[l], 0, 1)
        y_s, ssm_s, bo_x, bo_b, bo_c = _ssd_decode(proj, small, ssm_buf, lw['ssm_cw'], lw['ssm_cb'], lw['ssm_dtb'],
                                                   lw['ssm_alog'], lw['ssm_dvec'], lw['ssm_norm'], ssm_st, layer=l)
        x, qm = _merge(y_r, y_m, y_s, proj, x, lw['w_br_ret'], lw['w_br_ml'], lw['w_br_ssm'], lw['w_out'],
                       lw['norm_mem'], lw['mem_wq'], tm=ns, q_dtype=F32)
        att = _attn_decode(qm, ck, cv, layer=l, bb=DEC_BB)
        x = _mm(att, lw['mem_wo'], x, tm=ns, tn=d)
        x = _mlp(x, lw['norm_mlp'], lw['mlp_w1'], lw['mlp_w2'], norm_f, tm=ns, tf=1024,
                 final_norm=(l == DEPTH - 1))
        ml_cv = jnp.swapaxes(ml_bo, 0, 1)
        ssm_cv = jnp.swapaxes(jnp.concatenate([bo_x, bo_b, bo_c], axis=2), 0, 1)
        st_s.append((ret_s, ml_c, ml_n.reshape(ns, ML_H, ML_DH), ml_m, ml_cv, ssm_s, ssm_cv))
    y_sample = x.reshape(ns, 1, d)
    sample_states = [jnp.stack([st_s[l][i] for l in range(DEPTH)]) for i in range(7)]

    return (y_prompt, y_sample, *prompt_states, memk_p, memv_p, *sample_states)
```

```python
import functools
import math

import jax
import jax.numpy as jnp
from jax import lax
from jax.experimental import pallas as pl
from jax.experimental.pallas import tpu as pltpu

F32 = jnp.float32
BF16 = jnp.bfloat16

D_MODEL = 1024
DEPTH = 2
PAST_LEN = 16384
CHUNK = 128
CONV_K = 4
EPS = 1e-6
RET_H, RET_DK, RET_DV = 4, 128, 256
ROPE_THETA = 10000.0
ML_H, ML_DH = 4, 256
SSM_H, SSM_P, SSM_G, SSM_N = 16, 64, 4, 128
SSM_R = SSM_H // SSM_G
MEM_LEN, MEM_H, MEM_DH = 256, 4, 256
D_FF = 4 * D_MODEL

C_Q, C_K, C_V, C_G = 0, 512, 1024, 2048
C_U, C_VM, C_OM = 3072, 4096, 5120
C_Z, C_XS, C_B, C_C = 6144, 7168, 8192, 8704
C_GR, C_GM, C_GS = 9216, 10240, 11264
N_MAIN = 12288
L_IG, L_FG, L_DT = 0, 4, 8
N_SMALL = 128

VMEM_LIMIT = 56 * 1024 * 1024
NT_DIMS = (((1,), (1,)), ((), ()))

_LOG_G = [math.log1p(-(2.0 ** (-5.0 - h))) for h in range(RET_H)]


def _cparams(n_axes):
    return pltpu.CompilerParams(dimension_semantics=("arbitrary",) * n_axes,
                                vmem_limit_bytes=VMEM_LIMIT)


def _dot(a, b):
    return jnp.dot(a, b, preferred_element_type=F32)


def _dot_nt(a, b):
    return lax.dot_general(a, b, NT_DIMS, preferred_element_type=F32)


def _rmsnorm(x, w):
    ms = jnp.mean(x * x, axis=-1, keepdims=True)
    return x * lax.rsqrt(ms + EPS) * w


def _groupnorm(x, w, center):
    if center:
        x = x - jnp.mean(x, axis=-1, keepdims=True)
    ms = jnp.mean(x * x, axis=-1, keepdims=True)
    return x * lax.rsqrt(ms + EPS) * w


def _sigmoid(x):
    return 1.0 / (1.0 + jnp.exp(-x))


def _silu(x):
    return x * _sigmoid(x)


def _softplus(x):
    return jnp.maximum(x, 0.0) + jnp.log1p(jnp.exp(-jnp.abs(x)))


def _log_sigmoid(x):
    return -_softplus(-x)


def _cumsum_rows(x, tril_b):
    hi = x.astype(BF16)
    r1 = x - hi.astype(F32)
    mid = r1.astype(BF16)
    lo = (r1 - mid.astype(F32)).astype(BF16)
    return _dot(tril_b, hi) + _dot(tril_b, mid) + _dot(tril_b, lo)


def _pad_rows(x, rows):
    return jnp.concatenate([x, jnp.zeros((rows - x.shape[0], x.shape[1]), x.dtype)], axis=0)


def _pad_rows_t(x):
    return _pad_rows(x, 128).T


def _skip_alias(kernel_fn, n_alias):
    if n_alias == 0:
        return kernel_fn
    return lambda *refs: kernel_fn(*refs[n_alias:])


def _inproj_kernel(x_ref, nw_ref, w_ref, ws_ref, o_ref, os_ref, xn_ref):
    @pl.when(pl.program_id(1) == 0)
    def _():
        xn = _rmsnorm(x_ref[...], nw_ref[...]).astype(BF16)
        xn_ref[...] = xn
        os_ref[...] = _dot(xn, ws_ref[...])

    o_ref[...] = _dot(xn_ref[...], w_ref[...]).astype(o_ref.dtype)


def _inproj(x, nw, w_main, w_small, *, tm, tn, out_dtype):
    m = x.shape[0]
    return pl.pallas_call(
        _inproj_kernel,
        grid=(m // tm, N_MAIN // tn),
        in_specs=[pl.BlockSpec((tm, D_MODEL), lambda i, j: (i, 0)),
                  pl.BlockSpec((1, D_MODEL), lambda i, j: (0, 0)),
                  pl.BlockSpec((D_MODEL, tn), lambda i, j: (0, j)),
                  pl.BlockSpec((D_MODEL, N_SMALL), lambda i, j: (0, 0))],
        out_specs=[pl.BlockSpec((tm, tn), lambda i, j: (i, j)),
                   pl.BlockSpec((tm, N_SMALL), lambda i, j: (i, 0))],
        out_shape=[jax.ShapeDtypeStruct((m, N_MAIN), out_dtype),
                   jax.ShapeDtypeStruct((m, N_SMALL), F32)],
        scratch_shapes=[pltpu.VMEM((tm, D_MODEL), BF16)],
        compiler_params=_cparams(2),
        name="inproj",
    )(x, nw, w_main, w_small)


def _mm_kernel(*refs, has_res):
    if has_res:
        a_ref, w_ref, r_ref, o_ref = refs
    else:
        a_ref, w_ref, o_ref = refs
    acc = _dot(a_ref[...].astype(BF16), w_ref[...])
    if has_res:
        acc = r_ref[...] + acc
    o_ref[...] = acc.astype(o_ref.dtype)


def _mm(a, w, res=None, *, tm, tn, out_dtype=F32):
    m, k = a.shape
    n = w.shape[1]
    in_specs = [pl.BlockSpec((tm, k), lambda i, j: (i, 0)),
                pl.BlockSpec((k, tn), lambda i, j: (0, j))]
    args = [a, w]
    if res is not None:
        in_specs.append(pl.BlockSpec((tm, tn), lambda i, j: (i, j)))
        args.append(res)
    return pl.pallas_call(
        functools.partial(_mm_kernel, has_res=res is not None),
        grid=(m // tm, n // tn),
        in_specs=in_specs,
        out_specs=pl.BlockSpec((tm, tn), lambda i, j: (i, j)),
        out_shape=jax.ShapeDtypeStruct((m, n), out_dtype),
        compiler_params=_cparams(2),
        name="mm",
    )(*args)


def _memkv_kernel(a_ref, wk_ref, wv_ref, ok_ref, ov_ref):
    a = a_ref[...].astype(BF16)
    ok_ref[0] = _dot(a, wk_ref[0])
    ov_ref[0] = _dot(a, wv_ref[0])


def _memkv(a, wk, wv, *, tm):
    m = a.shape[0]
    n = wk.shape[2]
    return pl.pallas_call(
        _memkv_kernel,
        grid=(DEPTH, m // tm),
        in_specs=[pl.BlockSpec((tm, D_MODEL), lambda l, i: (i, 0)),
                  pl.BlockSpec((1, D_MODEL, n), lambda l, i: (l, 0, 0)),
                  pl.BlockSpec((1, D_MODEL, n), lambda l, i: (l, 0, 0))],
        out_specs=[pl.BlockSpec((1, tm, n), lambda l, i: (l, i, 0)),
                   pl.BlockSpec((1, tm, n), lambda l, i: (l, i, 0))],
        out_shape=[jax.ShapeDtypeStruct((DEPTH, m, n), F32)] * 2,
        compiler_params=_cparams(2),
        name="memkv",
    )(a, wk, wv)


def _state_io(prev, n_alias_slots):
    if prev is None:
        return [], [], {}
    return (list(prev), [pl.BlockSpec(memory_space=pl.ANY)] * len(prev),
            {i: n_alias_slots[i] for i in range(len(prev))})


def _ret_prefill_kernel(q_ref, k_ref, v_ref, g_ref, cos_ref, sin_ref, nw_ref, y_ref, s_ref):
    @pl.when(pl.program_id(0) == 0)
    def _():
        s_ref[...] = jnp.zeros_like(s_ref)

    c = CHUNK
    nb = q_ref.shape[0]
    cosf = cos_ref[...]
    sinf = sin_ref[...]
    row = lax.broadcasted_iota(jnp.int32, (c, c), 0)
    col = lax.broadcasted_iota(jnp.int32, (c, c), 1)
    causal = row >= col
    diff = jnp.where(causal, (row - col).astype(F32), 0.0)
    rowf = row.astype(F32)
    rowf_v = lax.broadcasted_iota(jnp.int32, (c, RET_DV), 0).astype(F32)
    decay = [jnp.where(causal, jnp.exp(lg * diff), 0.0) for lg in _LOG_G]
    q_decay = [jnp.exp(lg * (rowf_v + 1.0)) for lg in _LOG_G]
    k_decay = [jnp.exp(lg * (c - 1.0 - rowf)) * (RET_DK ** -0.5) for lg in _LOG_G]

    def body(b, carry):
        for h in range(RET_H):
            sk = slice(h * RET_DK, (h + 1) * RET_DK)
            sv = slice(h * RET_DV, (h + 1) * RET_DV)
            q = q_ref[b, :, sk].astype(F32)
            k = k_ref[b, :, sk].astype(F32)
            qr = q * cosf + pltpu.roll(q, RET_DK // 2, 1) * sinf
            kr = k * cosf + pltpu.roll(k, RET_DK // 2, 1) * sinf
            v = v_ref[b, :, sv]
            qb = qr.astype(BF16)
            kb = (kr * (RET_DK ** -0.5)).astype(BF16)
            sc = _dot_nt(qb, kb) * decay[h]
            s_old = s_ref[b, h]
            y = _dot(sc.astype(BF16), v) + _dot(qb, s_old.astype(BF16)) * q_decay[h]
            kt = (kr * k_decay[h]).T.astype(BF16)
            s_ref[b, h] = s_old * math.exp(_LOG_G[h] * c) + _dot(kt, v)
            g = g_ref[b, :, sv].astype(F32)
            yn = _groupnorm(y, nw_ref[:, sv], True) * _silu(g)
            y_ref[b, :, sv] = yn.astype(y_ref.dtype)
        return carry

    lax.fori_loop(0, nb, body, 0)


def _ret_prefill(proj, cosf, sinf, nw, prev, *, layer, nb, seq):
    c = CHUNK
    p3 = proj.reshape(nb, seq, N_MAIN)
    al_args, al_specs, al_map = _state_io(prev, [1])
    return pl.pallas_call(
        _skip_alias(_ret_prefill_kernel, len(al_args)),
        grid=(seq // c,),
        in_specs=al_specs + [
            pl.BlockSpec((nb, c, 512), lambda t: (0, t, C_Q // 512)),
            pl.BlockSpec((nb, c, 512), lambda t: (0, t, C_K // 512)),
            pl.BlockSpec((nb, c, 1024), lambda t: (0, t, C_V // 1024)),
            pl.BlockSpec((nb, c, 1024), lambda t: (0, t, C_G // 1024)),
            pl.BlockSpec((c, RET_DK), lambda t: (t, 0)),
            pl.BlockSpec((c, RET_DK), lambda t: (t, 0)),
            pl.BlockSpec((1, 1024), lambda t: (0, 0))],
        out_specs=[pl.BlockSpec((nb, c, 1024), lambda t: (0, t, 0)),
                   pl.BlockSpec((nb, RET_H, RET_DK, RET_DV), lambda t: (layer, 0, 0, 0))],
        out_shape=[jax.ShapeDtypeStruct((nb, seq, 1024), BF16),
                   jax.ShapeDtypeStruct((DEPTH * nb, RET_H, RET_DK, RET_DV), F32)],
        input_output_aliases=al_map,
        compiler_params=_cparams(1),
        name="ret_prefill",
    )(*al_args, p3, p3, p3, p3, cosf, sinf, nw)


def _conv_silu(xbuf, b, cw_ref, cb_ref, cv_ref):
    c = CHUNK
    acc = cb_ref[...]
    for j in range(CONV_K):
        acc = acc + xbuf[b, 5 + j:5 + j + c, :] * cw_ref[j:j + 1, :]
    cv_ref[b] = xbuf[b, c + 5:c + 8, :]
    xbuf[b, 0:8, :] = xbuf[b, c:c + 8, :]
    return _silu(acc)


def _ml_prefill_kernel(u_ref, v_ref, o_ref, sm_ref, cw_ref, cb_ref, wq_ref, wk_ref, gb_ref, nw_ref,
                       y_ref, c_ref, n_ref, m_ref, cv_ref, xbuf):
    c = CHUNK
    nb = u_ref.shape[0]

    @pl.when(pl.program_id(0) == 0)
    def _():
        c_ref[...] = jnp.zeros_like(c_ref)
        n_ref[...] = jnp.zeros_like(n_ref)
        m_ref[...] = jnp.zeros_like(m_ref)
        xbuf[:, 0:8, :] = jnp.zeros((nb, 8, xbuf.shape[2]), F32)

    row = lax.broadcasted_iota(jnp.int32, (c, c), 0)
    col = lax.broadcasted_iota(jnp.int32, (c, c), 1)
    causal = row >= col
    tril_b = jnp.where(causal, 1.0, 0.0).astype(BF16)
    lane_m = lax.broadcasted_iota(jnp.int32, (1, 128), 1)

    def body(b, carry):
        xbuf[b, 8:8 + c, :] = u_ref[b].astype(F32)
        uc = _conv_silu(xbuf, b, cw_ref, cb_ref, cv_ref).astype(BF16)

        gates = sm_ref[b] + gb_ref[...]
        b_all = _cumsum_rows(_log_sigmoid(gates), tril_b)
        b_all_t = b_all.T
        gates_t = gates.T
        m_row = m_ref[b]
        m_row_new = m_row
        for h in range(ML_H):
            sl = slice(h * ML_DH, (h + 1) * ML_DH)
            b_col = b_all[:, L_FG + h:L_FG + h + 1]
            b_row = b_all_t[L_FG + h:L_FG + h + 1, :]
            it_col = gates[:, L_IG + h:L_IG + h + 1]
            it_row = gates_t[L_IG + h:L_IG + h + 1, :]
            m_prev = m_row[:, h:h + 1]
            qm = _dot(uc[:, sl], wq_ref[h])
            km = _dot(uc[:, sl], wk_ref[h]) * (ML_DH ** -0.5)
            v = v_ref[b, :, sl]
            qb = qm.astype(BF16)
            kb = km.astype(BF16)

            logw = jnp.where(causal, b_col - b_row + it_row, -jnp.inf)
            inter = b_col + m_prev
            m_t = jnp.maximum(inter, jnp.max(logw, axis=-1, keepdims=True))
            sc = _dot_nt(qb, kb) * jnp.exp(logw - m_t)
            w_int = jnp.exp(inter - m_t)
            c_old = c_ref[b, h]
            n_old = n_ref[b, h:h + 1, :]
            num = _dot(sc.astype(BF16), v) + _dot(qb, c_old.astype(BF16)) * w_int
            den = jnp.sum(sc, axis=-1, keepdims=True) + jnp.sum(qm * n_old, axis=-1, keepdims=True) * w_int
            den = jnp.maximum(jnp.abs(den), jnp.exp(-m_t))
            hh = num / den

            b_end = b_col[c - 1:c, :]
            logw_s = b_end - b_col + it_col
            m_new = jnp.maximum(b_end + m_prev, jnp.max(logw_s, axis=0, keepdims=True))
            w_s = jnp.exp(logw_s - m_new)
            w_prev = jnp.exp(b_end + m_prev - m_new)
            kw = km * w_s
            c_ref[b, h] = c_old * w_prev + _dot(kw.T.astype(BF16), v)
            n_ref[b, h:h + 1, :] = n_old * w_prev + jnp.sum(kw, axis=0, keepdims=True)
            m_row_new = jnp.where(lane_m == h, m_new, m_row_new)

            og = o_ref[b, :, sl].astype(F32)
            yn = _groupnorm(hh, nw_ref[:, sl], True) * _sigmoid(og)
            y_ref[b, :, sl] = yn.astype(y_ref.dtype)
        m_ref[b] = m_row_new
        return carry

    lax.fori_loop(0, nb, body, 0)


def _ml_prefill(proj, small, cw, cb, wq, wk, gb, nw, prev, *, layer, nb, seq):
    c = CHUNK
    p3 = proj.reshape(nb, seq, N_MAIN)
    s3 = small.reshape(nb, seq, N_SMALL)
    w = ML_H * ML_DH
    full2 = lambda t: (0, 0)
    al_args, al_specs, al_map = _state_io(prev, [1])
    return pl.pallas_call(
        _skip_alias(_ml_prefill_kernel, len(al_args)),
        grid=(seq // c,),
        in_specs=al_specs + [
            pl.BlockSpec((nb, c, w), lambda t: (0, t, C_U // w)),
            pl.BlockSpec((nb, c, w), lambda t: (0, t, C_VM // w)),
            pl.BlockSpec((nb, c, w), lambda t: (0, t, C_OM // w)),
            pl.BlockSpec((nb, c, N_SMALL), lambda t: (0, t, 0)),
            pl.BlockSpec((CONV_K, w), full2),
            pl.BlockSpec((1, w), full2),
            pl.BlockSpec((ML_H, ML_DH, ML_DH), lambda t: (0, 0, 0)),
            pl.BlockSpec((ML_H, ML_DH, ML_DH), lambda t: (0, 0, 0)),
            pl.BlockSpec((1, N_SMALL), full2),
            pl.BlockSpec((1, w), full2)],
        out_specs=[pl.BlockSpec((nb, c, w), lambda t: (0, t, 0)),
                   pl.BlockSpec((nb, ML_H, ML_DH, ML_DH), lambda t: (layer, 0, 0, 0)),
                   pl.BlockSpec((nb, ML_H, ML_DH), lambda t: (0, 0, 0)),
                   pl.BlockSpec((nb, 1, 128), lambda t: (0, 0, 0)),
                   pl.BlockSpec((nb, CONV_K - 1, w), lambda t: (0, 0, 0))],
        out_shape=[jax.ShapeDtypeStruct((nb, seq, w), BF16),
                   jax.ShapeDtypeStruct((DEPTH * nb, ML_H, ML_DH, ML_DH), F32),
                   jax.ShapeDtypeStruct((nb, ML_H, ML_DH), F32),
                   jax.ShapeDtypeStruct((nb, 1, 128), F32),
                   jax.ShapeDtypeStruct((nb, CONV_K - 1, w), F32)],
        scratch_shapes=[pltpu.VMEM((nb, c + 8, w), F32)],
        input_output_aliases=al_map,
        compiler_params=_cparams(1),
        name="ml_prefill",
    )(*al_args, p3, p3, p3, s3, cw, cb, wq, wk, gb, nw)


def _block_bcast(tile, lanes):
    c = tile.shape[0]
    lane = lax.broadcasted_iota(jnp.int32, (c, 128), 1)
    cols = [jnp.broadcast_to(tile[:, l:l + 1], (c, 128)) for l in lanes]
    left = jnp.where(lane < SSM_P, cols[0], cols[1])
    right = jnp.where(lane < SSM_P, cols[2], cols[3])
    return jnp.concatenate([left, right], axis=1)


def _ssd_prefill_kernel(z_ref, xs_ref, bc_ref, sm_ref, cw_ref, cb_ref, dtb_ref, alog_ref, dv_ref, nw_ref,
                        y_ref, s_ref, cv_ref, xbuf):
    c = CHUNK
    nb = z_ref.shape[0]
    wx = SSM_H * SSM_P
    gw = SSM_R * SSM_P

    @pl.when(pl.program_id(0) == 0)
    def _():
        s_ref[...] = jnp.zeros_like(s_ref)
        xbuf[:, 0:8, :] = jnp.zeros((nb, 8, xbuf.shape[2]), F32)

    row = lax.broadcasted_iota(jnp.int32, (c, c), 0)
    col = lax.broadcasted_iota(jnp.int32, (c, c), 1)
    causal = row >= col
    tril_b = jnp.where(causal, 1.0, 0.0).astype(BF16)
    lane_blk = lax.shift_right_logical(lax.broadcasted_iota(jnp.int32, (c, gw), 1), 6)
    a_row = -jnp.exp(alog_ref[...])

    def body(b, carry):
        xbuf[b, 8:8 + c, 0:wx] = xs_ref[b].astype(F32)
        xbuf[b, 8:8 + c, wx:2 * wx] = bc_ref[b].astype(F32)
        xc = _conv_silu(xbuf, b, cw_ref, cb_ref, cv_ref)

        delta = _softplus(sm_ref[b] + dtb_ref[...])
        cum = _cumsum_rows(delta * a_row, tril_b)
        cum_t = cum.T
        delta_t = delta.T
        ecum = jnp.exp(cum)
        cum_end = cum[c - 1:c, :]
        w_state = jnp.exp(cum_end - cum) * delta
        dec_row = jnp.exp(cum_end)

        for g in range(SSM_G):
            gs = slice(g * gw, (g + 1) * gw)
            bg = xc[:, wx + g * SSM_N:wx + (g + 1) * SSM_N].astype(BF16)
            cg = xc[:, wx + (SSM_G + g) * SSM_N:wx + (SSM_G + g + 1) * SSM_N].astype(BF16)
            xg = xc[:, gs]
            cb_mat = _dot_nt(cg, bg)
            lanes = [L_DT + g * SSM_R + r for r in range(SSM_R)]
            m_parts = []
            x_parts = []
            for r in range(SSM_R):
                ln = lanes[r]
                seg = jnp.exp(jnp.where(causal, cum[:, ln:ln + 1] - cum_t[ln:ln + 1, :], -jnp.inf))
                m_parts.append((cb_mat * seg * delta_t[ln:ln + 1, :]).astype(BF16))
                x_parts.append(jnp.where(lane_blk == r, xg, 0.0).astype(BF16))
            y = _dot(jnp.concatenate(m_parts, axis=1), jnp.concatenate(x_parts, axis=0))
            s_old = s_ref[b, g * SSM_R:(g + 1) * SSM_R].reshape(gw, SSM_N)
            y = y + _dot_nt(cg, s_old.astype(BF16)) * _block_bcast(ecum, lanes)
            xw = xg * _block_bcast(w_state, lanes)
            upd = _dot(xw.T.astype(BF16), bg)
            for r in range(SSM_R):
                ln = lanes[r]
                s_ref[b, g * SSM_R + r] = (s_old[r * SSM_P:(r + 1) * SSM_P] * dec_row[:, ln:ln + 1]
                                           + upd[r * SSM_P:(r + 1) * SSM_P])
            y = y + dv_ref[:, gs] * xg
            z = z_ref[b, :, gs].astype(F32)
            yn = _groupnorm(y * _silu(z), nw_ref[:, gs], False)
            y_ref[b, :, gs] = yn.astype(y_ref.dtype)
        return carry

    lax.fori_loop(0, nb, body, 0)


def _ssd_prefill(proj, small, cw, cb, dtb, alog, dvec, nw, prev, *, layer, nb, seq):
    c = CHUNK
    wx = SSM_H * SSM_P
    p3 = proj.reshape(nb, seq, N_MAIN)
    s3 = small.reshape(nb, seq, N_SMALL)
    full2 = lambda t: (0, 0)
    al_args, al_specs, al_map = _state_io(prev, [1])
    return pl.pallas_call(
        _skip_alias(_ssd_prefill_kernel, len(al_args)),
        grid=(seq // c,),
        in_specs=al_specs + [
            pl.BlockSpec((nb, c, wx), lambda t: (0, t, C_Z // wx)),
            pl.BlockSpec((nb, c, wx), lambda t: (0, t, C_XS // wx)),
            pl.BlockSpec((nb, c, wx), lambda t: (0, t, C_B // wx)),
            pl.BlockSpec((nb, c, N_SMALL), lambda t: (0, t, 0)),
            pl.BlockSpec((CONV_K, 2 * wx), full2),
            pl.BlockSpec((1, 2 * wx), full2),
            pl.BlockSpec((1, N_SMALL), full2),
            pl.BlockSpec((1, N_SMALL), full2),
            pl.BlockSpec((1, wx), full2),
            pl.BlockSpec((1, wx), full2)],
        out_specs=[pl.BlockSpec((nb, c, wx), lambda t: (0, t, 0)),
                   pl.BlockSpec((nb, SSM_H, SSM_P, SSM_N), lambda t: (layer, 0, 0, 0)),
                   pl.BlockSpec((nb, CONV_K - 1, 2 * wx), lambda t: (0, 0, 0))],
        out_shape=[jax.ShapeDtypeStruct((nb, seq, wx), BF16),
                   jax.ShapeDtypeStruct((DEPTH * nb, SSM_H, SSM_P, SSM_N), F32),
                   jax.ShapeDtypeStruct((nb, CONV_K - 1, 2 * wx), F32)],
        scratch_shapes=[pltpu.VMEM((nb, c + 8, 2 * wx), F32)],
        input_output_aliases=al_map,
        compiler_params=_cparams(1),
        name="ssd_prefill",
    )(*al_args, p3, p3, p3, s3, cw, cb, dtb, alog, dvec, nw)


def _merge_kernel(yr_ref, ym_ref, ys_ref, gr_ref, gm_ref, gs_ref, x_ref, wr_ref, wm_ref, ws_ref,
                  wo_ref, nw_ref, wq_ref, xo_ref, qo_ref):
    def branch(y_ref, g_ref, w_ref):
        return _sigmoid(g_ref[...].astype(F32)) * _dot(y_ref[...].astype(BF16), w_ref[...])

    merged = branch(yr_ref, gr_ref, wr_ref) + branch(ym_ref, gm_ref, wm_ref) + branch(ys_ref, gs_ref, ws_ref)
    xn = x_ref[...] + _dot(merged.astype(BF16), wo_ref[...])
    xo_ref[...] = xn
    hq = _rmsnorm(xn, nw_ref[...]).astype(BF16)
    qo_ref[...] = _dot(hq, wq_ref[...]).astype(qo_ref.dtype)


def _merge(yr, ym, ys, proj, x, wr, wm, ws, wo, nw, wq, *, tm, q_dtype):
    m = x.shape[0]
    d = D_MODEL
    row = lambda i: (i, 0)
    wspec = pl.BlockSpec((d, d), lambda i: (0, 0))
    return pl.pallas_call(
        _merge_kernel,
        grid=(m // tm,),
        in_specs=[pl.BlockSpec((tm, d), row), pl.BlockSpec((tm, d), row), pl.BlockSpec((tm, d), row),
                  pl.BlockSpec((tm, d), lambda i: (i, C_GR // d)),
                  pl.BlockSpec((tm, d), lambda i: (i, C_GM // d)),
                  pl.BlockSpec((tm, d), lambda i: (i, C_GS // d)),
                  pl.BlockSpec((tm, d), row),
                  wspec, wspec, wspec, wspec,
                  pl.BlockSpec((1, d), lambda i: (0, 0)),
                  wspec],
        out_specs=[pl.BlockSpec((tm, d), row), pl.BlockSpec((tm, d), row)],
        out_shape=[jax.ShapeDtypeStruct((m, d), F32), jax.ShapeDtypeStruct((m, d), q_dtype)],
        compiler_params=_cparams(1),
        name="merge",
    )(yr, ym, ys, proj, proj, proj, x, wr, wm, ws, wo, nw, wq)


def _attn_prefill_kernel(q_ref, k_ref, v_ref, x_ref, wo_ref, o_ref):
    outs = []
    for h in range(MEM_H):
        sl = slice(h * MEM_DH, (h + 1) * MEM_DH)
        q = q_ref[0, :, sl]
        k = k_ref[0, :, sl].astype(BF16)
        v = v_ref[0, :, sl].astype(BF16)
        s = _dot_nt(q, k) * (MEM_DH ** -0.5)
        p = jnp.exp(s - jnp.max(s, axis=-1, keepdims=True))
        p = p / jnp.sum(p, axis=-1, keepdims=True)
        outs.append(_dot(p.astype(BF16), v).astype(BF16))
    o_all = jnp.concatenate(outs, axis=1)
    o_ref[0] = x_ref[0] + _dot(o_all, wo_ref[...])


def _attn_prefill(q, mk, mv, x, wo, *, layer, nb, seq, tq):
    d = D_MODEL
    q3 = q.reshape(nb, seq, d)
    x3 = x.reshape(nb, seq, d)
    out = pl.pallas_call(
        _attn_prefill_kernel,
        grid=(nb, seq // tq),
        in_specs=[pl.BlockSpec((1, tq, d), lambda b, t: (b, t, 0)),
                  pl.BlockSpec((1, MEM_LEN, d), lambda b, t: (layer * nb + b, 0, 0)),
                  pl.BlockSpec((1, MEM_LEN, d), lambda b, t: (layer * nb + b, 0, 0)),
                  pl.BlockSpec((1, tq, d), lambda b, t: (b, t, 0)),
                  pl.BlockSpec((d, d), lambda b, t: (0, 0))],
        out_specs=pl.BlockSpec((1, tq, d), lambda b, t: (b, t, 0)),
        out_shape=jax.ShapeDtypeStruct((nb, seq, d), F32),
        compiler_params=_cparams(2),
        name="attn_prefill",
    )(q3, mk, mv, x3, wo)
    return out.reshape(nb * seq, d)


def _attn_decode_kernel(q_ref, k_ref, v_ref, o_ref):
    for j in range(q_ref.shape[0]):
        s = jnp.sum(k_ref[j] * q_ref[j][None], axis=-1, keepdims=True) * (MEM_DH ** -0.5)
        p = jnp.exp(s - jnp.max(s, axis=0, keepdims=True))
        p = p / jnp.sum(p, axis=0, keepdims=True)
        o_ref[j] = jnp.sum(p * v_ref[j], axis=0)


def _attn_decode(q, ck, cv, *, layer, bb):
    nb = q.shape[0]
    nblk = nb // bb
    blk = (bb, MEM_LEN, MEM_H, MEM_DH)
    return pl.pallas_call(
        _attn_decode_kernel,
        grid=(nblk,),
        in_specs=[pl.BlockSpec((bb, MEM_H, MEM_DH), lambda i: (i, 0, 0)),
                  pl.BlockSpec(blk, lambda i: (layer * nblk + i, 0, 0, 0)),
                  pl.BlockSpec(blk, lambda i: (layer * nblk + i, 0, 0, 0))],
        out_specs=pl.BlockSpec((bb, MEM_H, MEM_DH), lambda i: (i, 0, 0)),
        out_shape=jax.ShapeDtypeStruct((nb, MEM_H, MEM_DH), F32),
        compiler_params=_cparams(1),
        name="attn_decode",
    )(q, ck, cv)


def _mlp_kernel(x_ref, nw_ref, w1_ref, w2_ref, nf_ref, o_ref, xn_ref, acc_ref, *, final_norm):
    j = pl.program_id(1)

    @pl.when(j == 0)
    def _():
        xn_ref[...] = _rmsnorm(x_ref[...], nw_ref[...]).astype(BF16)
        acc_ref[...] = jnp.zeros_like(acc_ref)

    hid = jnp.square(jnp.maximum(_dot(xn_ref[...], w1_ref[...]), 0.0))
    acc_ref[...] += _dot(hid.astype(BF16), w2_ref[...])

    @pl.when(j == pl.num_programs(1) - 1)
    def _():
        y = x_ref[...] + acc_ref[...]
        if final_norm:
            y = _rmsnorm(y, nf_ref[...])
        o_ref[...] = y


def _mlp(x, nw, w1, w2, nf, *, tm, tf, final_norm):
    m = x.shape[0]
    d = D_MODEL
    return pl.pallas_call(
        functools.partial(_mlp_kernel, final_norm=final_norm),
        grid=(m // tm, D_FF // tf),
        in_specs=[pl.BlockSpec((tm, d), lambda i, j: (i, 0)),
                  pl.BlockSpec((1, d), lambda i, j: (0, 0)),
                  pl.BlockSpec((d, tf), lambda i, j: (0, j)),
                  pl.BlockSpec((tf, d), lambda i, j: (j, 0)),
                  pl.BlockSpec((1, d), lambda i, j: (0, 0))],
        out_specs=pl.BlockSpec((tm, d), lambda i, j: (i, 0)),
        out_shape=jax.ShapeDtypeStruct((m, d), F32),
        scratch_shapes=[pltpu.VMEM((tm, d), BF16), pltpu.VMEM((tm, d), F32)],
        compiler_params=_cparams(2),
        name="mlp",
    )(x, nw, w1, w2, nf)


DEC_BB = 8


def _row_mask(x, j):
    rows = lax.broadcasted_iota(jnp.int32, x.shape, 0)
    return jnp.where(rows == j, x, 0.0).astype(BF16)


def _blockdiag_rows(v):
    n = v.shape[0]
    tiled = jnp.concatenate([v] * n, axis=1)
    rows = lax.broadcasted_iota(jnp.int32, tiled.shape, 0)
    blk = lax.shift_right_logical(lax.broadcasted_iota(jnp.int32, tiled.shape, 1), 8)
    return _pad_rows(jnp.where(rows == blk, tiled, 0.0), 128).astype(BF16)


def _ret_decode_kernel(q_ref, k_ref, v_ref, g_ref, cos_ref, sin_ref, nw_ref, s_ref, y_ref, so_ref, ybuf):
    cosf = cos_ref[...]
    sinf = sin_ref[...]
    for h in range(RET_H):
        sk = slice(h * RET_DK, (h + 1) * RET_DK)
        sv = slice(h * RET_DV, (h + 1) * RET_DV)
        g_dec = math.exp(_LOG_G[h])
        q = q_ref[:, sk]
        k = k_ref[:, sk]
        qr = q * cosf + pltpu.roll(q, RET_DK // 2, 1) * sinf
        kr = (k * cosf + pltpu.roll(k, RET_DK // 2, 1) * sinf) * (RET_DK ** -0.5)
        k_t = _pad_rows_t(kr).astype(BF16)
        q16 = _pad_rows(qr, 16).astype(BF16)
        upd = _dot(k_t, _blockdiag_rows(v_ref[:, sv]))
        s_wide = []
        for j in range(DEC_BB):
            s_new = s_ref[j, h] * g_dec + upd[:, j * RET_DV:(j + 1) * RET_DV]
            so_ref[j, h] = s_new
            s_wide.append(s_new.astype(BF16))
        y_all = _dot(q16, jnp.concatenate(s_wide, axis=1))
        for j in range(DEC_BB):
            ybuf[j:j + 1, sv] = y_all[j:j + 1, j * RET_DV:(j + 1) * RET_DV]
        y_ref[:, sv] = _groupnorm(ybuf[:, sv], nw_ref[:, sv], True) * _silu(g_ref[:, sv])


def _ret_decode(proj, cosf, sinf, nw, state, prev, *, layer):
    nb = proj.shape[0]
    bb = DEC_BB
    nblk = nb // bb
    al_args, al_specs, al_map = _state_io(prev, [1])
    sblk = (bb, RET_H, RET_DK, RET_DV)
    return pl.pallas_call(
        _skip_alias(_ret_decode_kernel, len(al_args)),
        grid=(nblk,),
        in_specs=al_specs + [
            pl.BlockSpec((bb, 512), lambda i: (i, C_Q // 512)),
            pl.BlockSpec((bb, 512), lambda i: (i, C_K // 512)),
            pl.BlockSpec((bb, 1024), lambda i: (i, C_V // 1024)),
            pl.BlockSpec((bb, 1024), lambda i: (i, C_G // 1024)),
            pl.BlockSpec((1, RET_DK), lambda i: (0, 0)),
            pl.BlockSpec((1, RET_DK), lambda i: (0, 0)),
            pl.BlockSpec((1, 1024), lambda i: (0, 0)),
            pl.BlockSpec(sblk, lambda i: (layer * nblk + i, 0, 0, 0))],
        out_specs=[pl.BlockSpec((bb, 1024), lambda i: (i, 0)),
                   pl.BlockSpec(sblk, lambda i: (layer * nblk + i, 0, 0, 0))],
        out_shape=[jax.ShapeDtypeStruct((nb, RET_H * RET_DV), F32),
                   jax.ShapeDtypeStruct((DEPTH * nb, RET_H, RET_DK, RET_DV), F32)],
        scratch_shapes=[pltpu.VMEM((bb, RET_H * RET_DV), F32)],
        input_output_aliases=al_map,
        compiler_params=_cparams(1),
        name="ret_decode",
    )(*al_args, proj, proj, proj, proj, cosf, sinf, nw, state)


def _conv_step(x, buf_ref, cw_ref, cb_ref, bo_ref):
    acc = cb_ref[...]
    for j in range(CONV_K - 1):
        acc = acc + buf_ref[j] * cw_ref[j:j + 1, :]
    acc = acc + x * cw_ref[CONV_K - 1:CONV_K, :]
    bo_ref[0] = buf_ref[1]
    bo_ref[1] = buf_ref[2]
    bo_ref[2] = x
    return _silu(acc)


def _ml_decode_kernel(u_ref, v_ref, o_ref, sm_ref, buf_ref, cw_ref, cb_ref, wq_ref, wk_ref, gb_ref, nw_ref,
                      c_ref, n_ref, m_ref,
                      y_ref, co_ref, no_ref, mo_ref, bo_ref, ybuf):
    uc = _conv_step(u_ref[...], buf_ref, cw_ref, cb_ref, bo_ref).astype(BF16)
    gates = sm_ref[...] + gb_ref[...]
    lane4 = lax.broadcasted_iota(jnp.int32, m_ref.shape, 1)
    m_all = m_ref[...]
    m_out = m_all
    for h in range(ML_H):
        sl = slice(h * ML_DH, (h + 1) * ML_DH)
        qm = _dot(uc[:, sl], wq_ref[h])
        km = _dot(uc[:, sl], wk_ref[h]) * (ML_DH ** -0.5)
        i_pre = gates[:, L_IG + h:L_IG + h + 1]
        log_f = _log_sigmoid(gates[:, L_FG + h:L_FG + h + 1])
        m_prev = m_all[:, h:h + 1]
        inter = log_f + m_prev
        m_new = jnp.maximum(inter, i_pre)
        w_s = jnp.exp(i_pre - m_new)
        w_prev = jnp.exp(inter - m_new)
        kw = km * w_s
        n_new = n_ref[:, sl] * w_prev + kw
        no_ref[:, sl] = n_new
        m_out = jnp.where(lane4 == h, m_new, m_out)

        k_t = _pad_rows_t(kw).astype(BF16)
        q16 = _pad_rows(qm, 16).astype(BF16)
        upd = _dot(k_t, _blockdiag_rows(v_ref[:, sl]))
        c_wide = []
        for j in range(DEC_BB):
            c_new = c_ref[j, h] * w_prev[j:j + 1, :] + upd[:, j * ML_DH:(j + 1) * ML_DH]
            co_ref[j, h] = c_new
            c_wide.append(c_new.astype(BF16))
        y_all = _dot(q16, jnp.concatenate(c_wide, axis=1))
        for j in range(DEC_BB):
            ybuf[j:j + 1, sl] = y_all[j:j + 1, j * ML_DH:(j + 1) * ML_DH]
        den = jnp.sum(qm * n_new, axis=-1, keepdims=True)
        den = jnp.maximum(jnp.abs(den), jnp.exp(-m_new))
        y_ref[:, sl] = _groupnorm(ybuf[:, sl] / den, nw_ref[:, sl], True) * _sigmoid(o_ref[:, sl])
    mo_ref[...] = m_out


def _ml_decode(proj, small, buf, cw, cb, wq, wk, gb, nw, c_state, n_state, m_state, prev, *, layer):
    nb = proj.shape[0]
    bb = DEC_BB
    nblk = nb // bb
    dh = ML_DH
    w = ML_H * ML_DH
    full2 = lambda i: (0, 0)
    al_args, al_specs, al_map = _state_io(prev, [1])
    cblk = (bb, ML_H, dh, dh)
    return pl.pallas_call(
        _skip_alias(_ml_decode_kernel, len(al_args)),
        grid=(nblk,),
        in_specs=al_specs + [
            pl.BlockSpec((bb, w), lambda i: (i, C_U // w)),
            pl.BlockSpec((bb, w), lambda i: (i, C_VM // w)),
            pl.BlockSpec((bb, w), lambda i: (i, C_OM // w)),
            pl.BlockSpec((bb, N_SMALL), lambda i: (i, 0)),
            pl.BlockSpec((CONV_K - 1, bb, w), lambda i: (0, i, 0)),
            pl.BlockSpec((CONV_K, w), full2),
            pl.BlockSpec((1, w), full2),
            pl.BlockSpec((ML_H, dh, dh), lambda i: (0, 0, 0)),
            pl.BlockSpec((ML_H, dh, dh), lambda i: (0, 0, 0)),
            pl.BlockSpec((1, N_SMALL), full2),
            pl.BlockSpec((1, w), full2),
            pl.BlockSpec(cblk, lambda i: (layer * nblk + i, 0, 0, 0)),
            pl.BlockSpec((bb, w), lambda i: (layer * nblk + i, 0)),
            pl.BlockSpec((bb, ML_H), lambda i: (layer * nblk + i, 0))],
        out_specs=[pl.BlockSpec((bb, w), lambda i: (i, 0)),
                   pl.BlockSpec(cblk, lambda i: (layer * nblk + i, 0, 0, 0)),
                   pl.BlockSpec((bb, w), lambda i: (i, 0)),
                   pl.BlockSpec((bb, ML_H), lambda i: (i, 0)),
                   pl.BlockSpec((CONV_K - 1, bb, w), lambda i: (0, i, 0))],
        out_shape=[jax.ShapeDtypeStruct((nb, w), F32),
                   jax.ShapeDtypeStruct((DEPTH * nb, ML_H, dh, dh), F32),
                   jax.ShapeDtypeStruct((nb, w), F32),
                   jax.ShapeDtypeStruct((nb, ML_H), F32),
                   jax.ShapeDtypeStruct((CONV_K - 1, nb, w), F32)],
        scratch_shapes=[pltpu.VMEM((bb, w), F32)],
        input_output_aliases=al_map,
        compiler_params=_cparams(1),
        name="ml_decode",
    )(*al_args, proj, proj, proj, small, buf, cw, cb, wq, wk, gb, nw, c_state, n_state, m_state)


def _ssd_decode_kernel(z_ref, xs_ref, bc_ref, sm_ref, buf_ref, cw_ref, cb_ref, dtb_ref, alog_ref, dv_ref, nw_ref,
                       s_ref, y_ref, so_ref, bo_ref, ybuf):
    wx = SSM_H * SSM_P
    gw = SSM_R * SSM_P
    x_in = jnp.concatenate([xs_ref[...], bc_ref[...]], axis=1)
    xc = _conv_step(x_in, buf_ref, cw_ref, cb_ref, bo_ref)
    xs = xc[:, 0:wx]

    delta = _softplus(sm_ref[...] + dtb_ref[...])
    d_a = jnp.exp(delta * (-jnp.exp(alog_ref[...])))
    lane_blk = lax.shift_right_logical(lax.broadcasted_iota(jnp.int32, (DEC_BB, wx), 1), 6)
    dt_full = jnp.zeros((DEC_BB, wx), F32)
    for hh in range(SSM_H):
        dt_full = jnp.where(lane_blk == hh, delta[:, L_DT + hh:L_DT + hh + 1], dt_full)
    x_t = _pad_rows_t(xs * dt_full).astype(BF16)

    for g in range(SSM_G):
        gs = slice(g * gw, (g + 1) * gw)
        b_pad = _pad_rows(xc[:, wx + g * SSM_N:wx + (g + 1) * SSM_N], 128)
        c16 = _pad_rows(xc[:, wx + (SSM_G + g) * SSM_N:wx + (SSM_G + g + 1) * SSM_N], 16).astype(BF16)
        for j in range(DEC_BB):
            upd = _dot(x_t[gs, :], _row_mask(b_pad, j))
            parts = []
            for r in range(SSM_R):
                hh = g * SSM_R + r
                s_new = (s_ref[j, hh] * d_a[j:j + 1, L_DT + hh:L_DT + hh + 1]
                         + upd[r * SSM_P:(r + 1) * SSM_P])
                so_ref[j, hh] = s_new
                parts.append(s_new.astype(BF16))
            ybuf[j:j + 1, gs] = _dot_nt(c16, jnp.concatenate(parts, axis=0))[j:j + 1, :]
        y = ybuf[:, gs] + dv_ref[:, gs] * xs[:, gs]
        y_ref[:, gs] = _groupnorm(y * _silu(z_ref[:, gs]), nw_ref[:, gs], False)


def _ssd_decode(proj, small, buf, cw, cb, dtb, alog, dvec, nw, state, prev, *, layer):
    nb = proj.shape[0]
    bb = DEC_BB
    nblk = nb // bb
    wx = SSM_H * SSM_P
    full2 = lambda i: (0, 0)
    al_args, al_specs, al_map = _state_io(prev, [1])
    sblk = (bb, SSM_H, SSM_P, SSM_N)
    return pl.pallas_call(
        _skip_alias(_ssd_decode_kernel, len(al_args)),
        grid=(nblk,),
        in_specs=al_specs + [
            pl.BlockSpec((bb, wx), lambda i: (i, C_Z // wx)),
            pl.BlockSpec((bb, wx), lambda i: (i, C_XS // wx)),
            pl.BlockSpec((bb, wx), lambda i: (i, C_B // wx)),
            pl.BlockSpec((bb, N_SMALL), lambda i: (i, 0)),
            pl.BlockSpec((CONV_K - 1, bb, 2 * wx), lambda i: (0, i, 0)),
            pl.BlockSpec((CONV_K, 2 * wx), full2),
            pl.BlockSpec((1, 2 * wx), full2),
            pl.BlockSpec((1, N_SMALL), full2),
            pl.BlockSpec((1, N_SMALL), full2),
            pl.BlockSpec((1, wx), full2),
            pl.BlockSpec((1, wx), full2),
            pl.BlockSpec(sblk, lambda i: (layer * nblk + i, 0, 0, 0))],
        out_specs=[pl.BlockSpec((bb, wx), lambda i: (i, 0)),
                   pl.BlockSpec(sblk, lambda i: (layer * nblk + i, 0, 0, 0)),
                   pl.BlockSpec((CONV_K - 1, bb, 2 * wx), lambda i: (0, i, 0))],
        out_shape=[jax.ShapeDtypeStruct((nb, wx), F32),
                   jax.ShapeDtypeStruct((DEPTH * nb, SSM_H, SSM_P, SSM_N), F32),
                   jax.ShapeDtypeStruct((CONV_K - 1, nb, 2 * wx), F32)],
        scratch_shapes=[pltpu.VMEM((bb, wx), F32)],
        input_output_aliases=al_map,
        compiler_params=_cparams(1),
        name="ssd_decode",
    )(*al_args, proj, proj, proj, small, buf, cw, cb, dtb, alog, dvec, nw, state)


def _rope_tables(pos):
    half = RET_DK // 2
    freqs = ROPE_THETA ** (-jnp.arange(half, dtype=F32) / half)
    ang = pos.astype(F32)[:, None] * freqs[None, :]
    cos = jnp.cos(ang)
    sin = jnp.sin(ang)
    return jnp.concatenate([cos, cos], axis=1), jnp.concatenate([-sin, sin], axis=1)


def _pad_lanes(v, offset):
    return jnp.zeros((1, N_SMALL), F32).at[0, offset:offset + v.shape[0]].set(v.astype(F32))


def _layer_weights(l, W):
    w_in = W['w_in'][l]
    n_ml0 = 2 * 512 + 2 * 1024
    n_ig = n_ml0 + 3 * 1024
    n_ssm0 = n_ig + 2 * ML_H
    n_dt = n_ssm0 + 1024 + 2048
    n_gate = n_dt + SSM_H
    w_main = jnp.concatenate([w_in[:, :n_ig], w_in[:, n_ssm0:n_dt], w_in[:, n_gate:]], axis=1).astype(BF16)
    w_small = jnp.concatenate([w_in[:, n_ig:n_ssm0], w_in[:, n_dt:n_gate],
                               jnp.zeros((D_MODEL, N_SMALL - 2 * ML_H - SSM_H), F32)], axis=1).astype(BF16)
    row = lambda v: v.reshape(1, -1).astype(F32)
    return dict(
        w_main=w_main, w_small=w_small,
        norm_mix=row(W['norm_mix_w'][l]),
        ret_norm=row(W['ret_norm_w'][l]),
        ml_cw=W['ml_conv_w'][l], ml_cb=row(W['ml_conv_b'][l]),
        ml_wq=W['ml_wq'][l].astype(BF16), ml_wk=W['ml_wk'][l].astype(BF16),
        ml_gb=_pad_lanes(W['ml_gate_b'][l], L_IG),
        ml_norm=row(W['ml_norm_w'][l]),
        ssm_cw=W['ssm_conv_w'][l], ssm_cb=row(W['ssm_conv_b'][l]),
        ssm_dtb=_pad_lanes(W['ssm_dt_bias'][l], L_DT),
        ssm_alog=_pad_lanes(W['ssm_A_log'][l], L_DT),
        ssm_dvec=row(jnp.repeat(W['ssm_D'][l], SSM_P)),
        ssm_norm=row(W['ssm_norm_w'][l]),
        w_br_ret=W['w_br_ret'][l].astype(BF16), w_br_ml=W['w_br_ml'][l].astype(BF16),
        w_br_ssm=W['w_br_ssm'][l].astype(BF16), w_out=W['w_out_mix'][l].astype(BF16),
        norm_mem=row(W['norm_mem_w'][l]),
        mem_wq=W['mem_wq'][l].astype(BF16), mem_wo=W['mem_wo'][l].astype(BF16),
        norm_mlp=row(W['norm_mlp_w'][l]),
        mlp_w1=W['mlp_w1'][l].astype(BF16), mlp_w2=W['mlp_w2'][l].astype(BF16),
    )


def kernel(x_prompt, x_sample, mem_prompt, state_ret, state_mlstm_C, state_mlstm_n, state_mlstm_m,
           state_mlstm_conv, state_ssm, state_ssm_conv, cache_mem_k, cache_mem_v,
           norm_mix_w, w_in, ret_norm_w, ml_conv_w, ml_conv_b, ml_wq, ml_wk, ml_gate_b, ml_norm_w,
           ssm_conv_w, ssm_conv_b, ssm_dt_bias, ssm_A_log, ssm_D, ssm_norm_w,
           w_br_ret, w_br_ml, w_br_ssm, w_out_mix, norm_mem_w, mem_wq, mem_wk, mem_wv, mem_wo,
           norm_mlp_w, mlp_w1, mlp_w2, norm_f_w):
    W = dict(norm_mix_w=norm_mix_w, w_in=w_in, ret_norm_w=ret_norm_w, ml_conv_w=ml_conv_w,
             ml_conv_b=ml_conv_b, ml_wq=ml_wq, ml_wk=ml_wk, ml_gate_b=ml_gate_b, ml_norm_w=ml_norm_w,
             ssm_conv_w=ssm_conv_w, ssm_conv_b=ssm_conv_b, ssm_dt_bias=ssm_dt_bias, ssm_A_log=ssm_A_log,
             ssm_D=ssm_D, ssm_norm_w=ssm_norm_w, w_br_ret=w_br_ret, w_br_ml=w_br_ml, w_br_ssm=w_br_ssm,
             w_out_mix=w_out_mix, norm_mem_w=norm_mem_w, mem_wq=mem_wq, mem_wo=mem_wo,
             norm_mlp_w=norm_mlp_w, mlp_w1=mlp_w1, mlp_w2=mlp_w2)
    LW = [_layer_weights(l, W) for l in range(DEPTH)]
    norm_f = norm_f_w.reshape(1, -1).astype(F32)
    d = D_MODEL

    nb, seq = x_prompt.shape[0], x_prompt.shape[1]
    cos_p, sin_p = _rope_tables(jnp.arange(seq, dtype=jnp.int32))
    memk, memv = _memkv(mem_prompt.reshape(nb * MEM_LEN, d), mem_wk.astype(BF16), mem_wv.astype(BF16), tm=1024)
    memk3 = memk.reshape(DEPTH * nb, MEM_LEN, d)
    memv3 = memv.reshape(DEPTH * nb, MEM_LEN, d)

    x = x_prompt.reshape(nb * seq, d)
    ret_p = mlc_p = ssm_p = None
    small_p = []
    for l in range(DEPTH):
        lw = LW[l]
        proj, small = _inproj(x, lw['norm_mix'], lw['w_main'], lw['w_small'], tm=1024, tn=1024, out_dtype=BF16)
        y_r, ret_p = _ret_prefill(proj, cos_p, sin_p, lw['ret_norm'], None if l == 0 else [ret_p],
                                  layer=l, nb=nb, seq=seq)
        y_m, mlc_p, ml_n, ml_m, ml_cv = _ml_prefill(proj, small, lw['ml_cw'], lw['ml_cb'], lw['ml_wq'], lw['ml_wk'],
                                                    lw['ml_gb'], lw['ml_norm'], None if l == 0 else [mlc_p],
                                                    layer=l, nb=nb, seq=seq)
        y_s, ssm_p, ssm_cv = _ssd_prefill(proj, small, lw['ssm_cw'], lw['ssm_cb'], lw['ssm_dtb'], lw['ssm_alog'],
                                          lw['ssm_dvec'], lw['ssm_norm'], None if l == 0 else [ssm_p],
                                          layer=l, nb=nb, seq=seq)
        x, qm = _merge(y_r.reshape(nb * seq, d), y_m.reshape(nb * seq, d), y_s.reshape(nb * seq, d), proj, x,
                       lw['w_br_ret'], lw['w_br_ml'], lw['w_br_ssm'], lw['w_out'], lw['norm_mem'], lw['mem_wq'],
                       tm=256, q_dtype=BF16)
        x = _attn_prefill(qm, memk3, memv3, x, lw['mem_wo'], layer=l, nb=nb, seq=seq, tq=512)
        x = _mlp(x, lw['norm_mlp'], lw['mlp_w1'], lw['mlp_w2'], norm_f, tm=1024, tf=1024,
                 final_norm=(l == DEPTH - 1))
        small_p.append((ml_n, ml_m[:, 0, :ML_H], ml_cv, ssm_cv))
    y_prompt = x.reshape(nb, seq, d)
    stack = lambda parts, i: jnp.stack([p[i] for p in parts])
    prompt_states = (ret_p.reshape(DEPTH, nb, RET_H, RET_DK, RET_DV),
                     mlc_p.reshape(DEPTH, nb, ML_H, ML_DH, ML_DH),
                     stack(small_p, 0), stack(small_p, 1), stack(small_p, 2),
                     ssm_p.reshape(DEPTH, nb, SSM_H, SSM_P, SSM_N),
                     stack(small_p, 3))
    memk_p = memk.reshape(DEPTH, nb, MEM_LEN, MEM_H, MEM_DH)
    memv_p = memv.reshape(DEPTH, nb, MEM_LEN, MEM_H, MEM_DH)

    ns = x_sample.shape[0]
    cos_s, sin_s = _rope_tables(PAST_LEN + jnp.arange(1, dtype=jnp.int32))
    ret_state = state_ret.reshape(DEPTH * ns, RET_H, RET_DK, RET_DV)
    mlc_state = state_mlstm_C.reshape(DEPTH * ns, ML_H, ML_DH, ML_DH)
    mln_state = state_mlstm_n.reshape(DEPTH * ns, ML_H * ML_DH)
    mlm_state = state_mlstm_m.reshape(DEPTH * ns, ML_H)
    ssm_st = state_ssm.reshape(DEPTH * ns, SSM_H, SSM_P, SSM_N)
    ck = cache_mem_k.reshape(DEPTH * ns, MEM_LEN, MEM_H, MEM_DH)
    cv = cache_mem_v.reshape(DEPTH * ns, MEM_LEN, MEM_H, MEM_DH)

    x = x_sample.reshape(ns, d)
    ret_s = mlc_s = ssm_s = None
    small_s = []
    for l in range(DEPTH):
        lw = LW[l]
        proj, small = _inproj(x, lw['norm_mix'], lw['w_main'], lw['w_small'], tm=ns, tn=1024, out_dtype=F32)
        y_r, ret_s = _ret_decode(proj, cos_s, sin_s, lw['ret_norm'], ret_state, None if l == 0 else [ret_s],
                                 layer=l)
        ml_buf = jnp.swapaxes(state_mlstm_conv[l], 0, 1)
        y_m, mlc_s, ml_n, ml_m, ml_bo = _ml_decode(proj, small, ml_buf, lw['ml_cw'], lw['ml_cb'], lw['ml_wq'],
                                                   lw['ml_wk'], lw['ml_gb'], lw['ml_norm'],
                                                   mlc_state, mln_state, mlm_state, None if l == 0 else [mlc_s],
                                                   layer=l)
        ssm_buf = jnp.swapaxes(state_ssm_conv[l], 0, 1)
        y_s, ssm_s, ssm_bo = _ssd_decode(proj, small, ssm_buf, lw['ssm_cw'], lw['ssm_cb'], lw['ssm_dtb'],
                                         lw['ssm_alog'], lw['ssm_dvec'], lw['ssm_norm'], ssm_st,
                                         None if l == 0 else [ssm_s], layer=l)
        x, qm = _merge(y_r, y_m, y_s, proj, x, lw['w_br_ret'], lw['w_br_ml'], lw['w_br_ssm'], lw['w_out'],
                       lw['norm_mem'], lw['mem_wq'], tm=ns, q_dtype=F32)
        att = _attn_decode(qm.reshape(ns, MEM_H, MEM_DH), ck, cv, layer=l, bb=4)
        x = _mm(att.reshape(ns, d), lw['mem_wo'], x, tm=ns, tn=d)
        x = _mlp(x, lw['norm_mlp'], lw['mlp_w1'], lw['mlp_w2'], norm_f, tm=ns, tf=1024,
                 final_norm=(l == DEPTH - 1))
        small_s.append((ml_n.reshape(ns, ML_H, ML_DH), ml_m, jnp.swapaxes(ml_bo, 0, 1),
                        jnp.swapaxes(ssm_bo, 0, 1)))
    y_sample = x.reshape(ns, 1, d)
    sample_states = (ret_s.reshape(DEPTH, ns, RET_H, RET_DK, RET_DV),
                     mlc_s.reshape(DEPTH, ns, ML_H, ML_DH, ML_DH),
                     stack(small_s, 0), stack(small_s, 1), stack(small_s, 2),
                     ssm_s.reshape(DEPTH, ns, SSM_H, SSM_P, SSM_N),
                     stack(small_s, 3))

    return (y_prompt, y_sample, *prompt_states, memk_p, memv_p, *sample_states)
```

```python
import functools
import math

import jax
import jax.numpy as jnp
from jax import lax
from jax.experimental import pallas as pl
from jax.experimental.pallas import tpu as pltpu

F32 = jnp.float32
BF16 = jnp.bfloat16

D_MODEL = 1024
DEPTH = 2
PAST_LEN = 16384
CHUNK = 128
CONV_K = 4
EPS = 1e-6
RET_H, RET_DK, RET_DV = 4, 128, 256
ROPE_THETA = 10000.0
ML_H, ML_DH = 4, 256
SSM_H, SSM_P, SSM_G, SSM_N = 16, 64, 4, 128
SSM_R = SSM_H // SSM_G
MEM_LEN, MEM_H, MEM_DH = 256, 4, 256
D_FF = 4 * D_MODEL

C_Q, C_K, C_V, C_G = 0, 512, 1024, 2048
C_U, C_VM, C_OM = 3072, 4096, 5120
C_Z, C_XS, C_B, C_C = 6144, 7168, 8192, 8704
C_GR, C_GM, C_GS = 9216, 10240, 11264
N_MAIN = 12288
L_IG, L_FG, L_DT = 0, 4, 8
N_SMALL = 128

VMEM_LIMIT = 56 * 1024 * 1024
NT_DIMS = (((1,), (1,)), ((), ()))

_LOG_G = [math.log1p(-(2.0 ** (-5.0 - h))) for h in range(RET_H)]


def _cparams(n_axes, flags=None):
    return pltpu.CompilerParams(dimension_semantics=("arbitrary",) * n_axes,
                                vmem_limit_bytes=VMEM_LIMIT, flags=flags)


MIXER_FLAGS = None


def _dot(a, b):
    return jnp.dot(a, b, preferred_element_type=F32)


def _dot_nt(a, b):
    return lax.dot_general(a, b, NT_DIMS, preferred_element_type=F32)


def _rmsnorm(x, w):
    ms = jnp.mean(x * x, axis=-1, keepdims=True)
    return x * lax.rsqrt(ms + EPS) * w


def _groupnorm(x, w, center):
    if center:
        x = x - jnp.mean(x, axis=-1, keepdims=True)
    ms = jnp.mean(x * x, axis=-1, keepdims=True)
    return x * lax.rsqrt(ms + EPS) * w


def _sigmoid(x):
    return 1.0 / (1.0 + jnp.exp(-x))


def _silu(x):
    return x * _sigmoid(x)


def _softplus(x):
    return jnp.maximum(x, 0.0) + jnp.log1p(jnp.exp(-jnp.abs(x)))


def _log_sigmoid(x):
    return -_softplus(-x)


def _cumsum_rows(x, tril_b):
    hi = x.astype(BF16)
    r1 = x - hi.astype(F32)
    mid = r1.astype(BF16)
    lo = (r1 - mid.astype(F32)).astype(BF16)
    return _dot(tril_b, hi) + _dot(tril_b, mid) + _dot(tril_b, lo)


def _pad_rows(x, rows):
    return jnp.concatenate([x, jnp.zeros((rows - x.shape[0], x.shape[1]), x.dtype)], axis=0)


def _pad_rows_t(x):
    return _pad_rows(x, 128).T


def _skip_alias(kernel_fn, n_alias):
    if n_alias == 0:
        return kernel_fn
    return lambda *refs: kernel_fn(*refs[n_alias:])


def _inproj_kernel(x_ref, nw_ref, w_ref, ws_ref, o_ref, os_ref, xn_ref):
    @pl.when(pl.program_id(1) == 0)
    def _():
        xn = _rmsnorm(x_ref[...], nw_ref[...]).astype(BF16)
        xn_ref[...] = xn
        os_ref[...] = _dot(xn, ws_ref[...])

    o_ref[...] = _dot(xn_ref[...], w_ref[...]).astype(o_ref.dtype)


def _inproj(x, nw, w_main, w_small, *, tm, tn, out_dtype):
    m = x.shape[0]
    return pl.pallas_call(
        _inproj_kernel,
        grid=(m // tm, N_MAIN // tn),
        in_specs=[pl.BlockSpec((tm, D_MODEL), lambda i, j: (i, 0)),
                  pl.BlockSpec((1, D_MODEL), lambda i, j: (0, 0)),
                  pl.BlockSpec((D_MODEL, tn), lambda i, j: (0, j)),
                  pl.BlockSpec((D_MODEL, N_SMALL), lambda i, j: (0, 0))],
        out_specs=[pl.BlockSpec((tm, tn), lambda i, j: (i, j)),
                   pl.BlockSpec((tm, N_SMALL), lambda i, j: (i, 0))],
        out_shape=[jax.ShapeDtypeStruct((m, N_MAIN), out_dtype),
                   jax.ShapeDtypeStruct((m, N_SMALL), F32)],
        scratch_shapes=[pltpu.VMEM((tm, D_MODEL), BF16)],
        compiler_params=_cparams(2),
        name="inproj",
    )(x, nw, w_main, w_small)


def _mm_kernel(*refs, has_res):
    if has_res:
        a_ref, w_ref, r_ref, o_ref = refs
    else:
        a_ref, w_ref, o_ref = refs
    acc = _dot(a_ref[...].astype(BF16), w_ref[...])
    if has_res:
        acc = r_ref[...] + acc
    o_ref[...] = acc.astype(o_ref.dtype)


def _mm(a, w, res=None, *, tm, tn, out_dtype=F32):
    m, k = a.shape
    n = w.shape[1]
    in_specs = [pl.BlockSpec((tm, k), lambda i, j: (i, 0)),
                pl.BlockSpec((k, tn), lambda i, j: (0, j))]
    args = [a, w]
    if res is not None:
        in_specs.append(pl.BlockSpec((tm, tn), lambda i, j: (i, j)))
        args.append(res)
    return pl.pallas_call(
        functools.partial(_mm_kernel, has_res=res is not None),
        grid=(m // tm, n // tn),
        in_specs=in_specs,
        out_specs=pl.BlockSpec((tm, tn), lambda i, j: (i, j)),
        out_shape=jax.ShapeDtypeStruct((m, n), out_dtype),
        compiler_params=_cparams(2),
        name="mm",
    )(*args)


def _memkv_kernel(a_ref, wk_ref, wv_ref, ok_ref, ov_ref):
    a = a_ref[...].astype(BF16)
    ok_ref[0] = _dot(a, wk_ref[0])
    ov_ref[0] = _dot(a, wv_ref[0])


def _memkv(a, wk, wv, *, tm):
    m = a.shape[0]
    n = wk.shape[2]
    return pl.pallas_call(
        _memkv_kernel,
        grid=(DEPTH, m // tm),
        in_specs=[pl.BlockSpec((tm, D_MODEL), lambda l, i: (i, 0)),
                  pl.BlockSpec((1, D_MODEL, n), lambda l, i: (l, 0, 0)),
                  pl.BlockSpec((1, D_MODEL, n), lambda l, i: (l, 0, 0))],
        out_specs=[pl.BlockSpec((1, tm, n), lambda l, i: (l, i, 0)),
                   pl.BlockSpec((1, tm, n), lambda l, i: (l, i, 0))],
        out_shape=[jax.ShapeDtypeStruct((DEPTH, m, n), F32)] * 2,
        compiler_params=_cparams(2),
        name="memkv",
    )(a, wk, wv)


SEQ_UNROLL = 2


class _Ctx(dict):
    __getattr__ = dict.__getitem__
    __setattr__ = dict.__setitem__


def _seq_loop(nb, seq_phases, n_items, item_phases, finish):
    def body(i, carry):
        seqs = [_Ctx(b=i * SEQ_UNROLL + u) for u in range(SEQ_UNROLL)]
        for phase in seq_phases:
            for s in seqs:
                phase(s)
        items = []
        for s in seqs:
            s.heads = [_Ctx(seq=s, b=s.b, k=k) for k in range(n_items)]
            items.extend(s.heads)
        for phase in item_phases:
            for it in items:
                phase(it)
        for s in seqs:
            finish(s)
        return carry

    lax.fori_loop(0, nb // SEQ_UNROLL, body, 0)


def _state_io(prev, n_alias_slots):
    if prev is None:
        return [], [], {}
    return (list(prev), [pl.BlockSpec(memory_space=pl.ANY)] * len(prev),
            {i: n_alias_slots[i] for i in range(len(prev))})


def _ret_prefill_kernel(q_ref, k_ref, v_ref, g_ref, cos_ref, sin_ref, nw_ref, y_ref, s_ref):
    @pl.when(pl.program_id(0) == 0)
    def _():
        s_ref[...] = jnp.zeros_like(s_ref)

    c = CHUNK
    nb = q_ref.shape[0]
    cosf = cos_ref[...]
    sinf = sin_ref[...]
    row = lax.broadcasted_iota(jnp.int32, (c, c), 0)
    col = lax.broadcasted_iota(jnp.int32, (c, c), 1)
    causal = row >= col
    diff = jnp.where(causal, (row - col).astype(F32), 0.0)
    rowf = row.astype(F32)
    rowf_v = lax.broadcasted_iota(jnp.int32, (c, RET_DV), 0).astype(F32)
    decay = [jnp.where(causal, jnp.exp(lg * diff), 0.0) for lg in _LOG_G]
    q_decay = [jnp.exp(lg * (rowf_v + 1.0)) for lg in _LOG_G]
    k_decay = [jnp.exp(lg * (c - 1.0 - rowf)) * (RET_DK ** -0.5) for lg in _LOG_G]

    def load(s):
        s.s_old = [s_ref[s.b, h] for h in range(RET_H)]

    def rotate(it):
        h = it.k
        sk = slice(h * RET_DK, (h + 1) * RET_DK)
        q = q_ref[it.b, :, sk].astype(F32)
        k = k_ref[it.b, :, sk].astype(F32)
        qr = q * cosf + pltpu.roll(q, RET_DK // 2, 1) * sinf
        kr = k * cosf + pltpu.roll(k, RET_DK // 2, 1) * sinf
        it.qb = qr.astype(BF16)
        it.kb = (kr * (RET_DK ** -0.5)).astype(BF16)
        it.kt = (kr * k_decay[h]).T.astype(BF16)
        it.v = v_ref[it.b, :, h * RET_DV:(h + 1) * RET_DV]

    def scores(it):
        it.sc = (_dot_nt(it.qb, it.kb) * decay[it.k]).astype(BF16)

    def mix(it):
        h = it.k
        s_old = it.seq.s_old[h]
        it.y = _dot(it.sc, it.v) + _dot(it.qb, s_old.astype(BF16)) * q_decay[h]
        it.s_new = s_old * math.exp(_LOG_G[h] * c) + _dot(it.kt, it.v)

    def emit(it):
        sv = slice(it.k * RET_DV, (it.k + 1) * RET_DV)
        g = g_ref[it.b, :, sv].astype(F32)
        yn = _groupnorm(it.y, nw_ref[:, sv], True) * _silu(g)
        y_ref[it.b, :, sv] = yn.astype(y_ref.dtype)

    def finish(s):
        for it in s.heads:
            s_ref[s.b, it.k] = it.s_new

    _seq_loop(nb, [load], RET_H, [rotate, scores, mix, emit], finish)


def _ret_prefill(proj, cosf, sinf, nw, prev, *, layer, nb, seq):
    c = CHUNK
    p3 = proj.reshape(nb, seq, N_MAIN)
    al_args, al_specs, al_map = _state_io(prev, [1])
    return pl.pallas_call(
        _skip_alias(_ret_prefill_kernel, len(al_args)),
        grid=(seq // c,),
        in_specs=al_specs + [
            pl.BlockSpec((nb, c, 512), lambda t: (0, t, C_Q // 512)),
            pl.BlockSpec((nb, c, 512), lambda t: (0, t, C_K // 512)),
            pl.BlockSpec((nb, c, 1024), lambda t: (0, t, C_V // 1024)),
            pl.BlockSpec((nb, c, 1024), lambda t: (0, t, C_G // 1024)),
            pl.BlockSpec((c, RET_DK), lambda t: (t, 0)),
            pl.BlockSpec((c, RET_DK), lambda t: (t, 0)),
            pl.BlockSpec((1, 1024), lambda t: (0, 0))],
        out_specs=[pl.BlockSpec((nb, c, 1024), lambda t: (0, t, 0)),
                   pl.BlockSpec((nb, RET_H, RET_DK, RET_DV), lambda t: (layer, 0, 0, 0))],
        out_shape=[jax.ShapeDtypeStruct((nb, seq, 1024), BF16),
                   jax.ShapeDtypeStruct((DEPTH * nb, RET_H, RET_DK, RET_DV), F32)],
        input_output_aliases=al_map,
        compiler_params=_cparams(1, MIXER_FLAGS),
        name="ret_prefill",
    )(*al_args, p3, p3, p3, p3, cosf, sinf, nw)


def _shift_matrix(first_chunk):
    c = CHUNK
    r = lax.broadcasted_iota(jnp.int32, (3 * c, 2 * c), 0)
    q = lax.broadcasted_iota(jnp.int32, (3 * c, 2 * c), 1)
    k = lax.shift_right_logical(r, 7)
    t = jnp.bitwise_and(r, c - 1)
    hit = (q == c + t - (CONV_K - 1) + k) & (q >= jnp.where(first_chunk, c, 0))
    return jnp.where(hit, 1.0, 0.0).astype(BF16)


def _conv_silu(shift, x_prev, x_cur, cw, cb):
    c = CHUNK
    p = _dot(shift, jnp.concatenate([x_prev, x_cur], axis=0))
    acc = cb
    for j in range(CONV_K - 1):
        acc = acc + p[j * c:(j + 1) * c] * cw[j:j + 1, :]
    acc = acc + x_cur.astype(F32) * cw[CONV_K - 1:CONV_K, :]
    return _silu(acc)


def _ml_prefill_kernel(u_ref, up_ref, v_ref, o_ref, sm_ref, cw_ref, cb_ref, wq_ref, wk_ref, gb_ref, nw_ref,
                       y_ref, c_ref, n_ref, m_ref, cv_ref, tail):
    c = CHUNK
    nb = u_ref.shape[0]
    first = pl.program_id(0) == 0

    @pl.when(first)
    def _():
        c_ref[...] = jnp.zeros_like(c_ref)
        n_ref[...] = jnp.zeros_like(n_ref)
        m_ref[...] = jnp.zeros_like(m_ref)

    row = lax.broadcasted_iota(jnp.int32, (c, c), 0)
    col = lax.broadcasted_iota(jnp.int32, (c, c), 1)
    causal = row >= col
    tril_b = jnp.where(causal, 1.0, 0.0).astype(BF16)
    lane_m = lax.broadcasted_iota(jnp.int32, (1, 128), 1)
    shift = _shift_matrix(first)

    def load(s):
        s.c_old = [c_ref[s.b, h] for h in range(ML_H)]
        s.n_old = n_ref[s.b]
        s.m_row = m_ref[s.b]

    def gates_phase(s):
        tail[...] = u_ref[s.b, c - 16:c, :].astype(F32)
        cv_ref[s.b] = tail[16 - (CONV_K - 1):16, :]
        s.gates = sm_ref[s.b] + gb_ref[...]
        s.b_all = _cumsum_rows(_log_sigmoid(s.gates), tril_b)
        s.b_all_t = s.b_all.T
        s.gates_t = s.gates.T

    def conv_phase(it):
        sl = slice(it.k * ML_DH, (it.k + 1) * ML_DH)
        it.uc = _conv_silu(shift, up_ref[it.b, :, sl], u_ref[it.b, :, sl], cw_ref[:, sl], cb_ref[:, sl]).astype(BF16)
        it.v = v_ref[it.b, :, sl]

    def weights_phase(it):
        h, s = it.k, it.seq
        b_col = s.b_all[:, L_FG + h:L_FG + h + 1]
        b_row = s.b_all_t[L_FG + h:L_FG + h + 1, :]
        it_col = s.gates[:, L_IG + h:L_IG + h + 1]
        it_row = s.gates_t[L_IG + h:L_IG + h + 1, :]
        m_prev = s.m_row[:, h:h + 1]
        logw = jnp.where(causal, b_col - b_row + it_row, -jnp.inf)
        inter = b_col + m_prev
        it.m_t = jnp.maximum(inter, jnp.max(logw, axis=-1, keepdims=True))
        it.dmat = jnp.exp(logw - it.m_t)
        it.w_int = jnp.exp(inter - it.m_t)
        b_end = b_col[c - 1:c, :]
        logw_s = b_end - b_col + it_col
        it.m_new = jnp.maximum(b_end + m_prev, jnp.max(logw_s, axis=0, keepdims=True))
        it.w_s = jnp.exp(logw_s - it.m_new)
        it.w_prev = jnp.exp(b_end + m_prev - it.m_new)

    def qk_phase(it):
        it.qm = _dot(it.uc, wq_ref[it.k])
        it.km = _dot(it.uc, wk_ref[it.k]) * (ML_DH ** -0.5)
        it.qb = it.qm.astype(BF16)
        it.kb = it.km.astype(BF16)
        it.kw = it.km * it.w_s
        it.kwt = it.kw.T.astype(BF16)

    def scores_phase(it):
        it.sc = _dot_nt(it.qb, it.kb) * it.dmat

    def mix_phase(it):
        c_old = it.seq.c_old[it.k]
        n_old = it.seq.n_old[it.k:it.k + 1, :]
        it.num = _dot(it.sc.astype(BF16), it.v) + _dot(it.qb, c_old.astype(BF16)) * it.w_int
        it.c_new = c_old * it.w_prev + _dot(it.kwt, it.v)
        it.n_new = n_old * it.w_prev + jnp.sum(it.kw, axis=0, keepdims=True)
        den = jnp.sum(it.sc, axis=-1, keepdims=True) + jnp.sum(it.qm * n_old, axis=-1, keepdims=True) * it.w_int
        it.den = jnp.maximum(jnp.abs(den), jnp.exp(-it.m_t))

    def emit_phase(it):
        sl = slice(it.k * ML_DH, (it.k + 1) * ML_DH)
        og = o_ref[it.b, :, sl].astype(F32)
        yn = _groupnorm(it.num / it.den, nw_ref[:, sl], True) * _sigmoid(og)
        y_ref[it.b, :, sl] = yn.astype(y_ref.dtype)

    def finish(s):
        m_row_new = s.m_row
        for it in s.heads:
            c_ref[s.b, it.k] = it.c_new
            n_ref[s.b, it.k:it.k + 1, :] = it.n_new
            m_row_new = jnp.where(lane_m == it.k, it.m_new, m_row_new)
        m_ref[s.b] = m_row_new

    _seq_loop(nb, [load, gates_phase], ML_H,
              [conv_phase, weights_phase, qk_phase, scores_phase, mix_phase, emit_phase], finish)


def _ml_prefill(proj, small, cw, cb, wq, wk, gb, nw, prev, *, layer, nb, seq):
    c = CHUNK
    p3 = proj.reshape(nb, seq, N_MAIN)
    s3 = small.reshape(nb, seq, N_SMALL)
    w = ML_H * ML_DH
    full2 = lambda t: (0, 0)
    al_args, al_specs, al_map = _state_io(prev, [1])
    return pl.pallas_call(
        _skip_alias(_ml_prefill_kernel, len(al_args)),
        grid=(seq // c,),
        in_specs=al_specs + [
            pl.BlockSpec((nb, c, w), lambda t: (0, t, C_U // w)),
            pl.BlockSpec((nb, c, w), lambda t: (0, jnp.maximum(t - 1, 0), C_U // w)),
            pl.BlockSpec((nb, c, w), lambda t: (0, t, C_VM // w)),
            pl.BlockSpec((nb, c, w), lambda t: (0, t, C_OM // w)),
            pl.BlockSpec((nb, c, N_SMALL), lambda t: (0, t, 0)),
            pl.BlockSpec((CONV_K, w), full2),
            pl.BlockSpec((1, w), full2),
            pl.BlockSpec((ML_H, ML_DH, ML_DH), lambda t: (0, 0, 0)),
            pl.BlockSpec((ML_H, ML_DH, ML_DH), lambda t: (0, 0, 0)),
            pl.BlockSpec((1, N_SMALL), full2),
            pl.BlockSpec((1, w), full2)],
        out_specs=[pl.BlockSpec((nb, c, w), lambda t: (0, t, 0)),
                   pl.BlockSpec((nb, ML_H, ML_DH, ML_DH), lambda t: (layer, 0, 0, 0)),
                   pl.BlockSpec((nb, ML_H, ML_DH), lambda t: (0, 0, 0)),
                   pl.BlockSpec((nb, 1, 128), lambda t: (0, 0, 0)),
                   pl.BlockSpec((nb, CONV_K - 1, w), lambda t: (0, 0, 0))],
        out_shape=[jax.ShapeDtypeStruct((nb, seq, w), BF16),
                   jax.ShapeDtypeStruct((DEPTH * nb, ML_H, ML_DH, ML_DH), F32),
                   jax.ShapeDtypeStruct((nb, ML_H, ML_DH), F32),
                   jax.ShapeDtypeStruct((nb, 1, 128), F32),
                   jax.ShapeDtypeStruct((nb, CONV_K - 1, w), F32)],
        scratch_shapes=[pltpu.VMEM((16, w), F32)],
        input_output_aliases=al_map,
        compiler_params=_cparams(1, MIXER_FLAGS),
        name="ml_prefill",
    )(*al_args, p3, p3, p3, p3, s3, cw, cb, wq, wk, gb, nw)


def _block_bcast(tile, lanes):
    c = tile.shape[0]
    lane = lax.broadcasted_iota(jnp.int32, (c, 128), 1)
    cols = [jnp.broadcast_to(tile[:, l:l + 1], (c, 128)) for l in lanes]
    left = jnp.where(lane < SSM_P, cols[0], cols[1])
    right = jnp.where(lane < SSM_P, cols[2], cols[3])
    return jnp.concatenate([left, right], axis=1)


def _ssd_prefill_kernel(z_ref, xs_ref, xsp_ref, bc_ref, bcp_ref, sm_ref, cw_ref, cb_ref, dtb_ref, alog_ref,
                        dv_ref, nw_ref, y_ref, s_ref, cv_ref, tail):
    c = CHUNK
    nb = z_ref.shape[0]
    wx = SSM_H * SSM_P
    gw = SSM_R * SSM_P
    first = pl.program_id(0) == 0

    @pl.when(first)
    def _():
        s_ref[...] = jnp.zeros_like(s_ref)

    row = lax.broadcasted_iota(jnp.int32, (c, c), 0)
    col = lax.broadcasted_iota(jnp.int32, (c, c), 1)
    causal = row >= col
    tril_b = jnp.where(causal, 1.0, 0.0).astype(BF16)
    lane_blk = lax.shift_right_logical(lax.broadcasted_iota(jnp.int32, (c, gw), 1), 6)
    a_row = -jnp.exp(alog_ref[...])
    shift = _shift_matrix(first)

    def conv(cur_ref, prv_ref, b, lanes, w_off):
        wl = slice(w_off + lanes.start, w_off + lanes.stop)
        return _conv_silu(shift, prv_ref[b, :, lanes], cur_ref[b, :, lanes], cw_ref[:, wl], cb_ref[:, wl])

    def load(s):
        s.s_old = [s_ref[s.b, g * SSM_R:(g + 1) * SSM_R].reshape(gw, SSM_N) for g in range(SSM_G)]

    def dt_phase(s):
        b = s.b
        tail[:, 0:wx] = xs_ref[b, c - 16:c, :].astype(F32)
        tail[:, wx:2 * wx] = bc_ref[b, c - 16:c, :].astype(F32)
        cv_ref[b] = tail[16 - (CONV_K - 1):16, :]
        s.delta = _softplus(sm_ref[b] + dtb_ref[...])
        s.cum = _cumsum_rows(s.delta * a_row, tril_b)
        s.cum_t = s.cum.T
        s.delta_t = s.delta.T
        s.ecum = jnp.exp(s.cum)
        cum_end = s.cum[c - 1:c, :]
        s.w_state = jnp.exp(cum_end - s.cum) * s.delta
        s.dec_row = jnp.exp(cum_end)

    def conv_phase(it):
        g = it.k
        it.lanes = [L_DT + g * SSM_R + r for r in range(SSM_R)]
        it.xg = conv(xs_ref, xsp_ref, it.b, slice(g * gw, (g + 1) * gw), 0)
        it.bg = conv(bc_ref, bcp_ref, it.b, slice(g * SSM_N, (g + 1) * SSM_N), wx).astype(BF16)
        it.cg = conv(bc_ref, bcp_ref, it.b, slice((SSM_G + g) * SSM_N, (SSM_G + g + 1) * SSM_N), wx).astype(BF16)

    def seg_phase(it):
        s = it.seq
        it.seg = [jnp.exp(jnp.where(causal, s.cum[:, ln:ln + 1] - s.cum_t[ln:ln + 1, :], -jnp.inf))
                  * s.delta_t[ln:ln + 1, :] for ln in it.lanes]
        it.x_stack = jnp.concatenate([jnp.where(lane_blk == r, it.xg, 0.0).astype(BF16) for r in range(SSM_R)],
                                     axis=0)
        it.xwt = (it.xg * _block_bcast(s.w_state, it.lanes)).T.astype(BF16)

    def cb_phase(it):
        it.cb_mat = _dot_nt(it.cg, it.bg)

    def mix_phase(it):
        s = it.seq
        s_old = s.s_old[it.k]
        m_wide = jnp.concatenate([(it.cb_mat * sg).astype(BF16) for sg in it.seg], axis=1)
        it.y = (_dot(m_wide, it.x_stack)
                + _dot_nt(it.cg, s_old.astype(BF16)) * _block_bcast(s.ecum, it.lanes))
        upd = _dot(it.xwt, it.bg)
        it.s_new = [s_old[r * SSM_P:(r + 1) * SSM_P] * s.dec_row[:, ln:ln + 1] + upd[r * SSM_P:(r + 1) * SSM_P]
                    for r, ln in enumerate(it.lanes)]

    def emit_phase(it):
        gs = slice(it.k * gw, (it.k + 1) * gw)
        y = it.y + dv_ref[:, gs] * it.xg
        z = z_ref[it.b, :, gs].astype(F32)
        yn = _groupnorm(y * _silu(z), nw_ref[:, gs], False)
        y_ref[it.b, :, gs] = yn.astype(y_ref.dtype)

    def finish(s):
        for it in s.heads:
            for r in range(SSM_R):
                s_ref[s.b, it.k * SSM_R + r] = it.s_new[r]

    _seq_loop(nb, [load, dt_phase], SSM_G, [conv_phase, seg_phase, cb_phase, mix_phase, emit_phase], finish)


def _ssd_prefill(proj, small, cw, cb, dtb, alog, dvec, nw, prev, *, layer, nb, seq):
    c = CHUNK
    wx = SSM_H * SSM_P
    p3 = proj.reshape(nb, seq, N_MAIN)
    s3 = small.reshape(nb, seq, N_SMALL)
    full2 = lambda t: (0, 0)
    al_args, al_specs, al_map = _state_io(prev, [1])
    return pl.pallas_call(
        _skip_alias(_ssd_prefill_kernel, len(al_args)),
        grid=(seq // c,),
        in_specs=al_specs + [
            pl.BlockSpec((nb, c, wx), lambda t: (0, t, C_Z // wx)),
            pl.BlockSpec((nb, c, wx), lambda t: (0, t, C_XS // wx)),
            pl.BlockSpec((nb, c, wx), lambda t: (0, jnp.maximum(t - 1, 0), C_XS // wx)),
            pl.BlockSpec((nb, c, wx), lambda t: (0, t, C_B // wx)),
            pl.BlockSpec((nb, c, wx), lambda t: (0, jnp.maximum(t - 1, 0), C_B // wx)),
            pl.BlockSpec((nb, c, N_SMALL), lambda t: (0, t, 0)),
            pl.BlockSpec((CONV_K, 2 * wx), full2),
            pl.BlockSpec((1, 2 * wx), full2),
            pl.BlockSpec((1, N_SMALL), full2),
            pl.BlockSpec((1, N_SMALL), full2),
            pl.BlockSpec((1, wx), full2),
            pl.BlockSpec((1, wx), full2)],
        out_specs=[pl.BlockSpec((nb, c, wx), lambda t: (0, t, 0)),
                   pl.BlockSpec((nb, SSM_H, SSM_P, SSM_N), lambda t: (layer, 0, 0, 0)),
                   pl.BlockSpec((nb, CONV_K - 1, 2 * wx), lambda t: (0, 0, 0))],
        out_shape=[jax.ShapeDtypeStruct((nb, seq, wx), BF16),
                   jax.ShapeDtypeStruct((DEPTH * nb, SSM_H, SSM_P, SSM_N), F32),
                   jax.ShapeDtypeStruct((nb, CONV_K - 1, 2 * wx), F32)],
        scratch_shapes=[pltpu.VMEM((16, 2 * wx), F32)],
        input_output_aliases=al_map,
        compiler_params=_cparams(1, MIXER_FLAGS),
        name="ssd_prefill",
    )(*al_args, p3, p3, p3, p3, p3, s3, cw, cb, dtb, alog, dvec, nw)


MERGE_SUB = 256


def _merge_kernel(yr_ref, ym_ref, ys_ref, gr_ref, gm_ref, gs_ref, x_ref, wr_ref, wm_ref, ws_ref,
                  wo_ref, nw_ref, wq_ref, xo_ref, qo_ref):
    tm = x_ref.shape[0]
    n_sub = max(1, tm // MERGE_SUB)
    subs = [slice(i * (tm // n_sub), (i + 1) * (tm // n_sub)) for i in range(n_sub)]
    branches = ((yr_ref, gr_ref, wr_ref), (ym_ref, gm_ref, wm_ref), (ys_ref, gs_ref, ws_ref))
    proj = [[_dot(y_ref[rs, :].astype(BF16), w_ref[...]) for y_ref, _, w_ref in branches] for rs in subs]
    merged = []
    for rs, pr in zip(subs, proj):
        gated = [_sigmoid(g_ref[rs, :].astype(F32)) * p for (_, g_ref, _), p in zip(branches, pr)]
        merged.append((gated[0] + gated[1] + gated[2]).astype(BF16))
    xns = [x_ref[rs, :] + _dot(m, wo_ref[...]) for rs, m in zip(subs, merged)]
    hqs = []
    for rs, xn in zip(subs, xns):
        xo_ref[rs, :] = xn
        hqs.append(_rmsnorm(xn, nw_ref[...]).astype(BF16))
    for rs, hq in zip(subs, hqs):
        qo_ref[rs, :] = _dot(hq, wq_ref[...]).astype(qo_ref.dtype)


def _merge(yr, ym, ys, proj, x, wr, wm, ws, wo, nw, wq, *, tm, q_dtype):
    m = x.shape[0]
    d = D_MODEL
    row = lambda i: (i, 0)
    wspec = pl.BlockSpec((d, d), lambda i: (0, 0), pipeline_mode=pl.Buffered(1))
    return pl.pallas_call(
        _merge_kernel,
        grid=(m // tm,),
        in_specs=[pl.BlockSpec((tm, d), row), pl.BlockSpec((tm, d), row), pl.BlockSpec((tm, d), row),
                  pl.BlockSpec((tm, d), lambda i: (i, C_GR // d)),
                  pl.BlockSpec((tm, d), lambda i: (i, C_GM // d)),
                  pl.BlockSpec((tm, d), lambda i: (i, C_GS // d)),
                  pl.BlockSpec((tm, d), row),
                  wspec, wspec, wspec, wspec,
                  pl.BlockSpec((1, d), lambda i: (0, 0)),
                  wspec],
        out_specs=[pl.BlockSpec((tm, d), row), pl.BlockSpec((tm, d), row)],
        out_shape=[jax.ShapeDtypeStruct((m, d), F32), jax.ShapeDtypeStruct((m, d), q_dtype)],
        compiler_params=_cparams(1),
        name="merge",
    )(yr, ym, ys, proj, proj, proj, x, wr, wm, ws, wo, nw, wq)


def _attn_prefill_kernel(q_ref, k_ref, v_ref, x_ref, wo_ref, o_ref):
    heads = [slice(h * MEM_DH, (h + 1) * MEM_DH) for h in range(MEM_H)]
    scores = [_dot_nt(q_ref[0, :, sl], k_ref[0, :, sl].astype(BF16)) * (MEM_DH ** -0.5) for sl in heads]
    probs = []
    for s in scores:
        p = jnp.exp(s - jnp.max(s, axis=-1, keepdims=True))
        probs.append((p / jnp.sum(p, axis=-1, keepdims=True)).astype(BF16))
    outs = [_dot(p, v_ref[0, :, sl].astype(BF16)).astype(BF16) for p, sl in zip(probs, heads)]
    o_all = jnp.concatenate(outs, axis=1)
    o_ref[0] = x_ref[0] + _dot(o_all, wo_ref[...])


def _attn_prefill(q, mk, mv, x, wo, *, layer, nb, seq, tq):
    d = D_MODEL
    q3 = q.reshape(nb, seq, d)
    x3 = x.reshape(nb, seq, d)
    out = pl.pallas_call(
        _attn_prefill_kernel,
        grid=(nb, seq // tq),
        in_specs=[pl.BlockSpec((1, tq, d), lambda b, t: (b, t, 0)),
                  pl.BlockSpec((1, MEM_LEN, d), lambda b, t: (layer * nb + b, 0, 0)),
                  pl.BlockSpec((1, MEM_LEN, d), lambda b, t: (layer * nb + b, 0, 0)),
                  pl.BlockSpec((1, tq, d), lambda b, t: (b, t, 0)),
                  pl.BlockSpec((d, d), lambda b, t: (0, 0))],
        out_specs=pl.BlockSpec((1, tq, d), lambda b, t: (b, t, 0)),
        out_shape=jax.ShapeDtypeStruct((nb, seq, d), F32),
        compiler_params=_cparams(2),
        name="attn_prefill",
    )(q3, mk, mv, x3, wo)
    return out.reshape(nb * seq, d)


def _attn_decode_kernel(q_ref, k_ref, v_ref, o_ref):
    for j in range(q_ref.shape[0]):
        s = jnp.sum(k_ref[j] * q_ref[j][None], axis=-1, keepdims=True) * (MEM_DH ** -0.5)
        p = jnp.exp(s - jnp.max(s, axis=0, keepdims=True))
        p = p / jnp.sum(p, axis=0, keepdims=True)
        o_ref[j] = jnp.sum(p * v_ref[j], axis=0)


def _attn_decode(q, ck, cv, *, layer, bb):
    nb = q.shape[0]
    nblk = nb // bb
    blk = (bb, MEM_LEN, MEM_H, MEM_DH)
    return pl.pallas_call(
        _attn_decode_kernel,
        grid=(nblk,),
        in_specs=[pl.BlockSpec((bb, MEM_H, MEM_DH), lambda i: (i, 0, 0)),
                  pl.BlockSpec(blk, lambda i: (layer * nblk + i, 0, 0, 0)),
                  pl.BlockSpec(blk, lambda i: (layer * nblk + i, 0, 0, 0))],
        out_specs=pl.BlockSpec((bb, MEM_H, MEM_DH), lambda i: (i, 0, 0)),
        out_shape=jax.ShapeDtypeStruct((nb, MEM_H, MEM_DH), F32),
        compiler_params=_cparams(1),
        name="attn_decode",
    )(q, ck, cv)


MLP_SUB = 1024


def _mlp_kernel(x_ref, nw_ref, w1_ref, w2_ref, nf_ref, o_ref, xn_ref, acc_ref, *, final_norm):
    j = pl.program_id(1)

    @pl.when(j == 0)
    def _():
        xn_ref[...] = _rmsnorm(x_ref[...], nw_ref[...]).astype(BF16)
        acc_ref[...] = jnp.zeros_like(acc_ref)

    subs = [slice(s * MLP_SUB, (s + 1) * MLP_SUB) for s in range(w1_ref.shape[1] // MLP_SUB)]
    hids = [_dot(xn_ref[...], w1_ref[:, sl]) for sl in subs]
    hids = [jnp.square(jnp.maximum(hid, 0.0)).astype(BF16) for hid in hids]
    acc = acc_ref[...]
    for hid, sl in zip(hids, subs):
        acc = acc + _dot(hid, w2_ref[sl, :])
    acc_ref[...] = acc

    @pl.when(j == pl.num_programs(1) - 1)
    def _():
        y = x_ref[...] + acc_ref[...]
        if final_norm:
            y = _rmsnorm(y, nf_ref[...])
        o_ref[...] = y


def _mlp(x, nw, w1, w2, nf, *, tm, tf, final_norm):
    m = x.shape[0]
    d = D_MODEL
    return pl.pallas_call(
        functools.partial(_mlp_kernel, final_norm=final_norm),
        grid=(m // tm, D_FF // tf),
        in_specs=[pl.BlockSpec((tm, d), lambda i, j: (i, 0)),
                  pl.BlockSpec((1, d), lambda i, j: (0, 0)),
                  pl.BlockSpec((d, tf), lambda i, j: (0, j)),
                  pl.BlockSpec((tf, d), lambda i, j: (j, 0)),
                  pl.BlockSpec((1, d), lambda i, j: (0, 0))],
        out_specs=pl.BlockSpec((tm, d), lambda i, j: (i, 0)),
        out_shape=jax.ShapeDtypeStruct((m, d), F32),
        scratch_shapes=[pltpu.VMEM((tm, d), BF16), pltpu.VMEM((tm, d), F32)],
        compiler_params=_cparams(2),
        name="mlp",
    )(x, nw, w1, w2, nf)


DEC_BB = 8


def _row_mask(x, j):
    rows = lax.broadcasted_iota(jnp.int32, x.shape, 0)
    return jnp.where(rows == j, x, 0.0).astype(BF16)


def _blockdiag_rows(v):
    n = v.shape[0]
    tiled = jnp.concatenate([v] * n, axis=1)
    rows = lax.broadcasted_iota(jnp.int32, tiled.shape, 0)
    blk = lax.shift_right_logical(lax.broadcasted_iota(jnp.int32, tiled.shape, 1), 8)
    return _pad_rows(jnp.where(rows == blk, tiled, 0.0), 128).astype(BF16)


def _ret_decode_kernel(q_ref, k_ref, v_ref, g_ref, cos_ref, sin_ref, nw_ref, s_ref, y_ref, so_ref, ybuf):
    cosf = cos_ref[...]
    sinf = sin_ref[...]
    for h in range(RET_H):
        sk = slice(h * RET_DK, (h + 1) * RET_DK)
        sv = slice(h * RET_DV, (h + 1) * RET_DV)
        g_dec = math.exp(_LOG_G[h])
        q = q_ref[:, sk]
        k = k_ref[:, sk]
        qr = q * cosf + pltpu.roll(q, RET_DK // 2, 1) * sinf
        kr = (k * cosf + pltpu.roll(k, RET_DK // 2, 1) * sinf) * (RET_DK ** -0.5)
        k_t = _pad_rows_t(kr).astype(BF16)
        q16 = _pad_rows(qr, 16).astype(BF16)
        upd = _dot(k_t, _blockdiag_rows(v_ref[:, sv]))
        s_wide = []
        for j in range(DEC_BB):
            s_new = s_ref[j, h] * g_dec + upd[:, j * RET_DV:(j + 1) * RET_DV]
            so_ref[j, h] = s_new
            s_wide.append(s_new.astype(BF16))
        y_all = _dot(q16, jnp.concatenate(s_wide, axis=1))
        for j in range(DEC_BB):
            ybuf[j:j + 1, sv] = y_all[j:j + 1, j * RET_DV:(j + 1) * RET_DV]
        y_ref[:, sv] = _groupnorm(ybuf[:, sv], nw_ref[:, sv], True) * _silu(g_ref[:, sv])


def _ret_decode(proj, cosf, sinf, nw, state, prev, *, layer):
    nb = proj.shape[0]
    bb = DEC_BB
    nblk = nb // bb
    al_args, al_specs, al_map = _state_io(prev, [1])
    sblk = (bb, RET_H, RET_DK, RET_DV)
    return pl.pallas_call(
        _skip_alias(_ret_decode_kernel, len(al_args)),
        grid=(nblk,),
        in_specs=al_specs + [
            pl.BlockSpec((bb, 512), lambda i: (i, C_Q // 512)),
            pl.BlockSpec((bb, 512), lambda i: (i, C_K // 512)),
            pl.BlockSpec((bb, 1024), lambda i: (i, C_V // 1024)),
            pl.BlockSpec((bb, 1024), lambda i: (i, C_G // 1024)),
            pl.BlockSpec((1, RET_DK), lambda i: (0, 0)),
            pl.BlockSpec((1, RET_DK), lambda i: (0, 0)),
            pl.BlockSpec((1, 1024), lambda i: (0, 0)),
            pl.BlockSpec(sblk, lambda i: (layer * nblk + i, 0, 0, 0))],
        out_specs=[pl.BlockSpec((bb, 1024), lambda i: (i, 0)),
                   pl.BlockSpec(sblk, lambda i: (layer * nblk + i, 0, 0, 0))],
        out_shape=[jax.ShapeDtypeStruct((nb, RET_H * RET_DV), F32),
                   jax.ShapeDtypeStruct((DEPTH * nb, RET_H, RET_DK, RET_DV), F32)],
        scratch_shapes=[pltpu.VMEM((bb, RET_H * RET_DV), F32)],
        input_output_aliases=al_map,
        compiler_params=_cparams(1),
        name="ret_decode",
    )(*al_args, proj, proj, proj, proj, cosf, sinf, nw, state)


def _conv_step(x, buf_ref, cw_ref, cb_ref, bo_ref):
    acc = cb_ref[...]
    for j in range(CONV_K - 1):
        acc = acc + buf_ref[j] * cw_ref[j:j + 1, :]
    acc = acc + x * cw_ref[CONV_K - 1:CONV_K, :]
    bo_ref[0] = buf_ref[1]
    bo_ref[1] = buf_ref[2]
    bo_ref[2] = x
    return _silu(acc)


def _ml_decode_kernel(u_ref, v_ref, o_ref, sm_ref, buf_ref, cw_ref, cb_ref, wq_ref, wk_ref, gb_ref, nw_ref,
                      c_ref, n_ref, m_ref,
                      y_ref, co_ref, no_ref, mo_ref, bo_ref, ybuf):
    uc = _conv_step(u_ref[...], buf_ref, cw_ref, cb_ref, bo_ref).astype(BF16)
    gates = sm_ref[...] + gb_ref[...]
    lane4 = lax.broadcasted_iota(jnp.int32, m_ref.shape, 1)
    m_all = m_ref[...]
    m_out = m_all
    for h in range(ML_H):
        sl = slice(h * ML_DH, (h + 1) * ML_DH)
        qm = _dot(uc[:, sl], wq_ref[h])
        km = _dot(uc[:, sl], wk_ref[h]) * (ML_DH ** -0.5)
        i_pre = gates[:, L_IG + h:L_IG + h + 1]
        log_f = _log_sigmoid(gates[:, L_FG + h:L_FG + h + 1])
        m_prev = m_all[:, h:h + 1]
        inter = log_f + m_prev
        m_new = jnp.maximum(inter, i_pre)
        w_s = jnp.exp(i_pre - m_new)
        w_prev = jnp.exp(inter - m_new)
        kw = km * w_s
        n_new = n_ref[:, sl] * w_prev + kw
        no_ref[:, sl] = n_new
        m_out = jnp.where(lane4 == h, m_new, m_out)

        k_t = _pad_rows_t(kw).astype(BF16)
        q16 = _pad_rows(qm, 16).astype(BF16)
        upd = _dot(k_t, _blockdiag_rows(v_ref[:, sl]))
        c_wide = []
        for j in range(DEC_BB):
            c_new = c_ref[j, h] * w_prev[j:j + 1, :] + upd[:, j * ML_DH:(j + 1) * ML_DH]
            co_ref[j, h] = c_new
            c_wide.append(c_new.astype(BF16))
        y_all = _dot(q16, jnp.concatenate(c_wide, axis=1))
        for j in range(DEC_BB):
            ybuf[j:j + 1, sl] = y_all[j:j + 1, j * ML_DH:(j + 1) * ML_DH]
        den = jnp.sum(qm * n_new, axis=-1, keepdims=True)
        den = jnp.maximum(jnp.abs(den), jnp.exp(-m_new))
        y_ref[:, sl] = _groupnorm(ybuf[:, sl] / den, nw_ref[:, sl], True) * _sigmoid(o_ref[:, sl])
    mo_ref[...] = m_out


def _ml_decode(proj, small, buf, cw, cb, wq, wk, gb, nw, c_state, n_state, m_state, prev, *, layer):
    nb = proj.shape[0]
    bb = DEC_BB
    nblk = nb // bb
    dh = ML_DH
    w = ML_H * ML_DH
    full2 = lambda i: (0, 0)
    al_args, al_specs, al_map = _state_io(prev, [1])
    cblk = (bb, ML_H, dh, dh)
    return pl.pallas_call(
        _skip_alias(_ml_decode_kernel, len(al_args)),
        grid=(nblk,),
        in_specs=al_specs + [
            pl.BlockSpec((bb, w), lambda i: (i, C_U // w)),
            pl.BlockSpec((bb, w), lambda i: (i, C_VM // w)),
            pl.BlockSpec((bb, w), lambda i: (i, C_OM // w)),
            pl.BlockSpec((bb, N_SMALL), lambda i: (i, 0)),
            pl.BlockSpec((CONV_K - 1, bb, w), lambda i: (0, i, 0)),
            pl.BlockSpec((CONV_K, w), full2),
            pl.BlockSpec((1, w), full2),
            pl.BlockSpec((ML_H, dh, dh), lambda i: (0, 0, 0)),
            pl.BlockSpec((ML_H, dh, dh), lambda i: (0, 0, 0)),
            pl.BlockSpec((1, N_SMALL), full2),
            pl.BlockSpec((1, w), full2),
            pl.BlockSpec(cblk, lambda i: (layer * nblk + i, 0, 0, 0)),
            pl.BlockSpec((bb, w), lambda i: (layer * nblk + i, 0)),
            pl.BlockSpec((bb, ML_H), lambda i: (layer * nblk + i, 0))],
        out_specs=[pl.BlockSpec((bb, w), lambda i: (i, 0)),
                   pl.BlockSpec(cblk, lambda i: (layer * nblk + i, 0, 0, 0)),
                   pl.BlockSpec((bb, w), lambda i: (i, 0)),
                   pl.BlockSpec((bb, ML_H), lambda i: (i, 0)),
                   pl.BlockSpec((CONV_K - 1, bb, w), lambda i: (0, i, 0))],
        out_shape=[jax.ShapeDtypeStruct((nb, w), F32),
                   jax.ShapeDtypeStruct((DEPTH * nb, ML_H, dh, dh), F32),
                   jax.ShapeDtypeStruct((nb, w), F32),
                   jax.ShapeDtypeStruct((nb, ML_H), F32),
                   jax.ShapeDtypeStruct((CONV_K - 1, nb, w), F32)],
        scratch_shapes=[pltpu.VMEM((bb, w), F32)],
        input_output_aliases=al_map,
        compiler_params=_cparams(1),
        name="ml_decode",
    )(*al_args, proj, proj, proj, small, buf, cw, cb, wq, wk, gb, nw, c_state, n_state, m_state)


def _ssd_decode_kernel(z_ref, xs_ref, bc_ref, sm_ref, buf_ref, cw_ref, cb_ref, dtb_ref, alog_ref, dv_ref, nw_ref,
                       s_ref, y_ref, so_ref, bo_ref, ybuf):
    wx = SSM_H * SSM_P
    gw = SSM_R * SSM_P
    x_in = jnp.concatenate([xs_ref[...], bc_ref[...]], axis=1)
    xc = _conv_step(x_in, buf_ref, cw_ref, cb_ref, bo_ref)
    xs = xc[:, 0:wx]

    delta = _softplus(sm_ref[...] + dtb_ref[...])
    d_a = jnp.exp(delta * (-jnp.exp(alog_ref[...])))
    lane_blk = lax.shift_right_logical(lax.broadcasted_iota(jnp.int32, (DEC_BB, wx), 1), 6)
    dt_full = jnp.zeros((DEC_BB, wx), F32)
    for hh in range(SSM_H):
        dt_full = jnp.where(lane_blk == hh, delta[:, L_DT + hh:L_DT + hh + 1], dt_full)
    x_t = _pad_rows_t(xs * dt_full).astype(BF16)

    for g in range(SSM_G):
        gs = slice(g * gw, (g + 1) * gw)
        b_pad = _pad_rows(xc[:, wx + g * SSM_N:wx + (g + 1) * SSM_N], 128)
        c16 = _pad_rows(xc[:, wx + (SSM_G + g) * SSM_N:wx + (SSM_G + g + 1) * SSM_N], 16).astype(BF16)
        for j in range(DEC_BB):
            upd = _dot(x_t[gs, :], _row_mask(b_pad, j))
            parts = []
            for r in range(SSM_R):
                hh = g * SSM_R + r
                s_new = (s_ref[j, hh] * d_a[j:j + 1, L_DT + hh:L_DT + hh + 1]
                         + upd[r * SSM_P:(r + 1) * SSM_P])
                so_ref[j, hh] = s_new
                parts.append(s_new.astype(BF16))
            ybuf[j:j + 1, gs] = _dot_nt(c16, jnp.concatenate(parts, axis=0))[j:j + 1, :]
        y = ybuf[:, gs] + dv_ref[:, gs] * xs[:, gs]
        y_ref[:, gs] = _groupnorm(y * _silu(z_ref[:, gs]), nw_ref[:, gs], False)


def _ssd_decode(proj, small, buf, cw, cb, dtb, alog, dvec, nw, state, prev, *, layer):
    nb = proj.shape[0]
    bb = DEC_BB
    nblk = nb // bb
    wx = SSM_H * SSM_P
    full2 = lambda i: (0, 0)
    al_args, al_specs, al_map = _state_io(prev, [1])
    sblk = (bb, SSM_H, SSM_P, SSM_N)
    return pl.pallas_call(
        _skip_alias(_ssd_decode_kernel, len(al_args)),
        grid=(nblk,),
        in_specs=al_specs + [
            pl.BlockSpec((bb, wx), lambda i: (i, C_Z // wx)),
            pl.BlockSpec((bb, wx), lambda i: (i, C_XS // wx)),
            pl.BlockSpec((bb, wx), lambda i: (i, C_B // wx)),
            pl.BlockSpec((bb, N_SMALL), lambda i: (i, 0)),
            pl.BlockSpec((CONV_K - 1, bb, 2 * wx), lambda i: (0, i, 0)),
            pl.BlockSpec((CONV_K, 2 * wx), full2),
            pl.BlockSpec((1, 2 * wx), full2),
            pl.BlockSpec((1, N_SMALL), full2),
            pl.BlockSpec((1, N_SMALL), full2),
            pl.BlockSpec((1, wx), full2),
            pl.BlockSpec((1, wx), full2),
            pl.BlockSpec(sblk, lambda i: (layer * nblk + i, 0, 0, 0))],
        out_specs=[pl.BlockSpec((bb, wx), lambda i: (i, 0)),
                   pl.BlockSpec(sblk, lambda i: (layer * nblk + i, 0, 0, 0)),
                   pl.BlockSpec((CONV_K - 1, bb, 2 * wx), lambda i: (0, i, 0))],
        out_shape=[jax.ShapeDtypeStruct((nb, wx), F32),
                   jax.ShapeDtypeStruct((DEPTH * nb, SSM_H, SSM_P, SSM_N), F32),
                   jax.ShapeDtypeStruct((CONV_K - 1, nb, 2 * wx), F32)],
        scratch_shapes=[pltpu.VMEM((bb, wx), F32)],
        input_output_aliases=al_map,
        compiler_params=_cparams(1),
        name="ssd_decode",
    )(*al_args, proj, proj, proj, small, buf, cw, cb, dtb, alog, dvec, nw, state)


def _rope_tables(pos):
    half = RET_DK // 2
    freqs = ROPE_THETA ** (-jnp.arange(half, dtype=F32) / half)
    ang = pos.astype(F32)[:, None] * freqs[None, :]
    cos = jnp.cos(ang)
    sin = jnp.sin(ang)
    return jnp.concatenate([cos, cos], axis=1), jnp.concatenate([-sin, sin], axis=1)


def _pad_lanes(v, offset):
    return jnp.zeros((1, N_SMALL), F32).at[0, offset:offset + v.shape[0]].set(v.astype(F32))


def _layer_weights(l, W):
    w_in = W['w_in'][l]
    n_ml0 = 2 * 512 + 2 * 1024
    n_ig = n_ml0 + 3 * 1024
    n_ssm0 = n_ig + 2 * ML_H
    n_dt = n_ssm0 + 1024 + 2048
    n_gate = n_dt + SSM_H
    w_main = jnp.concatenate([w_in[:, :n_ig], w_in[:, n_ssm0:n_dt], w_in[:, n_gate:]], axis=1).astype(BF16)
    w_small = jnp.concatenate([w_in[:, n_ig:n_ssm0], w_in[:, n_dt:n_gate],
                               jnp.zeros((D_MODEL, N_SMALL - 2 * ML_H - SSM_H), F32)], axis=1).astype(BF16)
    row = lambda v: v.reshape(1, -1).astype(F32)
    return dict(
        w_main=w_main, w_small=w_small,
        norm_mix=row(W['norm_mix_w'][l]),
        ret_norm=row(W['ret_norm_w'][l]),
        ml_cw=W['ml_conv_w'][l], ml_cb=row(W['ml_conv_b'][l]),
        ml_wq=W['ml_wq'][l].astype(BF16), ml_wk=W['ml_wk'][l].astype(BF16),
        ml_gb=_pad_lanes(W['ml_gate_b'][l], L_IG),
        ml_norm=row(W['ml_norm_w'][l]),
        ssm_cw=W['ssm_conv_w'][l], ssm_cb=row(W['ssm_conv_b'][l]),
        ssm_dtb=_pad_lanes(W['ssm_dt_bias'][l], L_DT),
        ssm_alog=_pad_lanes(W['ssm_A_log'][l], L_DT),
        ssm_dvec=row(jnp.repeat(W['ssm_D'][l], SSM_P)),
        ssm_norm=row(W['ssm_norm_w'][l]),
        w_br_ret=W['w_br_ret'][l].astype(BF16), w_br_ml=W['w_br_ml'][l].astype(BF16),
        w_br_ssm=W['w_br_ssm'][l].astype(BF16), w_out=W['w_out_mix'][l].astype(BF16),
        norm_mem=row(W['norm_mem_w'][l]),
        mem_wq=W['mem_wq'][l].astype(BF16), mem_wo=W['mem_wo'][l].astype(BF16),
        norm_mlp=row(W['norm_mlp_w'][l]),
        mlp_w1=W['mlp_w1'][l].astype(BF16), mlp_w2=W['mlp_w2'][l].astype(BF16),
    )


def kernel(x_prompt, x_sample, mem_prompt, state_ret, state_mlstm_C, state_mlstm_n, state_mlstm_m,
           state_mlstm_conv, state_ssm, state_ssm_conv, cache_mem_k, cache_mem_v,
           norm_mix_w, w_in, ret_norm_w, ml_conv_w, ml_conv_b, ml_wq, ml_wk, ml_gate_b, ml_norm_w,
           ssm_conv_w, ssm_conv_b, ssm_dt_bias, ssm_A_log, ssm_D, ssm_norm_w,
           w_br_ret, w_br_ml, w_br_ssm, w_out_mix, norm_mem_w, mem_wq, mem_wk, mem_wv, mem_wo,
           norm_mlp_w, mlp_w1, mlp_w2, norm_f_w):
    W = dict(norm_mix_w=norm_mix_w, w_in=w_in, ret_norm_w=ret_norm_w, ml_conv_w=ml_conv_w,
             ml_conv_b=ml_conv_b, ml_wq=ml_wq, ml_wk=ml_wk, ml_gate_b=ml_gate_b, ml_norm_w=ml_norm_w,
             ssm_conv_w=ssm_conv_w, ssm_conv_b=ssm_conv_b, ssm_dt_bias=ssm_dt_bias, ssm_A_log=ssm_A_log,
             ssm_D=ssm_D, ssm_norm_w=ssm_norm_w, w_br_ret=w_br_ret, w_br_ml=w_br_ml, w_br_ssm=w_br_ssm,
             w_out_mix=w_out_mix, norm_mem_w=norm_mem_w, mem_wq=mem_wq, mem_wo=mem_wo,
             norm_mlp_w=norm_mlp_w, mlp_w1=mlp_w1, mlp_w2=mlp_w2)
    LW = [_layer_weights(l, W) for l in range(DEPTH)]
    norm_f = norm_f_w.reshape(1, -1).astype(F32)
    d = D_MODEL

    nb, seq = x_prompt.shape[0], x_prompt.shape[1]
    cos_p, sin_p = _rope_tables(jnp.arange(seq, dtype=jnp.int32))
    memk, memv = _memkv(mem_prompt.reshape(nb * MEM_LEN, d), mem_wk.astype(BF16), mem_wv.astype(BF16), tm=1024)
    memk3 = memk.reshape(DEPTH * nb, MEM_LEN, d)
    memv3 = memv.reshape(DEPTH * nb, MEM_LEN, d)

    x = x_prompt.reshape(nb * seq, d)
    ret_p = mlc_p = ssm_p = None
    small_p = []
    for l in range(DEPTH):
        lw = LW[l]
        proj, small = _inproj(x, lw['norm_mix'], lw['w_main'], lw['w_small'], tm=2048, tn=1024, out_dtype=BF16)
        y_r, ret_p = _ret_prefill(proj, cos_p, sin_p, lw['ret_norm'], None if l == 0 else [ret_p],
                                  layer=l, nb=nb, seq=seq)
        y_m, mlc_p, ml_n, ml_m, ml_cv = _ml_prefill(proj, small, lw['ml_cw'], lw['ml_cb'], lw['ml_wq'], lw['ml_wk'],
                                                    lw['ml_gb'], lw['ml_norm'], None if l == 0 else [mlc_p],
                                                    layer=l, nb=nb, seq=seq)
        y_s, ssm_p, ssm_cv = _ssd_prefill(proj, small, lw['ssm_cw'], lw['ssm_cb'], lw['ssm_dtb'], lw['ssm_alog'],
                                          lw['ssm_dvec'], lw['ssm_norm'], None if l == 0 else [ssm_p],
                                          layer=l, nb=nb, seq=seq)
        x, qm = _merge(y_r.reshape(nb * seq, d), y_m.reshape(nb * seq, d), y_s.reshape(nb * seq, d), proj, x,
                       lw['w_br_ret'], lw['w_br_ml'], lw['w_br_ssm'], lw['w_out'], lw['norm_mem'], lw['mem_wq'],
                       tm=512, q_dtype=BF16)
        x = _attn_prefill(qm, memk3, memv3, x, lw['mem_wo'], layer=l, nb=nb, seq=seq, tq=512)
        x = _mlp(x, lw['norm_mlp'], lw['mlp_w1'], lw['mlp_w2'], norm_f, tm=1024, tf=2048,
                 final_norm=(l == DEPTH - 1))
        small_p.append((ml_n, ml_m[:, 0, :ML_H], ml_cv, ssm_cv))
    y_prompt = x.reshape(nb, seq, d)
    stack = lambda parts, i: jnp.stack([p[i] for p in parts])
    prompt_states = (ret_p.reshape(DEPTH, nb, RET_H, RET_DK, RET_DV),
                     mlc_p.reshape(DEPTH, nb, ML_H, ML_DH, ML_DH),
                     stack(small_p, 0), stack(small_p, 1), stack(small_p, 2),
                     ssm_p.reshape(DEPTH, nb, SSM_H, SSM_P, SSM_N),
                     stack(small_p, 3))
    memk_p = memk.reshape(DEPTH, nb, MEM_LEN, MEM_H, MEM_DH)
    memv_p = memv.reshape(DEPTH, nb, MEM_LEN, MEM_H, MEM_DH)

    ns = x_sample.shape[0]
    cos_s, sin_s = _rope_tables(PAST_LEN + jnp.arange(1, dtype=jnp.int32))
    ret_state = state_ret.reshape(DEPTH * ns, RET_H, RET_DK, RET_DV)
    mlc_state = state_mlstm_C.reshape(DEPTH * ns, ML_H, ML_DH, ML_DH)
    mln_state = state_mlstm_n.reshape(DEPTH * ns, ML_H * ML_DH)
    mlm_state = state_mlstm_m.reshape(DEPTH * ns, ML_H)
    ssm_st = state_ssm.reshape(DEPTH * ns, SSM_H, SSM_P, SSM_N)
    ck = cache_mem_k.reshape(DEPTH * ns, MEM_LEN, MEM_H, MEM_DH)
    cv = cache_mem_v.reshape(DEPTH * ns, MEM_LEN, MEM_H, MEM_DH)

    x = x_sample.reshape(ns, d)
    ret_s = mlc_s = ssm_s = None
    small_s = []
    for l in range(DEPTH):
        lw = LW[l]
        proj, small = _inproj(x, lw['norm_mix'], lw['w_main'], lw['w_small'], tm=ns, tn=1024, out_dtype=F32)
        y_r, ret_s = _ret_decode(proj, cos_s, sin_s, lw['ret_norm'], ret_state, None if l == 0 else [ret_s],
                                 layer=l)
        ml_buf = jnp.swapaxes(state_mlstm_conv[l], 0, 1)
        y_m, mlc_s, ml_n, ml_m, ml_bo = _ml_decode(proj, small, ml_buf, lw['ml_cw'], lw['ml_cb'], lw['ml_wq'],
                                                   lw['ml_wk'], lw['ml_gb'], lw['ml_norm'],
                                                   mlc_state, mln_state, mlm_state, None if l == 0 else [mlc_s],
                                                   layer=l)
        ssm_buf = jnp.swapaxes(state_ssm_conv[l], 0, 1)
        y_s, ssm_s, ssm_bo = _ssd_decode(proj, small, ssm_buf, lw['ssm_cw'], lw['ssm_cb'], lw['ssm_dtb'],
                                         lw['ssm_alog'], lw['ssm_dvec'], lw['ssm_norm'], ssm_st,
                                         None if l == 0 else [ssm_s], layer=l)
        x, qm = _merge(y_r, y_m, y_s, proj, x, lw['w_br_ret'], lw['w_br_ml'], lw['w_br_ssm'], lw['w_out'],
                       lw['norm_mem'], lw['mem_wq'], tm=ns, q_dtype=F32)
        att = _attn_decode(qm.reshape(ns, MEM_H, MEM_DH), ck, cv, layer=l, bb=4)
        x = _mm(att.reshape(ns, d), lw['mem_wo'], x, tm=ns, tn=d)
        x = _mlp(x, lw['norm_mlp'], lw['mlp_w1'], lw['mlp_w2'], norm_f, tm=ns, tf=2048,
                 final_norm=(l == DEPTH - 1))
        small_s.append((ml_n.reshape(ns, ML_H, ML_DH), ml_m, jnp.swapaxes(ml_bo, 0, 1),
                        jnp.swapaxes(ssm_bo, 0, 1)))
    y_sample = x.reshape(ns, 1, d)
    sample_states = (ret_s.reshape(DEPTH, ns, RET_H, RET_DK, RET_DV),
                     mlc_s.reshape(DEPTH, ns, ML_H, ML_DH, ML_DH),
                     stack(small_s, 0), stack(small_s, 1), stack(small_s, 2),
                     ssm_s.reshape(DEPTH, ns, SSM_H, SSM_P, SSM_N),
                     stack(small_s, 3))

    return (y_prompt, y_sample, *prompt_states, memk_p, memv_p, *sample_states)
```

```python
import functools
import math

import jax
import jax.numpy as jnp
from jax import lax
from jax.experimental import pallas as pl
from jax.experimental.pallas import tpu as pltpu

F32 = jnp.float32
BF16 = jnp.bfloat16

D_MODEL = 1024
DEPTH = 2
PAST_LEN = 16384
CHUNK = 128
CONV_K = 4
EPS = 1e-6
RET_H, RET_DK, RET_DV = 4, 128, 256
ROPE_THETA = 10000.0
ML_H, ML_DH = 4, 256
SSM_H, SSM_P, SSM_G, SSM_N = 16, 64, 4, 128
SSM_R = SSM_H // SSM_G
MEM_LEN, MEM_H, MEM_DH = 256, 4, 256
D_FF = 4 * D_MODEL

C_Q, C_K, C_V, C_G = 0, 512, 1024, 2048
C_U, C_VM, C_OM = 3072, 4096, 5120
C_Z, C_XS, C_B, C_C = 6144, 7168, 8192, 8704
C_GR, C_GM, C_GS = 9216, 10240, 11264
N_MAIN = 12288
L_IG, L_FG, L_DT = 0, 4, 8
N_SMALL = 128

VMEM_LIMIT = 56 * 1024 * 1024
NT_DIMS = (((1,), (1,)), ((), ()))

_LOG_G = [math.log1p(-(2.0 ** (-5.0 - h))) for h in range(RET_H)]


def _cparams(n_axes):
    return pltpu.CompilerParams(dimension_semantics=("arbitrary",) * n_axes,
                                vmem_limit_bytes=VMEM_LIMIT)


def _dot(a, b):
    return jnp.dot(a, b, preferred_element_type=F32)


def _dot_nt(a, b):
    return lax.dot_general(a, b, NT_DIMS, preferred_element_type=F32)


def _rmsnorm(x, w):
    ms = jnp.mean(x * x, axis=-1, keepdims=True)
    return x * lax.rsqrt(ms + EPS) * w


def _groupnorm(x, w, center):
    if center:
        x = x - jnp.mean(x, axis=-1, keepdims=True)
    ms = jnp.mean(x * x, axis=-1, keepdims=True)
    return x * lax.rsqrt(ms + EPS) * w


def _sigmoid(x):
    return 1.0 / (1.0 + jnp.exp(-x))


def _silu(x):
    return x * _sigmoid(x)


def _softplus(x):
    return jnp.maximum(x, 0.0) + jnp.log1p(jnp.exp(-jnp.abs(x)))


def _log_sigmoid(x):
    return -_softplus(-x)


def _cumsum_rows(x, tril_b):
    hi = x.astype(BF16)
    r1 = x - hi.astype(F32)
    mid = r1.astype(BF16)
    lo = (r1 - mid.astype(F32)).astype(BF16)
    return _dot(tril_b, hi) + _dot(tril_b, mid) + _dot(tril_b, lo)


def _pad_rows(x, rows):
    return jnp.concatenate([x, jnp.zeros((rows - x.shape[0], x.shape[1]), x.dtype)], axis=0)


def _pad_rows_t(x):
    return _pad_rows(x, 128).T


def _stacked_state_call(kernel_fn, state_out, nblk, layer, prev):
    if layer == 0:
        def with_zero_tail(*refs):
            step = pl.program_id(0)

            @pl.when(step < nblk)
            def _():
                kernel_fn(*refs)

            @pl.when(step >= nblk)
            def _():
                refs[state_out][...] = jnp.zeros_like(refs[state_out])

        return (with_zero_tail, 2 * nblk, lambda i: jnp.minimum(i, nblk - 1), lambda i: i, [], [], {})
    skip = lambda *refs: kernel_fn(*refs[1:])
    return (skip, nblk, lambda i: i, lambda i: layer * nblk + i, [prev], [pl.BlockSpec(memory_space=pl.ANY)],
            {0: 1})


def _inproj_kernel(x_ref, nw_ref, w_ref, ws_ref, o_ref, os_ref, xn_ref):
    @pl.when(pl.program_id(1) == 0)
    def _():
        xn = _rmsnorm(x_ref[...], nw_ref[...]).astype(BF16)
        xn_ref[...] = xn
        os_ref[...] = _dot(xn, ws_ref[...])

    o_ref[...] = _dot(xn_ref[...], w_ref[...]).astype(o_ref.dtype)


def _lsel(layer, *block, **kw):
    return pl.BlockSpec((None,) + block, lambda *_: (layer,) + (0,) * len(block), **kw)


def _inproj(x, nw, w_main, w_small, *, layer, tm, tn, out_dtype):
    m = x.shape[0]
    return pl.pallas_call(
        _inproj_kernel,
        grid=(m // tm, N_MAIN // tn),
        in_specs=[pl.BlockSpec((tm, D_MODEL), lambda i, j: (i, 0)),
                  _lsel(layer, 1, D_MODEL),
                  pl.BlockSpec((None, D_MODEL, tn), lambda i, j: (layer, 0, j)),
                  _lsel(layer, D_MODEL, N_SMALL)],
        out_specs=[pl.BlockSpec((tm, tn), lambda i, j: (i, j)),
                   pl.BlockSpec((tm, N_SMALL), lambda i, j: (i, 0))],
        out_shape=[jax.ShapeDtypeStruct((m, N_MAIN), out_dtype),
                   jax.ShapeDtypeStruct((m, N_SMALL), F32)],
        scratch_shapes=[pltpu.VMEM((tm, D_MODEL), BF16)],
        compiler_params=_cparams(2),
        name="inproj",
    )(x, nw, w_main, w_small)


def _mm_kernel(*refs, has_res):
    if has_res:
        a_ref, w_ref, r_ref, o_ref = refs
    else:
        a_ref, w_ref, o_ref = refs
    acc = _dot(a_ref[...].astype(BF16), w_ref[...])
    if has_res:
        acc = r_ref[...] + acc
    o_ref[...] = acc.astype(o_ref.dtype)


def _mm(a, w, res=None, *, layer, tm, tn, out_dtype=F32):
    m, k = a.shape
    n = w.shape[2]
    in_specs = [pl.BlockSpec((tm, k), lambda i, j: (i, 0)),
                pl.BlockSpec((None, k, tn), lambda i, j: (layer, 0, j))]
    args = [a, w]
    if res is not None:
        in_specs.append(pl.BlockSpec((tm, tn), lambda i, j: (i, j)))
        args.append(res)
    return pl.pallas_call(
        functools.partial(_mm_kernel, has_res=res is not None),
        grid=(m // tm, n // tn),
        in_specs=in_specs,
        out_specs=pl.BlockSpec((tm, tn), lambda i, j: (i, j)),
        out_shape=jax.ShapeDtypeStruct((m, n), out_dtype),
        compiler_params=_cparams(2),
        name="mm",
    )(*args)


def _memkv_kernel(a_ref, wk_ref, wv_ref, ok_ref, ov_ref):
    a = a_ref[...].astype(BF16)
    ok_ref[0] = _dot(a, wk_ref[0])
    ov_ref[0] = _dot(a, wv_ref[0])


def _memkv(a, wk, wv, *, tm):
    m = a.shape[0]
    n = wk.shape[2]
    return pl.pallas_call(
        _memkv_kernel,
        grid=(DEPTH, m // tm),
        in_specs=[pl.BlockSpec((tm, D_MODEL), lambda l, i: (i, 0)),
                  pl.BlockSpec((1, D_MODEL, n), lambda l, i: (l, 0, 0)),
                  pl.BlockSpec((1, D_MODEL, n), lambda l, i: (l, 0, 0))],
        out_specs=[pl.BlockSpec((1, tm, n), lambda l, i: (l, i, 0)),
                   pl.BlockSpec((1, tm, n), lambda l, i: (l, i, 0))],
        out_shape=[jax.ShapeDtypeStruct((DEPTH, m, n), F32)] * 2,
        compiler_params=_cparams(2),
        name="memkv",
    )(a, wk, wv)


class _Ctx(dict):
    __getattr__ = dict.__getitem__
    __setattr__ = dict.__setitem__


def _seq_loop(nb, per_trip, seq_phases, n_items, item_phases, finish):
    def body(i, carry):
        seqs = [_Ctx(b=i * per_trip + u) for u in range(per_trip)]
        for phase in seq_phases:
            for s in seqs:
                phase(s)
        items = []
        for s in seqs:
            s.heads = [_Ctx(seq=s, b=s.b, k=k) for k in range(n_items)]
            items.extend(s.heads)
        for phase in item_phases:
            for it in items:
                phase(it)
        for s in seqs:
            finish(s)
        return carry

    lax.fori_loop(0, nb // per_trip, body, 0)


def _ret_prefill_kernel(q_ref, k_ref, v_ref, g_ref, cos_ref, sin_ref, nw_ref, y_ref, s_ref):
    @pl.when(pl.program_id(0) == 0)
    def _():
        s_ref[...] = jnp.zeros_like(s_ref)

    c = CHUNK
    nb = q_ref.shape[0]
    cosf = cos_ref[...]
    sinf = sin_ref[...]
    row = lax.broadcasted_iota(jnp.int32, (c, c), 0)
    col = lax.broadcasted_iota(jnp.int32, (c, c), 1)
    causal = row >= col
    diff = jnp.where(causal, (row - col).astype(F32), 0.0)
    rowf = row.astype(F32)
    rowf_v = lax.broadcasted_iota(jnp.int32, (c, RET_DV), 0).astype(F32)
    decay = [jnp.where(causal, jnp.exp(lg * diff), 0.0) for lg in _LOG_G]
    q_decay = [jnp.exp(lg * (rowf_v + 1.0)) for lg in _LOG_G]
    k_decay = [jnp.exp(lg * (c - 1.0 - rowf)) * (RET_DK ** -0.5) for lg in _LOG_G]

    def load(s):
        s.s_old = [s_ref[s.b, h] for h in range(RET_H)]

    def rotate(it):
        h = it.k
        sk = slice(h * RET_DK, (h + 1) * RET_DK)
        q = q_ref[it.b, :, sk].astype(F32)
        k = k_ref[it.b, :, sk].astype(F32)
        qr = q * cosf + pltpu.roll(q, RET_DK // 2, 1) * sinf
        kr = k * cosf + pltpu.roll(k, RET_DK // 2, 1) * sinf
        it.qb = qr.astype(BF16)
        it.kb = (kr * (RET_DK ** -0.5)).astype(BF16)
        it.kt = (kr * k_decay[h]).T.astype(BF16)
        it.v = v_ref[it.b, :, h * RET_DV:(h + 1) * RET_DV]

    def scores(it):
        it.sc = (_dot_nt(it.qb, it.kb) * decay[it.k]).astype(BF16)

    def mix(it):
        h = it.k
        s_old = it.seq.s_old[h]
        it.y = _dot(it.sc, it.v) + _dot(it.qb, s_old.astype(BF16)) * q_decay[h]
        it.s_new = s_old * math.exp(_LOG_G[h] * c) + _dot(it.kt, it.v)

    def emit(it):
        sv = slice(it.k * RET_DV, (it.k + 1) * RET_DV)
        g = g_ref[it.b, :, sv].astype(F32)
        yn = _groupnorm(it.y, nw_ref[:, sv], True) * _silu(g)
        y_ref[it.b, :, sv] = yn.astype(y_ref.dtype)

    def finish(s):
        for it in s.heads:
            s_ref[s.b, it.k] = it.s_new

    _seq_loop(nb, 2, [load], RET_H, [rotate, scores, mix, emit], finish)


def _ret_prefill(proj, cosf, sinf, nw, *, layer, nb, seq):
    c = CHUNK
    p3 = proj.reshape(nb, seq, N_MAIN)
    return pl.pallas_call(
        _ret_prefill_kernel,
        grid=(seq // c,),
        in_specs=[
            pl.BlockSpec((nb, c, 512), lambda t: (0, t, C_Q // 512)),
            pl.BlockSpec((nb, c, 512), lambda t: (0, t, C_K // 512)),
            pl.BlockSpec((nb, c, 1024), lambda t: (0, t, C_V // 1024)),
            pl.BlockSpec((nb, c, 1024), lambda t: (0, t, C_G // 1024)),
            pl.BlockSpec((c, RET_DK), lambda t: (t, 0)),
            pl.BlockSpec((c, RET_DK), lambda t: (t, 0)),
            _lsel(layer, 1, 1024)],
        out_specs=[pl.BlockSpec((nb, c, 1024), lambda t: (0, t, 0)),
                   pl.BlockSpec((nb, RET_H, RET_DK, RET_DV), lambda t: (0, 0, 0, 0))],
        out_shape=[jax.ShapeDtypeStruct((nb, seq, 1024), BF16),
                   jax.ShapeDtypeStruct((nb, RET_H, RET_DK, RET_DV), F32)],
        compiler_params=_cparams(1),
        name="ret_prefill",
    )(p3, p3, p3, p3, cosf, sinf, nw)


def _shift_matrix(first_chunk):
    c = CHUNK
    r = lax.broadcasted_iota(jnp.int32, (3 * c, 2 * c), 0)
    q = lax.broadcasted_iota(jnp.int32, (3 * c, 2 * c), 1)
    k = lax.shift_right_logical(r, 7)
    t = jnp.bitwise_and(r, c - 1)
    hit = (q == c + t - (CONV_K - 1) + k) & (q >= jnp.where(first_chunk, c, 0))
    return jnp.where(hit, 1.0, 0.0).astype(BF16)


def _conv_silu(shift, x_prev, x_cur, cw, cb):
    c = CHUNK
    p = _dot(shift, jnp.concatenate([x_prev, x_cur], axis=0))
    acc = cb
    for j in range(CONV_K - 1):
        acc = acc + p[j * c:(j + 1) * c] * cw[j:j + 1, :]
    acc = acc + x_cur.astype(F32) * cw[CONV_K - 1:CONV_K, :]
    return _silu(acc)


def _ml_prefill_kernel(u_ref, up_ref, v_ref, o_ref, sm_ref, cw_ref, cb_ref, wq_ref, wk_ref, gb_ref, nw_ref,
                       y_ref, c_ref, n_ref, m_ref, cv_ref, tail):
    c = CHUNK
    nb = u_ref.shape[0]
    first = pl.program_id(0) == 0

    @pl.when(first)
    def _():
        c_ref[...] = jnp.zeros_like(c_ref)
        n_ref[...] = jnp.zeros_like(n_ref)
        m_ref[...] = jnp.zeros_like(m_ref)

    row = lax.broadcasted_iota(jnp.int32, (c, c), 0)
    col = lax.broadcasted_iota(jnp.int32, (c, c), 1)
    causal = row >= col
    tril_b = jnp.where(causal, 1.0, 0.0).astype(BF16)
    lane_m = lax.broadcasted_iota(jnp.int32, (1, 128), 1)
    shift = _shift_matrix(first)

    def load(s):
        s.c_old = [c_ref[s.b, h] for h in range(ML_H)]
        s.n_old = n_ref[s.b]
        s.m_row = m_ref[s.b]

    def gates_phase(s):
        tail[...] = u_ref[s.b, c - 16:c, :].astype(F32)
        cv_ref[s.b] = tail[16 - (CONV_K - 1):16, :]
        s.gates = sm_ref[s.b] + gb_ref[...]
        s.b_all = _cumsum_rows(_log_sigmoid(s.gates), tril_b)
        s.b_all_t = s.b_all.T
        s.gates_t = s.gates.T

    def conv_phase(it):
        sl = slice(it.k * ML_DH, (it.k + 1) * ML_DH)
        it.uc = _conv_silu(shift, up_ref[it.b, :, sl], u_ref[it.b, :, sl], cw_ref[:, sl], cb_ref[:, sl]).astype(BF16)
        it.v = v_ref[it.b, :, sl]

    def weights_phase(it):
        h, s = it.k, it.seq
        b_col = s.b_all[:, L_FG + h:L_FG + h + 1]
        b_row = s.b_all_t[L_FG + h:L_FG + h + 1, :]
        it_col = s.gates[:, L_IG + h:L_IG + h + 1]
        it_row = s.gates_t[L_IG + h:L_IG + h + 1, :]
        m_prev = s.m_row[:, h:h + 1]
        logw = jnp.where(causal, b_col - b_row + it_row, -jnp.inf)
        inter = b_col + m_prev
        it.m_t = jnp.maximum(inter, jnp.max(logw, axis=-1, keepdims=True))
        it.dmat = jnp.exp(logw - it.m_t)
        it.w_int = jnp.exp(inter - it.m_t)
        b_end = b_col[c - 1:c, :]
        logw_s = b_end - b_col + it_col
        it.m_new = jnp.maximum(b_end + m_prev, jnp.max(logw_s, axis=0, keepdims=True))
        it.w_s = jnp.exp(logw_s - it.m_new)
        it.w_prev = jnp.exp(b_end + m_prev - it.m_new)

    def qk_phase(it):
        it.qm = _dot(it.uc, wq_ref[it.k])
        it.km = _dot(it.uc, wk_ref[it.k]) * (ML_DH ** -0.5)
        it.qb = it.qm.astype(BF16)
        it.kb = it.km.astype(BF16)
        it.kw = it.km * it.w_s
        it.kwt = it.kw.T.astype(BF16)

    def scores_phase(it):
        it.sc = _dot_nt(it.qb, it.kb) * it.dmat

    def mix_phase(it):
        c_old = it.seq.c_old[it.k]
        n_old = it.seq.n_old[it.k:it.k + 1, :]
        it.num = _dot(it.sc.astype(BF16), it.v) + _dot(it.qb, c_old.astype(BF16)) * it.w_int
        it.c_new = c_old * it.w_prev + _dot(it.kwt, it.v)
        it.n_new = n_old * it.w_prev + jnp.sum(it.kw, axis=0, keepdims=True)
        den = jnp.sum(it.sc, axis=-1, keepdims=True) + jnp.sum(it.qm * n_old, axis=-1, keepdims=True) * it.w_int
        it.den = jnp.maximum(jnp.abs(den), jnp.exp(-it.m_t))

    def emit_phase(it):
        sl = slice(it.k * ML_DH, (it.k + 1) * ML_DH)
        og = o_ref[it.b, :, sl].astype(F32)
        yn = _groupnorm(it.num / it.den, nw_ref[:, sl], True) * _sigmoid(og)
        y_ref[it.b, :, sl] = yn.astype(y_ref.dtype)

    def finish(s):
        m_row_new = s.m_row
        for it in s.heads:
            c_ref[s.b, it.k] = it.c_new
            n_ref[s.b, it.k:it.k + 1, :] = it.n_new
            m_row_new = jnp.where(lane_m == it.k, it.m_new, m_row_new)
        m_ref[s.b] = m_row_new

    _seq_loop(nb, 1, [load, gates_phase], ML_H,
              [conv_phase, weights_phase, qk_phase, scores_phase, mix_phase, emit_phase], finish)


def _ml_prefill(proj, small, cw, cb, wq, wk, gb, nw, *, layer, nb, seq):
    c = CHUNK
    p3 = proj.reshape(nb, seq, N_MAIN)
    s3 = small.reshape(nb, seq, N_SMALL)
    w = ML_H * ML_DH
    full2 = lambda t: (0, 0)
    return pl.pallas_call(
        _ml_prefill_kernel,
        grid=(seq // c,),
        in_specs=[
            pl.BlockSpec((nb, c, w), lambda t: (0, t, C_U // w)),
            pl.BlockSpec((nb, c, w), lambda t: (0, jnp.maximum(t - 1, 0), C_U // w)),
            pl.BlockSpec((nb, c, w), lambda t: (0, t, C_VM // w)),
            pl.BlockSpec((nb, c, w), lambda t: (0, t, C_OM // w)),
            pl.BlockSpec((nb, c, N_SMALL), lambda t: (0, t, 0)),
            _lsel(layer, CONV_K, w),
            _lsel(layer, 1, w),
            _lsel(layer, ML_H, ML_DH, ML_DH),
            _lsel(layer, ML_H, ML_DH, ML_DH),
            _lsel(layer, 1, N_SMALL),
            _lsel(layer, 1, w)],
        out_specs=[pl.BlockSpec((nb, c, w), lambda t: (0, t, 0)),
                   pl.BlockSpec((nb, ML_H, ML_DH, ML_DH), lambda t: (0, 0, 0, 0)),
                   pl.BlockSpec((nb, ML_H, ML_DH), lambda t: (0, 0, 0)),
                   pl.BlockSpec((nb, 1, 128), lambda t: (0, 0, 0)),
                   pl.BlockSpec((nb, CONV_K - 1, w), lambda t: (0, 0, 0))],
        out_shape=[jax.ShapeDtypeStruct((nb, seq, w), BF16),
                   jax.ShapeDtypeStruct((nb, ML_H, ML_DH, ML_DH), F32),
                   jax.ShapeDtypeStruct((nb, ML_H, ML_DH), F32),
                   jax.ShapeDtypeStruct((nb, 1, 128), F32),
                   jax.ShapeDtypeStruct((nb, CONV_K - 1, w), F32)],
        scratch_shapes=[pltpu.VMEM((16, w), F32)],
        compiler_params=_cparams(1),
        name="ml_prefill",
    )(p3, p3, p3, p3, s3, cw, cb, wq, wk, gb, nw)


def _block_bcast(tile, lanes):
    c = tile.shape[0]
    lane = lax.broadcasted_iota(jnp.int32, (c, 128), 1)
    cols = [jnp.broadcast_to(tile[:, l:l + 1], (c, 128)) for l in lanes]
    left = jnp.where(lane < SSM_P, cols[0], cols[1])
    right = jnp.where(lane < SSM_P, cols[2], cols[3])
    return jnp.concatenate([left, right], axis=1)


def _ssd_prefill_kernel(z_ref, xs_ref, xsp_ref, bc_ref, bcp_ref, sm_ref, cw_ref, cb_ref, dtb_ref, alog_ref,
                        dv_ref, nw_ref, y_ref, s_ref, cv_ref, tail):
    c = CHUNK
    nb = z_ref.shape[0]
    wx = SSM_H * SSM_P
    gw = SSM_R * SSM_P
    first = pl.program_id(0) == 0

    @pl.when(first)
    def _():
        s_ref[...] = jnp.zeros_like(s_ref)

    row = lax.broadcasted_iota(jnp.int32, (c, c), 0)
    col = lax.broadcasted_iota(jnp.int32, (c, c), 1)
    causal = row >= col
    tril_b = jnp.where(causal, 1.0, 0.0).astype(BF16)
    lane_blk = lax.shift_right_logical(lax.broadcasted_iota(jnp.int32, (c, gw), 1), 6)
    a_row = -jnp.exp(alog_ref[...])
    shift = _shift_matrix(first)

    def conv(cur_ref, prv_ref, b, lanes, w_off):
        wl = slice(w_off + lanes.start, w_off + lanes.stop)
        return _conv_silu(shift, prv_ref[b, :, lanes], cur_ref[b, :, lanes], cw_ref[:, wl], cb_ref[:, wl])

    def load(s):
        s.s_old = [s_ref[s.b, g * SSM_R:(g + 1) * SSM_R].reshape(gw, SSM_N) for g in range(SSM_G)]

    def dt_phase(s):
        b = s.b
        tail[:, 0:wx] = xs_ref[b, c - 16:c, :].astype(F32)
        tail[:, wx:2 * wx] = bc_ref[b, c - 16:c, :].astype(F32)
        cv_ref[b] = tail[16 - (CONV_K - 1):16, :]
        s.delta = _softplus(sm_ref[b] + dtb_ref[...])
        s.cum = _cumsum_rows(s.delta * a_row, tril_b)
        s.cum_t = s.cum.T
        s.delta_t = s.delta.T
        s.ecum = jnp.exp(s.cum)
        cum_end = s.cum[c - 1:c, :]
        s.w_state = jnp.exp(cum_end - s.cum) * s.delta
        s.dec_row = jnp.exp(cum_end)

    def conv_phase(it):
        g = it.k
        it.lanes = [L_DT + g * SSM_R + r for r in range(SSM_R)]
        it.xg = conv(xs_ref, xsp_ref, it.b, slice(g * gw, (g + 1) * gw), 0)
        it.bg = conv(bc_ref, bcp_ref, it.b, slice(g * SSM_N, (g + 1) * SSM_N), wx).astype(BF16)
        it.cg = conv(bc_ref, bcp_ref, it.b, slice((SSM_G + g) * SSM_N, (SSM_G + g + 1) * SSM_N), wx).astype(BF16)

    def seg_phase(it):
        s = it.seq
        it.seg = [jnp.exp(jnp.where(causal, s.cum[:, ln:ln + 1] - s.cum_t[ln:ln + 1, :], -jnp.inf))
                  * s.delta_t[ln:ln + 1, :] for ln in it.lanes]
        it.x_stack = jnp.concatenate([jnp.where(lane_blk == r, it.xg, 0.0).astype(BF16) for r in range(SSM_R)],
                                     axis=0)
        it.xwt = (it.xg * _block_bcast(s.w_state, it.lanes)).T.astype(BF16)

    def cb_phase(it):
        it.cb_mat = _dot_nt(it.cg, it.bg)

    def mix_phase(it):
        s = it.seq
        s_old = s.s_old[it.k]
        m_wide = jnp.concatenate([(it.cb_mat * sg).astype(BF16) for sg in it.seg], axis=1)
        it.y = (_dot(m_wide, it.x_stack)
                + _dot_nt(it.cg, s_old.astype(BF16)) * _block_bcast(s.ecum, it.lanes))
        upd = _dot(it.xwt, it.bg)
        it.s_new = [s_old[r * SSM_P:(r + 1) * SSM_P] * s.dec_row[:, ln:ln + 1] + upd[r * SSM_P:(r + 1) * SSM_P]
                    for r, ln in enumerate(it.lanes)]

    def emit_phase(it):
        gs = slice(it.k * gw, (it.k + 1) * gw)
        y = it.y + dv_ref[:, gs] * it.xg
        z = z_ref[it.b, :, gs].astype(F32)
        yn = _groupnorm(y * _silu(z), nw_ref[:, gs], False)
        y_ref[it.b, :, gs] = yn.astype(y_ref.dtype)

    def finish(s):
        for it in s.heads:
            for r in range(SSM_R):
                s_ref[s.b, it.k * SSM_R + r] = it.s_new[r]

    _seq_loop(nb, 2, [load, dt_phase], SSM_G, [conv_phase, seg_phase, cb_phase, mix_phase, emit_phase], finish)


def _ssd_prefill(proj, small, cw, cb, dtb, alog, dvec, nw, *, layer, nb, seq):
    c = CHUNK
    wx = SSM_H * SSM_P
    p3 = proj.reshape(nb, seq, N_MAIN)
    s3 = small.reshape(nb, seq, N_SMALL)
    full2 = lambda t: (0, 0)
    return pl.pallas_call(
        _ssd_prefill_kernel,
        grid=(seq // c,),
        in_specs=[
            pl.BlockSpec((nb, c, wx), lambda t: (0, t, C_Z // wx)),
            pl.BlockSpec((nb, c, wx), lambda t: (0, t, C_XS // wx)),
            pl.BlockSpec((nb, c, wx), lambda t: (0, jnp.maximum(t - 1, 0), C_XS // wx)),
            pl.BlockSpec((nb, c, wx), lambda t: (0, t, C_B // wx)),
            pl.BlockSpec((nb, c, wx), lambda t: (0, jnp.maximum(t - 1, 0), C_B // wx)),
            pl.BlockSpec((nb, c, N_SMALL), lambda t: (0, t, 0)),
            _lsel(layer, CONV_K, 2 * wx),
            _lsel(layer, 1, 2 * wx),
            _lsel(layer, 1, N_SMALL),
            _lsel(layer, 1, N_SMALL),
            _lsel(layer, 1, wx),
            _lsel(layer, 1, wx)],
        out_specs=[pl.BlockSpec((nb, c, wx), lambda t: (0, t, 0)),
                   pl.BlockSpec((nb, SSM_H, SSM_P, SSM_N), lambda t: (0, 0, 0, 0)),
                   pl.BlockSpec((nb, CONV_K - 1, 2 * wx), lambda t: (0, 0, 0))],
        out_shape=[jax.ShapeDtypeStruct((nb, seq, wx), BF16),
                   jax.ShapeDtypeStruct((nb, SSM_H, SSM_P, SSM_N), F32),
                   jax.ShapeDtypeStruct((nb, CONV_K - 1, 2 * wx), F32)],
        scratch_shapes=[pltpu.VMEM((16, 2 * wx), F32)],
        compiler_params=_cparams(1),
        name="ssd_prefill",
    )(p3, p3, p3, p3, p3, s3, cw, cb, dtb, alog, dvec, nw)


MERGE_SUB = 256


def _merge_kernel(yr_ref, ym_ref, ys_ref, gr_ref, gm_ref, gs_ref, x_ref, wr_ref, wm_ref, ws_ref,
                  wo_ref, nw_ref, wq_ref, xo_ref, qo_ref):
    tm = x_ref.shape[0]
    n_sub = max(1, tm // MERGE_SUB)
    subs = [slice(i * (tm // n_sub), (i + 1) * (tm // n_sub)) for i in range(n_sub)]
    branches = ((yr_ref, gr_ref, wr_ref), (ym_ref, gm_ref, wm_ref), (ys_ref, gs_ref, ws_ref))
    proj = [[_dot(y_ref[rs, :].astype(BF16), w_ref[...]) for y_ref, _, w_ref in branches] for rs in subs]
    merged = []
    for rs, pr in zip(subs, proj):
        gated = [_sigmoid(g_ref[rs, :].astype(F32)) * p for (_, g_ref, _), p in zip(branches, pr)]
        merged.append((gated[0] + gated[1] + gated[2]).astype(BF16))
    xns = [x_ref[rs, :] + _dot(m, wo_ref[...]) for rs, m in zip(subs, merged)]
    hqs = []
    for rs, xn in zip(subs, xns):
        xo_ref[rs, :] = xn
        hqs.append(_rmsnorm(xn, nw_ref[...]).astype(BF16))
    for rs, hq in zip(subs, hqs):
        qo_ref[rs, :] = _dot(hq, wq_ref[...]).astype(qo_ref.dtype)


def _merge(yr, ym, ys, proj, x, wr, wm, ws, wo, nw, wq, *, layer, tm, q_dtype):
    m = x.shape[0]
    d = D_MODEL
    row = lambda i: (i, 0)
    wspec = _lsel(layer, d, d, pipeline_mode=pl.Buffered(1))
    return pl.pallas_call(
        _merge_kernel,
        grid=(m // tm,),
        in_specs=[pl.BlockSpec((tm, d), row), pl.BlockSpec((tm, d), row), pl.BlockSpec((tm, d), row),
                  pl.BlockSpec((tm, d), lambda i: (i, C_GR // d)),
                  pl.BlockSpec((tm, d), lambda i: (i, C_GM // d)),
                  pl.BlockSpec((tm, d), lambda i: (i, C_GS // d)),
                  pl.BlockSpec((tm, d), row),
                  wspec, wspec, wspec, wspec,
                  _lsel(layer, 1, d),
                  wspec],
        out_specs=[pl.BlockSpec((tm, d), row), pl.BlockSpec((tm, d), row)],
        out_shape=[jax.ShapeDtypeStruct((m, d), F32), jax.ShapeDtypeStruct((m, d), q_dtype)],
        compiler_params=_cparams(1),
        name="merge",
    )(yr, ym, ys, proj, proj, proj, x, wr, wm, ws, wo, nw, wq)


def _attn_prefill_kernel(q_ref, k_ref, v_ref, x_ref, wo_ref, o_ref):
    heads = [slice(h * MEM_DH, (h + 1) * MEM_DH) for h in range(MEM_H)]
    scores = [_dot_nt(q_ref[0, :, sl], k_ref[0, :, sl].astype(BF16)) * (MEM_DH ** -0.5) for sl in heads]
    probs = []
    for s in scores:
        p = jnp.exp(s - jnp.max(s, axis=-1, keepdims=True))
        probs.append((p / jnp.sum(p, axis=-1, keepdims=True)).astype(BF16))
    outs = [_dot(p, v_ref[0, :, sl].astype(BF16)).astype(BF16) for p, sl in zip(probs, heads)]
    o_all = jnp.concatenate(outs, axis=1)
    o_ref[0] = x_ref[0] + _dot(o_all, wo_ref[...])


def _attn_prefill(q, mk, mv, x, wo, *, layer, nb, seq, tq):
    d = D_MODEL
    q3 = q.reshape(nb, seq, d)
    x3 = x.reshape(nb, seq, d)
    out = pl.pallas_call(
        _attn_prefill_kernel,
        grid=(nb, seq // tq),
        in_specs=[pl.BlockSpec((1, tq, d), lambda b, t: (b, t, 0)),
                  pl.BlockSpec((1, MEM_LEN, d), lambda b, t: (layer * nb + b, 0, 0)),
                  pl.BlockSpec((1, MEM_LEN, d), lambda b, t: (layer * nb + b, 0, 0)),
                  pl.BlockSpec((1, tq, d), lambda b, t: (b, t, 0)),
                  _lsel(layer, d, d)],
        out_specs=pl.BlockSpec((1, tq, d), lambda b, t: (b, t, 0)),
        out_shape=jax.ShapeDtypeStruct((nb, seq, d), F32),
        compiler_params=_cparams(2),
        name="attn_prefill",
    )(q3, mk, mv, x3, wo)
    return out.reshape(nb * seq, d)


def _attn_decode_kernel(q_ref, k_ref, v_ref, o_ref):
    for j in range(q_ref.shape[0]):
        s = jnp.sum(k_ref[j] * q_ref[j][None], axis=-1, keepdims=True) * (MEM_DH ** -0.5)
        p = jnp.exp(s - jnp.max(s, axis=0, keepdims=True))
        p = p / jnp.sum(p, axis=0, keepdims=True)
        o_ref[j] = jnp.sum(p * v_ref[j], axis=0)


def _attn_decode(q, ck, cv, *, layer, bb):
    nb = q.shape[0]
    nblk = nb // bb
    blk = (bb, MEM_LEN, MEM_H, MEM_DH)
    return pl.pallas_call(
        _attn_decode_kernel,
        grid=(nblk,),
        in_specs=[pl.BlockSpec((bb, MEM_H, MEM_DH), lambda i: (i, 0, 0)),
                  pl.BlockSpec(blk, lambda i: (layer * nblk + i, 0, 0, 0)),
                  pl.BlockSpec(blk, lambda i: (layer * nblk + i, 0, 0, 0))],
        out_specs=pl.BlockSpec((bb, MEM_H, MEM_DH), lambda i: (i, 0, 0)),
        out_shape=jax.ShapeDtypeStruct((nb, MEM_H, MEM_DH), F32),
        compiler_params=_cparams(1),
        name="attn_decode",
    )(q, ck, cv)


MLP_SUB = 1024


def _mlp_kernel(x_ref, nw_ref, w1_ref, w2_ref, nf_ref, o_ref, xn_ref, acc_ref, *, final_norm):
    j = pl.program_id(1)

    @pl.when(j == 0)
    def _():
        xn_ref[...] = _rmsnorm(x_ref[...], nw_ref[...]).astype(BF16)
        acc_ref[...] = jnp.zeros_like(acc_ref)

    subs = [slice(s * MLP_SUB, (s + 1) * MLP_SUB) for s in range(w1_ref.shape[1] // MLP_SUB)]
    hids = [_dot(xn_ref[...], w1_ref[:, sl]) for sl in subs]
    hids = [jnp.square(jnp.maximum(hid, 0.0)).astype(BF16) for hid in hids]
    acc = acc_ref[...]
    for hid, sl in zip(hids, subs):
        acc = acc + _dot(hid, w2_ref[sl, :])
    acc_ref[...] = acc

    @pl.when(j == pl.num_programs(1) - 1)
    def _():
        y = x_ref[...] + acc_ref[...]
        if final_norm:
            y = _rmsnorm(y, nf_ref[...])
        o_ref[...] = y


def _mlp(x, nw, w1, w2, nf, *, layer, tm, tf, final_norm):
    m = x.shape[0]
    d = D_MODEL
    return pl.pallas_call(
        functools.partial(_mlp_kernel, final_norm=final_norm),
        grid=(m // tm, D_FF // tf),
        in_specs=[pl.BlockSpec((tm, d), lambda i, j: (i, 0)),
                  _lsel(layer, 1, d),
                  pl.BlockSpec((None, d, tf), lambda i, j: (layer, 0, j)),
                  pl.BlockSpec((None, tf, d), lambda i, j: (layer, j, 0)),
                  pl.BlockSpec((1, d), lambda i, j: (0, 0))],
        out_specs=pl.BlockSpec((tm, d), lambda i, j: (i, 0)),
        out_shape=jax.ShapeDtypeStruct((m, d), F32),
        scratch_shapes=[pltpu.VMEM((tm, d), BF16), pltpu.VMEM((tm, d), F32)],
        compiler_params=_cparams(2),
        name="mlp",
    )(x, nw, w1, w2, nf)


DEC_BB = 8


def _row_mask(x, j):
    rows = lax.broadcasted_iota(jnp.int32, x.shape, 0)
    return jnp.where(rows == j, x, 0.0).astype(BF16)


def _blockdiag_rows(v):
    n = v.shape[0]
    tiled = jnp.concatenate([v] * n, axis=1)
    rows = lax.broadcasted_iota(jnp.int32, tiled.shape, 0)
    blk = lax.shift_right_logical(lax.broadcasted_iota(jnp.int32, tiled.shape, 1), 8)
    return _pad_rows(jnp.where(rows == blk, tiled, 0.0), 128).astype(BF16)


def _ret_decode_kernel(q_ref, k_ref, v_ref, g_ref, cos_ref, sin_ref, nw_ref, s_ref, y_ref, so_ref, ybuf):
    cosf = cos_ref[...]
    sinf = sin_ref[...]
    for h in range(RET_H):
        sk = slice(h * RET_DK, (h + 1) * RET_DK)
        sv = slice(h * RET_DV, (h + 1) * RET_DV)
        g_dec = math.exp(_LOG_G[h])
        q = q_ref[:, sk]
        k = k_ref[:, sk]
        qr = q * cosf + pltpu.roll(q, RET_DK // 2, 1) * sinf
        kr = (k * cosf + pltpu.roll(k, RET_DK // 2, 1) * sinf) * (RET_DK ** -0.5)
        k_t = _pad_rows_t(kr).astype(BF16)
        q16 = _pad_rows(qr, 16).astype(BF16)
        upd = _dot(k_t, _blockdiag_rows(v_ref[:, sv]))
        s_wide = []
        for j in range(DEC_BB):
            s_new = s_ref[j, h] * g_dec + upd[:, j * RET_DV:(j + 1) * RET_DV]
            so_ref[j, h] = s_new
            s_wide.append(s_new.astype(BF16))
        y_all = _dot(q16, jnp.concatenate(s_wide, axis=1))
        for j in range(DEC_BB):
            ybuf[j:j + 1, sv] = y_all[j:j + 1, j * RET_DV:(j + 1) * RET_DV]
        y_ref[:, sv] = _groupnorm(ybuf[:, sv], nw_ref[:, sv], True) * _silu(g_ref[:, sv])


def _ret_decode(proj, cosf, sinf, nw, state, prev, *, layer):
    nb = proj.shape[0]
    bb = DEC_BB
    nblk = nb // bb
    kern, steps, ib, ob, al_args, al_specs, al_map = _stacked_state_call(_ret_decode_kernel, 9, nblk, layer, prev)
    sblk = (bb, RET_H, RET_DK, RET_DV)
    return pl.pallas_call(
        kern,
        grid=(steps,),
        in_specs=al_specs + [
            pl.BlockSpec((bb, 512), lambda i: (ib(i), C_Q // 512)),
            pl.BlockSpec((bb, 512), lambda i: (ib(i), C_K // 512)),
            pl.BlockSpec((bb, 1024), lambda i: (ib(i), C_V // 1024)),
            pl.BlockSpec((bb, 1024), lambda i: (ib(i), C_G // 1024)),
            pl.BlockSpec((1, RET_DK), lambda i: (0, 0)),
            pl.BlockSpec((1, RET_DK), lambda i: (0, 0)),
            _lsel(layer, 1, 1024),
            pl.BlockSpec(sblk, lambda i: (layer * nblk + ib(i), 0, 0, 0))],
        out_specs=[pl.BlockSpec((bb, 1024), lambda i: (ib(i), 0)),
                   pl.BlockSpec(sblk, lambda i: (ob(i), 0, 0, 0))],
        out_shape=[jax.ShapeDtypeStruct((nb, RET_H * RET_DV), F32),
                   jax.ShapeDtypeStruct((DEPTH * nb, RET_H, RET_DK, RET_DV), F32)],
        scratch_shapes=[pltpu.VMEM((bb, RET_H * RET_DV), F32)],
        input_output_aliases=al_map,
        compiler_params=_cparams(1),
        name="ret_decode",
    )(*al_args, proj, proj, proj, proj, cosf, sinf, nw, state)


def _conv_step(x, buf_ref, cw_ref, cb_ref, bo_ref):
    acc = cb_ref[...]
    for j in range(CONV_K - 1):
        acc = acc + buf_ref[j] * cw_ref[j:j + 1, :]
    acc = acc + x * cw_ref[CONV_K - 1:CONV_K, :]
    bo_ref[0] = buf_ref[1]
    bo_ref[1] = buf_ref[2]
    bo_ref[2] = x
    return _silu(acc)


def _ml_decode_kernel(u_ref, v_ref, o_ref, sm_ref, buf_ref, cw_ref, cb_ref, wq_ref, wk_ref, gb_ref, nw_ref,
                      c_ref, n_ref, m_ref,
                      y_ref, co_ref, no_ref, mo_ref, bo_ref, ybuf):
    uc = _conv_step(u_ref[...], buf_ref, cw_ref, cb_ref, bo_ref).astype(BF16)
    gates = sm_ref[...] + gb_ref[...]
    lane4 = lax.broadcasted_iota(jnp.int32, m_ref.shape, 1)
    m_all = m_ref[...]
    m_out = m_all
    for h in range(ML_H):
        sl = slice(h * ML_DH, (h + 1) * ML_DH)
        qm = _dot(uc[:, sl], wq_ref[h])
        km = _dot(uc[:, sl], wk_ref[h]) * (ML_DH ** -0.5)
        i_pre = gates[:, L_IG + h:L_IG + h + 1]
        log_f = _log_sigmoid(gates[:, L_FG + h:L_FG + h + 1])
        m_prev = m_all[:, h:h + 1]
        inter = log_f + m_prev
        m_new = jnp.maximum(inter, i_pre)
        w_s = jnp.exp(i_pre - m_new)
        w_prev = jnp.exp(inter - m_new)
        kw = km * w_s
        n_new = n_ref[:, sl] * w_prev + kw
        no_ref[:, sl] = n_new
        m_out = jnp.where(lane4 == h, m_new, m_out)

        k_t = _pad_rows_t(kw).astype(BF16)
        q16 = _pad_rows(qm, 16).astype(BF16)
        upd = _dot(k_t, _blockdiag_rows(v_ref[:, sl]))
        c_wide = []
        for j in range(DEC_BB):
            c_new = c_ref[j, h] * w_prev[j:j + 1, :] + upd[:, j * ML_DH:(j + 1) * ML_DH]
            co_ref[j, h] = c_new
            c_wide.append(c_new.astype(BF16))
        y_all = _dot(q16, jnp.concatenate(c_wide, axis=1))
        for j in range(DEC_BB):
            ybuf[j:j + 1, sl] = y_all[j:j + 1, j * ML_DH:(j + 1) * ML_DH]
        den = jnp.sum(qm * n_new, axis=-1, keepdims=True)
        den = jnp.maximum(jnp.abs(den), jnp.exp(-m_new))
        y_ref[:, sl] = _groupnorm(ybuf[:, sl] / den, nw_ref[:, sl], True) * _sigmoid(o_ref[:, sl])
    mo_ref[...] = m_out


def _ml_decode(proj, small, buf, cw, cb, wq, wk, gb, nw, c_state, n_state, m_state, prev, *, layer):
    nb = proj.shape[0]
    bb = DEC_BB
    nblk = nb // bb
    dh = ML_DH
    w = ML_H * ML_DH
    full2 = lambda i: (0, 0)
    kern, steps, ib, ob, al_args, al_specs, al_map = _stacked_state_call(_ml_decode_kernel, 15, nblk, layer, prev)
    cblk = (bb, ML_H, dh, dh)
    return pl.pallas_call(
        kern,
        grid=(steps,),
        in_specs=al_specs + [
            pl.BlockSpec((bb, w), lambda i: (ib(i), C_U // w)),
            pl.BlockSpec((bb, w), lambda i: (ib(i), C_VM // w)),
            pl.BlockSpec((bb, w), lambda i: (ib(i), C_OM // w)),
            pl.BlockSpec((bb, N_SMALL), lambda i: (ib(i), 0)),
            pl.BlockSpec((None, CONV_K - 1, bb, w), lambda i: (layer, 0, ib(i), 0)),
            _lsel(layer, CONV_K, w),
            _lsel(layer, 1, w),
            _lsel(layer, ML_H, dh, dh),
            _lsel(layer, ML_H, dh, dh),
            _lsel(layer, 1, N_SMALL),
            _lsel(layer, 1, w),
            pl.BlockSpec(cblk, lambda i: (layer * nblk + ib(i), 0, 0, 0)),
            pl.BlockSpec((bb, w), lambda i: (layer * nblk + ib(i), 0)),
            pl.BlockSpec((bb, ML_H), lambda i: (layer * nblk + ib(i), 0))],
        out_specs=[pl.BlockSpec((bb, w), lambda i: (ib(i), 0)),
                   pl.BlockSpec(cblk, lambda i: (ob(i), 0, 0, 0)),
                   pl.BlockSpec((bb, w), lambda i: (ib(i), 0)),
                   pl.BlockSpec((bb, ML_H), lambda i: (ib(i), 0)),
                   pl.BlockSpec((CONV_K - 1, bb, w), lambda i: (0, ib(i), 0))],
        out_shape=[jax.ShapeDtypeStruct((nb, w), F32),
                   jax.ShapeDtypeStruct((DEPTH * nb, ML_H, dh, dh), F32),
                   jax.ShapeDtypeStruct((nb, w), F32),
                   jax.ShapeDtypeStruct((nb, ML_H), F32),
                   jax.ShapeDtypeStruct((CONV_K - 1, nb, w), F32)],
        scratch_shapes=[pltpu.VMEM((bb, w), F32)],
        input_output_aliases=al_map,
        compiler_params=_cparams(1),
        name="ml_decode",
    )(*al_args, proj, proj, proj, small, buf, cw, cb, wq, wk, gb, nw, c_state, n_state, m_state)


def _ssd_decode_kernel(z_ref, xs_ref, bc_ref, sm_ref, buf_ref, cw_ref, cb_ref, dtb_ref, alog_ref, dv_ref, nw_ref,
                       s_ref, y_ref, so_ref, bo_ref, ybuf):
    wx = SSM_H * SSM_P
    gw = SSM_R * SSM_P
    x_in = jnp.concatenate([xs_ref[...], bc_ref[...]], axis=1)
    xc = _conv_step(x_in, buf_ref, cw_ref, cb_ref, bo_ref)
    xs = xc[:, 0:wx]

    delta = _softplus(sm_ref[...] + dtb_ref[...])
    d_a = jnp.exp(delta * (-jnp.exp(alog_ref[...])))
    lane_blk = lax.shift_right_logical(lax.broadcasted_iota(jnp.int32, (DEC_BB, wx), 1), 6)
    dt_full = jnp.zeros((DEC_BB, wx), F32)
    for hh in range(SSM_H):
        dt_full = jnp.where(lane_blk == hh, delta[:, L_DT + hh:L_DT + hh + 1], dt_full)
    x_t = _pad_rows_t(xs * dt_full).astype(BF16)

    for g in range(SSM_G):
        gs = slice(g * gw, (g + 1) * gw)
        b_pad = _pad_rows(xc[:, wx + g * SSM_N:wx + (g + 1) * SSM_N], 128)
        c16 = _pad_rows(xc[:, wx + (SSM_G + g) * SSM_N:wx + (SSM_G + g + 1) * SSM_N], 16).astype(BF16)
        for j in range(DEC_BB):
            upd = _dot(x_t[gs, :], _row_mask(b_pad, j))
            parts = []
            for r in range(SSM_R):
                hh = g * SSM_R + r
                s_new = (s_ref[j, hh] * d_a[j:j + 1, L_DT + hh:L_DT + hh + 1]
                         + upd[r * SSM_P:(r + 1) * SSM_P])
                so_ref[j, hh] = s_new
                parts.append(s_new.astype(BF16))
            ybuf[j:j + 1, gs] = _dot_nt(c16, jnp.concatenate(parts, axis=0))[j:j + 1, :]
        y = ybuf[:, gs] + dv_ref[:, gs] * xs[:, gs]
        y_ref[:, gs] = _groupnorm(y * _silu(z_ref[:, gs]), nw_ref[:, gs], False)


def _ssd_decode(proj, small, buf, cw, cb, dtb, alog, dvec, nw, state, prev, *, layer):
    nb = proj.shape[0]
    bb = DEC_BB
    nblk = nb // bb
    wx = SSM_H * SSM_P
    full2 = lambda i: (0, 0)
    kern, steps, ib, ob, al_args, al_specs, al_map = _stacked_state_call(_ssd_decode_kernel, 13, nblk, layer, prev)
    sblk = (bb, SSM_H, SSM_P, SSM_N)
    return pl.pallas_call(
        kern,
        grid=(steps,),
        in_specs=al_specs + [
            pl.BlockSpec((bb, wx), lambda i: (ib(i), C_Z // wx)),
            pl.BlockSpec((bb, wx), lambda i: (ib(i), C_XS // wx)),
            pl.BlockSpec((bb, wx), lambda i: (ib(i), C_B // wx)),
            pl.BlockSpec((bb, N_SMALL), lambda i: (ib(i), 0)),
            pl.BlockSpec((None, CONV_K - 1, bb, 2 * wx), lambda i: (layer, 0, ib(i), 0)),
            _lsel(layer, CONV_K, 2 * wx),
            _lsel(layer, 1, 2 * wx),
            _lsel(layer, 1, N_SMALL),
            _lsel(layer, 1, N_SMALL),
            _lsel(layer, 1, wx),
            _lsel(layer, 1, wx),
            pl.BlockSpec(sblk, lambda i: (layer * nblk + ib(i), 0, 0, 0))],
        out_specs=[pl.BlockSpec((bb, wx), lambda i: (ib(i), 0)),
                   pl.BlockSpec(sblk, lambda i: (ob(i), 0, 0, 0)),
                   pl.BlockSpec((CONV_K - 1, bb, 2 * wx), lambda i: (0, ib(i), 0))],
        out_shape=[jax.ShapeDtypeStruct((nb, wx), F32),
                   jax.ShapeDtypeStruct((DEPTH * nb, SSM_H, SSM_P, SSM_N), F32),
                   jax.ShapeDtypeStruct((CONV_K - 1, nb, 2 * wx), F32)],
        scratch_shapes=[pltpu.VMEM((bb, wx), F32)],
        input_output_aliases=al_map,
        compiler_params=_cparams(1),
        name="ssd_decode",
    )(*al_args, proj, proj, proj, small, buf, cw, cb, dtb, alog, dvec, nw, state)


def _rope_tables(pos):
    half = RET_DK // 2
    freqs = ROPE_THETA ** (-jnp.arange(half, dtype=F32) / half)
    ang = pos.astype(F32)[:, None] * freqs[None, :]
    cos = jnp.cos(ang)
    sin = jnp.sin(ang)
    return jnp.concatenate([cos, cos], axis=1), jnp.concatenate([-sin, sin], axis=1)


def _pad_lanes(v, offset):
    n = v.shape[1]
    return jnp.pad(v.astype(F32), ((0, 0), (offset, N_SMALL - offset - n))).reshape(DEPTH, 1, N_SMALL)


def _prep_weights(W):
    w_in = W['w_in']
    n_ml0 = 2 * 512 + 2 * 1024
    n_ig = n_ml0 + 3 * 1024
    n_ssm0 = n_ig + 2 * ML_H
    n_dt = n_ssm0 + 1024 + 2048
    n_gate = n_dt + SSM_H
    w_main = jnp.concatenate([w_in[:, :, :n_ig], w_in[:, :, n_ssm0:n_dt], w_in[:, :, n_gate:]],
                             axis=2).astype(BF16)
    w_small = jnp.concatenate([w_in[:, :, n_ig:n_ssm0], w_in[:, :, n_dt:n_gate],
                               jnp.zeros((DEPTH, D_MODEL, N_SMALL - 2 * ML_H - SSM_H), F32)], axis=2).astype(BF16)
    row = lambda v: v.reshape(DEPTH, 1, -1).astype(F32)
    bf = lambda v: v.astype(BF16)
    return dict(
        w_main=w_main, w_small=w_small,
        norm_mix=row(W['norm_mix_w']),
        ret_norm=row(W['ret_norm_w']),
        ml_cw=W['ml_conv_w'], ml_cb=row(W['ml_conv_b']),
        ml_wq=bf(W['ml_wq']), ml_wk=bf(W['ml_wk']),
        ml_gb=_pad_lanes(W['ml_gate_b'], L_IG),
        ml_norm=row(W['ml_norm_w']),
        ssm_cw=W['ssm_conv_w'], ssm_cb=row(W['ssm_conv_b']),
        ssm_dtb=_pad_lanes(W['ssm_dt_bias'], L_DT),
        ssm_alog=_pad_lanes(W['ssm_A_log'], L_DT),
        ssm_dvec=row(jnp.repeat(W['ssm_D'], SSM_P, axis=1)),
        ssm_norm=row(W['ssm_norm_w']),
        w_br_ret=bf(W['w_br_ret']), w_br_ml=bf(W['w_br_ml']),
        w_br_ssm=bf(W['w_br_ssm']), w_out=bf(W['w_out_mix']),
        norm_mem=row(W['norm_mem_w']),
        mem_wq=bf(W['mem_wq']), mem_wo=bf(W['mem_wo']),
        norm_mlp=row(W['norm_mlp_w']),
        mlp_w1=bf(W['mlp_w1']), mlp_w2=bf(W['mlp_w2']),
    )


def kernel(x_prompt, x_sample, mem_prompt, state_ret, state_mlstm_C, state_mlstm_n, state_mlstm_m,
           state_mlstm_conv, state_ssm, state_ssm_conv, cache_mem_k, cache_mem_v,
           norm_mix_w, w_in, ret_norm_w, ml_conv_w, ml_conv_b, ml_wq, ml_wk, ml_gate_b, ml_norm_w,
           ssm_conv_w, ssm_conv_b, ssm_dt_bias, ssm_A_log, ssm_D, ssm_norm_w,
           w_br_ret, w_br_ml, w_br_ssm, w_out_mix, norm_mem_w, mem_wq, mem_wk, mem_wv, mem_wo,
           norm_mlp_w, mlp_w1, mlp_w2, norm_f_w):
    W = dict(norm_mix_w=norm_mix_w, w_in=w_in, ret_norm_w=ret_norm_w, ml_conv_w=ml_conv_w,
             ml_conv_b=ml_conv_b, ml_wq=ml_wq, ml_wk=ml_wk, ml_gate_b=ml_gate_b, ml_norm_w=ml_norm_w,
             ssm_conv_w=ssm_conv_w, ssm_conv_b=ssm_conv_b, ssm_dt_bias=ssm_dt_bias, ssm_A_log=ssm_A_log,
             ssm_D=ssm_D, ssm_norm_w=ssm_norm_w, w_br_ret=w_br_ret, w_br_ml=w_br_ml, w_br_ssm=w_br_ssm,
             w_out_mix=w_out_mix, norm_mem_w=norm_mem_w, mem_wq=mem_wq, mem_wo=mem_wo,
             norm_mlp_w=norm_mlp_w, mlp_w1=mlp_w1, mlp_w2=mlp_w2)
    lw = _prep_weights(W)
    norm_f = norm_f_w.reshape(1, -1).astype(F32)
    d = D_MODEL

    nb, seq = x_prompt.shape[0], x_prompt.shape[1]
    cos_p, sin_p = _rope_tables(jnp.arange(seq, dtype=jnp.int32))
    memk, memv = _memkv(mem_prompt.reshape(nb * MEM_LEN, d), mem_wk.astype(BF16), mem_wv.astype(BF16), tm=1024)
    memk3 = memk.reshape(DEPTH * nb, MEM_LEN, d)
    memv3 = memv.reshape(DEPTH * nb, MEM_LEN, d)

    x = x_prompt.reshape(nb * seq, d)
    states_p = []
    for l in range(DEPTH):
        proj, small = _inproj(x, lw['norm_mix'], lw['w_main'], lw['w_small'], layer=l, tm=2048, tn=1024,
                              out_dtype=BF16)
        y_r, ret_st = _ret_prefill(proj, cos_p, sin_p, lw['ret_norm'], layer=l, nb=nb, seq=seq)
        y_m, ml_c, ml_n, ml_m, ml_cv = _ml_prefill(proj, small, lw['ml_cw'], lw['ml_cb'], lw['ml_wq'], lw['ml_wk'],
                                                   lw['ml_gb'], lw['ml_norm'], layer=l, nb=nb, seq=seq)
        y_s, ssm_st, ssm_cv = _ssd_prefill(proj, small, lw['ssm_cw'], lw['ssm_cb'], lw['ssm_dtb'], lw['ssm_alog'],
                                           lw['ssm_dvec'], lw['ssm_norm'], layer=l, nb=nb, seq=seq)
        x, qm = _merge(y_r.reshape(nb * seq, d), y_m.reshape(nb * seq, d), y_s.reshape(nb * seq, d), proj, x,
                       lw['w_br_ret'], lw['w_br_ml'], lw['w_br_ssm'], lw['w_out'], lw['norm_mem'], lw['mem_wq'],
                       layer=l, tm=512, q_dtype=BF16)
        x = _attn_prefill(qm, memk3, memv3, x, lw['mem_wo'], layer=l, nb=nb, seq=seq, tq=512)
        x = _mlp(x, lw['norm_mlp'], lw['mlp_w1'], lw['mlp_w2'], norm_f, layer=l, tm=1024, tf=2048,
                 final_norm=(l == DEPTH - 1))
        states_p.append((ret_st, ml_c, ml_n, ml_m[:, 0, :ML_H], ml_cv, ssm_st, ssm_cv))
    y_prompt = x.reshape(nb, seq, d)
    stack = lambda parts, i: jnp.stack([p[i] for p in parts])
    prompt_states = tuple(stack(states_p, i) for i in range(7))
    memk_p = memk.reshape(DEPTH, nb, MEM_LEN, MEM_H, MEM_DH)
    memv_p = memv.reshape(DEPTH, nb, MEM_LEN, MEM_H, MEM_DH)

    ns = x_sample.shape[0]
    cos_s, sin_s = _rope_tables(PAST_LEN + jnp.arange(1, dtype=jnp.int32))
    ret_state = state_ret.reshape(DEPTH * ns, RET_H, RET_DK, RET_DV)
    mlc_state = state_mlstm_C.reshape(DEPTH * ns, ML_H, ML_DH, ML_DH)
    mln_state = state_mlstm_n.reshape(DEPTH * ns, ML_H * ML_DH)
    mlm_state = state_mlstm_m.reshape(DEPTH * ns, ML_H)
    ssm_st = state_ssm.reshape(DEPTH * ns, SSM_H, SSM_P, SSM_N)
    ck = cache_mem_k.reshape(DEPTH * ns, MEM_LEN, MEM_H, MEM_DH)
    cv = cache_mem_v.reshape(DEPTH * ns, MEM_LEN, MEM_H, MEM_DH)

    ml_buf = jnp.swapaxes(state_mlstm_conv, 1, 2)
    ssm_buf = jnp.swapaxes(state_ssm_conv, 1, 2)

    x = x_sample.reshape(ns, d)
    ret_s = mlc_s = ssm_s = None
    small_s = []
    for l in range(DEPTH):
        proj, small = _inproj(x, lw['norm_mix'], lw['w_main'], lw['w_small'], layer=l, tm=ns, tn=1024, out_dtype=F32)
        y_r, ret_s = _ret_decode(proj, cos_s, sin_s, lw['ret_norm'], ret_state, ret_s,
                                 layer=l)
        y_m, mlc_s, ml_n, ml_m, ml_bo = _ml_decode(proj, small, ml_buf, lw['ml_cw'], lw['ml_cb'], lw['ml_wq'],
                                                   lw['ml_wk'], lw['ml_gb'], lw['ml_norm'],
                                                   mlc_state, mln_state, mlm_state, mlc_s,
                                                   layer=l)
        y_s, ssm_s, ssm_bo = _ssd_decode(proj, small, ssm_buf, lw['ssm_cw'], lw['ssm_cb'], lw['ssm_dtb'],
                                         lw['ssm_alog'], lw['ssm_dvec'], lw['ssm_norm'], ssm_st,
                                         ssm_s, layer=l)
        x, qm = _merge(y_r, y_m, y_s, proj, x, lw['w_br_ret'], lw['w_br_ml'], lw['w_br_ssm'], lw['w_out'],
                       lw['norm_mem'], lw['mem_wq'], layer=l, tm=ns, q_dtype=F32)
        att = _attn_decode(qm.reshape(ns, MEM_H, MEM_DH), ck, cv, layer=l, bb=4)
        x = _mm(att.reshape(ns, d), lw['mem_wo'], x, layer=l, tm=ns, tn=d)
        x = _mlp(x, lw['norm_mlp'], lw['mlp_w1'], lw['mlp_w2'], norm_f, layer=l, tm=ns, tf=2048,
                 final_norm=(l == DEPTH - 1))
        small_s.append((ml_n.reshape(ns, ML_H, ML_DH), ml_m, jnp.swapaxes(ml_bo, 0, 1),
                        jnp.swapaxes(ssm_bo, 0, 1)))
    y_sample = x.reshape(ns, 1, d)
    sample_states = (ret_s.reshape(DEPTH, ns, RET_H, RET_DK, RET_DV),
                     mlc_s.reshape(DEPTH, ns, ML_H, ML_DH, ML_DH),
                     stack(small_s, 0), stack(small_s, 1), stack(small_s, 2),
                     ssm_s.reshape(DEPTH, ns, SSM_H, SSM_P, SSM_N),
                     stack(small_s, 3))

    return (y_prompt, y_sample, *prompt_states, memk_p, memv_p, *sample_states)
```

```python
import functools
import math

import jax
import jax.numpy as jnp
from jax import lax
from jax.experimental import pallas as pl
from jax.experimental.pallas import tpu as pltpu

F32 = jnp.float32
BF16 = jnp.bfloat16

D_MODEL = 1024
DEPTH = 2
PAST_LEN = 16384
CHUNK = 128
CONV_K = 4
EPS = 1e-6
RET_H, RET_DK, RET_DV = 4, 128, 256
ROPE_THETA = 10000.0
ML_H, ML_DH = 4, 256
SSM_H, SSM_P, SSM_G, SSM_N = 16, 64, 4, 128
SSM_R = SSM_H // SSM_G
MEM_LEN, MEM_H, MEM_DH = 256, 4, 256
D_FF = 4 * D_MODEL

C_Q, C_K, C_V, C_G = 0, 512, 1024, 2048
C_U, C_VM, C_OM = 3072, 4096, 5120
C_Z, C_XS, C_B, C_C = 6144, 7168, 8192, 8704
C_GR, C_GM, C_GS = 9216, 10240, 11264
N_MAIN = 12288
L_IG, L_FG, L_DT = 0, 4, 8
N_SMALL = 128

VMEM_LIMIT = 56 * 1024 * 1024
NT_DIMS = (((1,), (1,)), ((), ()))

_LOG_G = [math.log1p(-(2.0 ** (-5.0 - h))) for h in range(RET_H)]


def _cparams(n_axes):
    return pltpu.CompilerParams(dimension_semantics=("arbitrary",) * n_axes,
                                vmem_limit_bytes=VMEM_LIMIT)


def _dot(a, b):
    return jnp.dot(a, b, preferred_element_type=F32)


def _dot_nt(a, b):
    return lax.dot_general(a, b, NT_DIMS, preferred_element_type=F32)


def _rmsnorm(x, w):
    ms = jnp.mean(x * x, axis=-1, keepdims=True)
    return x * lax.rsqrt(ms + EPS) * w


def _groupnorm(x, w, center):
    if center:
        x = x - jnp.mean(x, axis=-1, keepdims=True)
    ms = jnp.mean(x * x, axis=-1, keepdims=True)
    return x * lax.rsqrt(ms + EPS) * w


def _sigmoid(x):
    return 1.0 / (1.0 + jnp.exp(-x))


def _silu(x):
    return x * _sigmoid(x)


def _softplus(x):
    return jnp.maximum(x, 0.0) + jnp.log1p(jnp.exp(-jnp.abs(x)))


def _log_sigmoid(x):
    return -_softplus(-x)


def _cumsum_rows(x, tril_b):
    hi = x.astype(BF16)
    r1 = x - hi.astype(F32)
    mid = r1.astype(BF16)
    lo = (r1 - mid.astype(F32)).astype(BF16)
    return _dot(tril_b, hi) + _dot(tril_b, mid) + _dot(tril_b, lo)


def _pad_rows(x, rows):
    return jnp.concatenate([x, jnp.zeros((rows - x.shape[0], x.shape[1]), x.dtype)], axis=0)


def _pad_rows_t(x):
    return _pad_rows(x, 128).T


def _carried_call(kernel_fn, nblk, layer, prev):
    return (lambda *refs: kernel_fn(*refs[1:]), nblk, lambda i: i, lambda i: layer * nblk + i,
            [prev], [pl.BlockSpec(memory_space=pl.ANY)], {0: 1})


def _zero_out(shape, steps):
    blk = (shape[0] // steps,) + tuple(shape[1:])
    nz = len(shape) - 1
    return pl.BlockSpec(blk, lambda i: (i,) + (0,) * nz), jax.ShapeDtypeStruct(tuple(shape), F32)


def _zero_fill(zero_refs):
    for z_ref in zero_refs:
        z_ref[...] = jnp.zeros_like(z_ref)


def _inproj_kernel(x_ref, nw_ref, w_ref, ws_ref, o_ref, os_ref, xn_ref):
    @pl.when(pl.program_id(1) == 0)
    def _():
        xn = _rmsnorm(x_ref[...], nw_ref[...]).astype(BF16)
        xn_ref[...] = xn
        os_ref[...] = _dot(xn, ws_ref[...])

    o_ref[...] = _dot(xn_ref[...], w_ref[...]).astype(o_ref.dtype)


def _lsel(layer, *block, **kw):
    return pl.BlockSpec((None,) + block, lambda *_: (layer,) + (0,) * len(block), **kw)


def _inproj(x, nw, w_main, w_small, *, layer, tm, tn, out_dtype):
    m = x.shape[0]
    return pl.pallas_call(
        _inproj_kernel,
        grid=(m // tm, N_MAIN // tn),
        in_specs=[pl.BlockSpec((tm, D_MODEL), lambda i, j: (i, 0)),
                  _lsel(layer, 1, D_MODEL),
                  pl.BlockSpec((None, D_MODEL, tn), lambda i, j: (layer, 0, j)),
                  _lsel(layer, D_MODEL, N_SMALL)],
        out_specs=[pl.BlockSpec((tm, tn), lambda i, j: (i, j)),
                   pl.BlockSpec((tm, N_SMALL), lambda i, j: (i, 0))],
        out_shape=[jax.ShapeDtypeStruct((m, N_MAIN), out_dtype),
                   jax.ShapeDtypeStruct((m, N_SMALL), F32)],
        scratch_shapes=[pltpu.VMEM((tm, D_MODEL), BF16)],
        compiler_params=_cparams(2),
        name="inproj",
    )(x, nw, w_main, w_small)


def _mm_kernel(*refs, has_res):
    if has_res:
        a_ref, w_ref, r_ref, o_ref = refs
    else:
        a_ref, w_ref, o_ref = refs
    acc = _dot(a_ref[...].astype(BF16), w_ref[...])
    if has_res:
        acc = r_ref[...] + acc
    o_ref[...] = acc.astype(o_ref.dtype)


def _mm(a, w, res=None, *, layer, tm, tn, out_dtype=F32):
    m, k = a.shape
    n = w.shape[2]
    in_specs = [pl.BlockSpec((tm, k), lambda i, j: (i, 0)),
                pl.BlockSpec((None, k, tn), lambda i, j: (layer, 0, j))]
    args = [a, w]
    if res is not None:
        in_specs.append(pl.BlockSpec((tm, tn), lambda i, j: (i, j)))
        args.append(res)
    return pl.pallas_call(
        functools.partial(_mm_kernel, has_res=res is not None),
        grid=(m // tm, n // tn),
        in_specs=in_specs,
        out_specs=pl.BlockSpec((tm, tn), lambda i, j: (i, j)),
        out_shape=jax.ShapeDtypeStruct((m, n), out_dtype),
        compiler_params=_cparams(2),
        name="mm",
    )(*args)


def _memkv_kernel(a_ref, wk_ref, wv_ref, ok_ref, ov_ref):
    a = a_ref[...].astype(BF16)
    ok_ref[0] = _dot(a, wk_ref[0])
    ov_ref[0] = _dot(a, wv_ref[0])


def _memkv(a, wk, wv, *, tm):
    m = a.shape[0]
    n = wk.shape[2]
    return pl.pallas_call(
        _memkv_kernel,
        grid=(DEPTH, m // tm),
        in_specs=[pl.BlockSpec((tm, D_MODEL), lambda l, i: (i, 0)),
                  pl.BlockSpec((1, D_MODEL, n), lambda l, i: (l, 0, 0)),
                  pl.BlockSpec((1, D_MODEL, n), lambda l, i: (l, 0, 0))],
        out_specs=[pl.BlockSpec((1, tm, n), lambda l, i: (l, i, 0)),
                   pl.BlockSpec((1, tm, n), lambda l, i: (l, i, 0))],
        out_shape=[jax.ShapeDtypeStruct((DEPTH, m, n), F32)] * 2,
        compiler_params=_cparams(2),
        name="memkv",
    )(a, wk, wv)


class _Ctx(dict):
    __getattr__ = dict.__getitem__
    __setattr__ = dict.__setitem__


def _seq_loop(nb, per_trip, seq_phases, n_items, item_phases, finish):
    def body(i, carry):
        seqs = [_Ctx(b=i * per_trip + u) for u in range(per_trip)]
        for phase in seq_phases:
            for s in seqs:
                phase(s)
        items = []
        for s in seqs:
            s.heads = [_Ctx(seq=s, b=s.b, k=k) for k in range(n_items)]
            items.extend(s.heads)
        for phase in item_phases:
            for it in items:
                phase(it)
        for s in seqs:
            finish(s)
        return carry

    lax.fori_loop(0, nb // per_trip, body, 0)


def _ret_prefill_kernel(q_ref, k_ref, v_ref, g_ref, cos_ref, sin_ref, nw_ref, y_ref, s_ref, *zero_refs):
    _zero_fill(zero_refs)

    @pl.when(pl.program_id(0) == 0)
    def _():
        s_ref[...] = jnp.zeros_like(s_ref)

    c = CHUNK
    nb = q_ref.shape[0]
    cosf = cos_ref[...]
    sinf = sin_ref[...]
    row = lax.broadcasted_iota(jnp.int32, (c, c), 0)
    col = lax.broadcasted_iota(jnp.int32, (c, c), 1)
    causal = row >= col
    diff = jnp.where(causal, (row - col).astype(F32), 0.0)
    rowf = row.astype(F32)
    rowf_v = lax.broadcasted_iota(jnp.int32, (c, RET_DV), 0).astype(F32)
    decay = [jnp.where(causal, jnp.exp(lg * diff), 0.0) for lg in _LOG_G]
    q_decay = [jnp.exp(lg * (rowf_v + 1.0)) for lg in _LOG_G]
    k_decay = [jnp.exp(lg * (c - 1.0 - rowf)) * (RET_DK ** -0.5) for lg in _LOG_G]

    def load(s):
        s.s_old = [s_ref[s.b, h] for h in range(RET_H)]

    def rotate(it):
        h = it.k
        sk = slice(h * RET_DK, (h + 1) * RET_DK)
        q = q_ref[it.b, :, sk].astype(F32)
        k = k_ref[it.b, :, sk].astype(F32)
        qr = q * cosf + pltpu.roll(q, RET_DK // 2, 1) * sinf
        kr = k * cosf + pltpu.roll(k, RET_DK // 2, 1) * sinf
        it.qb = qr.astype(BF16)
        it.kb = (kr * (RET_DK ** -0.5)).astype(BF16)
        it.kt = (kr * k_decay[h]).T.astype(BF16)
        it.v = v_ref[it.b, :, h * RET_DV:(h + 1) * RET_DV]

    def scores(it):
        it.sc = (_dot_nt(it.qb, it.kb) * decay[it.k]).astype(BF16)

    def mix(it):
        h = it.k
        s_old = it.seq.s_old[h]
        it.y = _dot(it.sc, it.v) + _dot(it.qb, s_old.astype(BF16)) * q_decay[h]
        it.s_new = s_old * math.exp(_LOG_G[h] * c) + _dot(it.kt, it.v)

    def emit(it):
        sv = slice(it.k * RET_DV, (it.k + 1) * RET_DV)
        g = g_ref[it.b, :, sv].astype(F32)
        yn = _groupnorm(it.y, nw_ref[:, sv], True) * _silu(g)
        y_ref[it.b, :, sv] = yn.astype(y_ref.dtype)

    def finish(s):
        for it in s.heads:
            s_ref[s.b, it.k] = it.s_new

    _seq_loop(nb, 2, [load], RET_H, [rotate, scores, mix, emit], finish)


def _ret_prefill(proj, cosf, sinf, nw, *, layer, nb, seq, zero_shape=None):
    c = CHUNK
    p3 = proj.reshape(nb, seq, N_MAIN)
    z_specs, z_shapes = ([], []) if zero_shape is None else [[v] for v in _zero_out(zero_shape, seq // c)]
    return pl.pallas_call(
        _ret_prefill_kernel,
        grid=(seq // c,),
        in_specs=[
            pl.BlockSpec((nb, c, 512), lambda t: (0, t, C_Q // 512)),
            pl.BlockSpec((nb, c, 512), lambda t: (0, t, C_K // 512)),
            pl.BlockSpec((nb, c, 1024), lambda t: (0, t, C_V // 1024)),
            pl.BlockSpec((nb, c, 1024), lambda t: (0, t, C_G // 1024)),
            pl.BlockSpec((c, RET_DK), lambda t: (t, 0)),
            pl.BlockSpec((c, RET_DK), lambda t: (t, 0)),
            _lsel(layer, 1, 1024)],
        out_specs=[pl.BlockSpec((nb, c, 1024), lambda t: (0, t, 0)),
                   pl.BlockSpec((nb, RET_H, RET_DK, RET_DV), lambda t: (0, 0, 0, 0))] + z_specs,
        out_shape=[jax.ShapeDtypeStruct((nb, seq, 1024), BF16),
                   jax.ShapeDtypeStruct((nb, RET_H, RET_DK, RET_DV), F32)] + z_shapes,
        compiler_params=_cparams(1),
        name="ret_prefill",
    )(p3, p3, p3, p3, cosf, sinf, nw)


def _shift_matrix(first_chunk):
    c = CHUNK
    r = lax.broadcasted_iota(jnp.int32, (3 * c, 2 * c), 0)
    q = lax.broadcasted_iota(jnp.int32, (3 * c, 2 * c), 1)
    k = lax.shift_right_logical(r, 7)
    t = jnp.bitwise_and(r, c - 1)
    hit = (q == c + t - (CONV_K - 1) + k) & (q >= jnp.where(first_chunk, c, 0))
    return jnp.where(hit, 1.0, 0.0).astype(BF16)


def _conv_silu(shift, x_prev, x_cur, cw, cb):
    c = CHUNK
    p = _dot(shift, jnp.concatenate([x_prev, x_cur], axis=0))
    acc = cb
    for j in range(CONV_K - 1):
        acc = acc + p[j * c:(j + 1) * c] * cw[j:j + 1, :]
    acc = acc + x_cur.astype(F32) * cw[CONV_K - 1:CONV_K, :]
    return _silu(acc)


def _ml_prefill_kernel(u_ref, up_ref, v_ref, o_ref, sm_ref, cw_ref, cb_ref, wq_ref, wk_ref, gb_ref, nw_ref,
                       y_ref, c_ref, n_ref, m_ref, cv_ref, tail):
    c = CHUNK
    nb = u_ref.shape[0]
    first = pl.program_id(0) == 0

    @pl.when(first)
    def _():
        c_ref[...] = jnp.zeros_like(c_ref)
        n_ref[...] = jnp.zeros_like(n_ref)
        m_ref[...] = jnp.zeros_like(m_ref)

    row = lax.broadcasted_iota(jnp.int32, (c, c), 0)
    col = lax.broadcasted_iota(jnp.int32, (c, c), 1)
    causal = row >= col
    tril_b = jnp.where(causal, 1.0, 0.0).astype(BF16)
    lane_m = lax.broadcasted_iota(jnp.int32, (1, 128), 1)
    shift = _shift_matrix(first)

    def load(s):
        s.c_old = [c_ref[s.b, h] for h in range(ML_H)]
        s.n_old = n_ref[s.b]
        s.m_row = m_ref[s.b]

    def gates_phase(s):
        tail[...] = u_ref[s.b, c - 16:c, :].astype(F32)
        cv_ref[s.b] = tail[16 - (CONV_K - 1):16, :]
        s.gates = sm_ref[s.b] + gb_ref[...]
        s.b_all = _cumsum_rows(_log_sigmoid(s.gates), tril_b)
        s.b_all_t = s.b_all.T
        s.gates_t = s.gates.T

    def conv_phase(it):
        sl = slice(it.k * ML_DH, (it.k + 1) * ML_DH)
        it.uc = _conv_silu(shift, up_ref[it.b, :, sl], u_ref[it.b, :, sl], cw_ref[:, sl], cb_ref[:, sl]).astype(BF16)
        it.v = v_ref[it.b, :, sl]

    def weights_phase(it):
        h, s = it.k, it.seq
        b_col = s.b_all[:, L_FG + h:L_FG + h + 1]
        b_row = s.b_all_t[L_FG + h:L_FG + h + 1, :]
        it_col = s.gates[:, L_IG + h:L_IG + h + 1]
        it_row = s.gates_t[L_IG + h:L_IG + h + 1, :]
        m_prev = s.m_row[:, h:h + 1]
        logw = jnp.where(causal, b_col - b_row + it_row, -jnp.inf)
        inter = b_col + m_prev
        it.m_t = jnp.maximum(inter, jnp.max(logw, axis=-1, keepdims=True))
        it.dmat = jnp.exp(logw - it.m_t)
        it.w_int = jnp.exp(inter - it.m_t)
        b_end = b_col[c - 1:c, :]
        logw_s = b_end - b_col + it_col
        it.m_new = jnp.maximum(b_end + m_prev, jnp.max(logw_s, axis=0, keepdims=True))
        it.w_s = jnp.exp(logw_s - it.m_new)
        it.w_prev = jnp.exp(b_end + m_prev - it.m_new)

    def qk_phase(it):
        it.qm = _dot(it.uc, wq_ref[it.k])
        it.km = _dot(it.uc, wk_ref[it.k]) * (ML_DH ** -0.5)
        it.qb = it.qm.astype(BF16)
        it.kb = it.km.astype(BF16)
        it.kw = it.km * it.w_s
        it.kwt = it.kw.T.astype(BF16)

    def scores_phase(it):
        it.sc = _dot_nt(it.qb, it.kb) * it.dmat

    def mix_phase(it):
        c_old = it.seq.c_old[it.k]
        n_old = it.seq.n_old[it.k:it.k + 1, :]
        it.num = _dot(it.sc.astype(BF16), it.v) + _dot(it.qb, c_old.astype(BF16)) * it.w_int
        it.c_new = c_old * it.w_prev + _dot(it.kwt, it.v)
        it.n_new = n_old * it.w_prev + jnp.sum(it.kw, axis=0, keepdims=True)
        den = jnp.sum(it.sc, axis=-1, keepdims=True) + jnp.sum(it.qm * n_old, axis=-1, keepdims=True) * it.w_int
        it.den = jnp.maximum(jnp.abs(den), jnp.exp(-it.m_t))

    def emit_phase(it):
        sl = slice(it.k * ML_DH, (it.k + 1) * ML_DH)
        og = o_ref[it.b, :, sl].astype(F32)
        yn = _groupnorm(it.num / it.den, nw_ref[:, sl], True) * _sigmoid(og)
        y_ref[it.b, :, sl] = yn.astype(y_ref.dtype)

    def finish(s):
        m_row_new = s.m_row
        for it in s.heads:
            c_ref[s.b, it.k] = it.c_new
            n_ref[s.b, it.k:it.k + 1, :] = it.n_new
            m_row_new = jnp.where(lane_m == it.k, it.m_new, m_row_new)
        m_ref[s.b] = m_row_new

    _seq_loop(nb, 1, [load, gates_phase], ML_H,
              [conv_phase, weights_phase, qk_phase, scores_phase, mix_phase, emit_phase], finish)


def _ml_prefill(proj, small, cw, cb, wq, wk, gb, nw, *, layer, nb, seq):
    c = CHUNK
    p3 = proj.reshape(nb, seq, N_MAIN)
    s3 = small.reshape(nb, seq, N_SMALL)
    w = ML_H * ML_DH
    full2 = lambda t: (0, 0)
    return pl.pallas_call(
        _ml_prefill_kernel,
        grid=(seq // c,),
        in_specs=[
            pl.BlockSpec((nb, c, w), lambda t: (0, t, C_U // w)),
            pl.BlockSpec((nb, c, w), lambda t: (0, jnp.maximum(t - 1, 0), C_U // w)),
            pl.BlockSpec((nb, c, w), lambda t: (0, t, C_VM // w)),
            pl.BlockSpec((nb, c, w), lambda t: (0, t, C_OM // w)),
            pl.BlockSpec((nb, c, N_SMALL), lambda t: (0, t, 0)),
            _lsel(layer, CONV_K, w),
            _lsel(layer, 1, w),
            _lsel(layer, ML_H, ML_DH, ML_DH),
            _lsel(layer, ML_H, ML_DH, ML_DH),
            _lsel(layer, 1, N_SMALL),
            _lsel(layer, 1, w)],
        out_specs=[pl.BlockSpec((nb, c, w), lambda t: (0, t, 0)),
                   pl.BlockSpec((nb, ML_H, ML_DH, ML_DH), lambda t: (0, 0, 0, 0)),
                   pl.BlockSpec((nb, ML_H, ML_DH), lambda t: (0, 0, 0)),
                   pl.BlockSpec((nb, 1, 128), lambda t: (0, 0, 0)),
                   pl.BlockSpec((nb, CONV_K - 1, w), lambda t: (0, 0, 0))],
        out_shape=[jax.ShapeDtypeStruct((nb, seq, w), BF16),
                   jax.ShapeDtypeStruct((nb, ML_H, ML_DH, ML_DH), F32),
                   jax.ShapeDtypeStruct((nb, ML_H, ML_DH), F32),
                   jax.ShapeDtypeStruct((nb, 1, 128), F32),
                   jax.ShapeDtypeStruct((nb, CONV_K - 1, w), F32)],
        scratch_shapes=[pltpu.VMEM((16, w), F32)],
        compiler_params=_cparams(1),
        name="ml_prefill",
    )(p3, p3, p3, p3, s3, cw, cb, wq, wk, gb, nw)


def _block_bcast(tile, lanes):
    c = tile.shape[0]
    lane = lax.broadcasted_iota(jnp.int32, (c, 128), 1)
    cols = [jnp.broadcast_to(tile[:, l:l + 1], (c, 128)) for l in lanes]
    left = jnp.where(lane < SSM_P, cols[0], cols[1])
    right = jnp.where(lane < SSM_P, cols[2], cols[3])
    return jnp.concatenate([left, right], axis=1)


def _ssd_prefill_kernel(z_ref, xs_ref, xsp_ref, bc_ref, bcp_ref, sm_ref, cw_ref, cb_ref, dtb_ref, alog_ref,
                        dv_ref, nw_ref, y_ref, s_ref, cv_ref, *rest):
    *zero_refs, tail = rest
    _zero_fill(zero_refs)
    c = CHUNK
    nb = z_ref.shape[0]
    wx = SSM_H * SSM_P
    gw = SSM_R * SSM_P
    first = pl.program_id(0) == 0

    @pl.when(first)
    def _():
        s_ref[...] = jnp.zeros_like(s_ref)

    row = lax.broadcasted_iota(jnp.int32, (c, c), 0)
    col = lax.broadcasted_iota(jnp.int32, (c, c), 1)
    causal = row >= col
    tril_b = jnp.where(causal, 1.0, 0.0).astype(BF16)
    lane_blk = lax.shift_right_logical(lax.broadcasted_iota(jnp.int32, (c, gw), 1), 6)
    a_row = -jnp.exp(alog_ref[...])
    shift = _shift_matrix(first)

    def conv(cur_ref, prv_ref, b, lanes, w_off):
        wl = slice(w_off + lanes.start, w_off + lanes.stop)
        return _conv_silu(shift, prv_ref[b, :, lanes], cur_ref[b, :, lanes], cw_ref[:, wl], cb_ref[:, wl])

    def load(s):
        s.s_old = [s_ref[s.b, g * SSM_R:(g + 1) * SSM_R].reshape(gw, SSM_N) for g in range(SSM_G)]

    def dt_phase(s):
        b = s.b
        tail[:, 0:wx] = xs_ref[b, c - 16:c, :].astype(F32)
        tail[:, wx:2 * wx] = bc_ref[b, c - 16:c, :].astype(F32)
        cv_ref[b] = tail[16 - (CONV_K - 1):16, :]
        s.delta = _softplus(sm_ref[b] + dtb_ref[...])
        s.cum = _cumsum_rows(s.delta * a_row, tril_b)
        s.cum_t = s.cum.T
        s.delta_t = s.delta.T
        s.ecum = jnp.exp(s.cum)
        cum_end = s.cum[c - 1:c, :]
        s.w_state = jnp.exp(cum_end - s.cum) * s.delta
        s.dec_row = jnp.exp(cum_end)

    def conv_phase(it):
        g = it.k
        it.lanes = [L_DT + g * SSM_R + r for r in range(SSM_R)]
        it.xg = conv(xs_ref, xsp_ref, it.b, slice(g * gw, (g + 1) * gw), 0)
        it.bg = conv(bc_ref, bcp_ref, it.b, slice(g * SSM_N, (g + 1) * SSM_N), wx).astype(BF16)
        it.cg = conv(bc_ref, bcp_ref, it.b, slice((SSM_G + g) * SSM_N, (SSM_G + g + 1) * SSM_N), wx).astype(BF16)

    def seg_phase(it):
        s = it.seq
        it.seg = [jnp.exp(jnp.where(causal, s.cum[:, ln:ln + 1] - s.cum_t[ln:ln + 1, :], -jnp.inf))
                  * s.delta_t[ln:ln + 1, :] for ln in it.lanes]
        it.x_stack = jnp.concatenate([jnp.where(lane_blk == r, it.xg, 0.0).astype(BF16) for r in range(SSM_R)],
                                     axis=0)
        it.xwt = (it.xg * _block_bcast(s.w_state, it.lanes)).T.astype(BF16)

    def cb_phase(it):
        it.cb_mat = _dot_nt(it.cg, it.bg)

    def mix_phase(it):
        s = it.seq
        s_old = s.s_old[it.k]
        m_wide = jnp.concatenate([(it.cb_mat * sg).astype(BF16) for sg in it.seg], axis=1)
        it.y = (_dot(m_wide, it.x_stack)
                + _dot_nt(it.cg, s_old.astype(BF16)) * _block_bcast(s.ecum, it.lanes))
        upd = _dot(it.xwt, it.bg)
        it.s_new = [s_old[r * SSM_P:(r + 1) * SSM_P] * s.dec_row[:, ln:ln + 1] + upd[r * SSM_P:(r + 1) * SSM_P]
                    for r, ln in enumerate(it.lanes)]

    def emit_phase(it):
        gs = slice(it.k * gw, (it.k + 1) * gw)
        y = it.y + dv_ref[:, gs] * it.xg
        z = z_ref[it.b, :, gs].astype(F32)
        yn = _groupnorm(y * _silu(z), nw_ref[:, gs], False)
        y_ref[it.b, :, gs] = yn.astype(y_ref.dtype)

    def finish(s):
        for it in s.heads:
            for r in range(SSM_R):
                s_ref[s.b, it.k * SSM_R + r] = it.s_new[r]

    _seq_loop(nb, 2, [load, dt_phase], SSM_G, [conv_phase, seg_phase, cb_phase, mix_phase, emit_phase], finish)


def _ssd_prefill(proj, small, cw, cb, dtb, alog, dvec, nw, *, layer, nb, seq, zero_shape=None):
    c = CHUNK
    wx = SSM_H * SSM_P
    p3 = proj.reshape(nb, seq, N_MAIN)
    s3 = small.reshape(nb, seq, N_SMALL)
    z_specs, z_shapes = ([], []) if zero_shape is None else [[v] for v in _zero_out(zero_shape, seq // c)]
    return pl.pallas_call(
        _ssd_prefill_kernel,
        grid=(seq // c,),
        in_specs=[
            pl.BlockSpec((nb, c, wx), lambda t: (0, t, C_Z // wx)),
            pl.BlockSpec((nb, c, wx), lambda t: (0, t, C_XS // wx)),
            pl.BlockSpec((nb, c, wx), lambda t: (0, jnp.maximum(t - 1, 0), C_XS // wx)),
            pl.BlockSpec((nb, c, wx), lambda t: (0, t, C_B // wx)),
            pl.BlockSpec((nb, c, wx), lambda t: (0, jnp.maximum(t - 1, 0), C_B // wx)),
            pl.BlockSpec((nb, c, N_SMALL), lambda t: (0, t, 0)),
            _lsel(layer, CONV_K, 2 * wx),
            _lsel(layer, 1, 2 * wx),
            _lsel(layer, 1, N_SMALL),
            _lsel(layer, 1, N_SMALL),
            _lsel(layer, 1, wx),
            _lsel(layer, 1, wx)],
        out_specs=[pl.BlockSpec((nb, c, wx), lambda t: (0, t, 0)),
                   pl.BlockSpec((nb, SSM_H, SSM_P, SSM_N), lambda t: (0, 0, 0, 0)),
                   pl.BlockSpec((nb, CONV_K - 1, 2 * wx), lambda t: (0, 0, 0))] + z_specs,
        out_shape=[jax.ShapeDtypeStruct((nb, seq, wx), BF16),
                   jax.ShapeDtypeStruct((nb, SSM_H, SSM_P, SSM_N), F32),
                   jax.ShapeDtypeStruct((nb, CONV_K - 1, 2 * wx), F32)] + z_shapes,
        scratch_shapes=[pltpu.VMEM((16, 2 * wx), F32)],
        compiler_params=_cparams(1),
        name="ssd_prefill",
    )(p3, p3, p3, p3, p3, s3, cw, cb, dtb, alog, dvec, nw)


MERGE_SUB = 256


def _merge_kernel(yr_ref, ym_ref, ys_ref, gr_ref, gm_ref, gs_ref, x_ref, wr_ref, wm_ref, ws_ref,
                  wo_ref, nw_ref, wq_ref, xo_ref, qo_ref, *zero_refs):
    _zero_fill(zero_refs)
    tm = x_ref.shape[0]
    n_sub = max(1, tm // MERGE_SUB)
    subs = [slice(i * (tm // n_sub), (i + 1) * (tm // n_sub)) for i in range(n_sub)]
    branches = ((yr_ref, gr_ref, wr_ref), (ym_ref, gm_ref, wm_ref), (ys_ref, gs_ref, ws_ref))
    proj = [[_dot(y_ref[rs, :].astype(BF16), w_ref[...]) for y_ref, _, w_ref in branches] for rs in subs]
    merged = []
    for rs, pr in zip(subs, proj):
        gated = [_sigmoid(g_ref[rs, :].astype(F32)) * p for (_, g_ref, _), p in zip(branches, pr)]
        merged.append((gated[0] + gated[1] + gated[2]).astype(BF16))
    xns = [x_ref[rs, :] + _dot(m, wo_ref[...]) for rs, m in zip(subs, merged)]
    hqs = []
    for rs, xn in zip(subs, xns):
        xo_ref[rs, :] = xn
        hqs.append(_rmsnorm(xn, nw_ref[...]).astype(BF16))
    for rs, hq in zip(subs, hqs):
        qo_ref[rs, :] = _dot(hq, wq_ref[...]).astype(qo_ref.dtype)


def _merge(yr, ym, ys, proj, x, wr, wm, ws, wo, nw, wq, *, layer, tm, q_dtype, zero_shape=None):
    m = x.shape[0]
    d = D_MODEL
    row = lambda i: (i, 0)
    z_specs, z_shapes = ([], []) if zero_shape is None else [[v] for v in _zero_out(zero_shape, m // tm)]
    wspec = _lsel(layer, d, d, pipeline_mode=pl.Buffered(1))
    return pl.pallas_call(
        _merge_kernel,
        grid=(m // tm,),
        in_specs=[pl.BlockSpec((tm, d), row), pl.BlockSpec((tm, d), row), pl.BlockSpec((tm, d), row),
                  pl.BlockSpec((tm, d), lambda i: (i, C_GR // d)),
                  pl.BlockSpec((tm, d), lambda i: (i, C_GM // d)),
                  pl.BlockSpec((tm, d), lambda i: (i, C_GS // d)),
                  pl.BlockSpec((tm, d), row),
                  wspec, wspec, wspec, wspec,
                  _lsel(layer, 1, d),
                  wspec],
        out_specs=[pl.BlockSpec((tm, d), row), pl.BlockSpec((tm, d), row)] + z_specs,
        out_shape=[jax.ShapeDtypeStruct((m, d), F32), jax.ShapeDtypeStruct((m, d), q_dtype)] + z_shapes,
        compiler_params=_cparams(1),
        name="merge",
    )(yr, ym, ys, proj, proj, proj, x, wr, wm, ws, wo, nw, wq)


def _attn_prefill_kernel(q_ref, k_ref, v_ref, x_ref, wo_ref, o_ref):
    heads = [slice(h * MEM_DH, (h + 1) * MEM_DH) for h in range(MEM_H)]
    scores = [_dot_nt(q_ref[0, :, sl], k_ref[0, :, sl].astype(BF16)) * (MEM_DH ** -0.5) for sl in heads]
    probs = []
    for s in scores:
        p = jnp.exp(s - jnp.max(s, axis=-1, keepdims=True))
        probs.append((p / jnp.sum(p, axis=-1, keepdims=True)).astype(BF16))
    outs = [_dot(p, v_ref[0, :, sl].astype(BF16)).astype(BF16) for p, sl in zip(probs, heads)]
    o_all = jnp.concatenate(outs, axis=1)
    o_ref[0] = x_ref[0] + _dot(o_all, wo_ref[...])


def _attn_prefill(q, mk, mv, x, wo, *, layer, nb, seq, tq):
    d = D_MODEL
    q3 = q.reshape(nb, seq, d)
    x3 = x.reshape(nb, seq, d)
    out = pl.pallas_call(
        _attn_prefill_kernel,
        grid=(nb, seq // tq),
        in_specs=[pl.BlockSpec((1, tq, d), lambda b, t: (b, t, 0)),
                  pl.BlockSpec((1, MEM_LEN, d), lambda b, t: (layer * nb + b, 0, 0)),
                  pl.BlockSpec((1, MEM_LEN, d), lambda b, t: (layer * nb + b, 0, 0)),
                  pl.BlockSpec((1, tq, d), lambda b, t: (b, t, 0)),
                  _lsel(layer, d, d)],
        out_specs=pl.BlockSpec((1, tq, d), lambda b, t: (b, t, 0)),
        out_shape=jax.ShapeDtypeStruct((nb, seq, d), F32),
        compiler_params=_cparams(2),
        name="attn_prefill",
    )(q3, mk, mv, x3, wo)
    return out.reshape(nb * seq, d)


def _attn_decode_kernel(q_ref, k_ref, v_ref, o_ref):
    for j in range(q_ref.shape[0]):
        s = jnp.sum(k_ref[j] * q_ref[j][None], axis=-1, keepdims=True) * (MEM_DH ** -0.5)
        p = jnp.exp(s - jnp.max(s, axis=0, keepdims=True))
        p = p / jnp.sum(p, axis=0, keepdims=True)
        o_ref[j] = jnp.sum(p * v_ref[j], axis=0)


def _attn_decode(q, ck, cv, *, layer, bb):
    nb = q.shape[0]
    nblk = nb // bb
    blk = (bb, MEM_LEN, MEM_H, MEM_DH)
    return pl.pallas_call(
        _attn_decode_kernel,
        grid=(nblk,),
        in_specs=[pl.BlockSpec((bb, MEM_H, MEM_DH), lambda i: (i, 0, 0)),
                  pl.BlockSpec(blk, lambda i: (layer * nblk + i, 0, 0, 0)),
                  pl.BlockSpec(blk, lambda i: (layer * nblk + i, 0, 0, 0))],
        out_specs=pl.BlockSpec((bb, MEM_H, MEM_DH), lambda i: (i, 0, 0)),
        out_shape=jax.ShapeDtypeStruct((nb, MEM_H, MEM_DH), F32),
        compiler_params=_cparams(1),
        name="attn_decode",
    )(q, ck, cv)


MLP_SUB = 1024


def _mlp_kernel(x_ref, nw_ref, w1_ref, w2_ref, nf_ref, o_ref, xn_ref, acc_ref, *, final_norm):
    j = pl.program_id(1)

    @pl.when(j == 0)
    def _():
        xn_ref[...] = _rmsnorm(x_ref[...], nw_ref[...]).astype(BF16)
        acc_ref[...] = jnp.zeros_like(acc_ref)

    subs = [slice(s * MLP_SUB, (s + 1) * MLP_SUB) for s in range(w1_ref.shape[1] // MLP_SUB)]
    hids = [_dot(xn_ref[...], w1_ref[:, sl]) for sl in subs]
    hids = [jnp.square(jnp.maximum(hid, 0.0)).astype(BF16) for hid in hids]
    acc = acc_ref[...]
    for hid, sl in zip(hids, subs):
        acc = acc + _dot(hid, w2_ref[sl, :])
    acc_ref[...] = acc

    @pl.when(j == pl.num_programs(1) - 1)
    def _():
        y = x_ref[...] + acc_ref[...]
        if final_norm:
            y = _rmsnorm(y, nf_ref[...])
        o_ref[...] = y


def _mlp(x, nw, w1, w2, nf, *, layer, tm, tf, final_norm):
    m = x.shape[0]
    d = D_MODEL
    return pl.pallas_call(
        functools.partial(_mlp_kernel, final_norm=final_norm),
        grid=(m // tm, D_FF // tf),
        in_specs=[pl.BlockSpec((tm, d), lambda i, j: (i, 0)),
                  _lsel(layer, 1, d),
                  pl.BlockSpec((None, d, tf), lambda i, j: (layer, 0, j)),
                  pl.BlockSpec((None, tf, d), lambda i, j: (layer, j, 0)),
                  pl.BlockSpec((1, d), lambda i, j: (0, 0))],
        out_specs=pl.BlockSpec((tm, d), lambda i, j: (i, 0)),
        out_shape=jax.ShapeDtypeStruct((m, d), F32),
        scratch_shapes=[pltpu.VMEM((tm, d), BF16), pltpu.VMEM((tm, d), F32)],
        compiler_params=_cparams(2),
        name="mlp",
    )(x, nw, w1, w2, nf)


DEC_BB = 8


def _row_mask(x, j):
    rows = lax.broadcasted_iota(jnp.int32, x.shape, 0)
    return jnp.where(rows == j, x, 0.0).astype(BF16)


def _blockdiag_rows(v):
    n = v.shape[0]
    tiled = jnp.concatenate([v] * n, axis=1)
    rows = lax.broadcasted_iota(jnp.int32, tiled.shape, 0)
    blk = lax.shift_right_logical(lax.broadcasted_iota(jnp.int32, tiled.shape, 1), 8)
    return _pad_rows(jnp.where(rows == blk, tiled, 0.0), 128).astype(BF16)


def _ret_decode_kernel(q_ref, k_ref, v_ref, g_ref, cos_ref, sin_ref, nw_ref, s_ref, y_ref, so_ref, ybuf):
    cosf = cos_ref[...]
    sinf = sin_ref[...]
    for h in range(RET_H):
        sk = slice(h * RET_DK, (h + 1) * RET_DK)
        sv = slice(h * RET_DV, (h + 1) * RET_DV)
        g_dec = math.exp(_LOG_G[h])
        q = q_ref[:, sk]
        k = k_ref[:, sk]
        qr = q * cosf + pltpu.roll(q, RET_DK // 2, 1) * sinf
        kr = (k * cosf + pltpu.roll(k, RET_DK // 2, 1) * sinf) * (RET_DK ** -0.5)
        k_t = _pad_rows_t(kr).astype(BF16)
        q16 = _pad_rows(qr, 16).astype(BF16)
        upd = _dot(k_t, _blockdiag_rows(v_ref[:, sv]))
        s_wide = []
        for j in range(DEC_BB):
            s_new = s_ref[j, h] * g_dec + upd[:, j * RET_DV:(j + 1) * RET_DV]
            so_ref[j, h] = s_new
            s_wide.append(s_new.astype(BF16))
        y_all = _dot(q16, jnp.concatenate(s_wide, axis=1))
        for j in range(DEC_BB):
            ybuf[j:j + 1, sv] = y_all[j:j + 1, j * RET_DV:(j + 1) * RET_DV]
        y_ref[:, sv] = _groupnorm(ybuf[:, sv], nw_ref[:, sv], True) * _silu(g_ref[:, sv])


def _ret_decode(proj, cosf, sinf, nw, state, prev, *, layer):
    nb = proj.shape[0]
    bb = DEC_BB
    nblk = nb // bb
    kern, steps, ib, ob, al_args, al_specs, al_map = _carried_call(_ret_decode_kernel, nblk, layer, prev)
    sblk = (bb, RET_H, RET_DK, RET_DV)
    return pl.pallas_call(
        kern,
        grid=(steps,),
        in_specs=al_specs + [
            pl.BlockSpec((bb, 512), lambda i: (ib(i), C_Q // 512)),
            pl.BlockSpec((bb, 512), lambda i: (ib(i), C_K // 512)),
            pl.BlockSpec((bb, 1024), lambda i: (ib(i), C_V // 1024)),
            pl.BlockSpec((bb, 1024), lambda i: (ib(i), C_G // 1024)),
            pl.BlockSpec((1, RET_DK), lambda i: (0, 0)),
            pl.BlockSpec((1, RET_DK), lambda i: (0, 0)),
            _lsel(layer, 1, 1024),
            pl.BlockSpec(sblk, lambda i: (layer * nblk + ib(i), 0, 0, 0))],
        out_specs=[pl.BlockSpec((bb, 1024), lambda i: (ib(i), 0)),
                   pl.BlockSpec(sblk, lambda i: (ob(i), 0, 0, 0))],
        out_shape=[jax.ShapeDtypeStruct((nb, RET_H * RET_DV), F32),
                   jax.ShapeDtypeStruct((DEPTH * nb, RET_H, RET_DK, RET_DV), F32)],
        scratch_shapes=[pltpu.VMEM((bb, RET_H * RET_DV), F32)],
        input_output_aliases=al_map,
        compiler_params=_cparams(1),
        name="ret_decode",
    )(*al_args, proj, proj, proj, proj, cosf, sinf, nw, state)


def _conv_step(x, buf_ref, cw_ref, cb_ref, bo_ref):
    acc = cb_ref[...]
    for j in range(CONV_K - 1):
        acc = acc + buf_ref[j] * cw_ref[j:j + 1, :]
    acc = acc + x * cw_ref[CONV_K - 1:CONV_K, :]
    bo_ref[0] = buf_ref[1]
    bo_ref[1] = buf_ref[2]
    bo_ref[2] = x
    return _silu(acc)


def _ml_decode_kernel(u_ref, v_ref, o_ref, sm_ref, buf_ref, cw_ref, cb_ref, wq_ref, wk_ref, gb_ref, nw_ref,
                      c_ref, n_ref, m_ref,
                      y_ref, co_ref, no_ref, mo_ref, bo_ref, ybuf):
    uc = _conv_step(u_ref[...], buf_ref, cw_ref, cb_ref, bo_ref).astype(BF16)
    gates = sm_ref[...] + gb_ref[...]
    lane4 = lax.broadcasted_iota(jnp.int32, m_ref.shape, 1)
    m_all = m_ref[...]
    m_out = m_all
    for h in range(ML_H):
        sl = slice(h * ML_DH, (h + 1) * ML_DH)
        qm = _dot(uc[:, sl], wq_ref[h])
        km = _dot(uc[:, sl], wk_ref[h]) * (ML_DH ** -0.5)
        i_pre = gates[:, L_IG + h:L_IG + h + 1]
        log_f = _log_sigmoid(gates[:, L_FG + h:L_FG + h + 1])
        m_prev = m_all[:, h:h + 1]
        inter = log_f + m_prev
        m_new = jnp.maximum(inter, i_pre)
        w_s = jnp.exp(i_pre - m_new)
        w_prev = jnp.exp(inter - m_new)
        kw = km * w_s
        n_new = n_ref[:, sl] * w_prev + kw
        no_ref[:, sl] = n_new
        m_out = jnp.where(lane4 == h, m_new, m_out)

        k_t = _pad_rows_t(kw).astype(BF16)
        q16 = _pad_rows(qm, 16).astype(BF16)
        upd = _dot(k_t, _blockdiag_rows(v_ref[:, sl]))
        c_wide = []
        for j in range(DEC_BB):
            c_new = c_ref[j, h] * w_prev[j:j + 1, :] + upd[:, j * ML_DH:(j + 1) * ML_DH]
            co_ref[j, h] = c_new
            c_wide.append(c_new.astype(BF16))
        y_all = _dot(q16, jnp.concatenate(c_wide, axis=1))
        for j in range(DEC_BB):
            ybuf[j:j + 1, sl] = y_all[j:j + 1, j * ML_DH:(j + 1) * ML_DH]
        den = jnp.sum(qm * n_new, axis=-1, keepdims=True)
        den = jnp.maximum(jnp.abs(den), jnp.exp(-m_new))
        y_ref[:, sl] = _groupnorm(ybuf[:, sl] / den, nw_ref[:, sl], True) * _sigmoid(o_ref[:, sl])
    mo_ref[...] = m_out


def _ml_decode(proj, small, buf, cw, cb, wq, wk, gb, nw, c_state, n_state, m_state, prev, *, layer):
    nb = proj.shape[0]
    bb = DEC_BB
    nblk = nb // bb
    dh = ML_DH
    w = ML_H * ML_DH
    kern, steps, ib, ob, al_args, al_specs, al_map = _carried_call(_ml_decode_kernel, nblk, layer, prev)
    cblk = (bb, ML_H, dh, dh)
    return pl.pallas_call(
        kern,
        grid=(steps,),
        in_specs=al_specs + [
            pl.BlockSpec((bb, w), lambda i: (ib(i), C_U // w)),
            pl.BlockSpec((bb, w), lambda i: (ib(i), C_VM // w)),
            pl.BlockSpec((bb, w), lambda i: (ib(i), C_OM // w)),
            pl.BlockSpec((bb, N_SMALL), lambda i: (ib(i), 0)),
            pl.BlockSpec((None, CONV_K - 1, bb, w), lambda i: (layer, 0, ib(i), 0)),
            _lsel(layer, CONV_K, w),
            _lsel(layer, 1, w),
            _lsel(layer, ML_H, dh, dh),
            _lsel(layer, ML_H, dh, dh),
            _lsel(layer, 1, N_SMALL),
            _lsel(layer, 1, w),
            pl.BlockSpec(cblk, lambda i: (layer * nblk + ib(i), 0, 0, 0)),
            pl.BlockSpec((bb, w), lambda i: (layer * nblk + ib(i), 0)),
            pl.BlockSpec((bb, ML_H), lambda i: (layer * nblk + ib(i), 0))],
        out_specs=[pl.BlockSpec((bb, w), lambda i: (ib(i), 0)),
                   pl.BlockSpec(cblk, lambda i: (ob(i), 0, 0, 0)),
                   pl.BlockSpec((bb, w), lambda i: (ib(i), 0)),
                   pl.BlockSpec((bb, ML_H), lambda i: (ib(i), 0)),
                   pl.BlockSpec((CONV_K - 1, bb, w), lambda i: (0, ib(i), 0))],
        out_shape=[jax.ShapeDtypeStruct((nb, w), F32),
                   jax.ShapeDtypeStruct((DEPTH * nb, ML_H, dh, dh), F32),
                   jax.ShapeDtypeStruct((nb, w), F32),
                   jax.ShapeDtypeStruct((nb, ML_H), F32),
                   jax.ShapeDtypeStruct((CONV_K - 1, nb, w), F32)],
        scratch_shapes=[pltpu.VMEM((bb, w), F32)],
        input_output_aliases=al_map,
        compiler_params=_cparams(1),
        name="ml_decode",
    )(*al_args, proj, proj, proj, small, buf, cw, cb, wq, wk, gb, nw, c_state, n_state, m_state)


def _ssd_decode_kernel(z_ref, xs_ref, bc_ref, sm_ref, buf_ref, cw_ref, cb_ref, dtb_ref, alog_ref, dv_ref, nw_ref,
                       s_ref, y_ref, so_ref, bo_ref, ybuf):
    wx = SSM_H * SSM_P
    gw = SSM_R * SSM_P
    x_in = jnp.concatenate([xs_ref[...], bc_ref[...]], axis=1)
    xc = _conv_step(x_in, buf_ref, cw_ref, cb_ref, bo_ref)
    xs = xc[:, 0:wx]

    delta = _softplus(sm_ref[...] + dtb_ref[...])
    d_a = jnp.exp(delta * (-jnp.exp(alog_ref[...])))
    lane_blk = lax.shift_right_logical(lax.broadcasted_iota(jnp.int32, (DEC_BB, wx), 1), 6)
    dt_full = jnp.zeros((DEC_BB, wx), F32)
    for hh in range(SSM_H):
        dt_full = jnp.where(lane_blk == hh, delta[:, L_DT + hh:L_DT + hh + 1], dt_full)
    x_t = _pad_rows_t(xs * dt_full).astype(BF16)

    for g in range(SSM_G):
        gs = slice(g * gw, (g + 1) * gw)
        b_pad = _pad_rows(xc[:, wx + g * SSM_N:wx + (g + 1) * SSM_N], 128)
        c16 = _pad_rows(xc[:, wx + (SSM_G + g) * SSM_N:wx + (SSM_G + g + 1) * SSM_N], 16).astype(BF16)
        for j in range(DEC_BB):
            upd = _dot(x_t[gs, :], _row_mask(b_pad, j))
            parts = []
            for r in range(SSM_R):
                hh = g * SSM_R + r
                s_new = (s_ref[j, hh] * d_a[j:j + 1, L_DT + hh:L_DT + hh + 1]
                         + upd[r * SSM_P:(r + 1) * SSM_P])
                so_ref[j, hh] = s_new
                parts.append(s_new.astype(BF16))
            ybuf[j:j + 1, gs] = _dot_nt(c16, jnp.concatenate(parts, axis=0))[j:j + 1, :]
        y = ybuf[:, gs] + dv_ref[:, gs] * xs[:, gs]
        y_ref[:, gs] = _groupnorm(y * _silu(z_ref[:, gs]), nw_ref[:, gs], False)


def _ssd_decode(proj, small, buf, cw, cb, dtb, alog, dvec, nw, state, prev, *, layer):
    nb = proj.shape[0]
    bb = DEC_BB
    nblk = nb // bb
    wx = SSM_H * SSM_P
    kern, steps, ib, ob, al_args, al_specs, al_map = _carried_call(_ssd_decode_kernel, nblk, layer, prev)
    sblk = (bb, SSM_H, SSM_P, SSM_N)
    return pl.pallas_call(
        kern,
        grid=(steps,),
        in_specs=al_specs + [
            pl.BlockSpec((bb, wx), lambda i: (ib(i), C_Z // wx)),
            pl.BlockSpec((bb, wx), lambda i: (ib(i), C_XS // wx)),
            pl.BlockSpec((bb, wx), lambda i: (ib(i), C_B // wx)),
            pl.BlockSpec((bb, N_SMALL), lambda i: (ib(i), 0)),
            pl.BlockSpec((None, CONV_K - 1, bb, 2 * wx), lambda i: (layer, 0, ib(i), 0)),
            _lsel(layer, CONV_K, 2 * wx),
            _lsel(layer, 1, 2 * wx),
            _lsel(layer, 1, N_SMALL),
            _lsel(layer, 1, N_SMALL),
            _lsel(layer, 1, wx),
            _lsel(layer, 1, wx),
            pl.BlockSpec(sblk, lambda i: (layer * nblk + ib(i), 0, 0, 0))],
        out_specs=[pl.BlockSpec((bb, wx), lambda i: (ib(i), 0)),
                   pl.BlockSpec(sblk, lambda i: (ob(i), 0, 0, 0)),
                   pl.BlockSpec((CONV_K - 1, bb, 2 * wx), lambda i: (0, ib(i), 0))],
        out_shape=[jax.ShapeDtypeStruct((nb, wx), F32),
                   jax.ShapeDtypeStruct((DEPTH * nb, SSM_H, SSM_P, SSM_N), F32),
                   jax.ShapeDtypeStruct((CONV_K - 1, nb, 2 * wx), F32)],
        scratch_shapes=[pltpu.VMEM((bb, wx), F32)],
        input_output_aliases=al_map,
        compiler_params=_cparams(1),
        name="ssd_decode",
    )(*al_args, proj, proj, proj, small, buf, cw, cb, dtb, alog, dvec, nw, state)


def _rope_tables(pos):
    half = RET_DK // 2
    freqs = ROPE_THETA ** (-jnp.arange(half, dtype=F32) / half)
    ang = pos.astype(F32)[:, None] * freqs[None, :]
    cos = jnp.cos(ang)
    sin = jnp.sin(ang)
    return jnp.concatenate([cos, cos], axis=1), jnp.concatenate([-sin, sin], axis=1)


def _pad_lanes(v, offset):
    n = v.shape[1]
    return jnp.pad(v.astype(F32), ((0, 0), (offset, N_SMALL - offset - n))).reshape(DEPTH, 1, N_SMALL)


REPACK_TN = 1024
REPACK_SHIFTS = ((6, 0), (9, 2 * ML_H), (12, 2 * ML_H + SSM_H))
SRC_GATE_BLK = 6144 // 128
SRC_DT_BLK = (6144 + 2 * ML_H + 3072) // 128


def _repack_kernel(a_ref, b_ref, g_ref, dt_ref, o_ref, os_ref):
    j = pl.program_id(1)
    lane = lax.broadcasted_iota(jnp.int32, (D_MODEL, 128), 1)

    @pl.when(j == 0)
    def _():
        os_ref[...] = jnp.where(lane < L_DT, g_ref[...],
                                jnp.where(lane < L_DT + SSM_H, dt_ref[...], 0.0)).astype(BF16)

    @pl.when(j < REPACK_SHIFTS[0][0])
    def _():
        o_ref[...] = a_ref[...].astype(BF16)

    @pl.when(j >= REPACK_SHIFTS[0][0])
    def _():
        shift = jnp.where(j < REPACK_SHIFTS[1][0], REPACK_SHIFTS[1][1], REPACK_SHIFTS[2][1])
        over = jnp.where(lane < shift, b_ref[...], 0.0)
        src = jnp.concatenate([a_ref[...], over], axis=1).astype(BF16)
        r = lax.broadcasted_iota(jnp.int32, (REPACK_TN + 128, REPACK_TN), 0)
        c = lax.broadcasted_iota(jnp.int32, (REPACK_TN + 128, REPACK_TN), 1)
        sel = jnp.where(r == c + shift, 1.0, 0.0).astype(BF16)
        o_ref[...] = _dot(src, sel).astype(BF16)


def _repack_w_in(w_in):
    tn = REPACK_TN
    return pl.pallas_call(
        _repack_kernel,
        grid=(DEPTH, N_MAIN // tn),
        in_specs=[pl.BlockSpec((None, D_MODEL, tn), lambda l, j: (l, 0, j)),
                  pl.BlockSpec((None, D_MODEL, 128), lambda l, j: (l, 0, (j + 1) * (tn // 128))),
                  pl.BlockSpec((None, D_MODEL, 128), lambda l, j: (l, 0, SRC_GATE_BLK)),
                  pl.BlockSpec((None, D_MODEL, 128), lambda l, j: (l, 0, SRC_DT_BLK))],
        out_specs=[pl.BlockSpec((None, D_MODEL, tn), lambda l, j: (l, 0, j)),
                   pl.BlockSpec((None, D_MODEL, N_SMALL), lambda l, j: (l, 0, 0))],
        out_shape=[jax.ShapeDtypeStruct((DEPTH, D_MODEL, N_MAIN), BF16),
                   jax.ShapeDtypeStruct((DEPTH, D_MODEL, N_SMALL), BF16)],
        compiler_params=_cparams(2),
        name="repack_w_in",
    )(w_in, w_in, w_in, w_in)


def _prep_weights(W):
    w_main, w_small = _repack_w_in(W['w_in'])
    row = lambda v: v.reshape(DEPTH, 1, -1).astype(F32)
    bf = lambda v: v.astype(BF16)
    return dict(
        w_main=w_main, w_small=w_small,
        norm_mix=row(W['norm_mix_w']),
        ret_norm=row(W['ret_norm_w']),
        ml_cw=W['ml_conv_w'], ml_cb=row(W['ml_conv_b']),
        ml_wq=bf(W['ml_wq']), ml_wk=bf(W['ml_wk']),
        ml_gb=_pad_lanes(W['ml_gate_b'], L_IG),
        ml_norm=row(W['ml_norm_w']),
        ssm_cw=W['ssm_conv_w'], ssm_cb=row(W['ssm_conv_b']),
        ssm_dtb=_pad_lanes(W['ssm_dt_bias'], L_DT),
        ssm_alog=_pad_lanes(W['ssm_A_log'], L_DT),
        ssm_dvec=row(jnp.repeat(W['ssm_D'], SSM_P, axis=1)),
        ssm_norm=row(W['ssm_norm_w']),
        w_br_ret=bf(W['w_br_ret']), w_br_ml=bf(W['w_br_ml']),
        w_br_ssm=bf(W['w_br_ssm']), w_out=bf(W['w_out_mix']),
        norm_mem=row(W['norm_mem_w']),
        mem_wq=bf(W['mem_wq']), mem_wo=bf(W['mem_wo']),
        norm_mlp=row(W['norm_mlp_w']),
        mlp_w1=bf(W['mlp_w1']), mlp_w2=bf(W['mlp_w2']),
    )


def kernel(x_prompt, x_sample, mem_prompt, state_ret, state_mlstm_C, state_mlstm_n, state_mlstm_m,
           state_mlstm_conv, state_ssm, state_ssm_conv, cache_mem_k, cache_mem_v,
           norm_mix_w, w_in, ret_norm_w, ml_conv_w, ml_conv_b, ml_wq, ml_wk, ml_gate_b, ml_norm_w,
           ssm_conv_w, ssm_conv_b, ssm_dt_bias, ssm_A_log, ssm_D, ssm_norm_w,
           w_br_ret, w_br_ml, w_br_ssm, w_out_mix, norm_mem_w, mem_wq, mem_wk, mem_wv, mem_wo,
           norm_mlp_w, mlp_w1, mlp_w2, norm_f_w):
    W = dict(norm_mix_w=norm_mix_w, w_in=w_in, ret_norm_w=ret_norm_w, ml_conv_w=ml_conv_w,
             ml_conv_b=ml_conv_b, ml_wq=ml_wq, ml_wk=ml_wk, ml_gate_b=ml_gate_b, ml_norm_w=ml_norm_w,
             ssm_conv_w=ssm_conv_w, ssm_conv_b=ssm_conv_b, ssm_dt_bias=ssm_dt_bias, ssm_A_log=ssm_A_log,
             ssm_D=ssm_D, ssm_norm_w=ssm_norm_w, w_br_ret=w_br_ret, w_br_ml=w_br_ml, w_br_ssm=w_br_ssm,
             w_out_mix=w_out_mix, norm_mem_w=norm_mem_w, mem_wq=mem_wq, mem_wo=mem_wo,
             norm_mlp_w=norm_mlp_w, mlp_w1=mlp_w1, mlp_w2=mlp_w2)
    lw = _prep_weights(W)
    norm_f = norm_f_w.reshape(1, -1).astype(F32)
    d = D_MODEL

    nb, seq = x_prompt.shape[0], x_prompt.shape[1]
    ns = x_sample.shape[0]
    cos_p, sin_p = _rope_tables(jnp.arange(seq, dtype=jnp.int32))
    memk, memv = _memkv(mem_prompt.reshape(nb * MEM_LEN, d), mem_wk.astype(BF16), mem_wv.astype(BF16), tm=1024)
    memk3 = memk.reshape(DEPTH * nb, MEM_LEN, d)
    memv3 = memv.reshape(DEPTH * nb, MEM_LEN, d)

    x = x_prompt.reshape(nb * seq, d)
    states_p = []
    for l in range(DEPTH):
        proj, small = _inproj(x, lw['norm_mix'], lw['w_main'], lw['w_small'], layer=l, tm=2048, tn=1024,
                              out_dtype=BF16)
        zs = (lambda shape: shape) if l == 0 else (lambda shape: None)
        y_r, ret_st, *ret_s = _ret_prefill(proj, cos_p, sin_p, lw['ret_norm'], layer=l, nb=nb, seq=seq,
                                           zero_shape=zs((DEPTH * ns, RET_H, RET_DK, RET_DV)))
        y_m, ml_c, ml_n, ml_m, ml_cv = _ml_prefill(proj, small, lw['ml_cw'], lw['ml_cb'], lw['ml_wq'], lw['ml_wk'],
                                                   lw['ml_gb'], lw['ml_norm'], layer=l, nb=nb, seq=seq)
        y_s, ssm_st, ssm_cv, *ssm_s = _ssd_prefill(proj, small, lw['ssm_cw'], lw['ssm_cb'], lw['ssm_dtb'],
                                                   lw['ssm_alog'], lw['ssm_dvec'], lw['ssm_norm'], layer=l, nb=nb,
                                                   seq=seq, zero_shape=zs((DEPTH * ns, SSM_H, SSM_P, SSM_N)))
        x, qm, *mlc_s = _merge(y_r.reshape(nb * seq, d), y_m.reshape(nb * seq, d), y_s.reshape(nb * seq, d), proj,
                               x, lw['w_br_ret'], lw['w_br_ml'], lw['w_br_ssm'], lw['w_out'], lw['norm_mem'],
                               lw['mem_wq'], layer=l, tm=512, q_dtype=BF16,
                               zero_shape=zs((DEPTH * ns, ML_H, ML_DH, ML_DH)))
        if l == 0:
            zero_states = (ret_s[0], mlc_s[0], ssm_s[0])
        x = _attn_prefill(qm, memk3, memv3, x, lw['mem_wo'], layer=l, nb=nb, seq=seq, tq=512)
        x = _mlp(x, lw['norm_mlp'], lw['mlp_w1'], lw['mlp_w2'], norm_f, layer=l, tm=1024, tf=2048,
                 final_norm=(l == DEPTH - 1))
        states_p.append((ret_st, ml_c, ml_n, ml_m[:, 0, :ML_H], ml_cv, ssm_st, ssm_cv))
    y_prompt = x.reshape(nb, seq, d)
    stack = lambda parts, i: jnp.stack([p[i] for p in parts])
    prompt_states = tuple(stack(states_p, i) for i in range(7))
    memk_p = memk.reshape(DEPTH, nb, MEM_LEN, MEM_H, MEM_DH)
    memv_p = memv.reshape(DEPTH, nb, MEM_LEN, MEM_H, MEM_DH)

    cos_s, sin_s = _rope_tables(PAST_LEN + jnp.arange(1, dtype=jnp.int32))
    ret_state = state_ret.reshape(DEPTH * ns, RET_H, RET_DK, RET_DV)
    mlc_state = state_mlstm_C.reshape(DEPTH * ns, ML_H, ML_DH, ML_DH)
    mln_state = state_mlstm_n.reshape(DEPTH * ns, ML_H * ML_DH)
    mlm_state = state_mlstm_m.reshape(DEPTH * ns, ML_H)
    ssm_st = state_ssm.reshape(DEPTH * ns, SSM_H, SSM_P, SSM_N)
    ck = cache_mem_k.reshape(DEPTH * ns, MEM_LEN, MEM_H, MEM_DH)
    cv = cache_mem_v.reshape(DEPTH * ns, MEM_LEN, MEM_H, MEM_DH)

    ml_buf = jnp.swapaxes(state_mlstm_conv, 1, 2)
    ssm_buf = jnp.swapaxes(state_ssm_conv, 1, 2)

    x = x_sample.reshape(ns, d)
    ret_s, mlc_s, ssm_s = zero_states
    small_s = []
    for l in range(DEPTH):
        proj, small = _inproj(x, lw['norm_mix'], lw['w_main'], lw['w_small'], layer=l, tm=ns, tn=1024, out_dtype=F32)
        y_r, ret_s = _ret_decode(proj, cos_s, sin_s, lw['ret_norm'], ret_state, ret_s,
                                 layer=l)
        y_m, mlc_s, ml_n, ml_m, ml_bo = _ml_decode(proj, small, ml_buf, lw['ml_cw'], lw['ml_cb'], lw['ml_wq'],
                                                   lw['ml_wk'], lw['ml_gb'], lw['ml_norm'],
                                                   mlc_state, mln_state, mlm_state, mlc_s,
                                                   layer=l)
        y_s, ssm_s, ssm_bo = _ssd_decode(proj, small, ssm_buf, lw['ssm_cw'], lw['ssm_cb'], lw['ssm_dtb'],
                                         lw['ssm_alog'], lw['ssm_dvec'], lw['ssm_norm'], ssm_st,
                                         ssm_s, layer=l)
        x, qm = _merge(y_r, y_m, y_s, proj, x, lw['w_br_ret'], lw['w_br_ml'], lw['w_br_ssm'], lw['w_out'],
                       lw['norm_mem'], lw['mem_wq'], layer=l, tm=ns, q_dtype=F32)
        att = _attn_decode(qm.reshape(ns, MEM_H, MEM_DH), ck, cv, layer=l, bb=4)
        x = _mm(att.reshape(ns, d), lw['mem_wo'], x, layer=l, tm=ns, tn=d)
        x = _mlp(x, lw['norm_mlp'], lw['mlp_w1'], lw['mlp_w2'], norm_f, layer=l, tm=ns, tf=2048,
                 final_norm=(l == DEPTH - 1))
        small_s.append((ml_n.reshape(ns, ML_H, ML_DH), ml_m, jnp.swapaxes(ml_bo, 0, 1),
                        jnp.swapaxes(ssm_bo, 0, 1)))
    y_sample = x.reshape(ns, 1, d)
    sample_states = (ret_s.reshape(DEPTH, ns, RET_H, RET_DK, RET_DV),
                     mlc_s.reshape(DEPTH, ns, ML_H, ML_DH, ML_DH),
                     stack(small_s, 0), stack(small_s, 1), stack(small_s, 2),
                     ssm_s.reshape(DEPTH, ns, SSM_H, SSM_P, SSM_N),
                     stack(small_s, 3))

    return (y_prompt, y_sample, *prompt_states, memk_p, memv_p, *sample_states)
```

```python
import functools
import math

import jax
import jax.numpy as jnp
from jax import lax
from jax.experimental import pallas as pl
from jax.experimental.pallas import tpu as pltpu

F32 = jnp.float32
BF16 = jnp.bfloat16

D_MODEL = 1024
DEPTH = 2
PAST_LEN = 16384
CHUNK = 128
CONV_K = 4
EPS = 1e-6
RET_H, RET_DK, RET_DV = 4, 128, 256
ROPE_THETA = 10000.0
ML_H, ML_DH = 4, 256
SSM_H, SSM_P, SSM_G, SSM_N = 16, 64, 4, 128
SSM_R = SSM_H // SSM_G
MEM_LEN, MEM_H, MEM_DH = 256, 4, 256
D_FF = 4 * D_MODEL

C_Q, C_K, C_V, C_G = 0, 512, 1024, 2048
C_U, C_VM, C_OM = 3072, 4096, 5120
C_Z, C_XS, C_B, C_C = 6144, 7168, 8192, 8704
C_GR, C_GM, C_GS = 9216, 10240, 11264
N_MAIN = 12288
L_IG, L_FG, L_DT = 0, 4, 8
N_SMALL = 128

VMEM_LIMIT = 56 * 1024 * 1024
NT_DIMS = (((1,), (1,)), ((), ()))

_LOG_G = [math.log1p(-(2.0 ** (-5.0 - h))) for h in range(RET_H)]


def _cparams(n_axes):
    return pltpu.CompilerParams(dimension_semantics=("arbitrary",) * n_axes,
                                vmem_limit_bytes=VMEM_LIMIT)


def _dot(a, b):
    return jnp.dot(a, b, preferred_element_type=F32)


def _dot_nt(a, b):
    return lax.dot_general(a, b, NT_DIMS, preferred_element_type=F32)


def _rmsnorm(x, w):
    ms = jnp.mean(x * x, axis=-1, keepdims=True)
    return x * lax.rsqrt(ms + EPS) * w


def _groupnorm(x, w, center):
    if center:
        x = x - jnp.mean(x, axis=-1, keepdims=True)
    ms = jnp.mean(x * x, axis=-1, keepdims=True)
    return x * lax.rsqrt(ms + EPS) * w


def _sigmoid(x):
    return 1.0 / (1.0 + jnp.exp(-x))


def _silu(x):
    return x * _sigmoid(x)


def _softplus(x):
    return jnp.maximum(x, 0.0) + jnp.log1p(jnp.exp(-jnp.abs(x)))


def _log_sigmoid(x):
    return -_softplus(-x)


def _cumsum_rows(x, tril_b):
    hi = x.astype(BF16)
    r1 = x - hi.astype(F32)
    mid = r1.astype(BF16)
    lo = (r1 - mid.astype(F32)).astype(BF16)
    return _dot(tril_b, hi) + _dot(tril_b, mid) + _dot(tril_b, lo)


def _pad_rows(x, rows):
    return jnp.concatenate([x, jnp.zeros((rows - x.shape[0], x.shape[1]), x.dtype)], axis=0)


def _pad_rows_t(x):
    return _pad_rows(x, 128).T


def _carried_call(kernel_fn, nblk, layer, prev):
    return (lambda *refs: kernel_fn(*refs[1:]), nblk, lambda i: i, lambda i: layer * nblk + i,
            [prev], [pl.BlockSpec(memory_space=pl.ANY)], {0: 1})


def _zero_out(shape, steps):
    blk = (shape[0] // steps,) + tuple(shape[1:])
    nz = len(shape) - 1
    return pl.BlockSpec(blk, lambda i: (i,) + (0,) * nz), jax.ShapeDtypeStruct(tuple(shape), F32)


def _zero_fill(zero_refs):
    for z_ref in zero_refs:
        z_ref[...] = jnp.zeros_like(z_ref)


def _inproj_kernel(x_ref, nw_ref, w_ref, ws_ref, o_ref, os_ref, xn_ref):
    @pl.when(pl.program_id(1) == 0)
    def _():
        xn = _rmsnorm(x_ref[...], nw_ref[...]).astype(BF16)
        xn_ref[...] = xn
        os_ref[...] = _dot_nt(xn, ws_ref[...])

    o_ref[...] = _dot_nt(xn_ref[...], w_ref[...]).astype(o_ref.dtype)


def _lsel(layer, *block, **kw):
    return pl.BlockSpec((None,) + block, lambda *_: (layer,) + (0,) * len(block), **kw)


def _inproj(x, nw, w_main, w_small, *, layer, tm, tn, out_dtype):
    m = x.shape[0]
    return pl.pallas_call(
        _inproj_kernel,
        grid=(m // tm, N_MAIN // tn),
        in_specs=[pl.BlockSpec((tm, D_MODEL), lambda i, j: (i, 0)),
                  _lsel(layer, 1, D_MODEL),
                  pl.BlockSpec((None, tn, D_MODEL), lambda i, j: (layer, j, 0)),
                  _lsel(layer, N_SMALL, D_MODEL)],
        out_specs=[pl.BlockSpec((tm, tn), lambda i, j: (i, j)),
                   pl.BlockSpec((tm, N_SMALL), lambda i, j: (i, 0))],
        out_shape=[jax.ShapeDtypeStruct((m, N_MAIN), out_dtype),
                   jax.ShapeDtypeStruct((m, N_SMALL), F32)],
        scratch_shapes=[pltpu.VMEM((tm, D_MODEL), BF16)],
        compiler_params=_cparams(2),
        name="inproj",
    )(x, nw, w_main, w_small)


def _mm_kernel(*refs, has_res):
    if has_res:
        a_ref, w_ref, r_ref, o_ref = refs
    else:
        a_ref, w_ref, o_ref = refs
    acc = _dot(a_ref[...].astype(BF16), w_ref[...])
    if has_res:
        acc = r_ref[...] + acc
    o_ref[...] = acc.astype(o_ref.dtype)


def _mm(a, w, res=None, *, layer, tm, tn, out_dtype=F32):
    m, k = a.shape
    n = w.shape[2]
    in_specs = [pl.BlockSpec((tm, k), lambda i, j: (i, 0)),
                pl.BlockSpec((None, k, tn), lambda i, j: (layer, 0, j))]
    args = [a, w]
    if res is not None:
        in_specs.append(pl.BlockSpec((tm, tn), lambda i, j: (i, j)))
        args.append(res)
    return pl.pallas_call(
        functools.partial(_mm_kernel, has_res=res is not None),
        grid=(m // tm, n // tn),
        in_specs=in_specs,
        out_specs=pl.BlockSpec((tm, tn), lambda i, j: (i, j)),
        out_shape=jax.ShapeDtypeStruct((m, n), out_dtype),
        compiler_params=_cparams(2),
        name="mm",
    )(*args)


def _memkv_kernel(a_ref, wk_ref, wv_ref, ok_ref, ov_ref):
    a = a_ref[...].astype(BF16)
    ok_ref[0] = _dot(a, wk_ref[0])
    ov_ref[0] = _dot(a, wv_ref[0])


def _memkv(a, wk, wv, *, tm):
    m = a.shape[0]
    n = wk.shape[2]
    return pl.pallas_call(
        _memkv_kernel,
        grid=(DEPTH, m // tm),
        in_specs=[pl.BlockSpec((tm, D_MODEL), lambda l, i: (i, 0)),
                  pl.BlockSpec((1, D_MODEL, n), lambda l, i: (l, 0, 0)),
                  pl.BlockSpec((1, D_MODEL, n), lambda l, i: (l, 0, 0))],
        out_specs=[pl.BlockSpec((1, tm, n), lambda l, i: (l, i, 0)),
                   pl.BlockSpec((1, tm, n), lambda l, i: (l, i, 0))],
        out_shape=[jax.ShapeDtypeStruct((DEPTH, m, n), F32)] * 2,
        compiler_params=_cparams(2),
        name="memkv",
    )(a, wk, wv)


class _Ctx(dict):
    __getattr__ = dict.__getitem__
    __setattr__ = dict.__setitem__


def _seq_loop(nb, per_trip, seq_phases, n_items, item_phases, finish):
    def body(i, carry):
        seqs = [_Ctx(b=i * per_trip + u) for u in range(per_trip)]
        for phase in seq_phases:
            for s in seqs:
                phase(s)
        items = []
        for s in seqs:
            s.heads = [_Ctx(seq=s, b=s.b, k=k) for k in range(n_items)]
            items.extend(s.heads)
        for phase in item_phases:
            for it in items:
                phase(it)
        for s in seqs:
            finish(s)
        return carry

    lax.fori_loop(0, nb // per_trip, body, 0)


def _ret_prefill_kernel(q_ref, k_ref, v_ref, g_ref, cos_ref, sin_ref, nw_ref, y_ref, s_ref, *zero_refs):
    _zero_fill(zero_refs)

    @pl.when(pl.program_id(0) == 0)
    def _():
        s_ref[...] = jnp.zeros_like(s_ref)

    c = CHUNK
    nb = q_ref.shape[0]
    cosf = cos_ref[...]
    sinf = sin_ref[...]
    row = lax.broadcasted_iota(jnp.int32, (c, c), 0)
    col = lax.broadcasted_iota(jnp.int32, (c, c), 1)
    causal = row >= col
    diff = jnp.where(causal, (row - col).astype(F32), 0.0)
    rowf = row.astype(F32)
    rowf_v = lax.broadcasted_iota(jnp.int32, (c, RET_DV), 0).astype(F32)
    decay = [jnp.where(causal, jnp.exp(lg * diff), 0.0) for lg in _LOG_G]
    q_decay = [jnp.exp(lg * (rowf_v + 1.0)) for lg in _LOG_G]
    k_decay = [jnp.exp(lg * (c - 1.0 - rowf)) * (RET_DK ** -0.5) for lg in _LOG_G]

    def load(s):
        s.s_old = [s_ref[s.b, h] for h in range(RET_H)]

    def rotate(it):
        h = it.k
        sk = slice(h * RET_DK, (h + 1) * RET_DK)
        q = q_ref[it.b, :, sk].astype(F32)
        k = k_ref[it.b, :, sk].astype(F32)
        qr = q * cosf + pltpu.roll(q, RET_DK // 2, 1) * sinf
        kr = k * cosf + pltpu.roll(k, RET_DK // 2, 1) * sinf
        it.qb = qr.astype(BF16)
        it.kb = (kr * (RET_DK ** -0.5)).astype(BF16)
        it.kt = (kr * k_decay[h]).T.astype(BF16)
        it.v = v_ref[it.b, :, h * RET_DV:(h + 1) * RET_DV]

    def scores(it):
        it.sc = (_dot_nt(it.qb, it.kb) * decay[it.k]).astype(BF16)

    def mix(it):
        h = it.k
        s_old = it.seq.s_old[h]
        it.y = _dot(it.sc, it.v) + _dot(it.qb, s_old.astype(BF16)) * q_decay[h]
        it.s_new = s_old * math.exp(_LOG_G[h] * c) + _dot(it.kt, it.v)

    def emit(it):
        sv = slice(it.k * RET_DV, (it.k + 1) * RET_DV)
        g = g_ref[it.b, :, sv].astype(F32)
        yn = _groupnorm(it.y, nw_ref[:, sv], True) * _silu(g)
        y_ref[it.b, :, sv] = yn.astype(y_ref.dtype)

    def finish(s):
        for it in s.heads:
            s_ref[s.b, it.k] = it.s_new

    _seq_loop(nb, 2, [load], RET_H, [rotate, scores, mix, emit], finish)


def _ret_prefill(proj, cosf, sinf, nw, *, layer, nb, seq, zero_shape=None):
    c = CHUNK
    p3 = proj.reshape(nb, seq, N_MAIN)
    z_specs, z_shapes = ([], []) if zero_shape is None else [[v] for v in _zero_out(zero_shape, seq // c)]
    return pl.pallas_call(
        _ret_prefill_kernel,
        grid=(seq // c,),
        in_specs=[
            pl.BlockSpec((nb, c, 512), lambda t: (0, t, C_Q // 512)),
            pl.BlockSpec((nb, c, 512), lambda t: (0, t, C_K // 512)),
            pl.BlockSpec((nb, c, 1024), lambda t: (0, t, C_V // 1024)),
            pl.BlockSpec((nb, c, 1024), lambda t: (0, t, C_G // 1024)),
            pl.BlockSpec((c, RET_DK), lambda t: (t, 0)),
            pl.BlockSpec((c, RET_DK), lambda t: (t, 0)),
            _lsel(layer, 1, 1024)],
        out_specs=[pl.BlockSpec((nb, c, 1024), lambda t: (0, t, 0)),
                   pl.BlockSpec((nb, RET_H, RET_DK, RET_DV), lambda t: (0, 0, 0, 0))] + z_specs,
        out_shape=[jax.ShapeDtypeStruct((nb, seq, 1024), BF16),
                   jax.ShapeDtypeStruct((nb, RET_H, RET_DK, RET_DV), F32)] + z_shapes,
        compiler_params=_cparams(1),
        name="ret_prefill",
    )(p3, p3, p3, p3, cosf, sinf, nw)


def _shift_matrix(first_chunk):
    c = CHUNK
    r = lax.broadcasted_iota(jnp.int32, (3 * c, 2 * c), 0)
    q = lax.broadcasted_iota(jnp.int32, (3 * c, 2 * c), 1)
    k = lax.shift_right_logical(r, 7)
    t = jnp.bitwise_and(r, c - 1)
    hit = (q == c + t - (CONV_K - 1) + k) & (q >= jnp.where(first_chunk, c, 0))
    return jnp.where(hit, 1.0, 0.0).astype(BF16)


def _conv_silu(shift, x_prev, x_cur, cw, cb):
    c = CHUNK
    p = _dot(shift, jnp.concatenate([x_prev, x_cur], axis=0))
    acc = cb
    for j in range(CONV_K - 1):
        acc = acc + p[j * c:(j + 1) * c] * cw[j:j + 1, :]
    acc = acc + x_cur.astype(F32) * cw[CONV_K - 1:CONV_K, :]
    return _silu(acc)


def _ml_prefill_kernel(u_ref, up_ref, v_ref, o_ref, sm_ref, cw_ref, cb_ref, wq_ref, wk_ref, gb_ref, nw_ref,
                       y_ref, c_ref, n_ref, m_ref, cv_ref, tail):
    c = CHUNK
    nb = u_ref.shape[0]
    first = pl.program_id(0) == 0

    @pl.when(first)
    def _():
        c_ref[...] = jnp.zeros_like(c_ref)
        n_ref[...] = jnp.zeros_like(n_ref)
        m_ref[...] = jnp.zeros_like(m_ref)

    row = lax.broadcasted_iota(jnp.int32, (c, c), 0)
    col = lax.broadcasted_iota(jnp.int32, (c, c), 1)
    causal = row >= col
    tril_b = jnp.where(causal, 1.0, 0.0).astype(BF16)
    lane_m = lax.broadcasted_iota(jnp.int32, (1, 128), 1)
    shift = _shift_matrix(first)

    def load(s):
        s.c_old = [c_ref[s.b, h] for h in range(ML_H)]
        s.n_old = n_ref[s.b]
        s.m_row = m_ref[s.b]

    def gates_phase(s):
        tail[...] = u_ref[s.b, c - 16:c, :].astype(F32)
        cv_ref[s.b] = tail[16 - (CONV_K - 1):16, :]
        s.gates = sm_ref[s.b] + gb_ref[...]
        s.b_all = _cumsum_rows(_log_sigmoid(s.gates), tril_b)
        s.b_all_t = s.b_all.T
        s.gates_t = s.gates.T

    def conv_phase(it):
        sl = slice(it.k * ML_DH, (it.k + 1) * ML_DH)
        it.uc = _conv_silu(shift, up_ref[it.b, :, sl], u_ref[it.b, :, sl], cw_ref[:, sl], cb_ref[:, sl]).astype(BF16)
        it.v = v_ref[it.b, :, sl]

    def weights_phase(it):
        h, s = it.k, it.seq
        b_col = s.b_all[:, L_FG + h:L_FG + h + 1]
        b_row = s.b_all_t[L_FG + h:L_FG + h + 1, :]
        it_col = s.gates[:, L_IG + h:L_IG + h + 1]
        it_row = s.gates_t[L_IG + h:L_IG + h + 1, :]
        m_prev = s.m_row[:, h:h + 1]
        logw = jnp.where(causal, b_col - b_row + it_row, -jnp.inf)
        inter = b_col + m_prev
        it.m_t = jnp.maximum(inter, jnp.max(logw, axis=-1, keepdims=True))
        it.dmat = jnp.exp(logw - it.m_t)
        it.w_int = jnp.exp(inter - it.m_t)
        b_end = b_col[c - 1:c, :]
        logw_s = b_end - b_col + it_col
        it.m_new = jnp.maximum(b_end + m_prev, jnp.max(logw_s, axis=0, keepdims=True))
        it.w_s = jnp.exp(logw_s - it.m_new)
        it.w_prev = jnp.exp(b_end + m_prev - it.m_new)

    def qk_phase(it):
        it.qm = _dot(it.uc, wq_ref[it.k])
        it.km = _dot(it.uc, wk_ref[it.k]) * (ML_DH ** -0.5)
        it.qb = it.qm.astype(BF16)
        it.kb = it.km.astype(BF16)
        it.kw = it.km * it.w_s
        it.kwt = it.kw.T.astype(BF16)

    def scores_phase(it):
        it.sc = _dot_nt(it.qb, it.kb) * it.dmat

    def mix_phase(it):
        c_old = it.seq.c_old[it.k]
        n_old = it.seq.n_old[it.k:it.k + 1, :]
        it.num = _dot(it.sc.astype(BF16), it.v) + _dot(it.qb, c_old.astype(BF16)) * it.w_int
        it.c_new = c_old * it.w_prev + _dot(it.kwt, it.v)
        it.n_new = n_old * it.w_prev + jnp.sum(it.kw, axis=0, keepdims=True)
        den = jnp.sum(it.sc, axis=-1, keepdims=True) + jnp.sum(it.qm * n_old, axis=-1, keepdims=True) * it.w_int
        it.den = jnp.maximum(jnp.abs(den), jnp.exp(-it.m_t))

    def emit_phase(it):
        sl = slice(it.k * ML_DH, (it.k + 1) * ML_DH)
        og = o_ref[it.b, :, sl].astype(F32)
        yn = _groupnorm(it.num / it.den, nw_ref[:, sl], True) * _sigmoid(og)
        y_ref[it.b, :, sl] = yn.astype(y_ref.dtype)

    def finish(s):
        m_row_new = s.m_row
        for it in s.heads:
            c_ref[s.b, it.k] = it.c_new
            n_ref[s.b, it.k:it.k + 1, :] = it.n_new
            m_row_new = jnp.where(lane_m == it.k, it.m_new, m_row_new)
        m_ref[s.b] = m_row_new

    _seq_loop(nb, 1, [load, gates_phase], ML_H,
              [conv_phase, weights_phase, qk_phase, scores_phase, mix_phase, emit_phase], finish)


def _ml_prefill(proj, small, cw, cb, wq, wk, gb, nw, *, layer, nb, seq):
    c = CHUNK
    p3 = proj.reshape(nb, seq, N_MAIN)
    s3 = small.reshape(nb, seq, N_SMALL)
    w = ML_H * ML_DH
    full2 = lambda t: (0, 0)
    return pl.pallas_call(
        _ml_prefill_kernel,
        grid=(seq // c,),
        in_specs=[
            pl.BlockSpec((nb, c, w), lambda t: (0, t, C_U // w)),
            pl.BlockSpec((nb, c, w), lambda t: (0, jnp.maximum(t - 1, 0), C_U // w)),
            pl.BlockSpec((nb, c, w), lambda t: (0, t, C_VM // w)),
            pl.BlockSpec((nb, c, w), lambda t: (0, t, C_OM // w)),
            pl.BlockSpec((nb, c, N_SMALL), lambda t: (0, t, 0)),
            _lsel(layer, CONV_K, w),
            _lsel(layer, 1, w),
            _lsel(layer, ML_H, ML_DH, ML_DH),
            _lsel(layer, ML_H, ML_DH, ML_DH),
            _lsel(layer, 1, N_SMALL),
            _lsel(layer, 1, w)],
        out_specs=[pl.BlockSpec((nb, c, w), lambda t: (0, t, 0)),
                   pl.BlockSpec((nb, ML_H, ML_DH, ML_DH), lambda t: (0, 0, 0, 0)),
                   pl.BlockSpec((nb, ML_H, ML_DH), lambda t: (0, 0, 0)),
                   pl.BlockSpec((nb, 1, 128), lambda t: (0, 0, 0)),
                   pl.BlockSpec((nb, CONV_K - 1, w), lambda t: (0, 0, 0))],
        out_shape=[jax.ShapeDtypeStruct((nb, seq, w), BF16),
                   jax.ShapeDtypeStruct((nb, ML_H, ML_DH, ML_DH), F32),
                   jax.ShapeDtypeStruct((nb, ML_H, ML_DH), F32),
                   jax.ShapeDtypeStruct((nb, 1, 128), F32),
                   jax.ShapeDtypeStruct((nb, CONV_K - 1, w), F32)],
        scratch_shapes=[pltpu.VMEM((16, w), F32)],
        compiler_params=_cparams(1),
        name="ml_prefill",
    )(p3, p3, p3, p3, s3, cw, cb, wq, wk, gb, nw)


def _block_bcast(tile, lanes):
    c = tile.shape[0]
    lane = lax.broadcasted_iota(jnp.int32, (c, 128), 1)
    cols = [jnp.broadcast_to(tile[:, l:l + 1], (c, 128)) for l in lanes]
    left = jnp.where(lane < SSM_P, cols[0], cols[1])
    right = jnp.where(lane < SSM_P, cols[2], cols[3])
    return jnp.concatenate([left, right], axis=1)


def _ssd_prefill_kernel(z_ref, xs_ref, xsp_ref, bc_ref, bcp_ref, sm_ref, cw_ref, cb_ref, dtb_ref, alog_ref,
                        dv_ref, nw_ref, y_ref, s_ref, cv_ref, *rest):
    *zero_refs, tail = rest
    _zero_fill(zero_refs)
    c = CHUNK
    nb = z_ref.shape[0]
    wx = SSM_H * SSM_P
    gw = SSM_R * SSM_P
    first = pl.program_id(0) == 0

    @pl.when(first)
    def _():
        s_ref[...] = jnp.zeros_like(s_ref)

    row = lax.broadcasted_iota(jnp.int32, (c, c), 0)
    col = lax.broadcasted_iota(jnp.int32, (c, c), 1)
    causal = row >= col
    tril_b = jnp.where(causal, 1.0, 0.0).astype(BF16)
    lane_blk = lax.shift_right_logical(lax.broadcasted_iota(jnp.int32, (c, gw), 1), 6)
    a_row = -jnp.exp(alog_ref[...])
    shift = _shift_matrix(first)

    def conv(cur_ref, prv_ref, b, lanes, w_off):
        wl = slice(w_off + lanes.start, w_off + lanes.stop)
        return _conv_silu(shift, prv_ref[b, :, lanes], cur_ref[b, :, lanes], cw_ref[:, wl], cb_ref[:, wl])

    def load(s):
        s.s_old = [s_ref[s.b, g * SSM_R:(g + 1) * SSM_R].reshape(gw, SSM_N) for g in range(SSM_G)]

    def dt_phase(s):
        b = s.b
        tail[:, 0:wx] = xs_ref[b, c - 16:c, :].astype(F32)
        tail[:, wx:2 * wx] = bc_ref[b, c - 16:c, :].astype(F32)
        cv_ref[b] = tail[16 - (CONV_K - 1):16, :]
        s.delta = _softplus(sm_ref[b] + dtb_ref[...])
        s.cum = _cumsum_rows(s.delta * a_row, tril_b)
        s.cum_t = s.cum.T
        s.delta_t = s.delta.T
        s.ecum = jnp.exp(s.cum)
        cum_end = s.cum[c - 1:c, :]
        s.w_state = jnp.exp(cum_end - s.cum) * s.delta
        s.dec_row = jnp.exp(cum_end)

    def conv_phase(it):
        g = it.k
        it.lanes = [L_DT + g * SSM_R + r for r in range(SSM_R)]
        it.xg = conv(xs_ref, xsp_ref, it.b, slice(g * gw, (g + 1) * gw), 0)
        it.bg = conv(bc_ref, bcp_ref, it.b, slice(g * SSM_N, (g + 1) * SSM_N), wx).astype(BF16)
        it.cg = conv(bc_ref, bcp_ref, it.b, slice((SSM_G + g) * SSM_N, (SSM_G + g + 1) * SSM_N), wx).astype(BF16)

    def seg_phase(it):
        s = it.seq
        it.seg = [jnp.exp(jnp.where(causal, s.cum[:, ln:ln + 1] - s.cum_t[ln:ln + 1, :], -jnp.inf))
                  * s.delta_t[ln:ln + 1, :] for ln in it.lanes]
        it.x_stack = jnp.concatenate([jnp.where(lane_blk == r, it.xg, 0.0).astype(BF16) for r in range(SSM_R)],
                                     axis=0)
        it.xwt = (it.xg * _block_bcast(s.w_state, it.lanes)).T.astype(BF16)

    def cb_phase(it):
        it.cb_mat = _dot_nt(it.cg, it.bg)

    def mix_phase(it):
        s = it.seq
        s_old = s.s_old[it.k]
        m_wide = jnp.concatenate([(it.cb_mat * sg).astype(BF16) for sg in it.seg], axis=1)
        it.y = (_dot(m_wide, it.x_stack)
                + _dot_nt(it.cg, s_old.astype(BF16)) * _block_bcast(s.ecum, it.lanes))
        upd = _dot(it.xwt, it.bg)
        it.s_new = [s_old[r * SSM_P:(r + 1) * SSM_P] * s.dec_row[:, ln:ln + 1] + upd[r * SSM_P:(r + 1) * SSM_P]
                    for r, ln in enumerate(it.lanes)]

    def emit_phase(it):
        gs = slice(it.k * gw, (it.k + 1) * gw)
        y = it.y + dv_ref[:, gs] * it.xg
        z = z_ref[it.b, :, gs].astype(F32)
        yn = _groupnorm(y * _silu(z), nw_ref[:, gs], False)
        y_ref[it.b, :, gs] = yn.astype(y_ref.dtype)

    def finish(s):
        for it in s.heads:
            for r in range(SSM_R):
                s_ref[s.b, it.k * SSM_R + r] = it.s_new[r]

    _seq_loop(nb, 2, [load, dt_phase], SSM_G, [conv_phase, seg_phase, cb_phase, mix_phase, emit_phase], finish)


def _ssd_prefill(proj, small, cw, cb, dtb, alog, dvec, nw, *, layer, nb, seq, zero_shape=None):
    c = CHUNK
    wx = SSM_H * SSM_P
    p3 = proj.reshape(nb, seq, N_MAIN)
    s3 = small.reshape(nb, seq, N_SMALL)
    z_specs, z_shapes = ([], []) if zero_shape is None else [[v] for v in _zero_out(zero_shape, seq // c)]
    return pl.pallas_call(
        _ssd_prefill_kernel,
        grid=(seq // c,),
        in_specs=[
            pl.BlockSpec((nb, c, wx), lambda t: (0, t, C_Z // wx)),
            pl.BlockSpec((nb, c, wx), lambda t: (0, t, C_XS // wx)),
            pl.BlockSpec((nb, c, wx), lambda t: (0, jnp.maximum(t - 1, 0), C_XS // wx)),
            pl.BlockSpec((nb, c, wx), lambda t: (0, t, C_B // wx)),
            pl.BlockSpec((nb, c, wx), lambda t: (0, jnp.maximum(t - 1, 0), C_B // wx)),
            pl.BlockSpec((nb, c, N_SMALL), lambda t: (0, t, 0)),
            _lsel(layer, CONV_K, 2 * wx),
            _lsel(layer, 1, 2 * wx),
            _lsel(layer, 1, N_SMALL),
            _lsel(layer, 1, N_SMALL),
            _lsel(layer, 1, wx),
            _lsel(layer, 1, wx)],
        out_specs=[pl.BlockSpec((nb, c, wx), lambda t: (0, t, 0)),
                   pl.BlockSpec((nb, SSM_H, SSM_P, SSM_N), lambda t: (0, 0, 0, 0)),
                   pl.BlockSpec((nb, CONV_K - 1, 2 * wx), lambda t: (0, 0, 0))] + z_specs,
        out_shape=[jax.ShapeDtypeStruct((nb, seq, wx), BF16),
                   jax.ShapeDtypeStruct((nb, SSM_H, SSM_P, SSM_N), F32),
                   jax.ShapeDtypeStruct((nb, CONV_K - 1, 2 * wx), F32)] + z_shapes,
        scratch_shapes=[pltpu.VMEM((16, 2 * wx), F32)],
        compiler_params=_cparams(1),
        name="ssd_prefill",
    )(p3, p3, p3, p3, p3, s3, cw, cb, dtb, alog, dvec, nw)


MERGE_SUB = 256


def _merge_kernel(yr_ref, ym_ref, ys_ref, gr_ref, gm_ref, gs_ref, x_ref, wr_ref, wm_ref, ws_ref,
                  wo_ref, nw_ref, wq_ref, xo_ref, qo_ref, *zero_refs):
    _zero_fill(zero_refs)
    tm = x_ref.shape[0]
    n_sub = max(1, tm // MERGE_SUB)
    subs = [slice(i * (tm // n_sub), (i + 1) * (tm // n_sub)) for i in range(n_sub)]
    branches = ((yr_ref, gr_ref, wr_ref), (ym_ref, gm_ref, wm_ref), (ys_ref, gs_ref, ws_ref))
    proj = [[_dot(y_ref[rs, :].astype(BF16), w_ref[...]) for y_ref, _, w_ref in branches] for rs in subs]
    merged = []
    for rs, pr in zip(subs, proj):
        gated = [_sigmoid(g_ref[rs, :].astype(F32)) * p for (_, g_ref, _), p in zip(branches, pr)]
        merged.append((gated[0] + gated[1] + gated[2]).astype(BF16))
    xns = [x_ref[rs, :] + _dot(m, wo_ref[...]) for rs, m in zip(subs, merged)]
    hqs = []
    for rs, xn in zip(subs, xns):
        xo_ref[rs, :] = xn
        hqs.append(_rmsnorm(xn, nw_ref[...]).astype(BF16))
    for rs, hq in zip(subs, hqs):
        qo_ref[rs, :] = _dot(hq, wq_ref[...]).astype(qo_ref.dtype)


def _merge(yr, ym, ys, proj, x, wr, wm, ws, wo, nw, wq, *, layer, tm, q_dtype, zero_shape=None):
    m = x.shape[0]
    d = D_MODEL
    row = lambda i: (i, 0)
    z_specs, z_shapes = ([], []) if zero_shape is None else [[v] for v in _zero_out(zero_shape, m // tm)]
    wspec = _lsel(layer, d, d, pipeline_mode=pl.Buffered(1))
    return pl.pallas_call(
        _merge_kernel,
        grid=(m // tm,),
        in_specs=[pl.BlockSpec((tm, d), row), pl.BlockSpec((tm, d), row), pl.BlockSpec((tm, d), row),
                  pl.BlockSpec((tm, d), lambda i: (i, C_GR // d)),
                  pl.BlockSpec((tm, d), lambda i: (i, C_GM // d)),
                  pl.BlockSpec((tm, d), lambda i: (i, C_GS // d)),
                  pl.BlockSpec((tm, d), row),
                  wspec, wspec, wspec, wspec,
                  _lsel(layer, 1, d),
                  wspec],
        out_specs=[pl.BlockSpec((tm, d), row), pl.BlockSpec((tm, d), row)] + z_specs,
        out_shape=[jax.ShapeDtypeStruct((m, d), F32), jax.ShapeDtypeStruct((m, d), q_dtype)] + z_shapes,
        compiler_params=_cparams(1),
        name="merge",
    )(yr, ym, ys, proj, proj, proj, x, wr, wm, ws, wo, nw, wq)


def _attn_prefill_kernel(q_ref, k_ref, v_ref, x_ref, wo_ref, o_ref):
    heads = [slice(h * MEM_DH, (h + 1) * MEM_DH) for h in range(MEM_H)]
    scores = [_dot_nt(q_ref[0, :, sl], k_ref[0, :, sl].astype(BF16)) * (MEM_DH ** -0.5) for sl in heads]
    probs = []
    for s in scores:
        p = jnp.exp(s - jnp.max(s, axis=-1, keepdims=True))
        probs.append((p / jnp.sum(p, axis=-1, keepdims=True)).astype(BF16))
    outs = [_dot(p, v_ref[0, :, sl].astype(BF16)).astype(BF16) for p, sl in zip(probs, heads)]
    o_all = jnp.concatenate(outs, axis=1)
    o_ref[0] = x_ref[0] + _dot(o_all, wo_ref[...])


def _attn_prefill(q, mk, mv, x, wo, *, layer, nb, seq, tq):
    d = D_MODEL
    q3 = q.reshape(nb, seq, d)
    x3 = x.reshape(nb, seq, d)
    out = pl.pallas_call(
        _attn_prefill_kernel,
        grid=(nb, seq // tq),
        in_specs=[pl.BlockSpec((1, tq, d), lambda b, t: (b, t, 0)),
                  pl.BlockSpec((1, MEM_LEN, d), lambda b, t: (layer * nb + b, 0, 0)),
                  pl.BlockSpec((1, MEM_LEN, d), lambda b, t: (layer * nb + b, 0, 0)),
                  pl.BlockSpec((1, tq, d), lambda b, t: (b, t, 0)),
                  _lsel(layer, d, d)],
        out_specs=pl.BlockSpec((1, tq, d), lambda b, t: (b, t, 0)),
        out_shape=jax.ShapeDtypeStruct((nb, seq, d), F32),
        compiler_params=_cparams(2),
        name="attn_prefill",
    )(q3, mk, mv, x3, wo)
    return out.reshape(nb * seq, d)


def _attn_decode_kernel(q_ref, k_ref, v_ref, o_ref):
    for j in range(q_ref.shape[0]):
        s = jnp.sum(k_ref[j] * q_ref[j][None], axis=-1, keepdims=True) * (MEM_DH ** -0.5)
        p = jnp.exp(s - jnp.max(s, axis=0, keepdims=True))
        p = p / jnp.sum(p, axis=0, keepdims=True)
        o_ref[j] = jnp.sum(p * v_ref[j], axis=0)


def _attn_decode(q, ck, cv, *, layer, bb):
    nb = q.shape[0]
    nblk = nb // bb
    blk = (bb, MEM_LEN, MEM_H, MEM_DH)
    return pl.pallas_call(
        _attn_decode_kernel,
        grid=(nblk,),
        in_specs=[pl.BlockSpec((bb, MEM_H, MEM_DH), lambda i: (i, 0, 0)),
                  pl.BlockSpec(blk, lambda i: (layer * nblk + i, 0, 0, 0)),
                  pl.BlockSpec(blk, lambda i: (layer * nblk + i, 0, 0, 0))],
        out_specs=pl.BlockSpec((bb, MEM_H, MEM_DH), lambda i: (i, 0, 0)),
        out_shape=jax.ShapeDtypeStruct((nb, MEM_H, MEM_DH), F32),
        compiler_params=_cparams(1),
        name="attn_decode",
    )(q, ck, cv)


MLP_SUB = 1024


def _mlp_kernel(x_ref, nw_ref, w1_ref, w2_ref, nf_ref, o_ref, xn_ref, acc_ref, *, final_norm):
    j = pl.program_id(1)

    @pl.when(j == 0)
    def _():
        xn_ref[...] = _rmsnorm(x_ref[...], nw_ref[...]).astype(BF16)
        acc_ref[...] = jnp.zeros_like(acc_ref)

    subs = [slice(s * MLP_SUB, (s + 1) * MLP_SUB) for s in range(w1_ref.shape[1] // MLP_SUB)]
    hids = [_dot(xn_ref[...], w1_ref[:, sl]) for sl in subs]
    hids = [jnp.square(jnp.maximum(hid, 0.0)).astype(BF16) for hid in hids]
    acc = acc_ref[...]
    for hid, sl in zip(hids, subs):
        acc = acc + _dot(hid, w2_ref[sl, :])
    acc_ref[...] = acc

    @pl.when(j == pl.num_programs(1) - 1)
    def _():
        y = x_ref[...] + acc_ref[...]
        if final_norm:
            y = _rmsnorm(y, nf_ref[...])
        o_ref[...] = y


def _mlp(x, nw, w1, w2, nf, *, layer, tm, tf, final_norm):
    m = x.shape[0]
    d = D_MODEL
    return pl.pallas_call(
        functools.partial(_mlp_kernel, final_norm=final_norm),
        grid=(m // tm, D_FF // tf),
        in_specs=[pl.BlockSpec((tm, d), lambda i, j: (i, 0)),
                  _lsel(layer, 1, d),
                  pl.BlockSpec((None, d, tf), lambda i, j: (layer, 0, j)),
                  pl.BlockSpec((None, tf, d), lambda i, j: (layer, j, 0)),
                  pl.BlockSpec((1, d), lambda i, j: (0, 0))],
        out_specs=pl.BlockSpec((tm, d), lambda i, j: (i, 0)),
        out_shape=jax.ShapeDtypeStruct((m, d), F32),
        scratch_shapes=[pltpu.VMEM((tm, d), BF16), pltpu.VMEM((tm, d), F32)],
        compiler_params=_cparams(2),
        name="mlp",
    )(x, nw, w1, w2, nf)


DEC_BB = 8


def _row_mask(x, j):
    rows = lax.broadcasted_iota(jnp.int32, x.shape, 0)
    return jnp.where(rows == j, x, 0.0).astype(BF16)


def _blockdiag_rows(v):
    n = v.shape[0]
    tiled = jnp.concatenate([v] * n, axis=1)
    rows = lax.broadcasted_iota(jnp.int32, tiled.shape, 0)
    blk = lax.shift_right_logical(lax.broadcasted_iota(jnp.int32, tiled.shape, 1), 8)
    return _pad_rows(jnp.where(rows == blk, tiled, 0.0), 128).astype(BF16)


def _ret_decode_kernel(q_ref, k_ref, v_ref, g_ref, cos_ref, sin_ref, nw_ref, s_ref, y_ref, so_ref, ybuf):
    cosf = cos_ref[...]
    sinf = sin_ref[...]
    for h in range(RET_H):
        sk = slice(h * RET_DK, (h + 1) * RET_DK)
        sv = slice(h * RET_DV, (h + 1) * RET_DV)
        g_dec = math.exp(_LOG_G[h])
        q = q_ref[:, sk]
        k = k_ref[:, sk]
        qr = q * cosf + pltpu.roll(q, RET_DK // 2, 1) * sinf
        kr = (k * cosf + pltpu.roll(k, RET_DK // 2, 1) * sinf) * (RET_DK ** -0.5)
        k_t = _pad_rows_t(kr).astype(BF16)
        q16 = _pad_rows(qr, 16).astype(BF16)
        upd = _dot(k_t, _blockdiag_rows(v_ref[:, sv]))
        s_wide = []
        for j in range(DEC_BB):
            s_new = s_ref[j, h] * g_dec + upd[:, j * RET_DV:(j + 1) * RET_DV]
            so_ref[j, h] = s_new
            s_wide.append(s_new.astype(BF16))
        y_all = _dot(q16, jnp.concatenate(s_wide, axis=1))
        for j in range(DEC_BB):
            ybuf[j:j + 1, sv] = y_all[j:j + 1, j * RET_DV:(j + 1) * RET_DV]
        y_ref[:, sv] = _groupnorm(ybuf[:, sv], nw_ref[:, sv], True) * _silu(g_ref[:, sv])


def _ret_decode(proj, cosf, sinf, nw, state, prev, *, layer):
    nb = proj.shape[0]
    bb = DEC_BB
    nblk = nb // bb
    kern, steps, ib, ob, al_args, al_specs, al_map = _carried_call(_ret_decode_kernel, nblk, layer, prev)
    sblk = (bb, RET_H, RET_DK, RET_DV)
    return pl.pallas_call(
        kern,
        grid=(steps,),
        in_specs=al_specs + [
            pl.BlockSpec((bb, 512), lambda i: (ib(i), C_Q // 512)),
            pl.BlockSpec((bb, 512), lambda i: (ib(i), C_K // 512)),
            pl.BlockSpec((bb, 1024), lambda i: (ib(i), C_V // 1024)),
            pl.BlockSpec((bb, 1024), lambda i: (ib(i), C_G // 1024)),
            pl.BlockSpec((1, RET_DK), lambda i: (0, 0)),
            pl.BlockSpec((1, RET_DK), lambda i: (0, 0)),
            _lsel(layer, 1, 1024),
            pl.BlockSpec(sblk, lambda i: (layer * nblk + ib(i), 0, 0, 0))],
        out_specs=[pl.BlockSpec((bb, 1024), lambda i: (ib(i), 0)),
                   pl.BlockSpec(sblk, lambda i: (ob(i), 0, 0, 0))],
        out_shape=[jax.ShapeDtypeStruct((nb, RET_H * RET_DV), F32),
                   jax.ShapeDtypeStruct((DEPTH * nb, RET_H, RET_DK, RET_DV), F32)],
        scratch_shapes=[pltpu.VMEM((bb, RET_H * RET_DV), F32)],
        input_output_aliases=al_map,
        compiler_params=_cparams(1),
        name="ret_decode",
    )(*al_args, proj, proj, proj, proj, cosf, sinf, nw, state)


def _conv_step(x, buf_ref, cw_ref, cb_ref, bo_ref):
    acc = cb_ref[...]
    for j in range(CONV_K - 1):
        acc = acc + buf_ref[j] * cw_ref[j:j + 1, :]
    acc = acc + x * cw_ref[CONV_K - 1:CONV_K, :]
    bo_ref[0] = buf_ref[1]
    bo_ref[1] = buf_ref[2]
    bo_ref[2] = x
    return _silu(acc)


def _ml_decode_kernel(u_ref, v_ref, o_ref, sm_ref, buf_ref, cw_ref, cb_ref, wq_ref, wk_ref, gb_ref, nw_ref,
                      c_ref, n_ref, m_ref,
                      y_ref, co_ref, no_ref, mo_ref, bo_ref, ybuf):
    uc = _conv_step(u_ref[...], buf_ref, cw_ref, cb_ref, bo_ref).astype(BF16)
    gates = sm_ref[...] + gb_ref[...]
    lane4 = lax.broadcasted_iota(jnp.int32, m_ref.shape, 1)
    m_all = m_ref[...]
    m_out = m_all
    for h in range(ML_H):
        sl = slice(h * ML_DH, (h + 1) * ML_DH)
        qm = _dot(uc[:, sl], wq_ref[h])
        km = _dot(uc[:, sl], wk_ref[h]) * (ML_DH ** -0.5)
        i_pre = gates[:, L_IG + h:L_IG + h + 1]
        log_f = _log_sigmoid(gates[:, L_FG + h:L_FG + h + 1])
        m_prev = m_all[:, h:h + 1]
        inter = log_f + m_prev
        m_new = jnp.maximum(inter, i_pre)
        w_s = jnp.exp(i_pre - m_new)
        w_prev = jnp.exp(inter - m_new)
        kw = km * w_s
        n_new = n_ref[:, sl] * w_prev + kw
        no_ref[:, sl] = n_new
        m_out = jnp.where(lane4 == h, m_new, m_out)

        k_t = _pad_rows_t(kw).astype(BF16)
        q16 = _pad_rows(qm, 16).astype(BF16)
        upd = _dot(k_t, _blockdiag_rows(v_ref[:, sl]))
        c_wide = []
        for j in range(DEC_BB):
            c_new = c_ref[j, h] * w_prev[j:j + 1, :] + upd[:, j * ML_DH:(j + 1) * ML_DH]
            co_ref[j, h] = c_new
            c_wide.append(c_new.astype(BF16))
        y_all = _dot(q16, jnp.concatenate(c_wide, axis=1))
        for j in range(DEC_BB):
            ybuf[j:j + 1, sl] = y_all[j:j + 1, j * ML_DH:(j + 1) * ML_DH]
        den = jnp.sum(qm * n_new, axis=-1, keepdims=True)
        den = jnp.maximum(jnp.abs(den), jnp.exp(-m_new))
        y_ref[:, sl] = _groupnorm(ybuf[:, sl] / den, nw_ref[:, sl], True) * _sigmoid(o_ref[:, sl])
    mo_ref[...] = m_out


def _ml_decode(proj, small, buf, cw, cb, wq, wk, gb, nw, c_state, n_state, m_state, prev, *, layer):
    nb = proj.shape[0]
    bb = DEC_BB
    nblk = nb // bb
    dh = ML_DH
    w = ML_H * ML_DH
    kern, steps, ib, ob, al_args, al_specs, al_map = _carried_call(_ml_decode_kernel, nblk, layer, prev)
    cblk = (bb, ML_H, dh, dh)
    return pl.pallas_call(
        kern,
        grid=(steps,),
        in_specs=al_specs + [
            pl.BlockSpec((bb, w), lambda i: (ib(i), C_U // w)),
            pl.BlockSpec((bb, w), lambda i: (ib(i), C_VM // w)),
            pl.BlockSpec((bb, w), lambda i: (ib(i), C_OM // w)),
            pl.BlockSpec((bb, N_SMALL), lambda i: (ib(i), 0)),
            pl.BlockSpec((None, CONV_K - 1, bb, w), lambda i: (layer, 0, ib(i), 0)),
            _lsel(layer, CONV_K, w),
            _lsel(layer, 1, w),
            _lsel(layer, ML_H, dh, dh),
            _lsel(layer, ML_H, dh, dh),
            _lsel(layer, 1, N_SMALL),
            _lsel(layer, 1, w),
            pl.BlockSpec(cblk, lambda i: (layer * nblk + ib(i), 0, 0, 0)),
            pl.BlockSpec((bb, w), lambda i: (layer * nblk + ib(i), 0)),
            pl.BlockSpec((bb, ML_H), lambda i: (layer * nblk + ib(i), 0))],
        out_specs=[pl.BlockSpec((bb, w), lambda i: (ib(i), 0)),
                   pl.BlockSpec(cblk, lambda i: (ob(i), 0, 0, 0)),
                   pl.BlockSpec((bb, w), lambda i: (ib(i), 0)),
                   pl.BlockSpec((bb, ML_H), lambda i: (ib(i), 0)),
                   pl.BlockSpec((CONV_K - 1, bb, w), lambda i: (0, ib(i), 0))],
        out_shape=[jax.ShapeDtypeStruct((nb, w), F32),
                   jax.ShapeDtypeStruct((DEPTH * nb, ML_H, dh, dh), F32),
                   jax.ShapeDtypeStruct((nb, w), F32),
                   jax.ShapeDtypeStruct((nb, ML_H), F32),
                   jax.ShapeDtypeStruct((CONV_K - 1, nb, w), F32)],
        scratch_shapes=[pltpu.VMEM((bb, w), F32)],
        input_output_aliases=al_map,
        compiler_params=_cparams(1),
        name="ml_decode",
    )(*al_args, proj, proj, proj, small, buf, cw, cb, wq, wk, gb, nw, c_state, n_state, m_state)


def _ssd_decode_kernel(z_ref, xs_ref, bc_ref, sm_ref, buf_ref, cw_ref, cb_ref, dtb_ref, alog_ref, dv_ref, nw_ref,
                       s_ref, y_ref, so_ref, bo_ref, ybuf):
    wx = SSM_H * SSM_P
    gw = SSM_R * SSM_P
    x_in = jnp.concatenate([xs_ref[...], bc_ref[...]], axis=1)
    xc = _conv_step(x_in, buf_ref, cw_ref, cb_ref, bo_ref)
    xs = xc[:, 0:wx]

    delta = _softplus(sm_ref[...] + dtb_ref[...])
    d_a = jnp.exp(delta * (-jnp.exp(alog_ref[...])))
    lane_blk = lax.shift_right_logical(lax.broadcasted_iota(jnp.int32, (DEC_BB, wx), 1), 6)
    dt_full = jnp.zeros((DEC_BB, wx), F32)
    for hh in range(SSM_H):
        dt_full = jnp.where(lane_blk == hh, delta[:, L_DT + hh:L_DT + hh + 1], dt_full)
    x_t = _pad_rows_t(xs * dt_full).astype(BF16)

    for g in range(SSM_G):
        gs = slice(g * gw, (g + 1) * gw)
        b_pad = _pad_rows(xc[:, wx + g * SSM_N:wx + (g + 1) * SSM_N], 128)
        c16 = _pad_rows(xc[:, wx + (SSM_G + g) * SSM_N:wx + (SSM_G + g + 1) * SSM_N], 16).astype(BF16)
        for j in range(DEC_BB):
            upd = _dot(x_t[gs, :], _row_mask(b_pad, j))
            parts = []
            for r in range(SSM_R):
                hh = g * SSM_R + r
                s_new = (s_ref[j, hh] * d_a[j:j + 1, L_DT + hh:L_DT + hh + 1]
                         + upd[r * SSM_P:(r + 1) * SSM_P])
                so_ref[j, hh] = s_new
                parts.append(s_new.astype(BF16))
            ybuf[j:j + 1, gs] = _dot_nt(c16, jnp.concatenate(parts, axis=0))[j:j + 1, :]
        y = ybuf[:, gs] + dv_ref[:, gs] * xs[:, gs]
        y_ref[:, gs] = _groupnorm(y * _silu(z_ref[:, gs]), nw_ref[:, gs], False)


def _ssd_decode(proj, small, buf, cw, cb, dtb, alog, dvec, nw, state, prev, *, layer):
    nb = proj.shape[0]
    bb = DEC_BB
    nblk = nb // bb
    wx = SSM_H * SSM_P
    kern, steps, ib, ob, al_args, al_specs, al_map = _carried_call(_ssd_decode_kernel, nblk, layer, prev)
    sblk = (bb, SSM_H, SSM_P, SSM_N)
    return pl.pallas_call(
        kern,
        grid=(steps,),
        in_specs=al_specs + [
            pl.BlockSpec((bb, wx), lambda i: (ib(i), C_Z // wx)),
            pl.BlockSpec((bb, wx), lambda i: (ib(i), C_XS // wx)),
            pl.BlockSpec((bb, wx), lambda i: (ib(i), C_B // wx)),
            pl.BlockSpec((bb, N_SMALL), lambda i: (ib(i), 0)),
            pl.BlockSpec((None, CONV_K - 1, bb, 2 * wx), lambda i: (layer, 0, ib(i), 0)),
            _lsel(layer, CONV_K, 2 * wx),
            _lsel(layer, 1, 2 * wx),
            _lsel(layer, 1, N_SMALL),
            _lsel(layer, 1, N_SMALL),
            _lsel(layer, 1, wx),
            _lsel(layer, 1, wx),
            pl.BlockSpec(sblk, lambda i: (layer * nblk + ib(i), 0, 0, 0))],
        out_specs=[pl.BlockSpec((bb, wx), lambda i: (ib(i), 0)),
                   pl.BlockSpec(sblk, lambda i: (ob(i), 0, 0, 0)),
                   pl.BlockSpec((CONV_K - 1, bb, 2 * wx), lambda i: (0, ib(i), 0))],
        out_shape=[jax.ShapeDtypeStruct((nb, wx), F32),
                   jax.ShapeDtypeStruct((DEPTH * nb, SSM_H, SSM_P, SSM_N), F32),
                   jax.ShapeDtypeStruct((CONV_K - 1, nb, 2 * wx), F32)],
        scratch_shapes=[pltpu.VMEM((bb, wx), F32)],
        input_output_aliases=al_map,
        compiler_params=_cparams(1),
        name="ssd_decode",
    )(*al_args, proj, proj, proj, small, buf, cw, cb, dtb, alog, dvec, nw, state)


def _rope_tables(pos):
    half = RET_DK // 2
    freqs = ROPE_THETA ** (-jnp.arange(half, dtype=F32) / half)
    ang = pos.astype(F32)[:, None] * freqs[None, :]
    cos = jnp.cos(ang)
    sin = jnp.sin(ang)
    return jnp.concatenate([cos, cos], axis=1), jnp.concatenate([-sin, sin], axis=1)


def _pad_lanes(v, offset):
    n = v.shape[1]
    return jnp.pad(v.astype(F32), ((0, 0), (offset, N_SMALL - offset - n))).reshape(DEPTH, 1, N_SMALL)


REPACK_TN = 1024
REPACK_RUNS = ((0, 6, 0), (6, 9, 2 * ML_H), (9, 12, 2 * ML_H + SSM_H))
SRC_GATE_BLK = 6144 // 128
SRC_DT_BLK = (6144 + 2 * ML_H + 3072) // 128


def _repack_kernel(a_ref, b_ref, g_ref, dt_ref, o_ref, os_ref):
    j = pl.program_id(1)

    @pl.when(j == 0)
    def _():
        row = lax.broadcasted_iota(jnp.int32, (N_SMALL, D_MODEL), 0)
        os_ref[...] = jnp.where(row < L_DT, g_ref[...],
                                jnp.where(row < L_DT + SSM_H, dt_ref[...], 0.0)).astype(BF16)

    for lo, hi, skip in REPACK_RUNS:
        @pl.when((j >= lo) & (j < hi))
        def _(skip=skip):
            src = a_ref[...] if skip == 0 else jnp.concatenate([a_ref[skip:, :], b_ref[0:skip, :]], axis=0)
            o_ref[...] = src.astype(BF16)


def _repack_w_in(w_in_t):
    tn = REPACK_TN
    return pl.pallas_call(
        _repack_kernel,
        grid=(DEPTH, N_MAIN // tn),
        in_specs=[pl.BlockSpec((None, tn, D_MODEL), lambda l, j: (l, j, 0)),
                  pl.BlockSpec((None, 128, D_MODEL), lambda l, j: (l, (j + 1) * (tn // 128), 0)),
                  pl.BlockSpec((None, 128, D_MODEL), lambda l, j: (l, SRC_GATE_BLK, 0)),
                  pl.BlockSpec((None, 128, D_MODEL), lambda l, j: (l, SRC_DT_BLK, 0))],
        out_specs=[pl.BlockSpec((None, tn, D_MODEL), lambda l, j: (l, j, 0)),
                   pl.BlockSpec((None, N_SMALL, D_MODEL), lambda l, j: (l, 0, 0))],
        out_shape=[jax.ShapeDtypeStruct((DEPTH, N_MAIN, D_MODEL), BF16),
                   jax.ShapeDtypeStruct((DEPTH, N_SMALL, D_MODEL), BF16)],
        compiler_params=_cparams(2),
        name="repack_w_in",
    )(w_in_t, w_in_t, w_in_t, w_in_t)


def _prep_weights(W):
    w_main, w_small = _repack_w_in(jnp.swapaxes(W['w_in'], 1, 2))
    row = lambda v: v.reshape(DEPTH, 1, -1).astype(F32)
    bf = lambda v: v.astype(BF16)
    return dict(
        w_main=w_main, w_small=w_small,
        norm_mix=row(W['norm_mix_w']),
        ret_norm=row(W['ret_norm_w']),
        ml_cw=W['ml_conv_w'], ml_cb=row(W['ml_conv_b']),
        ml_wq=bf(W['ml_wq']), ml_wk=bf(W['ml_wk']),
        ml_gb=_pad_lanes(W['ml_gate_b'], L_IG),
        ml_norm=row(W['ml_norm_w']),
        ssm_cw=W['ssm_conv_w'], ssm_cb=row(W['ssm_conv_b']),
        ssm_dtb=_pad_lanes(W['ssm_dt_bias'], L_DT),
        ssm_alog=_pad_lanes(W['ssm_A_log'], L_DT),
        ssm_dvec=row(jnp.repeat(W['ssm_D'], SSM_P, axis=1)),
        ssm_norm=row(W['ssm_norm_w']),
        w_br_ret=bf(W['w_br_ret']), w_br_ml=bf(W['w_br_ml']),
        w_br_ssm=bf(W['w_br_ssm']), w_out=bf(W['w_out_mix']),
        norm_mem=row(W['norm_mem_w']),
        mem_wq=bf(W['mem_wq']), mem_wo=bf(W['mem_wo']),
        norm_mlp=row(W['norm_mlp_w']),
        mlp_w1=bf(W['mlp_w1']), mlp_w2=bf(W['mlp_w2']),
    )


def kernel(x_prompt, x_sample, mem_prompt, state_ret, state_mlstm_C, state_mlstm_n, state_mlstm_m,
           state_mlstm_conv, state_ssm, state_ssm_conv, cache_mem_k, cache_mem_v,
           norm_mix_w, w_in, ret_norm_w, ml_conv_w, ml_conv_b, ml_wq, ml_wk, ml_gate_b, ml_norm_w,
           ssm_conv_w, ssm_conv_b, ssm_dt_bias, ssm_A_log, ssm_D, ssm_norm_w,
           w_br_ret, w_br_ml, w_br_ssm, w_out_mix, norm_mem_w, mem_wq, mem_wk, mem_wv, mem_wo,
           norm_mlp_w, mlp_w1, mlp_w2, norm_f_w):
    W = dict(norm_mix_w=norm_mix_w, w_in=w_in, ret_norm_w=ret_norm_w, ml_conv_w=ml_conv_w,
             ml_conv_b=ml_conv_b, ml_wq=ml_wq, ml_wk=ml_wk, ml_gate_b=ml_gate_b, ml_norm_w=ml_norm_w,
             ssm_conv_w=ssm_conv_w, ssm_conv_b=ssm_conv_b, ssm_dt_bias=ssm_dt_bias, ssm_A_log=ssm_A_log,
             ssm_D=ssm_D, ssm_norm_w=ssm_norm_w, w_br_ret=w_br_ret, w_br_ml=w_br_ml, w_br_ssm=w_br_ssm,
             w_out_mix=w_out_mix, norm_mem_w=norm_mem_w, mem_wq=mem_wq, mem_wo=mem_wo,
             norm_mlp_w=norm_mlp_w, mlp_w1=mlp_w1, mlp_w2=mlp_w2)
    lw = _prep_weights(W)
    norm_f = norm_f_w.reshape(1, -1).astype(F32)
    d = D_MODEL

    nb, seq = x_prompt.shape[0], x_prompt.shape[1]
    ns = x_sample.shape[0]
    cos_p, sin_p = _rope_tables(jnp.arange(seq, dtype=jnp.int32))
    memk, memv = _memkv(mem_prompt.reshape(nb * MEM_LEN, d), mem_wk.astype(BF16), mem_wv.astype(BF16), tm=1024)
    memk3 = memk.reshape(DEPTH * nb, MEM_LEN, d)
    memv3 = memv.reshape(DEPTH * nb, MEM_LEN, d)

    x = x_prompt.reshape(nb * seq, d)
    states_p = []
    for l in range(DEPTH):
        proj, small = _inproj(x, lw['norm_mix'], lw['w_main'], lw['w_small'], layer=l, tm=2048, tn=1024,
                              out_dtype=BF16)
        zs = (lambda shape: shape) if l == 0 else (lambda shape: None)
        y_r, ret_st, *ret_s = _ret_prefill(proj, cos_p, sin_p, lw['ret_norm'], layer=l, nb=nb, seq=seq,
                                           zero_shape=zs((DEPTH * ns, RET_H, RET_DK, RET_DV)))
        y_m, ml_c, ml_n, ml_m, ml_cv = _ml_prefill(proj, small, lw['ml_cw'], lw['ml_cb'], lw['ml_wq'], lw['ml_wk'],
                                                   lw['ml_gb'], lw['ml_norm'], layer=l, nb=nb, seq=seq)
        y_s, ssm_st, ssm_cv, *ssm_s = _ssd_prefill(proj, small, lw['ssm_cw'], lw['ssm_cb'], lw['ssm_dtb'],
                                                   lw['ssm_alog'], lw['ssm_dvec'], lw['ssm_norm'], layer=l, nb=nb,
                                                   seq=seq, zero_shape=zs((DEPTH * ns, SSM_H, SSM_P, SSM_N)))
        x, qm, *mlc_s = _merge(y_r.reshape(nb * seq, d), y_m.reshape(nb * seq, d), y_s.reshape(nb * seq, d), proj,
                               x, lw['w_br_ret'], lw['w_br_ml'], lw['w_br_ssm'], lw['w_out'], lw['norm_mem'],
                               lw['mem_wq'], layer=l, tm=512, q_dtype=BF16,
                               zero_shape=zs((DEPTH * ns, ML_H, ML_DH, ML_DH)))
        if l == 0:
            zero_states = (ret_s[0], mlc_s[0], ssm_s[0])
        x = _attn_prefill(qm, memk3, memv3, x, lw['mem_wo'], layer=l, nb=nb, seq=seq, tq=512)
        x = _mlp(x, lw['norm_mlp'], lw['mlp_w1'], lw['mlp_w2'], norm_f, layer=l, tm=1024, tf=2048,
                 final_norm=(l == DEPTH - 1))
        states_p.append((ret_st, ml_c, ml_n, ml_m[:, 0, :ML_H], ml_cv, ssm_st, ssm_cv))
    y_prompt = x.reshape(nb, seq, d)
    stack = lambda parts, i: jnp.stack([p[i] for p in parts])
    prompt_states = tuple(stack(states_p, i) for i in range(7))
    memk_p = memk.reshape(DEPTH, nb, MEM_LEN, MEM_H, MEM_DH)
    memv_p = memv.reshape(DEPTH, nb, MEM_LEN, MEM_H, MEM_DH)

    cos_s, sin_s = _rope_tables(PAST_LEN + jnp.arange(1, dtype=jnp.int32))
    ret_state = state_ret.reshape(DEPTH * ns, RET_H, RET_DK, RET_DV)
    mlc_state = state_mlstm_C.reshape(DEPTH * ns, ML_H, ML_DH, ML_DH)
    mln_state = state_mlstm_n.reshape(DEPTH * ns, ML_H * ML_DH)
    mlm_state = state_mlstm_m.reshape(DEPTH * ns, ML_H)
    ssm_st = state_ssm.reshape(DEPTH * ns, SSM_H, SSM_P, SSM_N)
    ck = cache_mem_k.reshape(DEPTH * ns, MEM_LEN, MEM_H, MEM_DH)
    cv = cache_mem_v.reshape(DEPTH * ns, MEM_LEN, MEM_H, MEM_DH)

    ml_buf = jnp.swapaxes(state_mlstm_conv, 1, 2)
    ssm_buf = jnp.swapaxes(state_ssm_conv, 1, 2)

    x = x_sample.reshape(ns, d)
    ret_s, mlc_s, ssm_s = zero_states
    small_s = []
    for l in range(DEPTH):
        proj, small = _inproj(x, lw['norm_mix'], lw['w_main'], lw['w_small'], layer=l, tm=ns, tn=1024, out_dtype=F32)
        y_r, ret_s = _ret_decode(proj, cos_s, sin_s, lw['ret_norm'], ret_state, ret_s,
                                 layer=l)
        y_m, mlc_s, ml_n, ml_m, ml_bo = _ml_decode(proj, small, ml_buf, lw['ml_cw'], lw['ml_cb'], lw['ml_wq'],
                                                   lw['ml_wk'], lw['ml_gb'], lw['ml_norm'],
                                                   mlc_state, mln_state, mlm_state, mlc_s,
                                                   layer=l)
        y_s, ssm_s, ssm_bo = _ssd_decode(proj, small, ssm_buf, lw['ssm_cw'], lw['ssm_cb'], lw['ssm_dtb'],
                                         lw['ssm_alog'], lw['ssm_dvec'], lw['ssm_norm'], ssm_st,
                                         ssm_s, layer=l)
        x, qm = _merge(y_r, y_m, y_s, proj, x, lw['w_br_ret'], lw['w_br_ml'], lw['w_br_ssm'], lw['w_out'],
                       lw['norm_mem'], lw['mem_wq'], layer=l, tm=ns, q_dtype=F32)
        att = _attn_decode(qm.reshape(ns, MEM_H, MEM_DH), ck, cv, layer=l, bb=4)
        x = _mm(att.reshape(ns, d), lw['mem_wo'], x, layer=l, tm=ns, tn=d)
        x = _mlp(x, lw['norm_mlp'], lw['mlp_w1'], lw['mlp_w2'], norm_f, layer=l, tm=ns, tf=2048,
                 final_norm=(l == DEPTH - 1))
        small_s.append((ml_n.reshape(ns, ML_H, ML_DH), ml_m, jnp.swapaxes(ml_bo, 0, 1),
                        jnp.swapaxes(ssm_bo, 0, 1)))
    y_sample = x.reshape(ns, 1, d)
    sample_states = (ret_s.reshape(DEPTH, ns, RET_H, RET_DK, RET_DV),
                     mlc_s.reshape(DEPTH, ns, ML_H, ML_DH, ML_DH),
                     stack(small_s, 0), stack(small_s, 1), stack(small_s, 2),
                     ssm_s.reshape(DEPTH, ns, SSM_H, SSM_P, SSM_N),
                     stack(small_s, 3))

    return (y_prompt, y_sample, *prompt_states, memk_p, memv_p, *sample_states)
```

```python
import functools
import math

import jax
import jax.numpy as jnp
from jax import lax
from jax.experimental import pallas as pl
from jax.experimental.pallas import tpu as pltpu

F32 = jnp.float32
BF16 = jnp.bfloat16

D_MODEL = 1024
DEPTH = 2
PAST_LEN = 16384
CHUNK = 128
CONV_K = 4
EPS = 1e-6
RET_H, RET_DK, RET_DV = 4, 128, 256
ROPE_THETA = 10000.0
ML_H, ML_DH = 4, 256
SSM_H, SSM_P, SSM_G, SSM_N = 16, 64, 4, 128
SSM_R = SSM_H // SSM_G
MEM_LEN, MEM_H, MEM_DH = 256, 4, 256
D_FF = 4 * D_MODEL

C_Q, C_K, C_V, C_G = 0, 512, 1024, 2048
C_U, C_VM, C_OM = 3072, 4096, 5120
C_Z, C_XS, C_B, C_C = 6144, 7168, 8192, 8704
C_GR, C_GM, C_GS = 9216, 10240, 11264
N_MAIN = 12288
L_IG, L_FG, L_DT = 0, 4, 8
N_SMALL = 128

VMEM_LIMIT = 56 * 1024 * 1024
NT_DIMS = (((1,), (1,)), ((), ()))

_LOG_G = [math.log1p(-(2.0 ** (-5.0 - h))) for h in range(RET_H)]


def _cparams(n_axes):
    return pltpu.CompilerParams(dimension_semantics=("arbitrary",) * n_axes,
                                vmem_limit_bytes=VMEM_LIMIT)


def _dot(a, b):
    return jnp.dot(a, b, preferred_element_type=F32)


def _dot_nt(a, b):
    return lax.dot_general(a, b, NT_DIMS, preferred_element_type=F32)


def _rmsnorm(x, w):
    ms = jnp.mean(x * x, axis=-1, keepdims=True)
    return x * lax.rsqrt(ms + EPS) * w


def _groupnorm(x, w, center):
    if center:
        x = x - jnp.mean(x, axis=-1, keepdims=True)
    ms = jnp.mean(x * x, axis=-1, keepdims=True)
    return x * lax.rsqrt(ms + EPS) * w


def _sigmoid(x):
    return 0.5 * jnp.tanh(0.5 * x) + 0.5


def _silu(x):
    return x * _sigmoid(x)


def _softplus(x):
    return jnp.maximum(x, 0.0) + jnp.log1p(jnp.exp(-jnp.abs(x)))


def _log_sigmoid(x):
    return -_softplus(-x)


def _cumsum_rows(x, tril_b):
    hi = x.astype(BF16)
    r1 = x - hi.astype(F32)
    mid = r1.astype(BF16)
    lo = (r1 - mid.astype(F32)).astype(BF16)
    return _dot(tril_b, hi) + _dot(tril_b, mid) + _dot(tril_b, lo)


def _pad_rows(x, rows):
    return jnp.concatenate([x, jnp.zeros((rows - x.shape[0], x.shape[1]), x.dtype)], axis=0)


def _pad_rows_t(x):
    return _pad_rows(x, 128).T


def _carried_call(kernel_fn, nblk, layer, prev):
    return (lambda *refs: kernel_fn(*refs[1:]), nblk, lambda i: i, lambda i: layer * nblk + i,
            [prev], [pl.BlockSpec(memory_space=pl.ANY)], {0: 1})


def _zero_out(shape, steps):
    blk = (shape[0] // steps,) + tuple(shape[1:])
    nz = len(shape) - 1
    return pl.BlockSpec(blk, lambda i: (i,) + (0,) * nz), jax.ShapeDtypeStruct(tuple(shape), F32)


def _zero_fill(zero_refs):
    for z_ref in zero_refs:
        z_ref[...] = jnp.zeros_like(z_ref)


def _inproj_kernel(x_ref, nw_ref, w_ref, ws_ref, o_ref, os_ref, xn_ref):
    @pl.when(pl.program_id(1) == 0)
    def _():
        xn = _rmsnorm(x_ref[...], nw_ref[...]).astype(BF16)
        xn_ref[...] = xn
        os_ref[...] = _dot_nt(xn, ws_ref[...])

    o_ref[...] = _dot_nt(xn_ref[...], w_ref[...]).astype(o_ref.dtype)


def _lsel(layer, *block, **kw):
    return pl.BlockSpec((None,) + block, lambda *_: (layer,) + (0,) * len(block), **kw)


def _inproj(x, nw, w_main, w_small, *, layer, tm, tn, out_dtype):
    m = x.shape[0]
    return pl.pallas_call(
        _inproj_kernel,
        grid=(m // tm, N_MAIN // tn),
        in_specs=[pl.BlockSpec((tm, D_MODEL), lambda i, j: (i, 0)),
                  _lsel(layer, 1, D_MODEL),
                  pl.BlockSpec((None, tn, D_MODEL), lambda i, j: (layer, j, 0)),
                  _lsel(layer, N_SMALL, D_MODEL)],
        out_specs=[pl.BlockSpec((tm, tn), lambda i, j: (i, j)),
                   pl.BlockSpec((tm, N_SMALL), lambda i, j: (i, 0))],
        out_shape=[jax.ShapeDtypeStruct((m, N_MAIN), out_dtype),
                   jax.ShapeDtypeStruct((m, N_SMALL), F32)],
        scratch_shapes=[pltpu.VMEM((tm, D_MODEL), BF16)],
        compiler_params=_cparams(2),
        name="inproj",
    )(x, nw, w_main, w_small)


def _mm_kernel(*refs, has_res):
    if has_res:
        a_ref, w_ref, r_ref, o_ref = refs
    else:
        a_ref, w_ref, o_ref = refs
    acc = _dot(a_ref[...].astype(BF16), w_ref[...])
    if has_res:
        acc = r_ref[...] + acc
    o_ref[...] = acc.astype(o_ref.dtype)


def _mm(a, w, res=None, *, layer, tm, tn, out_dtype=F32):
    m, k = a.shape
    n = w.shape[2]
    in_specs = [pl.BlockSpec((tm, k), lambda i, j: (i, 0)),
                pl.BlockSpec((None, k, tn), lambda i, j: (layer, 0, j))]
    args = [a, w]
    if res is not None:
        in_specs.append(pl.BlockSpec((tm, tn), lambda i, j: (i, j)))
        args.append(res)
    return pl.pallas_call(
        functools.partial(_mm_kernel, has_res=res is not None),
        grid=(m // tm, n // tn),
        in_specs=in_specs,
        out_specs=pl.BlockSpec((tm, tn), lambda i, j: (i, j)),
        out_shape=jax.ShapeDtypeStruct((m, n), out_dtype),
        compiler_params=_cparams(2),
        name="mm",
    )(*args)


def _memkv_kernel(a_ref, wk_ref, wv_ref, ok_ref, ov_ref, kb_ref, vb_ref):
    a = a_ref[...].astype(BF16)
    for w_ref, o_ref, b_ref in ((wk_ref, ok_ref, kb_ref), (wv_ref, ov_ref, vb_ref)):
        acc = _dot(a, w_ref[...])
        b_ref[...] = acc.astype(BF16)
        for h in range(MEM_H):
            o_ref[:, h, :] = acc[:, h * MEM_DH:(h + 1) * MEM_DH]


def _memkv(a, wk, wv, *, tm):
    m = a.shape[0]
    n = wk.shape[2]
    return pl.pallas_call(
        _memkv_kernel,
        grid=(DEPTH, m // tm),
        in_specs=[pl.BlockSpec((tm, D_MODEL), lambda l, i: (i, 0)),
                  pl.BlockSpec((None, D_MODEL, n), lambda l, i: (l, 0, 0)),
                  pl.BlockSpec((None, D_MODEL, n), lambda l, i: (l, 0, 0))],
        out_specs=[pl.BlockSpec((None, tm, MEM_H, MEM_DH), lambda l, i: (l, i, 0, 0)),
                   pl.BlockSpec((None, tm, MEM_H, MEM_DH), lambda l, i: (l, i, 0, 0)),
                   pl.BlockSpec((None, tm, n), lambda l, i: (l, i, 0)),
                   pl.BlockSpec((None, tm, n), lambda l, i: (l, i, 0))],
        out_shape=[jax.ShapeDtypeStruct((DEPTH, m, MEM_H, MEM_DH), F32)] * 2
        + [jax.ShapeDtypeStruct((DEPTH, m, n), BF16)] * 2,
        compiler_params=_cparams(2),
        name="memkv",
    )(a, wk, wv)


class _Ctx(dict):
    __getattr__ = dict.__getitem__
    __setattr__ = dict.__setitem__


def _seq_loop(nb, per_trip, seq_phases, n_items, item_phases, finish):
    def body(i, carry):
        seqs = [_Ctx(b=i * per_trip + u) for u in range(per_trip)]
        for phase in seq_phases:
            for s in seqs:
                phase(s)
        items = []
        for s in seqs:
            s.heads = [_Ctx(seq=s, b=s.b, k=k) for k in range(n_items)]
            items.extend(s.heads)
        for phase in item_phases:
            for it in items:
                phase(it)
        for s in seqs:
            finish(s)
        return carry

    lax.fori_loop(0, nb // per_trip, body, 0)


def _ret_prefill_kernel(q_ref, k_ref, v_ref, g_ref, cos_ref, sin_ref, nw_ref, y_ref, s_ref, *zero_refs):
    _zero_fill(zero_refs)

    @pl.when(pl.program_id(0) == 0)
    def _():
        s_ref[...] = jnp.zeros_like(s_ref)

    c = CHUNK
    nb = q_ref.shape[0]
    cosf = cos_ref[...]
    sinf = sin_ref[...]
    row = lax.broadcasted_iota(jnp.int32, (c, c), 0)
    col = lax.broadcasted_iota(jnp.int32, (c, c), 1)
    causal = row >= col
    diff = jnp.where(causal, (row - col).astype(F32), 0.0)
    rowf = row.astype(F32)
    rowf_v = lax.broadcasted_iota(jnp.int32, (c, RET_DV), 0).astype(F32)
    decay = [jnp.where(causal, jnp.exp(lg * diff), 0.0) for lg in _LOG_G]
    q_decay = [jnp.exp(lg * (rowf_v + 1.0)) for lg in _LOG_G]
    k_decay = [jnp.exp(lg * (c - 1.0 - rowf)) * (RET_DK ** -0.5) for lg in _LOG_G]

    def load(s):
        s.s_old = [s_ref[s.b, h] for h in range(RET_H)]

    def rotate(it):
        h = it.k
        sk = slice(h * RET_DK, (h + 1) * RET_DK)
        q = q_ref[it.b, :, sk].astype(F32)
        k = k_ref[it.b, :, sk].astype(F32)
        qr = q * cosf + pltpu.roll(q, RET_DK // 2, 1) * sinf
        kr = k * cosf + pltpu.roll(k, RET_DK // 2, 1) * sinf
        it.qb = qr.astype(BF16)
        it.kb = (kr * (RET_DK ** -0.5)).astype(BF16)
        it.kt = (kr * k_decay[h]).T.astype(BF16)
        it.v = v_ref[it.b, :, h * RET_DV:(h + 1) * RET_DV]

    def scores(it):
        it.sc = (_dot_nt(it.qb, it.kb) * decay[it.k]).astype(BF16)

    def mix(it):
        h = it.k
        s_old = it.seq.s_old[h]
        it.y = _dot(it.sc, it.v) + _dot(it.qb, s_old.astype(BF16)) * q_decay[h]
        it.s_new = s_old * math.exp(_LOG_G[h] * c) + _dot(it.kt, it.v)

    def emit(it):
        sv = slice(it.k * RET_DV, (it.k + 1) * RET_DV)
        g = g_ref[it.b, :, sv].astype(F32)
        yn = _groupnorm(it.y, nw_ref[:, sv], True) * _silu(g)
        y_ref[it.b, :, sv] = yn.astype(y_ref.dtype)

    def finish(s):
        for it in s.heads:
            s_ref[s.b, it.k] = it.s_new

    _seq_loop(nb, 2, [load], RET_H, [rotate, scores, mix, emit], finish)


def _ret_prefill(proj, cosf, sinf, nw, *, layer, nb, seq, zero_shape=None):
    c = CHUNK
    p3 = proj.reshape(nb, seq, N_MAIN)
    z_specs, z_shapes = ([], []) if zero_shape is None else [[v] for v in _zero_out(zero_shape, seq // c)]
    return pl.pallas_call(
        _ret_prefill_kernel,
        grid=(seq // c,),
        in_specs=[
            pl.BlockSpec((nb, c, 512), lambda t: (0, t, C_Q // 512)),
            pl.BlockSpec((nb, c, 512), lambda t: (0, t, C_K // 512)),
            pl.BlockSpec((nb, c, 1024), lambda t: (0, t, C_V // 1024)),
            pl.BlockSpec((nb, c, 1024), lambda t: (0, t, C_G // 1024)),
            pl.BlockSpec((c, RET_DK), lambda t: (t, 0)),
            pl.BlockSpec((c, RET_DK), lambda t: (t, 0)),
            _lsel(layer, 1, 1024)],
        out_specs=[pl.BlockSpec((nb, c, 1024), lambda t: (0, t, 0)),
                   pl.BlockSpec((nb, RET_H, RET_DK, RET_DV), lambda t: (0, 0, 0, 0))] + z_specs,
        out_shape=[jax.ShapeDtypeStruct((nb, seq, 1024), BF16),
                   jax.ShapeDtypeStruct((nb, RET_H, RET_DK, RET_DV), F32)] + z_shapes,
        compiler_params=_cparams(1),
        name="ret_prefill",
    )(p3, p3, p3, p3, cosf, sinf, nw)


def _shift_matrix(first_chunk):
    c = CHUNK
    r = lax.broadcasted_iota(jnp.int32, (3 * c, 2 * c), 0)
    q = lax.broadcasted_iota(jnp.int32, (3 * c, 2 * c), 1)
    k = lax.shift_right_logical(r, 7)
    t = jnp.bitwise_and(r, c - 1)
    hit = (q == c + t - (CONV_K - 1) + k) & (q >= jnp.where(first_chunk, c, 0))
    return jnp.where(hit, 1.0, 0.0).astype(BF16)


def _conv_silu(shift, x_prev, x_cur, cw, cb):
    c = CHUNK
    p = _dot(shift, jnp.concatenate([x_prev, x_cur], axis=0))
    acc = cb
    for j in range(CONV_K - 1):
        acc = acc + p[j * c:(j + 1) * c] * cw[j:j + 1, :]
    acc = acc + x_cur.astype(F32) * cw[CONV_K - 1:CONV_K, :]
    return _silu(acc)


def _ml_prefill_kernel(u_ref, up_ref, v_ref, o_ref, sm_ref, cw_ref, cb_ref, wq_ref, wk_ref, gb_ref, nw_ref,
                       y_ref, c_ref, n_ref, m_ref, cv_ref, tail):
    c = CHUNK
    nb = u_ref.shape[0]
    first = pl.program_id(0) == 0

    @pl.when(first)
    def _():
        c_ref[...] = jnp.zeros_like(c_ref)
        n_ref[...] = jnp.zeros_like(n_ref)
        m_ref[...] = jnp.zeros_like(m_ref)

    row = lax.broadcasted_iota(jnp.int32, (c, c), 0)
    col = lax.broadcasted_iota(jnp.int32, (c, c), 1)
    causal = row >= col
    tril_b = jnp.where(causal, 1.0, 0.0).astype(BF16)
    lane_m = lax.broadcasted_iota(jnp.int32, (1, 128), 1)
    shift = _shift_matrix(first)

    def load(s):
        s.c_old = [c_ref[s.b, h] for h in range(ML_H)]
        s.n_old = n_ref[s.b]
        s.m_row = m_ref[s.b]

    def gates_phase(s):
        tail[...] = u_ref[s.b, c - 16:c, :].astype(F32)
        cv_ref[s.b] = tail[16 - (CONV_K - 1):16, :]
        s.gates = sm_ref[s.b] + gb_ref[...]
        s.b_all = _cumsum_rows(_log_sigmoid(s.gates), tril_b)
        s.b_all_t = s.b_all.T
        s.gates_t = s.gates.T

    def conv_phase(it):
        sl = slice(it.k * ML_DH, (it.k + 1) * ML_DH)
        it.uc = _conv_silu(shift, up_ref[it.b, :, sl], u_ref[it.b, :, sl], cw_ref[:, sl], cb_ref[:, sl]).astype(BF16)
        it.v = v_ref[it.b, :, sl]

    def weights_phase(it):
        h, s = it.k, it.seq
        b_col = s.b_all[:, L_FG + h:L_FG + h + 1]
        b_row = s.b_all_t[L_FG + h:L_FG + h + 1, :]
        it_col = s.gates[:, L_IG + h:L_IG + h + 1]
        it_row = s.gates_t[L_IG + h:L_IG + h + 1, :]
        m_prev = s.m_row[:, h:h + 1]
        logw = jnp.where(causal, b_col - b_row + it_row, -jnp.inf)
        inter = b_col + m_prev
        it.m_t = jnp.maximum(inter, jnp.max(logw, axis=-1, keepdims=True))
        it.dmat = jnp.exp(logw - it.m_t)
        it.w_int = jnp.exp(inter - it.m_t)
        b_end = b_col[c - 1:c, :]
        logw_s = b_end - b_col + it_col
        it.m_new = jnp.maximum(b_end + m_prev, jnp.max(logw_s, axis=0, keepdims=True))
        it.w_s = jnp.exp(logw_s - it.m_new)
        it.w_prev = jnp.exp(b_end + m_prev - it.m_new)

    def qk_phase(it):
        it.qm = _dot(it.uc, wq_ref[it.k])
        it.km = _dot(it.uc, wk_ref[it.k]) * (ML_DH ** -0.5)
        it.qb = it.qm.astype(BF16)
        it.kb = it.km.astype(BF16)
        it.kw = it.km * it.w_s
        it.kwt = it.kw.T.astype(BF16)

    def scores_phase(it):
        it.sc = _dot_nt(it.qb, it.kb) * it.dmat

    def mix_phase(it):
        c_old = it.seq.c_old[it.k]
        n_old = it.seq.n_old[it.k:it.k + 1, :]
        it.num = _dot(it.sc.astype(BF16), it.v) + _dot(it.qb, c_old.astype(BF16)) * it.w_int
        it.c_new = c_old * it.w_prev + _dot(it.kwt, it.v)
        it.n_new = n_old * it.w_prev + jnp.sum(it.kw, axis=0, keepdims=True)
        den = jnp.sum(it.sc, axis=-1, keepdims=True) + jnp.sum(it.qm * n_old, axis=-1, keepdims=True) * it.w_int
        it.den = jnp.maximum(jnp.abs(den), jnp.exp(-it.m_t))

    def emit_phase(it):
        sl = slice(it.k * ML_DH, (it.k + 1) * ML_DH)
        og = o_ref[it.b, :, sl].astype(F32)
        yn = _groupnorm(it.num / it.den, nw_ref[:, sl], True) * _sigmoid(og)
        y_ref[it.b, :, sl] = yn.astype(y_ref.dtype)

    def finish(s):
        m_row_new = s.m_row
        for it in s.heads:
            c_ref[s.b, it.k] = it.c_new
            n_ref[s.b, it.k:it.k + 1, :] = it.n_new
            m_row_new = jnp.where(lane_m == it.k, it.m_new, m_row_new)
        m_ref[s.b] = m_row_new

    _seq_loop(nb, 1, [load, gates_phase], ML_H,
              [conv_phase, weights_phase, qk_phase, scores_phase, mix_phase, emit_phase], finish)


def _ml_prefill(proj, small, cw, cb, wq, wk, gb, nw, *, layer, nb, seq):
    c = CHUNK
    p3 = proj.reshape(nb, seq, N_MAIN)
    s3 = small.reshape(nb, seq, N_SMALL)
    w = ML_H * ML_DH
    full2 = lambda t: (0, 0)
    return pl.pallas_call(
        _ml_prefill_kernel,
        grid=(seq // c,),
        in_specs=[
            pl.BlockSpec((nb, c, w), lambda t: (0, t, C_U // w)),
            pl.BlockSpec((nb, c, w), lambda t: (0, jnp.maximum(t - 1, 0), C_U // w)),
            pl.BlockSpec((nb, c, w), lambda t: (0, t, C_VM // w)),
            pl.BlockSpec((nb, c, w), lambda t: (0, t, C_OM // w)),
            pl.BlockSpec((nb, c, N_SMALL), lambda t: (0, t, 0)),
            _lsel(layer, CONV_K, w),
            _lsel(layer, 1, w),
            _lsel(layer, ML_H, ML_DH, ML_DH),
            _lsel(layer, ML_H, ML_DH, ML_DH),
            _lsel(layer, 1, N_SMALL),
            _lsel(layer, 1, w)],
        out_specs=[pl.BlockSpec((nb, c, w), lambda t: (0, t, 0)),
                   pl.BlockSpec((nb, ML_H, ML_DH, ML_DH), lambda t: (0, 0, 0, 0)),
                   pl.BlockSpec((nb, ML_H, ML_DH), lambda t: (0, 0, 0)),
                   pl.BlockSpec((nb, 1, 128), lambda t: (0, 0, 0)),
                   pl.BlockSpec((nb, CONV_K - 1, w), lambda t: (0, 0, 0))],
        out_shape=[jax.ShapeDtypeStruct((nb, seq, w), BF16),
                   jax.ShapeDtypeStruct((nb, ML_H, ML_DH, ML_DH), F32),
                   jax.ShapeDtypeStruct((nb, ML_H, ML_DH), F32),
                   jax.ShapeDtypeStruct((nb, 1, 128), F32),
                   jax.ShapeDtypeStruct((nb, CONV_K - 1, w), F32)],
        scratch_shapes=[pltpu.VMEM((16, w), F32)],
        compiler_params=_cparams(1),
        name="ml_prefill",
    )(p3, p3, p3, p3, s3, cw, cb, wq, wk, gb, nw)


def _block_bcast(tile, lanes):
    c = tile.shape[0]
    lane = lax.broadcasted_iota(jnp.int32, (c, 128), 1)
    cols = [jnp.broadcast_to(tile[:, l:l + 1], (c, 128)) for l in lanes]
    left = jnp.where(lane < SSM_P, cols[0], cols[1])
    right = jnp.where(lane < SSM_P, cols[2], cols[3])
    return jnp.concatenate([left, right], axis=1)


def _ssd_prefill_kernel(z_ref, xs_ref, xsp_ref, bc_ref, bcp_ref, sm_ref, cw_ref, cb_ref, dtb_ref, alog_ref,
                        dv_ref, nw_ref, y_ref, s_ref, cv_ref, *rest):
    *zero_refs, tail = rest
    _zero_fill(zero_refs)
    c = CHUNK
    nb = z_ref.shape[0]
    wx = SSM_H * SSM_P
    gw = SSM_R * SSM_P
    first = pl.program_id(0) == 0

    @pl.when(first)
    def _():
        s_ref[...] = jnp.zeros_like(s_ref)

    row = lax.broadcasted_iota(jnp.int32, (c, c), 0)
    col = lax.broadcasted_iota(jnp.int32, (c, c), 1)
    causal = row >= col
    tril_b = jnp.where(causal, 1.0, 0.0).astype(BF16)
    lane_blk = lax.shift_right_logical(lax.broadcasted_iota(jnp.int32, (c, gw), 1), 6)
    head_mask = [jnp.where(lane_blk == r, 1.0, 0.0).astype(BF16) for r in range(SSM_R)]
    a_row = -jnp.exp(alog_ref[...])
    shift = _shift_matrix(first)

    def conv(cur_ref, prv_ref, b, lanes, w_off):
        wl = slice(w_off + lanes.start, w_off + lanes.stop)
        return _conv_silu(shift, prv_ref[b, :, lanes], cur_ref[b, :, lanes], cw_ref[:, wl], cb_ref[:, wl])

    def load(s):
        s.s_old = [s_ref[s.b, g * SSM_R:(g + 1) * SSM_R].reshape(gw, SSM_N) for g in range(SSM_G)]

    def dt_phase(s):
        b = s.b
        tail[:, 0:wx] = xs_ref[b, c - 16:c, :].astype(F32)
        tail[:, wx:2 * wx] = bc_ref[b, c - 16:c, :].astype(F32)
        cv_ref[b] = tail[16 - (CONV_K - 1):16, :]
        s.delta = _softplus(sm_ref[b] + dtb_ref[...])
        s.cum = _cumsum_rows(s.delta * a_row, tril_b)
        s.cum_t = s.cum.T
        s.delta_t = s.delta.T
        s.ecum = jnp.exp(s.cum)
        cum_end = s.cum[c - 1:c, :]
        s.w_state = jnp.exp(cum_end - s.cum) * s.delta
        s.dec_row = jnp.exp(cum_end)

    def conv_phase(it):
        g = it.k
        it.lanes = [L_DT + g * SSM_R + r for r in range(SSM_R)]
        it.xg = conv(xs_ref, xsp_ref, it.b, slice(g * gw, (g + 1) * gw), 0)
        it.bg = conv(bc_ref, bcp_ref, it.b, slice(g * SSM_N, (g + 1) * SSM_N), wx).astype(BF16)
        it.cg = conv(bc_ref, bcp_ref, it.b, slice((SSM_G + g) * SSM_N, (SSM_G + g + 1) * SSM_N), wx).astype(BF16)

    def seg_phase(it):
        s = it.seq
        it.seg = [jnp.exp(jnp.where(causal, s.cum[:, ln:ln + 1] - s.cum_t[ln:ln + 1, :], -jnp.inf))
                  * s.delta_t[ln:ln + 1, :] for ln in it.lanes]
        xgb = it.xg.astype(BF16)
        it.x_stack = jnp.concatenate([xgb * head_mask[r] for r in range(SSM_R)], axis=0)
        it.xwt = (it.xg * _block_bcast(s.w_state, it.lanes)).T.astype(BF16)

    def cb_phase(it):
        it.cb_mat = _dot_nt(it.cg, it.bg)

    def mix_phase(it):
        s = it.seq
        s_old = s.s_old[it.k]
        m_wide = jnp.concatenate([(it.cb_mat * sg).astype(BF16) for sg in it.seg], axis=1)
        it.y = (_dot(m_wide, it.x_stack)
                + _dot_nt(it.cg, s_old.astype(BF16)) * _block_bcast(s.ecum, it.lanes))
        upd = _dot(it.xwt, it.bg)
        it.s_new = [s_old[r * SSM_P:(r + 1) * SSM_P] * s.dec_row[:, ln:ln + 1] + upd[r * SSM_P:(r + 1) * SSM_P]
                    for r, ln in enumerate(it.lanes)]

    def emit_phase(it):
        gs = slice(it.k * gw, (it.k + 1) * gw)
        y = it.y + dv_ref[:, gs] * it.xg
        z = z_ref[it.b, :, gs].astype(F32)
        yn = _groupnorm(y * _silu(z), nw_ref[:, gs], False)
        y_ref[it.b, :, gs] = yn.astype(y_ref.dtype)

    def finish(s):
        for it in s.heads:
            for r in range(SSM_R):
                s_ref[s.b, it.k * SSM_R + r] = it.s_new[r]

    _seq_loop(nb, 2, [load, dt_phase], SSM_G, [conv_phase, seg_phase, cb_phase, mix_phase, emit_phase], finish)


def _ssd_prefill(proj, small, cw, cb, dtb, alog, dvec, nw, *, layer, nb, seq, zero_shape=None):
    c = CHUNK
    wx = SSM_H * SSM_P
    p3 = proj.reshape(nb, seq, N_MAIN)
    s3 = small.reshape(nb, seq, N_SMALL)
    z_specs, z_shapes = ([], []) if zero_shape is None else [[v] for v in _zero_out(zero_shape, seq // c)]
    return pl.pallas_call(
        _ssd_prefill_kernel,
        grid=(seq // c,),
        in_specs=[
            pl.BlockSpec((nb, c, wx), lambda t: (0, t, C_Z // wx)),
            pl.BlockSpec((nb, c, wx), lambda t: (0, t, C_XS // wx)),
            pl.BlockSpec((nb, c, wx), lambda t: (0, jnp.maximum(t - 1, 0), C_XS // wx)),
            pl.BlockSpec((nb, c, wx), lambda t: (0, t, C_B // wx)),
            pl.BlockSpec((nb, c, wx), lambda t: (0, jnp.maximum(t - 1, 0), C_B // wx)),
            pl.BlockSpec((nb, c, N_SMALL), lambda t: (0, t, 0)),
            _lsel(layer, CONV_K, 2 * wx),
            _lsel(layer, 1, 2 * wx),
            _lsel(layer, 1, N_SMALL),
            _lsel(layer, 1, N_SMALL),
            _lsel(layer, 1, wx),
            _lsel(layer, 1, wx)],
        out_specs=[pl.BlockSpec((nb, c, wx), lambda t: (0, t, 0)),
                   pl.BlockSpec((nb, SSM_H, SSM_P, SSM_N), lambda t: (0, 0, 0, 0)),
                   pl.BlockSpec((nb, CONV_K - 1, 2 * wx), lambda t: (0, 0, 0))] + z_specs,
        out_shape=[jax.ShapeDtypeStruct((nb, seq, wx), BF16),
                   jax.ShapeDtypeStruct((nb, SSM_H, SSM_P, SSM_N), F32),
                   jax.ShapeDtypeStruct((nb, CONV_K - 1, 2 * wx), F32)] + z_shapes,
        scratch_shapes=[pltpu.VMEM((16, 2 * wx), F32)],
        compiler_params=_cparams(1),
        name="ssd_prefill",
    )(p3, p3, p3, p3, p3, s3, cw, cb, dtb, alog, dvec, nw)


MERGE_SUB = 256


def _merge_kernel(yr_ref, ym_ref, ys_ref, gr_ref, gm_ref, gs_ref, x_ref, wr_ref, wm_ref, ws_ref,
                  wo_ref, nw_ref, wq_ref, xo_ref, qo_ref, *zero_refs):
    _zero_fill(zero_refs)
    tm = x_ref.shape[0]
    n_sub = max(1, tm // MERGE_SUB)
    subs = [slice(i * (tm // n_sub), (i + 1) * (tm // n_sub)) for i in range(n_sub)]
    branches = ((yr_ref, gr_ref, wr_ref), (ym_ref, gm_ref, wm_ref), (ys_ref, gs_ref, ws_ref))
    proj = [[_dot(y_ref[rs, :].astype(BF16), w_ref[...]) for y_ref, _, w_ref in branches] for rs in subs]
    merged = []
    for rs, pr in zip(subs, proj):
        gated = [_sigmoid(g_ref[rs, :].astype(F32)) * p for (_, g_ref, _), p in zip(branches, pr)]
        merged.append((gated[0] + gated[1] + gated[2]).astype(BF16))
    xns = [x_ref[rs, :] + _dot(m, wo_ref[...]) for rs, m in zip(subs, merged)]
    hqs = []
    for rs, xn in zip(subs, xns):
        xo_ref[rs, :] = xn
        hqs.append(_rmsnorm(xn, nw_ref[...]).astype(BF16))
    for rs, hq in zip(subs, hqs):
        qo_ref[rs, :] = _dot(hq, wq_ref[...]).astype(qo_ref.dtype)


def _merge(yr, ym, ys, proj, x, wr, wm, ws, wo, nw, wq, *, layer, tm, q_dtype, zero_shape=None):
    m = x.shape[0]
    d = D_MODEL
    row = lambda i: (i, 0)
    z_specs, z_shapes = ([], []) if zero_shape is None else [[v] for v in _zero_out(zero_shape, m // tm)]
    wspec = _lsel(layer, d, d, pipeline_mode=pl.Buffered(1))
    return pl.pallas_call(
        _merge_kernel,
        grid=(m // tm,),
        in_specs=[pl.BlockSpec((tm, d), row), pl.BlockSpec((tm, d), row), pl.BlockSpec((tm, d), row),
                  pl.BlockSpec((tm, d), lambda i: (i, C_GR // d)),
                  pl.BlockSpec((tm, d), lambda i: (i, C_GM // d)),
                  pl.BlockSpec((tm, d), lambda i: (i, C_GS // d)),
                  pl.BlockSpec((tm, d), row),
                  wspec, wspec, wspec, wspec,
                  _lsel(layer, 1, d),
                  wspec],
        out_specs=[pl.BlockSpec((tm, d), row), pl.BlockSpec((tm, d), row)] + z_specs,
        out_shape=[jax.ShapeDtypeStruct((m, d), F32), jax.ShapeDtypeStruct((m, d), q_dtype)] + z_shapes,
        compiler_params=_cparams(1),
        name="merge",
    )(yr, ym, ys, proj, proj, proj, x, wr, wm, ws, wo, nw, wq)


def _attn_prefill_kernel(q_ref, k_ref, v_ref, x_ref, wo_ref, o_ref):
    heads = [slice(h * MEM_DH, (h + 1) * MEM_DH) for h in range(MEM_H)]
    scores = [_dot_nt(q_ref[0, :, sl], k_ref[0, :, sl].astype(BF16)) * (MEM_DH ** -0.5) for sl in heads]
    probs = []
    for s in scores:
        p = jnp.exp(s - jnp.max(s, axis=-1, keepdims=True))
        probs.append((p / jnp.sum(p, axis=-1, keepdims=True)).astype(BF16))
    outs = [_dot(p, v_ref[0, :, sl].astype(BF16)).astype(BF16) for p, sl in zip(probs, heads)]
    o_all = jnp.concatenate(outs, axis=1)
    o_ref[0] = x_ref[0] + _dot(o_all, wo_ref[...])


def _attn_prefill(q, mk, mv, x, wo, *, layer, nb, seq, tq):
    d = D_MODEL
    q3 = q.reshape(nb, seq, d)
    x3 = x.reshape(nb, seq, d)
    out = pl.pallas_call(
        _attn_prefill_kernel,
        grid=(nb, seq // tq),
        in_specs=[pl.BlockSpec((1, tq, d), lambda b, t: (b, t, 0)),
                  pl.BlockSpec((1, MEM_LEN, d), lambda b, t: (layer * nb + b, 0, 0)),
                  pl.BlockSpec((1, MEM_LEN, d), lambda b, t: (layer * nb + b, 0, 0)),
                  pl.BlockSpec((1, tq, d), lambda b, t: (b, t, 0)),
                  _lsel(layer, d, d)],
        out_specs=pl.BlockSpec((1, tq, d), lambda b, t: (b, t, 0)),
        out_shape=jax.ShapeDtypeStruct((nb, seq, d), F32),
        compiler_params=_cparams(2),
        name="attn_prefill",
    )(q3, mk, mv, x3, wo)
    return out.reshape(nb * seq, d)


def _attn_decode_kernel(q_ref, k_ref, v_ref, o_ref):
    for j in range(q_ref.shape[0]):
        s = jnp.sum(k_ref[j] * q_ref[j][None], axis=-1, keepdims=True) * (MEM_DH ** -0.5)
        p = jnp.exp(s - jnp.max(s, axis=0, keepdims=True))
        p = p / jnp.sum(p, axis=0, keepdims=True)
        o_ref[j] = jnp.sum(p * v_ref[j], axis=0)


def _attn_decode(q, ck, cv, *, layer, bb):
    nb = q.shape[0]
    nblk = nb // bb
    blk = (bb, MEM_LEN, MEM_H, MEM_DH)
    return pl.pallas_call(
        _attn_decode_kernel,
        grid=(nblk,),
        in_specs=[pl.BlockSpec((bb, MEM_H, MEM_DH), lambda i: (i, 0, 0)),
                  pl.BlockSpec(blk, lambda i: (layer * nblk + i, 0, 0, 0)),
                  pl.BlockSpec(blk, lambda i: (layer * nblk + i, 0, 0, 0))],
        out_specs=pl.BlockSpec((bb, MEM_H, MEM_DH), lambda i: (i, 0, 0)),
        out_shape=jax.ShapeDtypeStruct((nb, MEM_H, MEM_DH), F32),
        compiler_params=_cparams(1),
        name="attn_decode",
    )(q, ck, cv)


MLP_SUB = 1024


def _mlp_kernel(x_ref, nw_ref, w1_ref, w2_ref, nf_ref, o_ref, xn_ref, acc_ref, *, final_norm):
    j = pl.program_id(1)

    @pl.when(j == 0)
    def _():
        xn_ref[...] = _rmsnorm(x_ref[...], nw_ref[...]).astype(BF16)
        acc_ref[...] = jnp.zeros_like(acc_ref)

    subs = [slice(s * MLP_SUB, (s + 1) * MLP_SUB) for s in range(w1_ref.shape[1] // MLP_SUB)]
    hids = [_dot(xn_ref[...], w1_ref[:, sl]) for sl in subs]
    hids = [jnp.square(jnp.maximum(hid, 0.0)).astype(BF16) for hid in hids]
    acc = acc_ref[...]
    for hid, sl in zip(hids, subs):
        acc = acc + _dot(hid, w2_ref[sl, :])
    acc_ref[...] = acc

    @pl.when(j == pl.num_programs(1) - 1)
    def _():
        y = x_ref[...] + acc_ref[...]
        if final_norm:
            y = _rmsnorm(y, nf_ref[...])
        o_ref[...] = y


def _mlp(x, nw, w1, w2, nf, *, layer, tm, tf, final_norm):
    m = x.shape[0]
    d = D_MODEL
    return pl.pallas_call(
        functools.partial(_mlp_kernel, final_norm=final_norm),
        grid=(m // tm, D_FF // tf),
        in_specs=[pl.BlockSpec((tm, d), lambda i, j: (i, 0)),
                  _lsel(layer, 1, d),
                  pl.BlockSpec((None, d, tf), lambda i, j: (layer, 0, j)),
                  pl.BlockSpec((None, tf, d), lambda i, j: (layer, j, 0)),
                  pl.BlockSpec((1, d), lambda i, j: (0, 0))],
        out_specs=pl.BlockSpec((tm, d), lambda i, j: (i, 0)),
        out_shape=jax.ShapeDtypeStruct((m, d), F32),
        scratch_shapes=[pltpu.VMEM((tm, d), BF16), pltpu.VMEM((tm, d), F32)],
        compiler_params=_cparams(2),
        name="mlp",
    )(x, nw, w1, w2, nf)


DEC_BB = 8


def _row_mask(x, j):
    rows = lax.broadcasted_iota(jnp.int32, x.shape, 0)
    return jnp.where(rows == j, x, 0.0).astype(BF16)


def _blockdiag_rows(v):
    n = v.shape[0]
    tiled = jnp.concatenate([v] * n, axis=1)
    rows = lax.broadcasted_iota(jnp.int32, tiled.shape, 0)
    blk = lax.shift_right_logical(lax.broadcasted_iota(jnp.int32, tiled.shape, 1), 8)
    return _pad_rows(jnp.where(rows == blk, tiled, 0.0), 128).astype(BF16)


def _ret_decode_kernel(q_ref, k_ref, v_ref, g_ref, cos_ref, sin_ref, nw_ref, s_ref, y_ref, so_ref, ybuf):
    cosf = cos_ref[...]
    sinf = sin_ref[...]
    for h in range(RET_H):
        sk = slice(h * RET_DK, (h + 1) * RET_DK)
        sv = slice(h * RET_DV, (h + 1) * RET_DV)
        g_dec = math.exp(_LOG_G[h])
        q = q_ref[:, sk]
        k = k_ref[:, sk]
        qr = q * cosf + pltpu.roll(q, RET_DK // 2, 1) * sinf
        kr = (k * cosf + pltpu.roll(k, RET_DK // 2, 1) * sinf) * (RET_DK ** -0.5)
        k_t = _pad_rows_t(kr).astype(BF16)
        q16 = _pad_rows(qr, 16).astype(BF16)
        upd = _dot(k_t, _blockdiag_rows(v_ref[:, sv]))
        s_wide = []
        for j in range(DEC_BB):
            s_new = s_ref[j, h] * g_dec + upd[:, j * RET_DV:(j + 1) * RET_DV]
            so_ref[j, h] = s_new
            s_wide.append(s_new.astype(BF16))
        y_all = _dot(q16, jnp.concatenate(s_wide, axis=1))
        for j in range(DEC_BB):
            ybuf[j:j + 1, sv] = y_all[j:j + 1, j * RET_DV:(j + 1) * RET_DV]
        y_ref[:, sv] = _groupnorm(ybuf[:, sv], nw_ref[:, sv], True) * _silu(g_ref[:, sv])


def _ret_decode(proj, cosf, sinf, nw, state, prev, *, layer):
    nb = proj.shape[0]
    bb = DEC_BB
    nblk = nb // bb
    kern, steps, ib, ob, al_args, al_specs, al_map = _carried_call(_ret_decode_kernel, nblk, layer, prev)
    sblk = (bb, RET_H, RET_DK, RET_DV)
    return pl.pallas_call(
        kern,
        grid=(steps,),
        in_specs=al_specs + [
            pl.BlockSpec((bb, 512), lambda i: (ib(i), C_Q // 512)),
            pl.BlockSpec((bb, 512), lambda i: (ib(i), C_K // 512)),
            pl.BlockSpec((bb, 1024), lambda i: (ib(i), C_V // 1024)),
            pl.BlockSpec((bb, 1024), lambda i: (ib(i), C_G // 1024)),
            pl.BlockSpec((1, RET_DK), lambda i: (0, 0)),
            pl.BlockSpec((1, RET_DK), lambda i: (0, 0)),
            _lsel(layer, 1, 1024),
            pl.BlockSpec(sblk, lambda i: (layer * nblk + ib(i), 0, 0, 0))],
        out_specs=[pl.BlockSpec((bb, 1024), lambda i: (ib(i), 0)),
                   pl.BlockSpec(sblk, lambda i: (ob(i), 0, 0, 0))],
        out_shape=[jax.ShapeDtypeStruct((nb, RET_H * RET_DV), F32),
                   jax.ShapeDtypeStruct((DEPTH * nb, RET_H, RET_DK, RET_DV), F32)],
        scratch_shapes=[pltpu.VMEM((bb, RET_H * RET_DV), F32)],
        input_output_aliases=al_map,
        compiler_params=_cparams(1),
        name="ret_decode",
    )(*al_args, proj, proj, proj, proj, cosf, sinf, nw, state)


def _conv_step(x, buf_ref, cw_ref, cb_ref, bo_ref):
    acc = cb_ref[...]
    for j in range(CONV_K - 1):
        acc = acc + buf_ref[j] * cw_ref[j:j + 1, :]
    acc = acc + x * cw_ref[CONV_K - 1:CONV_K, :]
    bo_ref[0] = buf_ref[1]
    bo_ref[1] = buf_ref[2]
    bo_ref[2] = x
    return _silu(acc)


def _ml_decode_kernel(u_ref, v_ref, o_ref, sm_ref, buf_ref, cw_ref, cb_ref, wq_ref, wk_ref, gb_ref, nw_ref,
                      c_ref, n_ref, m_ref,
                      y_ref, co_ref, no_ref, mo_ref, bo_ref, ybuf):
    uc = _conv_step(u_ref[...], buf_ref, cw_ref, cb_ref, bo_ref).astype(BF16)
    gates = sm_ref[...] + gb_ref[...]
    lane4 = lax.broadcasted_iota(jnp.int32, m_ref.shape, 1)
    m_all = m_ref[...]
    m_out = m_all
    for h in range(ML_H):
        sl = slice(h * ML_DH, (h + 1) * ML_DH)
        qm = _dot(uc[:, sl], wq_ref[h])
        km = _dot(uc[:, sl], wk_ref[h]) * (ML_DH ** -0.5)
        i_pre = gates[:, L_IG + h:L_IG + h + 1]
        log_f = _log_sigmoid(gates[:, L_FG + h:L_FG + h + 1])
        m_prev = m_all[:, h:h + 1]
        inter = log_f + m_prev
        m_new = jnp.maximum(inter, i_pre)
        w_s = jnp.exp(i_pre - m_new)
        w_prev = jnp.exp(inter - m_new)
        kw = km * w_s
        n_new = n_ref[:, sl] * w_prev + kw
        no_ref[:, sl] = n_new
        m_out = jnp.where(lane4 == h, m_new, m_out)

        k_t = _pad_rows_t(kw).astype(BF16)
        q16 = _pad_rows(qm, 16).astype(BF16)
        upd = _dot(k_t, _blockdiag_rows(v_ref[:, sl]))
        c_wide = []
        for j in range(DEC_BB):
            c_new = c_ref[j, h] * w_prev[j:j + 1, :] + upd[:, j * ML_DH:(j + 1) * ML_DH]
            co_ref[j, h] = c_new
            c_wide.append(c_new.astype(BF16))
        y_all = _dot(q16, jnp.concatenate(c_wide, axis=1))
        for j in range(DEC_BB):
            ybuf[j:j + 1, sl] = y_all[j:j + 1, j * ML_DH:(j + 1) * ML_DH]
        den = jnp.sum(qm * n_new, axis=-1, keepdims=True)
        den = jnp.maximum(jnp.abs(den), jnp.exp(-m_new))
        y_ref[:, sl] = _groupnorm(ybuf[:, sl] / den, nw_ref[:, sl], True) * _sigmoid(o_ref[:, sl])
    mo_ref[...] = m_out


def _ml_decode(proj, small, buf, cw, cb, wq, wk, gb, nw, c_state, n_state, m_state, prev, *, layer):
    nb = proj.shape[0]
    bb = DEC_BB
    nblk = nb // bb
    dh = ML_DH
    w = ML_H * ML_DH
    kern, steps, ib, ob, al_args, al_specs, al_map = _carried_call(_ml_decode_kernel, nblk, layer, prev)
    cblk = (bb, ML_H, dh, dh)
    return pl.pallas_call(
        kern,
        grid=(steps,),
        in_specs=al_specs + [
            pl.BlockSpec((bb, w), lambda i: (ib(i), C_U // w)),
            pl.BlockSpec((bb, w), lambda i: (ib(i), C_VM // w)),
            pl.BlockSpec((bb, w), lambda i: (ib(i), C_OM // w)),
            pl.BlockSpec((bb, N_SMALL), lambda i: (ib(i), 0)),
            pl.BlockSpec((None, CONV_K - 1, bb, w), lambda i: (layer, 0, ib(i), 0)),
            _lsel(layer, CONV_K, w),
            _lsel(layer, 1, w),
            _lsel(layer, ML_H, dh, dh),
            _lsel(layer, ML_H, dh, dh),
            _lsel(layer, 1, N_SMALL),
            _lsel(layer, 1, w),
            pl.BlockSpec(cblk, lambda i: (layer * nblk + ib(i), 0, 0, 0)),
            pl.BlockSpec((bb, w), lambda i: (layer * nblk + ib(i), 0)),
            pl.BlockSpec((bb, ML_H), lambda i: (layer * nblk + ib(i), 0))],
        out_specs=[pl.BlockSpec((bb, w), lambda i: (ib(i), 0)),
                   pl.BlockSpec(cblk, lambda i: (ob(i), 0, 0, 0)),
                   pl.BlockSpec((bb, w), lambda i: (ib(i), 0)),
                   pl.BlockSpec((bb, ML_H), lambda i: (ib(i), 0)),
                   pl.BlockSpec((CONV_K - 1, bb, w), lambda i: (0, ib(i), 0))],
        out_shape=[jax.ShapeDtypeStruct((nb, w), F32),
                   jax.ShapeDtypeStruct((DEPTH * nb, ML_H, dh, dh), F32),
                   jax.ShapeDtypeStruct((nb, w), F32),
                   jax.ShapeDtypeStruct((nb, ML_H), F32),
                   jax.ShapeDtypeStruct((CONV_K - 1, nb, w), F32)],
        scratch_shapes=[pltpu.VMEM((bb, w), F32)],
        input_output_aliases=al_map,
        compiler_params=_cparams(1),
        name="ml_decode",
    )(*al_args, proj, proj, proj, small, buf, cw, cb, wq, wk, gb, nw, c_state, n_state, m_state)


def _ssd_decode_kernel(z_ref, xs_ref, bc_ref, sm_ref, buf_ref, cw_ref, cb_ref, dtb_ref, alog_ref, dv_ref, nw_ref,
                       s_ref, y_ref, so_ref, bo_ref, ybuf):
    wx = SSM_H * SSM_P
    gw = SSM_R * SSM_P
    x_in = jnp.concatenate([xs_ref[...], bc_ref[...]], axis=1)
    xc = _conv_step(x_in, buf_ref, cw_ref, cb_ref, bo_ref)
    xs = xc[:, 0:wx]

    delta = _softplus(sm_ref[...] + dtb_ref[...])
    d_a = jnp.exp(delta * (-jnp.exp(alog_ref[...])))
    lane_blk = lax.shift_right_logical(lax.broadcasted_iota(jnp.int32, (DEC_BB, wx), 1), 6)
    dt_full = jnp.zeros((DEC_BB, wx), F32)
    for hh in range(SSM_H):
        dt_full = jnp.where(lane_blk == hh, delta[:, L_DT + hh:L_DT + hh + 1], dt_full)
    x_t = _pad_rows_t(xs * dt_full).astype(BF16)

    for g in range(SSM_G):
        gs = slice(g * gw, (g + 1) * gw)
        b_pad = _pad_rows(xc[:, wx + g * SSM_N:wx + (g + 1) * SSM_N], 128)
        c16 = _pad_rows(xc[:, wx + (SSM_G + g) * SSM_N:wx + (SSM_G + g + 1) * SSM_N], 16).astype(BF16)
        for j in range(DEC_BB):
            upd = _dot(x_t[gs, :], _row_mask(b_pad, j))
            parts = []
            for r in range(SSM_R):
                hh = g * SSM_R + r
                s_new = (s_ref[j, hh] * d_a[j:j + 1, L_DT + hh:L_DT + hh + 1]
                         + upd[r * SSM_P:(r + 1) * SSM_P])
                so_ref[j, hh] = s_new
                parts.append(s_new.astype(BF16))
            ybuf[j:j + 1, gs] = _dot_nt(c16, jnp.concatenate(parts, axis=0))[j:j + 1, :]
        y = ybuf[:, gs] + dv_ref[:, gs] * xs[:, gs]
        y_ref[:, gs] = _groupnorm(y * _silu(z_ref[:, gs]), nw_ref[:, gs], False)


def _ssd_decode(proj, small, buf, cw, cb, dtb, alog, dvec, nw, state, prev, *, layer):
    nb = proj.shape[0]
    bb = DEC_BB
    nblk = nb // bb
    wx = SSM_H * SSM_P
    kern, steps, ib, ob, al_args, al_specs, al_map = _carried_call(_ssd_decode_kernel, nblk, layer, prev)
    sblk = (bb, SSM_H, SSM_P, SSM_N)
    return pl.pallas_call(
        kern,
        grid=(steps,),
        in_specs=al_specs + [
            pl.BlockSpec((bb, wx), lambda i: (ib(i), C_Z // wx)),
            pl.BlockSpec((bb, wx), lambda i: (ib(i), C_XS // wx)),
            pl.BlockSpec((bb, wx), lambda i: (ib(i), C_B // wx)),
            pl.BlockSpec((bb, N_SMALL), lambda i: (ib(i), 0)),
            pl.BlockSpec((None, CONV_K - 1, bb, 2 * wx), lambda i: (layer, 0, ib(i), 0)),
            _lsel(layer, CONV_K, 2 * wx),
            _lsel(layer, 1, 2 * wx),
            _lsel(layer, 1, N_SMALL),
            _lsel(layer, 1, N_SMALL),
            _lsel(layer, 1, wx),
            _lsel(layer, 1, wx),
            pl.BlockSpec(sblk, lambda i: (layer * nblk + ib(i), 0, 0, 0))],
        out_specs=[pl.BlockSpec((bb, wx), lambda i: (ib(i), 0)),
                   pl.BlockSpec(sblk, lambda i: (ob(i), 0, 0, 0)),
                   pl.BlockSpec((CONV_K - 1, bb, 2 * wx), lambda i: (0, ib(i), 0))],
        out_shape=[jax.ShapeDtypeStruct((nb, wx), F32),
                   jax.ShapeDtypeStruct((DEPTH * nb, SSM_H, SSM_P, SSM_N), F32),
                   jax.ShapeDtypeStruct((CONV_K - 1, nb, 2 * wx), F32)],
        scratch_shapes=[pltpu.VMEM((bb, wx), F32)],
        input_output_aliases=al_map,
        compiler_params=_cparams(1),
        name="ssd_decode",
    )(*al_args, proj, proj, proj, small, buf, cw, cb, dtb, alog, dvec, nw, state)


def _rope_tables(pos):
    half = RET_DK // 2
    freqs = ROPE_THETA ** (-jnp.arange(half, dtype=F32) / half)
    ang = pos.astype(F32)[:, None] * freqs[None, :]
    cos = jnp.cos(ang)
    sin = jnp.sin(ang)
    return jnp.concatenate([cos, cos], axis=1), jnp.concatenate([-sin, sin], axis=1)


def _pad_lanes(v, offset):
    n = v.shape[1]
    return jnp.pad(v.astype(F32), ((0, 0), (offset, N_SMALL - offset - n))).reshape(DEPTH, 1, N_SMALL)


REPACK_TN = 1024
REPACK_RUNS = ((0, 6, 0), (6, 9, 2 * ML_H), (9, 12, 2 * ML_H + SSM_H))
SRC_GATE_BLK = 6144 // 128
SRC_DT_BLK = (6144 + 2 * ML_H + 3072) // 128


def _repack_kernel(a_ref, b_ref, g_ref, dt_ref, o_ref, os_ref):
    j = pl.program_id(1)

    @pl.when(j == 0)
    def _():
        row = lax.broadcasted_iota(jnp.int32, (N_SMALL, D_MODEL), 0)
        os_ref[...] = jnp.where(row < L_DT, g_ref[...],
                                jnp.where(row < L_DT + SSM_H, dt_ref[...], 0.0)).astype(BF16)

    for lo, hi, skip in REPACK_RUNS:
        @pl.when((j >= lo) & (j < hi))
        def _(skip=skip):
            src = a_ref[...] if skip == 0 else jnp.concatenate([a_ref[skip:, :], b_ref[0:skip, :]], axis=0)
            o_ref[...] = src.astype(BF16)


def _repack_w_in(w_in_t):
    tn = REPACK_TN
    return pl.pallas_call(
        _repack_kernel,
        grid=(DEPTH, N_MAIN // tn),
        in_specs=[pl.BlockSpec((None, tn, D_MODEL), lambda l, j: (l, j, 0)),
                  pl.BlockSpec((None, 128, D_MODEL), lambda l, j: (l, (j + 1) * (tn // 128), 0)),
                  pl.BlockSpec((None, 128, D_MODEL), lambda l, j: (l, SRC_GATE_BLK, 0)),
                  pl.BlockSpec((None, 128, D_MODEL), lambda l, j: (l, SRC_DT_BLK, 0))],
        out_specs=[pl.BlockSpec((None, tn, D_MODEL), lambda l, j: (l, j, 0)),
                   pl.BlockSpec((None, N_SMALL, D_MODEL), lambda l, j: (l, 0, 0))],
        out_shape=[jax.ShapeDtypeStruct((DEPTH, N_MAIN, D_MODEL), BF16),
                   jax.ShapeDtypeStruct((DEPTH, N_SMALL, D_MODEL), BF16)],
        compiler_params=_cparams(2),
        name="repack_w_in",
    )(w_in_t, w_in_t, w_in_t, w_in_t)


def _prep_weights(W):
    w_main, w_small = _repack_w_in(jnp.swapaxes(W['w_in'], 1, 2))
    row = lambda v: v.reshape(DEPTH, 1, -1).astype(F32)
    bf = lambda v: v.astype(BF16)
    return dict(
        w_main=w_main, w_small=w_small,
        norm_mix=row(W['norm_mix_w']),
        ret_norm=row(W['ret_norm_w']),
        ml_cw=W['ml_conv_w'], ml_cb=row(W['ml_conv_b']),
        ml_wq=bf(W['ml_wq']), ml_wk=bf(W['ml_wk']),
        ml_gb=_pad_lanes(W['ml_gate_b'], L_IG),
        ml_norm=row(W['ml_norm_w']),
        ssm_cw=W['ssm_conv_w'], ssm_cb=row(W['ssm_conv_b']),
        ssm_dtb=_pad_lanes(W['ssm_dt_bias'], L_DT),
        ssm_alog=_pad_lanes(W['ssm_A_log'], L_DT),
        ssm_dvec=row(jnp.repeat(W['ssm_D'], SSM_P, axis=1)),
        ssm_norm=row(W['ssm_norm_w']),
        w_br_ret=bf(W['w_br_ret']), w_br_ml=bf(W['w_br_ml']),
        w_br_ssm=bf(W['w_br_ssm']), w_out=bf(W['w_out_mix']),
        norm_mem=row(W['norm_mem_w']),
        mem_wq=bf(W['mem_wq']), mem_wo=bf(W['mem_wo']),
        norm_mlp=row(W['norm_mlp_w']),
        mlp_w1=bf(W['mlp_w1']), mlp_w2=bf(W['mlp_w2']),
    )


def kernel(x_prompt, x_sample, mem_prompt, state_ret, state_mlstm_C, state_mlstm_n, state_mlstm_m,
           state_mlstm_conv, state_ssm, state_ssm_conv, cache_mem_k, cache_mem_v,
           norm_mix_w, w_in, ret_norm_w, ml_conv_w, ml_conv_b, ml_wq, ml_wk, ml_gate_b, ml_norm_w,
           ssm_conv_w, ssm_conv_b, ssm_dt_bias, ssm_A_log, ssm_D, ssm_norm_w,
           w_br_ret, w_br_ml, w_br_ssm, w_out_mix, norm_mem_w, mem_wq, mem_wk, mem_wv, mem_wo,
           norm_mlp_w, mlp_w1, mlp_w2, norm_f_w):
    W = dict(norm_mix_w=norm_mix_w, w_in=w_in, ret_norm_w=ret_norm_w, ml_conv_w=ml_conv_w,
             ml_conv_b=ml_conv_b, ml_wq=ml_wq, ml_wk=ml_wk, ml_gate_b=ml_gate_b, ml_norm_w=ml_norm_w,
             ssm_conv_w=ssm_conv_w, ssm_conv_b=ssm_conv_b, ssm_dt_bias=ssm_dt_bias, ssm_A_log=ssm_A_log,
             ssm_D=ssm_D, ssm_norm_w=ssm_norm_w, w_br_ret=w_br_ret, w_br_ml=w_br_ml, w_br_ssm=w_br_ssm,
             w_out_mix=w_out_mix, norm_mem_w=norm_mem_w, mem_wq=mem_wq, mem_wo=mem_wo,
             norm_mlp_w=norm_mlp_w, mlp_w1=mlp_w1, mlp_w2=mlp_w2)
    lw = _prep_weights(W)
    norm_f = norm_f_w.reshape(1, -1).astype(F32)
    d = D_MODEL

    nb, seq = x_prompt.shape[0], x_prompt.shape[1]
    ns = x_sample.shape[0]
    cos_p, sin_p = _rope_tables(jnp.arange(seq, dtype=jnp.int32))
    memk, memv, memk_b, memv_b = _memkv(mem_prompt.reshape(nb * MEM_LEN, d), mem_wk.astype(BF16),
                                        mem_wv.astype(BF16), tm=1024)
    memk3 = memk_b.reshape(DEPTH * nb, MEM_LEN, d)
    memv3 = memv_b.reshape(DEPTH * nb, MEM_LEN, d)

    x = x_prompt.reshape(nb * seq, d)
    states_p = []
    for l in range(DEPTH):
        proj, small = _inproj(x, lw['norm_mix'], lw['w_main'], lw['w_small'], layer=l, tm=2048, tn=1024,
                              out_dtype=BF16)
        zs = (lambda shape: shape) if l == 0 else (lambda shape: None)
        y_r, ret_st, *ret_s = _ret_prefill(proj, cos_p, sin_p, lw['ret_norm'], layer=l, nb=nb, seq=seq,
                                           zero_shape=zs((DEPTH * ns, RET_H, RET_DK, RET_DV)))
        y_m, ml_c, ml_n, ml_m, ml_cv = _ml_prefill(proj, small, lw['ml_cw'], lw['ml_cb'], lw['ml_wq'], lw['ml_wk'],
                                                   lw['ml_gb'], lw['ml_norm'], layer=l, nb=nb, seq=seq)
        y_s, ssm_st, ssm_cv, *ssm_s = _ssd_prefill(proj, small, lw['ssm_cw'], lw['ssm_cb'], lw['ssm_dtb'],
                                                   lw['ssm_alog'], lw['ssm_dvec'], lw['ssm_norm'], layer=l, nb=nb,
                                                   seq=seq, zero_shape=zs((DEPTH * ns, SSM_H, SSM_P, SSM_N)))
        x, qm, *mlc_s = _merge(y_r.reshape(nb * seq, d), y_m.reshape(nb * seq, d), y_s.reshape(nb * seq, d), proj,
                               x, lw['w_br_ret'], lw['w_br_ml'], lw['w_br_ssm'], lw['w_out'], lw['norm_mem'],
                               lw['mem_wq'], layer=l, tm=512, q_dtype=BF16,
                               zero_shape=zs((DEPTH * ns, ML_H, ML_DH, ML_DH)))
        if l == 0:
            zero_states = (ret_s[0], mlc_s[0], ssm_s[0])
        x = _attn_prefill(qm, memk3, memv3, x, lw['mem_wo'], layer=l, nb=nb, seq=seq, tq=1024)
        x = _mlp(x, lw['norm_mlp'], lw['mlp_w1'], lw['mlp_w2'], norm_f, layer=l, tm=1024, tf=2048,
                 final_norm=(l == DEPTH - 1))
        states_p.append((ret_st, ml_c, ml_n, ml_m[:, 0, :ML_H], ml_cv, ssm_st, ssm_cv))
    y_prompt = x.reshape(nb, seq, d)
    stack = lambda parts, i: jnp.stack([p[i] for p in parts])
    prompt_states = tuple(stack(states_p, i) for i in range(7))
    memk_p = memk.reshape(DEPTH, nb, MEM_LEN, MEM_H, MEM_DH)
    memv_p = memv.reshape(DEPTH, nb, MEM_LEN, MEM_H, MEM_DH)

    cos_s, sin_s = _rope_tables(PAST_LEN + jnp.arange(1, dtype=jnp.int32))
    ret_state = state_ret.reshape(DEPTH * ns, RET_H, RET_DK, RET_DV)
    mlc_state = state_mlstm_C.reshape(DEPTH * ns, ML_H, ML_DH, ML_DH)
    mln_state = state_mlstm_n.reshape(DEPTH * ns, ML_H * ML_DH)
    mlm_state = state_mlstm_m.reshape(DEPTH * ns, ML_H)
    ssm_st = state_ssm.reshape(DEPTH * ns, SSM_H, SSM_P, SSM_N)
    ck = cache_mem_k.reshape(DEPTH * ns, MEM_LEN, MEM_H, MEM_DH)
    cv = cache_mem_v.reshape(DEPTH * ns, MEM_LEN, MEM_H, MEM_DH)

    ml_buf = jnp.swapaxes(state_mlstm_conv, 1, 2)
    ssm_buf = jnp.swapaxes(state_ssm_conv, 1, 2)

    x = x_sample.reshape(ns, d)
    ret_s, mlc_s, ssm_s = zero_states
    small_s = []
    for l in range(DEPTH):
        proj, small = _inproj(x, lw['norm_mix'], lw['w_main'], lw['w_small'], layer=l, tm=ns, tn=1024, out_dtype=F32)
        y_r, ret_s = _ret_decode(proj, cos_s, sin_s, lw['ret_norm'], ret_state, ret_s,
                                 layer=l)
        y_m, mlc_s, ml_n, ml_m, ml_bo = _ml_decode(proj, small, ml_buf, lw['ml_cw'], lw['ml_cb'], lw['ml_wq'],
                                                   lw['ml_wk'], lw['ml_gb'], lw['ml_norm'],
                                                   mlc_state, mln_state, mlm_state, mlc_s,
                                                   layer=l)
        y_s, ssm_s, ssm_bo = _ssd_decode(proj, small, ssm_buf, lw['ssm_cw'], lw['ssm_cb'], lw['ssm_dtb'],
                                         lw['ssm_alog'], lw['ssm_dvec'], lw['ssm_norm'], ssm_st,
                                         ssm_s, layer=l)
        x, qm = _merge(y_r, y_m, y_s, proj, x, lw['w_br_ret'], lw['w_br_ml'], lw['w_br_ssm'], lw['w_out'],
                       lw['norm_mem'], lw['mem_wq'], layer=l, tm=ns, q_dtype=F32)
        att = _attn_decode(qm.reshape(ns, MEM_H, MEM_DH), ck, cv, layer=l, bb=4)
        x = _mm(att.reshape(ns, d), lw['mem_wo'], x, layer=l, tm=ns, tn=d)
        x = _mlp(x, lw['norm_mlp'], lw['mlp_w1'], lw['mlp_w2'], norm_f, layer=l, tm=ns, tf=2048,
                 final_norm=(l == DEPTH - 1))
        small_s.append((ml_n.reshape(ns, ML_H, ML_DH), ml_m, jnp.swapaxes(ml_bo, 0, 1),
                        jnp.swapaxes(ssm_bo, 0, 1)))
    y_sample = x.reshape(ns, 1, d)
    sample_states = (ret_s.reshape(DEPTH, ns, RET_H, RET_DK, RET_DV),
                     mlc_s.reshape(DEPTH, ns, ML_H, ML_DH, ML_DH),
                     stack(small_s, 0), stack(small_s, 1), stack(small_s, 2),
                     ssm_s.reshape(DEPTH, ns, SSM_H, SSM_P, SSM_N),
                     stack(small_s, 3))

    return (y_prompt, y_sample, *prompt_states, memk_p, memv_p, *sample_states)
```

```python
import functools
import math

import jax
import jax.numpy as jnp
from jax import lax
from jax.experimental import pallas as pl
from jax.experimental.pallas import tpu as pltpu

F32 = jnp.float32
BF16 = jnp.bfloat16

D_MODEL = 1024
DEPTH = 2
PAST_LEN = 16384
CHUNK = 128
CONV_K = 4
EPS = 1e-6
RET_H, RET_DK, RET_DV = 4, 128, 256
ROPE_THETA = 10000.0
ML_H, ML_DH = 4, 256
SSM_H, SSM_P, SSM_G, SSM_N = 16, 64, 4, 128
SSM_R = SSM_H // SSM_G
MEM_LEN, MEM_H, MEM_DH = 256, 4, 256
D_FF = 4 * D_MODEL

C_Q, C_K, C_V, C_G = 0, 512, 1024, 2048
C_U, C_VM, C_OM = 3072, 4096, 5120
C_Z, C_XS, C_B, C_C = 6144, 7168, 8192, 8704
C_GR, C_GM, C_GS = 9216, 10240, 11264
N_MAIN = 12288
L_IG, L_FG, L_DT = 0, 4, 8
N_SMALL = 128

VMEM_LIMIT = 56 * 1024 * 1024
NT_DIMS = (((1,), (1,)), ((), ()))

_LOG_G = [math.log1p(-(2.0 ** (-5.0 - h))) for h in range(RET_H)]


def _cparams(n_axes):
    return pltpu.CompilerParams(dimension_semantics=("arbitrary",) * n_axes,
                                vmem_limit_bytes=VMEM_LIMIT)


def _dot(a, b):
    return jnp.dot(a, b, preferred_element_type=F32)


def _dot_nt(a, b):
    return lax.dot_general(a, b, NT_DIMS, preferred_element_type=F32)


def _rmsnorm(x, w):
    ms = jnp.mean(x * x, axis=-1, keepdims=True)
    return x * lax.rsqrt(ms + EPS) * w


def _groupnorm(x, w, center):
    if center:
        x = x - jnp.mean(x, axis=-1, keepdims=True)
    ms = jnp.mean(x * x, axis=-1, keepdims=True)
    return x * lax.rsqrt(ms + EPS) * w


def _sigmoid(x):
    return 0.5 * jnp.tanh(0.5 * x) + 0.5


def _silu(x):
    return x * _sigmoid(x)


def _softplus(x):
    return jnp.maximum(x, 0.0) + jnp.log1p(jnp.exp(-jnp.abs(x)))


def _log_sigmoid(x):
    return -_softplus(-x)


def _cumsum_rows(x, tril_b):
    hi = x.astype(BF16)
    r1 = x - hi.astype(F32)
    mid = r1.astype(BF16)
    lo = (r1 - mid.astype(F32)).astype(BF16)
    return _dot(tril_b, hi) + _dot(tril_b, mid) + _dot(tril_b, lo)


def _pad_rows(x, rows):
    return jnp.concatenate([x, jnp.zeros((rows - x.shape[0], x.shape[1]), x.dtype)], axis=0)


def _pad_rows_t(x):
    return _pad_rows(x, 128).T


def _carried_call(kernel_fn, nblk, layer, prev):
    return (lambda *refs: kernel_fn(*refs[1:]), nblk, lambda i: i, lambda i: layer * nblk + i,
            [prev], [pl.BlockSpec(memory_space=pl.ANY)], {0: 1})


def _zero_out(shape, steps):
    blk = (shape[0] // steps,) + tuple(shape[1:])
    nz = len(shape) - 1
    return pl.BlockSpec(blk, lambda i: (i,) + (0,) * nz), jax.ShapeDtypeStruct(tuple(shape), F32)


def _zero_fill(zero_refs):
    for z_ref in zero_refs:
        z_ref[...] = jnp.zeros_like(z_ref)


def _inproj_kernel(x_ref, nw_ref, w_ref, ws_ref, o_ref, os_ref, *rest):
    *zero_refs, xn_ref = rest
    _zero_fill(zero_refs)

    @pl.when(pl.program_id(1) == 0)
    def _():
        xn = _rmsnorm(x_ref[...], nw_ref[...]).astype(BF16)
        xn_ref[...] = xn
        os_ref[...] = _dot_nt(xn, ws_ref[...])

    o_ref[...] = _dot_nt(xn_ref[...], w_ref[...]).astype(o_ref.dtype)


def _lsel(layer, *block, **kw):
    return pl.BlockSpec((None,) + block, lambda *_: (layer,) + (0,) * len(block), **kw)


def _inproj(x, nw, w_main, w_small, *, layer, tm, tn, out_dtype, zero_shape=None):
    m = x.shape[0]
    z_specs, z_shapes = [], []
    if zero_shape is not None:
        n_i, n_j = m // tm, N_MAIN // tn
        per_i = max(p for p in range(1, n_j + 1) if (zero_shape[0] // n_i) % p == 0)
        blk = (zero_shape[0] // (n_i * per_i),) + tuple(zero_shape[1:])
        z_specs = [pl.BlockSpec(blk, lambda i, j: (i * per_i + jnp.minimum(j, per_i - 1), 0, 0, 0))]
        z_shapes = [jax.ShapeDtypeStruct(tuple(zero_shape), F32)]
    return pl.pallas_call(
        _inproj_kernel,
        grid=(m // tm, N_MAIN // tn),
        in_specs=[pl.BlockSpec((tm, D_MODEL), lambda i, j: (i, 0)),
                  _lsel(layer, 1, D_MODEL),
                  pl.BlockSpec((None, tn, D_MODEL), lambda i, j: (layer, j, 0)),
                  _lsel(layer, N_SMALL, D_MODEL)],
        out_specs=[pl.BlockSpec((tm, tn), lambda i, j: (i, j)),
                   pl.BlockSpec((tm, N_SMALL), lambda i, j: (i, 0))] + z_specs,
        out_shape=[jax.ShapeDtypeStruct((m, N_MAIN), out_dtype),
                   jax.ShapeDtypeStruct((m, N_SMALL), F32)] + z_shapes,
        scratch_shapes=[pltpu.VMEM((tm, D_MODEL), BF16)],
        compiler_params=_cparams(2),
        name="inproj",
    )(x, nw, w_main, w_small)


def _mm_kernel(*refs, has_res):
    if has_res:
        a_ref, w_ref, r_ref, o_ref = refs
    else:
        a_ref, w_ref, o_ref = refs
    acc = _dot(a_ref[...].astype(BF16), w_ref[...])
    if has_res:
        acc = r_ref[...] + acc
    o_ref[...] = acc.astype(o_ref.dtype)


def _mm(a, w, res=None, *, layer, tm, tn, out_dtype=F32):
    m, k = a.shape
    n = w.shape[2]
    in_specs = [pl.BlockSpec((tm, k), lambda i, j: (i, 0)),
                pl.BlockSpec((None, k, tn), lambda i, j: (layer, 0, j))]
    args = [a, w]
    if res is not None:
        in_specs.append(pl.BlockSpec((tm, tn), lambda i, j: (i, j)))
        args.append(res)
    return pl.pallas_call(
        functools.partial(_mm_kernel, has_res=res is not None),
        grid=(m // tm, n // tn),
        in_specs=in_specs,
        out_specs=pl.BlockSpec((tm, tn), lambda i, j: (i, j)),
        out_shape=jax.ShapeDtypeStruct((m, n), out_dtype),
        compiler_params=_cparams(2),
        name="mm",
    )(*args)


def _memkv_kernel(a_ref, wk_ref, wv_ref, ok_ref, ov_ref, kb_ref, vb_ref):
    a = a_ref[...].astype(BF16)
    for w_ref, o_ref, b_ref in ((wk_ref, ok_ref, kb_ref), (wv_ref, ov_ref, vb_ref)):
        acc = _dot(a, w_ref[...])
        b_ref[...] = acc.astype(BF16)
        for h in range(MEM_H):
            o_ref[:, h, :] = acc[:, h * MEM_DH:(h + 1) * MEM_DH]


def _memkv(a, wk, wv, *, tm):
    m = a.shape[0]
    n = wk.shape[2]
    return pl.pallas_call(
        _memkv_kernel,
        grid=(DEPTH, m // tm),
        in_specs=[pl.BlockSpec((tm, D_MODEL), lambda l, i: (i, 0)),
                  pl.BlockSpec((None, D_MODEL, n), lambda l, i: (l, 0, 0)),
                  pl.BlockSpec((None, D_MODEL, n), lambda l, i: (l, 0, 0))],
        out_specs=[pl.BlockSpec((None, tm, MEM_H, MEM_DH), lambda l, i: (l, i, 0, 0)),
                   pl.BlockSpec((None, tm, MEM_H, MEM_DH), lambda l, i: (l, i, 0, 0)),
                   pl.BlockSpec((None, tm, n), lambda l, i: (l, i, 0)),
                   pl.BlockSpec((None, tm, n), lambda l, i: (l, i, 0))],
        out_shape=[jax.ShapeDtypeStruct((DEPTH, m, MEM_H, MEM_DH), F32)] * 2
        + [jax.ShapeDtypeStruct((DEPTH, m, n), BF16)] * 2,
        compiler_params=_cparams(2),
        name="memkv",
    )(a, wk, wv)


class _Ctx(dict):
    __getattr__ = dict.__getitem__
    __setattr__ = dict.__setitem__


def _seq_loop(nb, per_trip, seq_phases, n_items, item_phases, finish):
    def body(i, carry):
        seqs = [_Ctx(b=i * per_trip + u) for u in range(per_trip)]
        for phase in seq_phases:
            for s in seqs:
                phase(s)
        items = []
        for s in seqs:
            s.heads = [_Ctx(seq=s, b=s.b, k=k) for k in range(n_items)]
            items.extend(s.heads)
        for phase in item_phases:
            for it in items:
                phase(it)
        for s in seqs:
            finish(s)
        return carry

    lax.fori_loop(0, nb // per_trip, body, 0)


def _ret_prefill_kernel(q_ref, k_ref, v_ref, cos_ref, sin_ref, y_ref, s_ref, *zero_refs):
    _zero_fill(zero_refs)

    @pl.when(pl.program_id(0) == 0)
    def _():
        s_ref[...] = jnp.zeros_like(s_ref)

    c = CHUNK
    nb = q_ref.shape[0]
    cosf = cos_ref[...]
    sinf = sin_ref[...]
    row = lax.broadcasted_iota(jnp.int32, (c, c), 0)
    col = lax.broadcasted_iota(jnp.int32, (c, c), 1)
    causal = row >= col
    diff = jnp.where(causal, (row - col).astype(F32), 0.0)
    rowf = row.astype(F32)
    rowf_v = lax.broadcasted_iota(jnp.int32, (c, RET_DV), 0).astype(F32)
    decay = [jnp.where(causal, jnp.exp(lg * diff), 0.0) for lg in _LOG_G]
    q_decay = [jnp.exp(lg * (rowf_v + 1.0)) for lg in _LOG_G]
    k_decay = [jnp.exp(lg * (c - 1.0 - rowf)) * (RET_DK ** -0.5) for lg in _LOG_G]

    def load(s):
        s.s_old = [s_ref[s.b, h] for h in range(RET_H)]

    def rotate(it):
        h = it.k
        sk = slice(h * RET_DK, (h + 1) * RET_DK)
        q = q_ref[it.b, :, sk].astype(F32)
        k = k_ref[it.b, :, sk].astype(F32)
        qr = q * cosf + pltpu.roll(q, RET_DK // 2, 1) * sinf
        kr = k * cosf + pltpu.roll(k, RET_DK // 2, 1) * sinf
        it.qb = qr.astype(BF16)
        it.kb = (kr * (RET_DK ** -0.5)).astype(BF16)
        it.kt = (kr * k_decay[h]).T.astype(BF16)
        it.v = v_ref[it.b, :, h * RET_DV:(h + 1) * RET_DV]

    def scores(it):
        it.sc = (_dot_nt(it.qb, it.kb) * decay[it.k]).astype(BF16)

    def mix(it):
        h = it.k
        s_old = it.seq.s_old[h]
        it.y = _dot(it.sc, it.v) + _dot(it.qb, s_old.astype(BF16)) * q_decay[h]
        it.s_new = s_old * math.exp(_LOG_G[h] * c) + _dot(it.kt, it.v)

    def emit(it):
        y_ref[it.b, :, it.k * RET_DV:(it.k + 1) * RET_DV] = it.y.astype(y_ref.dtype)

    def finish(s):
        for it in s.heads:
            s_ref[s.b, it.k] = it.s_new

    _seq_loop(nb, 2, [load], RET_H, [rotate, scores, mix, emit], finish)


def _ret_prefill(proj, cosf, sinf, *, nb, seq, zero_shape=None):
    c = CHUNK
    p3 = proj.reshape(nb, seq, N_MAIN)
    z_specs, z_shapes = ([], []) if zero_shape is None else [[v] for v in _zero_out(zero_shape, seq // c)]
    return pl.pallas_call(
        _ret_prefill_kernel,
        grid=(seq // c,),
        in_specs=[
            pl.BlockSpec((nb, c, 512), lambda t: (0, t, C_Q // 512)),
            pl.BlockSpec((nb, c, 512), lambda t: (0, t, C_K // 512)),
            pl.BlockSpec((nb, c, 1024), lambda t: (0, t, C_V // 1024)),
            pl.BlockSpec((c, RET_DK), lambda t: (t, 0)),
            pl.BlockSpec((c, RET_DK), lambda t: (t, 0))],
        out_specs=[pl.BlockSpec((nb, c, 1024), lambda t: (0, t, 0)),
                   pl.BlockSpec((nb, RET_H, RET_DK, RET_DV), lambda t: (0, 0, 0, 0))] + z_specs,
        out_shape=[jax.ShapeDtypeStruct((nb, seq, 1024), BF16),
                   jax.ShapeDtypeStruct((nb, RET_H, RET_DK, RET_DV), F32)] + z_shapes,
        compiler_params=_cparams(1),
        name="ret_prefill",
    )(p3, p3, p3, cosf, sinf)


def _shift_matrix(first_chunk):
    c = CHUNK
    r = lax.broadcasted_iota(jnp.int32, (3 * c, 2 * c), 0)
    q = lax.broadcasted_iota(jnp.int32, (3 * c, 2 * c), 1)
    k = lax.shift_right_logical(r, 7)
    t = jnp.bitwise_and(r, c - 1)
    hit = (q == c + t - (CONV_K - 1) + k) & (q >= jnp.where(first_chunk, c, 0))
    return jnp.where(hit, 1.0, 0.0).astype(BF16)


def _conv_silu(shift, x_prev, x_cur, cw, cb):
    c = CHUNK
    p = _dot(shift, jnp.concatenate([x_prev, x_cur], axis=0))
    acc = cb
    for j in range(CONV_K - 1):
        acc = acc + p[j * c:(j + 1) * c] * cw[j:j + 1, :]
    acc = acc + x_cur.astype(F32) * cw[CONV_K - 1:CONV_K, :]
    return _silu(acc)


def _ml_prefill_kernel(u_ref, up_ref, v_ref, sm_ref, cw_ref, cb_ref, wq_ref, wk_ref, gb_ref,
                       y_ref, c_ref, n_ref, m_ref, cv_ref, tail):
    c = CHUNK
    nb = u_ref.shape[0]
    first = pl.program_id(0) == 0

    @pl.when(first)
    def _():
        c_ref[...] = jnp.zeros_like(c_ref)
        n_ref[...] = jnp.zeros_like(n_ref)
        m_ref[...] = jnp.zeros_like(m_ref)

    row = lax.broadcasted_iota(jnp.int32, (c, c), 0)
    col = lax.broadcasted_iota(jnp.int32, (c, c), 1)
    causal = row >= col
    tril_b = jnp.where(causal, 1.0, 0.0).astype(BF16)
    lane_m = lax.broadcasted_iota(jnp.int32, (1, 128), 1)
    shift = _shift_matrix(first)

    def load(s):
        s.c_old = [c_ref[s.b, h] for h in range(ML_H)]
        s.n_old = n_ref[s.b]
        s.m_row = m_ref[s.b]

    def gates_phase(s):
        tail[...] = u_ref[s.b, c - 16:c, :].astype(F32)
        cv_ref[s.b] = tail[16 - (CONV_K - 1):16, :]
        s.gates = sm_ref[s.b] + gb_ref[...]
        s.b_all = _cumsum_rows(_log_sigmoid(s.gates), tril_b)
        s.b_all_t = s.b_all.T
        s.gates_t = s.gates.T

    def conv_phase(it):
        sl = slice(it.k * ML_DH, (it.k + 1) * ML_DH)
        it.uc = _conv_silu(shift, up_ref[it.b, :, sl], u_ref[it.b, :, sl], cw_ref[:, sl], cb_ref[:, sl]).astype(BF16)
        it.v = v_ref[it.b, :, sl]

    def weights_phase(it):
        h, s = it.k, it.seq
        b_col = s.b_all[:, L_FG + h:L_FG + h + 1]
        b_row = s.b_all_t[L_FG + h:L_FG + h + 1, :]
        it_col = s.gates[:, L_IG + h:L_IG + h + 1]
        it_row = s.gates_t[L_IG + h:L_IG + h + 1, :]
        m_prev = s.m_row[:, h:h + 1]
        logw = jnp.where(causal, b_col - b_row + it_row, -jnp.inf)
        inter = b_col + m_prev
        it.m_t = jnp.maximum(inter, jnp.max(logw, axis=-1, keepdims=True))
        it.dmat = jnp.exp(logw - it.m_t)
        it.w_int = jnp.exp(inter - it.m_t)
        b_end = b_col[c - 1:c, :]
        logw_s = b_end - b_col + it_col
        it.m_new = jnp.maximum(b_end + m_prev, jnp.max(logw_s, axis=0, keepdims=True))
        it.w_s = jnp.exp(logw_s - it.m_new)
        it.w_prev = jnp.exp(b_end + m_prev - it.m_new)

    def qk_phase(it):
        it.qm = _dot(it.uc, wq_ref[it.k])
        it.km = _dot(it.uc, wk_ref[it.k]) * (ML_DH ** -0.5)
        it.qb = it.qm.astype(BF16)
        it.kb = it.km.astype(BF16)
        it.kw = it.km * it.w_s
        it.kwt = it.kw.T.astype(BF16)

    def scores_phase(it):
        it.sc = _dot_nt(it.qb, it.kb) * it.dmat

    def mix_phase(it):
        c_old = it.seq.c_old[it.k]
        n_old = it.seq.n_old[it.k:it.k + 1, :]
        it.num = _dot(it.sc.astype(BF16), it.v) + _dot(it.qb, c_old.astype(BF16)) * it.w_int
        it.c_new = c_old * it.w_prev + _dot(it.kwt, it.v)
        it.n_new = n_old * it.w_prev + jnp.sum(it.kw, axis=0, keepdims=True)
        den = jnp.sum(it.sc, axis=-1, keepdims=True) + jnp.sum(it.qm * n_old, axis=-1, keepdims=True) * it.w_int
        it.den = jnp.maximum(jnp.abs(den), jnp.exp(-it.m_t))

    def emit_phase(it):
        y_ref[it.b, :, it.k * ML_DH:(it.k + 1) * ML_DH] = (it.num / it.den).astype(y_ref.dtype)

    def finish(s):
        m_row_new = s.m_row
        for it in s.heads:
            c_ref[s.b, it.k] = it.c_new
            n_ref[s.b, it.k:it.k + 1, :] = it.n_new
            m_row_new = jnp.where(lane_m == it.k, it.m_new, m_row_new)
        m_ref[s.b] = m_row_new

    _seq_loop(nb, 1, [load, gates_phase], ML_H,
              [conv_phase, weights_phase, qk_phase, scores_phase, mix_phase, emit_phase], finish)


def _ml_prefill(proj, small, cw, cb, wq, wk, gb, *, layer, nb, seq):
    c = CHUNK
    p3 = proj.reshape(nb, seq, N_MAIN)
    s3 = small.reshape(nb, seq, N_SMALL)
    w = ML_H * ML_DH
    return pl.pallas_call(
        _ml_prefill_kernel,
        grid=(seq // c,),
        in_specs=[
            pl.BlockSpec((nb, c, w), lambda t: (0, t, C_U // w)),
            pl.BlockSpec((nb, c, w), lambda t: (0, jnp.maximum(t - 1, 0), C_U // w)),
            pl.BlockSpec((nb, c, w), lambda t: (0, t, C_VM // w)),
            pl.BlockSpec((nb, c, N_SMALL), lambda t: (0, t, 0)),
            _lsel(layer, CONV_K, w),
            _lsel(layer, 1, w),
            _lsel(layer, ML_H, ML_DH, ML_DH),
            _lsel(layer, ML_H, ML_DH, ML_DH),
            _lsel(layer, 1, N_SMALL)],
        out_specs=[pl.BlockSpec((nb, c, w), lambda t: (0, t, 0)),
                   pl.BlockSpec((nb, ML_H, ML_DH, ML_DH), lambda t: (0, 0, 0, 0)),
                   pl.BlockSpec((nb, ML_H, ML_DH), lambda t: (0, 0, 0)),
                   pl.BlockSpec((nb, 1, 128), lambda t: (0, 0, 0)),
                   pl.BlockSpec((nb, CONV_K - 1, w), lambda t: (0, 0, 0))],
        out_shape=[jax.ShapeDtypeStruct((nb, seq, w), BF16),
                   jax.ShapeDtypeStruct((nb, ML_H, ML_DH, ML_DH), F32),
                   jax.ShapeDtypeStruct((nb, ML_H, ML_DH), F32),
                   jax.ShapeDtypeStruct((nb, 1, 128), F32),
                   jax.ShapeDtypeStruct((nb, CONV_K - 1, w), F32)],
        scratch_shapes=[pltpu.VMEM((16, w), F32)],
        compiler_params=_cparams(1),
        name="ml_prefill",
    )(p3, p3, p3, s3, cw, cb, wq, wk, gb)


def _block_bcast(tile, lanes):
    c = tile.shape[0]
    lane = lax.broadcasted_iota(jnp.int32, (c, 128), 1)
    cols = [jnp.broadcast_to(tile[:, l:l + 1], (c, 128)) for l in lanes]
    left = jnp.where(lane < SSM_P, cols[0], cols[1])
    right = jnp.where(lane < SSM_P, cols[2], cols[3])
    return jnp.concatenate([left, right], axis=1)


def _ssd_prefill_kernel(xs_ref, xsp_ref, bc_ref, bcp_ref, sm_ref, cw_ref, cb_ref, dtb_ref, alog_ref,
                        dv_ref, y_ref, s_ref, cv_ref, *rest):
    *zero_refs, tail = rest
    _zero_fill(zero_refs)
    c = CHUNK
    nb = xs_ref.shape[0]
    wx = SSM_H * SSM_P
    gw = SSM_R * SSM_P
    first = pl.program_id(0) == 0

    @pl.when(first)
    def _():
        s_ref[...] = jnp.zeros_like(s_ref)

    row = lax.broadcasted_iota(jnp.int32, (c, c), 0)
    col = lax.broadcasted_iota(jnp.int32, (c, c), 1)
    causal = row >= col
    tril_b = jnp.where(causal, 1.0, 0.0).astype(BF16)
    lane_blk = lax.shift_right_logical(lax.broadcasted_iota(jnp.int32, (c, gw), 1), 6)
    head_mask = [jnp.where(lane_blk == r, 1.0, 0.0).astype(BF16) for r in range(SSM_R)]
    a_row = -jnp.exp(alog_ref[...])
    shift = _shift_matrix(first)

    def conv(cur_ref, prv_ref, b, lanes, w_off):
        wl = slice(w_off + lanes.start, w_off + lanes.stop)
        return _conv_silu(shift, prv_ref[b, :, lanes], cur_ref[b, :, lanes], cw_ref[:, wl], cb_ref[:, wl])

    def load(s):
        s.s_old = [s_ref[s.b, g * SSM_R:(g + 1) * SSM_R].reshape(gw, SSM_N) for g in range(SSM_G)]

    def dt_phase(s):
        b = s.b
        tail[:, 0:wx] = xs_ref[b, c - 16:c, :].astype(F32)
        tail[:, wx:2 * wx] = bc_ref[b, c - 16:c, :].astype(F32)
        cv_ref[b] = tail[16 - (CONV_K - 1):16, :]
        s.delta = _softplus(sm_ref[b] + dtb_ref[...])
        s.cum = _cumsum_rows(s.delta * a_row, tril_b)
        s.cum_t = s.cum.T
        s.delta_t = s.delta.T
        s.ecum = jnp.exp(s.cum)
        cum_end = s.cum[c - 1:c, :]
        s.w_state = jnp.exp(cum_end - s.cum) * s.delta
        s.dec_row = jnp.exp(cum_end)

    def conv_phase(it):
        g = it.k
        it.lanes = [L_DT + g * SSM_R + r for r in range(SSM_R)]
        it.xg = conv(xs_ref, xsp_ref, it.b, slice(g * gw, (g + 1) * gw), 0)
        it.bg = conv(bc_ref, bcp_ref, it.b, slice(g * SSM_N, (g + 1) * SSM_N), wx).astype(BF16)
        it.cg = conv(bc_ref, bcp_ref, it.b, slice((SSM_G + g) * SSM_N, (SSM_G + g + 1) * SSM_N), wx).astype(BF16)

    def seg_phase(it):
        s = it.seq
        it.seg = [jnp.exp(jnp.where(causal, s.cum[:, ln:ln + 1] - s.cum_t[ln:ln + 1, :], -jnp.inf))
                  * s.delta_t[ln:ln + 1, :] for ln in it.lanes]
        xgb = it.xg.astype(BF16)
        it.x_stack = jnp.concatenate([xgb * head_mask[r] for r in range(SSM_R)], axis=0)
        it.xwt = (it.xg * _block_bcast(s.w_state, it.lanes)).T.astype(BF16)

    def cb_phase(it):
        it.cb_mat = _dot_nt(it.cg, it.bg)

    def mix_phase(it):
        s = it.seq
        s_old = s.s_old[it.k]
        m_wide = jnp.concatenate([(it.cb_mat * sg).astype(BF16) for sg in it.seg], axis=1)
        it.y = (_dot(m_wide, it.x_stack)
                + _dot_nt(it.cg, s_old.astype(BF16)) * _block_bcast(s.ecum, it.lanes))
        upd = _dot(it.xwt, it.bg)
        it.s_new = [s_old[r * SSM_P:(r + 1) * SSM_P] * s.dec_row[:, ln:ln + 1] + upd[r * SSM_P:(r + 1) * SSM_P]
                    for r, ln in enumerate(it.lanes)]

    def emit_phase(it):
        gs = slice(it.k * gw, (it.k + 1) * gw)
        y_ref[it.b, :, gs] = (it.y + dv_ref[:, gs] * it.xg).astype(y_ref.dtype)

    def finish(s):
        for it in s.heads:
            for r in range(SSM_R):
                s_ref[s.b, it.k * SSM_R + r] = it.s_new[r]

    _seq_loop(nb, 2, [load, dt_phase], SSM_G, [conv_phase, seg_phase, cb_phase, mix_phase, emit_phase], finish)


def _ssd_prefill(proj, small, cw, cb, dtb, alog, dvec, *, layer, nb, seq, zero_shape=None):
    c = CHUNK
    wx = SSM_H * SSM_P
    p3 = proj.reshape(nb, seq, N_MAIN)
    s3 = small.reshape(nb, seq, N_SMALL)
    z_specs, z_shapes = ([], []) if zero_shape is None else [[v] for v in _zero_out(zero_shape, seq // c)]
    return pl.pallas_call(
        _ssd_prefill_kernel,
        grid=(seq // c,),
        in_specs=[
            pl.BlockSpec((nb, c, wx), lambda t: (0, t, C_XS // wx)),
            pl.BlockSpec((nb, c, wx), lambda t: (0, jnp.maximum(t - 1, 0), C_XS // wx)),
            pl.BlockSpec((nb, c, wx), lambda t: (0, t, C_B // wx)),
            pl.BlockSpec((nb, c, wx), lambda t: (0, jnp.maximum(t - 1, 0), C_B // wx)),
            pl.BlockSpec((nb, c, N_SMALL), lambda t: (0, t, 0)),
            _lsel(layer, CONV_K, 2 * wx),
            _lsel(layer, 1, 2 * wx),
            _lsel(layer, 1, N_SMALL),
            _lsel(layer, 1, N_SMALL),
            _lsel(layer, 1, wx)],
        out_specs=[pl.BlockSpec((nb, c, wx), lambda t: (0, t, 0)),
                   pl.BlockSpec((nb, SSM_H, SSM_P, SSM_N), lambda t: (0, 0, 0, 0)),
                   pl.BlockSpec((nb, CONV_K - 1, 2 * wx), lambda t: (0, 0, 0))] + z_specs,
        out_shape=[jax.ShapeDtypeStruct((nb, seq, wx), BF16),
                   jax.ShapeDtypeStruct((nb, SSM_H, SSM_P, SSM_N), F32),
                   jax.ShapeDtypeStruct((nb, CONV_K - 1, 2 * wx), F32)] + z_shapes,
        scratch_shapes=[pltpu.VMEM((16, 2 * wx), F32)],
        compiler_params=_cparams(1),
        name="ssd_prefill",
    )(p3, p3, p3, p3, s3, cw, cb, dtb, alog, dvec)


MERGE_SUB = 256


def _head_norm_gate(y_ref, g_ref, n_ref, rs, n_heads, center, gate_fn, gate_first):
    wh = y_ref.shape[1] // n_heads
    parts = []
    for h in range(n_heads):
        hs = slice(h * wh, (h + 1) * wh)
        y = y_ref[rs, hs].astype(F32)
        gate = gate_fn(g_ref[rs, hs].astype(F32))
        y = _groupnorm(y * gate, n_ref[:, hs], center) if gate_first else _groupnorm(y, n_ref[:, hs], center) * gate
        parts.append(y.astype(BF16))
    return jnp.concatenate(parts, axis=1)


def _merge_kernel(yr_ref, ym_ref, ys_ref, gr_ref, gm_ref, gs_ref, x_ref, wr_ref, wm_ref, ws_ref,
                  wo_ref, nw_ref, wq_ref, *rest, fused_norm):
    if fused_norm:
        g_ref, om_ref, z_ref, nr_ref, nm_ref, ns_ref, xo_ref, qo_ref, *zero_refs = rest
    else:
        xo_ref, qo_ref, *zero_refs = rest
    _zero_fill(zero_refs)
    tm = x_ref.shape[0]
    n_sub = max(1, tm // MERGE_SUB)
    subs = [slice(i * (tm // n_sub), (i + 1) * (tm // n_sub)) for i in range(n_sub)]
    branches = ((yr_ref, gr_ref, wr_ref), (ym_ref, gm_ref, wm_ref), (ys_ref, gs_ref, ws_ref))
    if fused_norm:
        def operands(rs):
            return (_head_norm_gate(yr_ref, g_ref, nr_ref, rs, RET_H, True, _silu, False),
                    _head_norm_gate(ym_ref, om_ref, nm_ref, rs, ML_H, True, _sigmoid, False),
                    _head_norm_gate(ys_ref, z_ref, ns_ref, rs, SSM_G, False, _silu, True))
    else:
        def operands(rs):
            return tuple(y_ref[rs, :].astype(BF16) for y_ref, _, _ in branches)
    proj = [[_dot(y, w_ref[...]) for y, (_, _, w_ref) in zip(operands(rs), branches)] for rs in subs]
    merged = []
    for rs, pr in zip(subs, proj):
        gated = [_sigmoid(g_ref[rs, :].astype(F32)) * p for (_, g_ref, _), p in zip(branches, pr)]
        merged.append((gated[0] + gated[1] + gated[2]).astype(BF16))
    xns = [x_ref[rs, :] + _dot(m, wo_ref[...]) for rs, m in zip(subs, merged)]
    hqs = []
    for rs, xn in zip(subs, xns):
        xo_ref[rs, :] = xn
        hqs.append(_rmsnorm(xn, nw_ref[...]).astype(BF16))
    for rs, hq in zip(subs, hqs):
        qo_ref[rs, :] = _dot(hq, wq_ref[...]).astype(qo_ref.dtype)


def _merge(yr, ym, ys, proj, x, wr, wm, ws, wo, nw, wq, norms=None, *, layer, tm, q_dtype, zero_shape=None):
    m = x.shape[0]
    d = D_MODEL
    row = lambda i: (i, 0)
    z_specs, z_shapes = ([], []) if zero_shape is None else [[v] for v in _zero_out(zero_shape, m // tm)]
    wspec = _lsel(layer, d, d, pipeline_mode=pl.Buffered(1))
    fused = norms is not None
    extra_specs, extra_args = [], []
    if fused:
        extra_specs = [pl.BlockSpec((tm, d), lambda i: (i, C_G // d)),
                       pl.BlockSpec((tm, d), lambda i: (i, C_OM // d)),
                       pl.BlockSpec((tm, d), lambda i: (i, C_Z // d))] + [_lsel(layer, 1, d)] * 3
        extra_args = [proj, proj, proj, *norms]
    return pl.pallas_call(
        functools.partial(_merge_kernel, fused_norm=fused),
        grid=(m // tm,),
        in_specs=[pl.BlockSpec((tm, d), row), pl.BlockSpec((tm, d), row), pl.BlockSpec((tm, d), row),
                  pl.BlockSpec((tm, d), lambda i: (i, C_GR // d)),
                  pl.BlockSpec((tm, d), lambda i: (i, C_GM // d)),
                  pl.BlockSpec((tm, d), lambda i: (i, C_GS // d)),
                  pl.BlockSpec((tm, d), row),
                  wspec, wspec, wspec, wspec,
                  _lsel(layer, 1, d),
                  wspec] + extra_specs,
        out_specs=[pl.BlockSpec((tm, d), row), pl.BlockSpec((tm, d), row)] + z_specs,
        out_shape=[jax.ShapeDtypeStruct((m, d), F32), jax.ShapeDtypeStruct((m, d), q_dtype)] + z_shapes,
        compiler_params=_cparams(1),
        name="merge",
    )(yr, ym, ys, proj, proj, proj, x, wr, wm, ws, wo, nw, wq, *extra_args)


def _attn_prefill_kernel(q_ref, k_ref, v_ref, x_ref, wo_ref, o_ref):
    heads = [slice(h * MEM_DH, (h + 1) * MEM_DH) for h in range(MEM_H)]
    scores = [_dot_nt(q_ref[0, :, sl], k_ref[0, :, sl].astype(BF16)) * (MEM_DH ** -0.5) for sl in heads]
    probs = []
    for s in scores:
        p = jnp.exp(s - jnp.max(s, axis=-1, keepdims=True))
        probs.append((p / jnp.sum(p, axis=-1, keepdims=True)).astype(BF16))
    outs = [_dot(p, v_ref[0, :, sl].astype(BF16)).astype(BF16) for p, sl in zip(probs, heads)]
    o_all = jnp.concatenate(outs, axis=1)
    o_ref[0] = x_ref[0] + _dot(o_all, wo_ref[...])


def _attn_prefill(q, mk, mv, x, wo, *, layer, nb, seq, tq):
    d = D_MODEL
    q3 = q.reshape(nb, seq, d)
    x3 = x.reshape(nb, seq, d)
    out = pl.pallas_call(
        _attn_prefill_kernel,
        grid=(nb, seq // tq),
        in_specs=[pl.BlockSpec((1, tq, d), lambda b, t: (b, t, 0)),
                  pl.BlockSpec((1, MEM_LEN, d), lambda b, t: (layer * nb + b, 0, 0)),
                  pl.BlockSpec((1, MEM_LEN, d), lambda b, t: (layer * nb + b, 0, 0)),
                  pl.BlockSpec((1, tq, d), lambda b, t: (b, t, 0)),
                  _lsel(layer, d, d)],
        out_specs=pl.BlockSpec((1, tq, d), lambda b, t: (b, t, 0)),
        out_shape=jax.ShapeDtypeStruct((nb, seq, d), F32),
        compiler_params=_cparams(2),
        name="attn_prefill",
    )(q3, mk, mv, x3, wo)
    return out.reshape(nb * seq, d)


def _attn_decode_kernel(q_ref, k_ref, v_ref, o_ref):
    for j in range(q_ref.shape[0]):
        qs = q_ref[j] * (MEM_DH ** -0.5)
        s = jnp.sum(k_ref[j] * qs[None], axis=-1, keepdims=True)
        p = jnp.exp(s - jnp.max(s, axis=0, keepdims=True))
        o_ref[j] = jnp.sum(p * v_ref[j], axis=0) / jnp.sum(p, axis=0)


def _attn_decode(q, ck, cv, *, layer, bb):
    nb = q.shape[0]
    nblk = nb // bb
    blk = (bb, MEM_LEN, MEM_H, MEM_DH)
    return pl.pallas_call(
        _attn_decode_kernel,
        grid=(nblk,),
        in_specs=[pl.BlockSpec((bb, MEM_H, MEM_DH), lambda i: (i, 0, 0)),
                  pl.BlockSpec(blk, lambda i: (layer * nblk + i, 0, 0, 0)),
                  pl.BlockSpec(blk, lambda i: (layer * nblk + i, 0, 0, 0))],
        out_specs=pl.BlockSpec((bb, MEM_H, MEM_DH), lambda i: (i, 0, 0)),
        out_shape=jax.ShapeDtypeStruct((nb, MEM_H, MEM_DH), F32),
        compiler_params=_cparams(1),
        name="attn_decode",
    )(q, ck, cv)


MLP_SUB = 1024


def _mlp_kernel(x_ref, nw_ref, w1_ref, w2_ref, nf_ref, o_ref, xn_ref, acc_ref, *, final_norm):
    j = pl.program_id(1)

    @pl.when(j == 0)
    def _():
        xn_ref[...] = _rmsnorm(x_ref[...], nw_ref[...]).astype(BF16)
        acc_ref[...] = jnp.zeros_like(acc_ref)

    subs = [slice(s * MLP_SUB, (s + 1) * MLP_SUB) for s in range(w1_ref.shape[1] // MLP_SUB)]
    hids = [_dot(xn_ref[...], w1_ref[:, sl]) for sl in subs]
    hids = [jnp.square(jnp.maximum(hid, 0.0)).astype(BF16) for hid in hids]
    acc = acc_ref[...]
    for hid, sl in zip(hids, subs):
        acc = acc + _dot(hid, w2_ref[sl, :])
    acc_ref[...] = acc

    @pl.when(j == pl.num_programs(1) - 1)
    def _():
        y = x_ref[...] + acc_ref[...]
        if final_norm:
            y = _rmsnorm(y, nf_ref[...])
        o_ref[...] = y


def _mlp(x, nw, w1, w2, nf, *, layer, tm, tf, final_norm):
    m = x.shape[0]
    d = D_MODEL
    return pl.pallas_call(
        functools.partial(_mlp_kernel, final_norm=final_norm),
        grid=(m // tm, D_FF // tf),
        in_specs=[pl.BlockSpec((tm, d), lambda i, j: (i, 0)),
                  _lsel(layer, 1, d),
                  pl.BlockSpec((None, d, tf), lambda i, j: (layer, 0, j)),
                  pl.BlockSpec((None, tf, d), lambda i, j: (layer, j, 0)),
                  pl.BlockSpec((1, d), lambda i, j: (0, 0))],
        out_specs=pl.BlockSpec((tm, d), lambda i, j: (i, 0)),
        out_shape=jax.ShapeDtypeStruct((m, d), F32),
        scratch_shapes=[pltpu.VMEM((tm, d), BF16), pltpu.VMEM((tm, d), F32)],
        compiler_params=_cparams(2),
        name="mlp",
    )(x, nw, w1, w2, nf)


DEC_BB = 8


def _row_mask(x, j):
    rows = lax.broadcasted_iota(jnp.int32, x.shape, 0)
    return jnp.where(rows == j, x, 0.0).astype(BF16)


def _blockdiag_rows(v):
    n = v.shape[0]
    tiled = jnp.concatenate([v] * n, axis=1)
    rows = lax.broadcasted_iota(jnp.int32, tiled.shape, 0)
    blk = lax.shift_right_logical(lax.broadcasted_iota(jnp.int32, tiled.shape, 1), 8)
    return _pad_rows(jnp.where(rows == blk, tiled, 0.0), 128).astype(BF16)


def _ret_decode_kernel(q_ref, k_ref, v_ref, g_ref, cos_ref, sin_ref, nw_ref, s_ref, y_ref, so_ref, ybuf):
    cosf = cos_ref[...]
    sinf = sin_ref[...]
    for h in range(RET_H):
        sk = slice(h * RET_DK, (h + 1) * RET_DK)
        sv = slice(h * RET_DV, (h + 1) * RET_DV)
        g_dec = math.exp(_LOG_G[h])
        q = q_ref[:, sk]
        k = k_ref[:, sk]
        qr = q * cosf + pltpu.roll(q, RET_DK // 2, 1) * sinf
        kr = (k * cosf + pltpu.roll(k, RET_DK // 2, 1) * sinf) * (RET_DK ** -0.5)
        k_t = _pad_rows_t(kr).astype(BF16)
        q16 = _pad_rows(qr, 16).astype(BF16)
        upd = _dot(k_t, _blockdiag_rows(v_ref[:, sv]))
        s_wide = []
        for j in range(DEC_BB):
            s_new = s_ref[j, h] * g_dec + upd[:, j * RET_DV:(j + 1) * RET_DV]
            so_ref[j, h] = s_new
            s_wide.append(s_new.astype(BF16))
        y_all = _dot(q16, jnp.concatenate(s_wide, axis=1))
        for j in range(DEC_BB):
            ybuf[j:j + 1, sv] = y_all[j:j + 1, j * RET_DV:(j + 1) * RET_DV]
        y_ref[:, sv] = _groupnorm(ybuf[:, sv], nw_ref[:, sv], True) * _silu(g_ref[:, sv])


def _ret_decode(proj, cosf, sinf, nw, state, prev, *, layer):
    nb = proj.shape[0]
    bb = DEC_BB
    nblk = nb // bb
    kern, steps, ib, ob, al_args, al_specs, al_map = _carried_call(_ret_decode_kernel, nblk, layer, prev)
    sblk = (bb, RET_H, RET_DK, RET_DV)
    return pl.pallas_call(
        kern,
        grid=(steps,),
        in_specs=al_specs + [
            pl.BlockSpec((bb, 512), lambda i: (ib(i), C_Q // 512)),
            pl.BlockSpec((bb, 512), lambda i: (ib(i), C_K // 512)),
            pl.BlockSpec((bb, 1024), lambda i: (ib(i), C_V // 1024)),
            pl.BlockSpec((bb, 1024), lambda i: (ib(i), C_G // 1024)),
            pl.BlockSpec((1, RET_DK), lambda i: (0, 0)),
            pl.BlockSpec((1, RET_DK), lambda i: (0, 0)),
            _lsel(layer, 1, 1024),
            pl.BlockSpec(sblk, lambda i: (layer * nblk + ib(i), 0, 0, 0))],
        out_specs=[pl.BlockSpec((bb, 1024), lambda i: (ib(i), 0)),
                   pl.BlockSpec(sblk, lambda i: (ob(i), 0, 0, 0))],
        out_shape=[jax.ShapeDtypeStruct((nb, RET_H * RET_DV), F32),
                   jax.ShapeDtypeStruct((DEPTH * nb, RET_H, RET_DK, RET_DV), F32)],
        scratch_shapes=[pltpu.VMEM((bb, RET_H * RET_DV), F32)],
        input_output_aliases=al_map,
        compiler_params=_cparams(1),
        name="ret_decode",
    )(*al_args, proj, proj, proj, proj, cosf, sinf, nw, state)


def _conv_step(x, buf_ref, cw_ref, cb_ref, bo_ref):
    acc = cb_ref[...]
    for j in range(CONV_K - 1):
        acc = acc + buf_ref[j] * cw_ref[j:j + 1, :]
    acc = acc + x * cw_ref[CONV_K - 1:CONV_K, :]
    bo_ref[0] = buf_ref[1]
    bo_ref[1] = buf_ref[2]
    bo_ref[2] = x
    return _silu(acc)


def _ml_decode_kernel(u_ref, v_ref, o_ref, sm_ref, buf_ref, cw_ref, cb_ref, wq_ref, wk_ref, gb_ref, nw_ref,
                      c_ref, n_ref, m_ref,
                      y_ref, co_ref, no_ref, mo_ref, bo_ref, ybuf):
    uc = _conv_step(u_ref[...], buf_ref, cw_ref, cb_ref, bo_ref).astype(BF16)
    gates = sm_ref[...] + gb_ref[...]
    lane4 = lax.broadcasted_iota(jnp.int32, m_ref.shape, 1)
    m_all = m_ref[...]
    m_out = m_all
    for h in range(ML_H):
        sl = slice(h * ML_DH, (h + 1) * ML_DH)
        qm = _dot(uc[:, sl], wq_ref[h])
        km = _dot(uc[:, sl], wk_ref[h]) * (ML_DH ** -0.5)
        i_pre = gates[:, L_IG + h:L_IG + h + 1]
        log_f = _log_sigmoid(gates[:, L_FG + h:L_FG + h + 1])
        m_prev = m_all[:, h:h + 1]
        inter = log_f + m_prev
        m_new = jnp.maximum(inter, i_pre)
        w_s = jnp.exp(i_pre - m_new)
        w_prev = jnp.exp(inter - m_new)
        kw = km * w_s
        n_new = n_ref[:, sl] * w_prev + kw
        no_ref[:, sl] = n_new
        m_out = jnp.where(lane4 == h, m_new, m_out)

        k_t = _pad_rows_t(kw).astype(BF16)
        q16 = _pad_rows(qm, 16).astype(BF16)
        upd = _dot(k_t, _blockdiag_rows(v_ref[:, sl]))
        c_wide = []
        for j in range(DEC_BB):
            c_new = c_ref[j, h] * w_prev[j:j + 1, :] + upd[:, j * ML_DH:(j + 1) * ML_DH]
            co_ref[j, h] = c_new
            c_wide.append(c_new.astype(BF16))
        y_all = _dot(q16, jnp.concatenate(c_wide, axis=1))
        for j in range(DEC_BB):
            ybuf[j:j + 1, sl] = y_all[j:j + 1, j * ML_DH:(j + 1) * ML_DH]
        den = jnp.sum(qm * n_new, axis=-1, keepdims=True)
        den = jnp.maximum(jnp.abs(den), jnp.exp(-m_new))
        y_ref[:, sl] = _groupnorm(ybuf[:, sl] / den, nw_ref[:, sl], True) * _sigmoid(o_ref[:, sl])
    mo_ref[...] = m_out


def _ml_decode(proj, small, buf, cw, cb, wq, wk, gb, nw, c_state, n_state, m_state, prev, *, layer):
    nb = proj.shape[0]
    bb = DEC_BB
    nblk = nb // bb
    dh = ML_DH
    w = ML_H * ML_DH
    kern, steps, ib, ob, al_args, al_specs, al_map = _carried_call(_ml_decode_kernel, nblk, layer, prev)
    cblk = (bb, ML_H, dh, dh)
    return pl.pallas_call(
        kern,
        grid=(steps,),
        in_specs=al_specs + [
            pl.BlockSpec((bb, w), lambda i: (ib(i), C_U // w)),
            pl.BlockSpec((bb, w), lambda i: (ib(i), C_VM // w)),
            pl.BlockSpec((bb, w), lambda i: (ib(i), C_OM // w)),
            pl.BlockSpec((bb, N_SMALL), lambda i: (ib(i), 0)),
            pl.BlockSpec((None, CONV_K - 1, bb, w), lambda i: (layer, 0, ib(i), 0)),
            _lsel(layer, CONV_K, w),
            _lsel(layer, 1, w),
            _lsel(layer, ML_H, dh, dh),
            _lsel(layer, ML_H, dh, dh),
            _lsel(layer, 1, N_SMALL),
            _lsel(layer, 1, w),
            pl.BlockSpec(cblk, lambda i: (layer * nblk + ib(i), 0, 0, 0)),
            pl.BlockSpec((bb, w), lambda i: (layer * nblk + ib(i), 0)),
            pl.BlockSpec((bb, ML_H), lambda i: (layer * nblk + ib(i), 0))],
        out_specs=[pl.BlockSpec((bb, w), lambda i: (ib(i), 0)),
                   pl.BlockSpec(cblk, lambda i: (ob(i), 0, 0, 0)),
                   pl.BlockSpec((bb, w), lambda i: (ib(i), 0)),
                   pl.BlockSpec((bb, ML_H), lambda i: (ib(i), 0)),
                   pl.BlockSpec((CONV_K - 1, bb, w), lambda i: (0, ib(i), 0))],
        out_shape=[jax.ShapeDtypeStruct((nb, w), F32),
                   jax.ShapeDtypeStruct((DEPTH * nb, ML_H, dh, dh), F32),
                   jax.ShapeDtypeStruct((nb, w), F32),
                   jax.ShapeDtypeStruct((nb, ML_H), F32),
                   jax.ShapeDtypeStruct((CONV_K - 1, nb, w), F32)],
        scratch_shapes=[pltpu.VMEM((bb, w), F32)],
        input_output_aliases=al_map,
        compiler_params=_cparams(1),
        name="ml_decode",
    )(*al_args, proj, proj, proj, small, buf, cw, cb, wq, wk, gb, nw, c_state, n_state, m_state)


def _ssd_decode_kernel(z_ref, xs_ref, bc_ref, sm_ref, buf_ref, cw_ref, cb_ref, dtb_ref, alog_ref, dv_ref, nw_ref,
                       s_ref, y_ref, so_ref, bo_ref, ybuf):
    wx = SSM_H * SSM_P
    gw = SSM_R * SSM_P
    x_in = jnp.concatenate([xs_ref[...], bc_ref[...]], axis=1)
    xc = _conv_step(x_in, buf_ref, cw_ref, cb_ref, bo_ref)
    xs = xc[:, 0:wx]

    delta = _softplus(sm_ref[...] + dtb_ref[...])
    d_a = jnp.exp(delta * (-jnp.exp(alog_ref[...])))
    lane_blk = lax.shift_right_logical(lax.broadcasted_iota(jnp.int32, (DEC_BB, wx), 1), 6)
    dt_full = jnp.zeros((DEC_BB, wx), F32)
    for hh in range(SSM_H):
        dt_full = jnp.where(lane_blk == hh, delta[:, L_DT + hh:L_DT + hh + 1], dt_full)
    x_t = _pad_rows_t(xs * dt_full).astype(BF16)

    for g in range(SSM_G):
        gs = slice(g * gw, (g + 1) * gw)
        b_pad = _pad_rows(xc[:, wx + g * SSM_N:wx + (g + 1) * SSM_N], 128)
        c16 = _pad_rows(xc[:, wx + (SSM_G + g) * SSM_N:wx + (SSM_G + g + 1) * SSM_N], 16).astype(BF16)
        for j in range(DEC_BB):
            upd = _dot(x_t[gs, :], _row_mask(b_pad, j))
            parts = []
            for r in range(SSM_R):
                hh = g * SSM_R + r
                s_new = (s_ref[j, hh] * d_a[j:j + 1, L_DT + hh:L_DT + hh + 1]
                         + upd[r * SSM_P:(r + 1) * SSM_P])
                so_ref[j, hh] = s_new
                parts.append(s_new.astype(BF16))
            ybuf[j:j + 1, gs] = _dot_nt(c16, jnp.concatenate(parts, axis=0))[j:j + 1, :]
        y = ybuf[:, gs] + dv_ref[:, gs] * xs[:, gs]
        y_ref[:, gs] = _groupnorm(y * _silu(z_ref[:, gs]), nw_ref[:, gs], False)


def _ssd_decode(proj, small, buf, cw, cb, dtb, alog, dvec, nw, state, prev, *, layer):
    nb = proj.shape[0]
    bb = DEC_BB
    nblk = nb // bb
    wx = SSM_H * SSM_P
    kern, steps, ib, ob, al_args, al_specs, al_map = _carried_call(_ssd_decode_kernel, nblk, layer, prev)
    sblk = (bb, SSM_H, SSM_P, SSM_N)
    return pl.pallas_call(
        kern,
        grid=(steps,),
        in_specs=al_specs + [
            pl.BlockSpec((bb, wx), lambda i: (ib(i), C_Z // wx)),
            pl.BlockSpec((bb, wx), lambda i: (ib(i), C_XS // wx)),
            pl.BlockSpec((bb, wx), lambda i: (ib(i), C_B // wx)),
            pl.BlockSpec((bb, N_SMALL), lambda i: (ib(i), 0)),
            pl.BlockSpec((None, CONV_K - 1, bb, 2 * wx), lambda i: (layer, 0, ib(i), 0)),
            _lsel(layer, CONV_K, 2 * wx),
            _lsel(layer, 1, 2 * wx),
            _lsel(layer, 1, N_SMALL),
            _lsel(layer, 1, N_SMALL),
            _lsel(layer, 1, wx),
            _lsel(layer, 1, wx),
            pl.BlockSpec(sblk, lambda i: (layer * nblk + ib(i), 0, 0, 0))],
        out_specs=[pl.BlockSpec((bb, wx), lambda i: (ib(i), 0)),
                   pl.BlockSpec(sblk, lambda i: (ob(i), 0, 0, 0)),
                   pl.BlockSpec((CONV_K - 1, bb, 2 * wx), lambda i: (0, ib(i), 0))],
        out_shape=[jax.ShapeDtypeStruct((nb, wx), F32),
                   jax.ShapeDtypeStruct((DEPTH * nb, SSM_H, SSM_P, SSM_N), F32),
                   jax.ShapeDtypeStruct((CONV_K - 1, nb, 2 * wx), F32)],
        scratch_shapes=[pltpu.VMEM((bb, wx), F32)],
        input_output_aliases=al_map,
        compiler_params=_cparams(1),
        name="ssd_decode",
    )(*al_args, proj, proj, proj, small, buf, cw, cb, dtb, alog, dvec, nw, state)


def _rope_tables(pos):
    half = RET_DK // 2
    freqs = ROPE_THETA ** (-jnp.arange(half, dtype=F32) / half)
    ang = pos.astype(F32)[:, None] * freqs[None, :]
    cos = jnp.cos(ang)
    sin = jnp.sin(ang)
    return jnp.concatenate([cos, cos], axis=1), jnp.concatenate([-sin, sin], axis=1)


def _pad_lanes(v, offset):
    n = v.shape[1]
    return jnp.pad(v.astype(F32), ((0, 0), (offset, N_SMALL - offset - n))).reshape(DEPTH, 1, N_SMALL)


REPACK_TN = 1024
REPACK_RUNS = ((0, 6, 0), (6, 9, 2 * ML_H), (9, 12, 2 * ML_H + SSM_H))
SRC_GATE_BLK = 6144 // 128
SRC_DT_BLK = (6144 + 2 * ML_H + 3072) // 128


def _repack_kernel(a_ref, b_ref, g_ref, dt_ref, o_ref, os_ref):
    j = pl.program_id(1)

    @pl.when(j == 0)
    def _():
        row = lax.broadcasted_iota(jnp.int32, (N_SMALL, D_MODEL), 0)
        os_ref[...] = jnp.where(row < L_DT, g_ref[...],
                                jnp.where(row < L_DT + SSM_H, dt_ref[...], 0.0)).astype(BF16)

    for lo, hi, skip in REPACK_RUNS:
        @pl.when((j >= lo) & (j < hi))
        def _(skip=skip):
            src = a_ref[...] if skip == 0 else jnp.concatenate([a_ref[skip:, :], b_ref[0:skip, :]], axis=0)
            o_ref[...] = src.astype(BF16)


def _repack_w_in(w_in_t):
    tn = REPACK_TN
    return pl.pallas_call(
        _repack_kernel,
        grid=(DEPTH, N_MAIN // tn),
        in_specs=[pl.BlockSpec((None, tn, D_MODEL), lambda l, j: (l, j, 0)),
                  pl.BlockSpec((None, 128, D_MODEL), lambda l, j: (l, (j + 1) * (tn // 128), 0)),
                  pl.BlockSpec((None, 128, D_MODEL), lambda l, j: (l, SRC_GATE_BLK, 0)),
                  pl.BlockSpec((None, 128, D_MODEL), lambda l, j: (l, SRC_DT_BLK, 0))],
        out_specs=[pl.BlockSpec((None, tn, D_MODEL), lambda l, j: (l, j, 0)),
                   pl.BlockSpec((None, N_SMALL, D_MODEL), lambda l, j: (l, 0, 0))],
        out_shape=[jax.ShapeDtypeStruct((DEPTH, N_MAIN, D_MODEL), BF16),
                   jax.ShapeDtypeStruct((DEPTH, N_SMALL, D_MODEL), BF16)],
        compiler_params=_cparams(2),
        name="repack_w_in",
    )(w_in_t, w_in_t, w_in_t, w_in_t)


def _prep_weights(W):
    w_main, w_small = _repack_w_in(jnp.swapaxes(W['w_in'], 1, 2))
    row = lambda v: v.reshape(DEPTH, 1, -1).astype(F32)
    bf = lambda v: v.astype(BF16)
    return dict(
        w_main=w_main, w_small=w_small,
        norm_mix=row(W['norm_mix_w']),
        ret_norm=row(W['ret_norm_w']),
        ml_cw=W['ml_conv_w'], ml_cb=row(W['ml_conv_b']),
        ml_wq=bf(W['ml_wq']), ml_wk=bf(W['ml_wk']),
        ml_gb=_pad_lanes(W['ml_gate_b'], L_IG),
        ml_norm=row(W['ml_norm_w']),
        ssm_cw=W['ssm_conv_w'], ssm_cb=row(W['ssm_conv_b']),
        ssm_dtb=_pad_lanes(W['ssm_dt_bias'], L_DT),
        ssm_alog=_pad_lanes(W['ssm_A_log'], L_DT),
        ssm_dvec=row(jnp.repeat(W['ssm_D'], SSM_P, axis=1)),
        ssm_norm=row(W['ssm_norm_w']),
        w_br_ret=bf(W['w_br_ret']), w_br_ml=bf(W['w_br_ml']),
        w_br_ssm=bf(W['w_br_ssm']), w_out=bf(W['w_out_mix']),
        norm_mem=row(W['norm_mem_w']),
        mem_wq=bf(W['mem_wq']), mem_wo=bf(W['mem_wo']),
        norm_mlp=row(W['norm_mlp_w']),
        mlp_w1=bf(W['mlp_w1']), mlp_w2=bf(W['mlp_w2']),
    )


def kernel(x_prompt, x_sample, mem_prompt, state_ret, state_mlstm_C, state_mlstm_n, state_mlstm_m,
           state_mlstm_conv, state_ssm, state_ssm_conv, cache_mem_k, cache_mem_v,
           norm_mix_w, w_in, ret_norm_w, ml_conv_w, ml_conv_b, ml_wq, ml_wk, ml_gate_b, ml_norm_w,
           ssm_conv_w, ssm_conv_b, ssm_dt_bias, ssm_A_log, ssm_D, ssm_norm_w,
           w_br_ret, w_br_ml, w_br_ssm, w_out_mix, norm_mem_w, mem_wq, mem_wk, mem_wv, mem_wo,
           norm_mlp_w, mlp_w1, mlp_w2, norm_f_w):
    W = dict(norm_mix_w=norm_mix_w, w_in=w_in, ret_norm_w=ret_norm_w, ml_conv_w=ml_conv_w,
             ml_conv_b=ml_conv_b, ml_wq=ml_wq, ml_wk=ml_wk, ml_gate_b=ml_gate_b, ml_norm_w=ml_norm_w,
             ssm_conv_w=ssm_conv_w, ssm_conv_b=ssm_conv_b, ssm_dt_bias=ssm_dt_bias, ssm_A_log=ssm_A_log,
             ssm_D=ssm_D, ssm_norm_w=ssm_norm_w, w_br_ret=w_br_ret, w_br_ml=w_br_ml, w_br_ssm=w_br_ssm,
             w_out_mix=w_out_mix, norm_mem_w=norm_mem_w, mem_wq=mem_wq, mem_wo=mem_wo,
             norm_mlp_w=norm_mlp_w, mlp_w1=mlp_w1, mlp_w2=mlp_w2)
    lw = _prep_weights(W)
    norm_f = norm_f_w.reshape(1, -1).astype(F32)
    d = D_MODEL

    nb, seq = x_prompt.shape[0], x_prompt.shape[1]
    ns = x_sample.shape[0]
    cos_p, sin_p = _rope_tables(jnp.arange(seq, dtype=jnp.int32))
    memk, memv, memk_b, memv_b = _memkv(mem_prompt.reshape(nb * MEM_LEN, d), mem_wk.astype(BF16),
                                        mem_wv.astype(BF16), tm=1024)
    memk3 = memk_b.reshape(DEPTH * nb, MEM_LEN, d)
    memv3 = memv_b.reshape(DEPTH * nb, MEM_LEN, d)

    x = x_prompt.reshape(nb * seq, d)
    states_p = []
    for l in range(DEPTH):
        zs = (lambda shape: shape) if l == 0 else (lambda shape: None)
        proj, small, *mlc_s = _inproj(x, lw['norm_mix'], lw['w_main'], lw['w_small'], layer=l, tm=2048, tn=1024,
                                      out_dtype=BF16, zero_shape=zs((DEPTH * ns, ML_H, ML_DH, ML_DH)))
        y_r, ret_st, *ret_s = _ret_prefill(proj, cos_p, sin_p, nb=nb, seq=seq,
                                           zero_shape=zs((DEPTH * ns, RET_H, RET_DK, RET_DV)))
        y_m, ml_c, ml_n, ml_m, ml_cv = _ml_prefill(proj, small, lw['ml_cw'], lw['ml_cb'], lw['ml_wq'], lw['ml_wk'],
                                                   lw['ml_gb'], layer=l, nb=nb, seq=seq)
        y_s, ssm_st, ssm_cv, *ssm_s = _ssd_prefill(proj, small, lw['ssm_cw'], lw['ssm_cb'], lw['ssm_dtb'],
                                                   lw['ssm_alog'], lw['ssm_dvec'], layer=l, nb=nb,
                                                   seq=seq, zero_shape=zs((DEPTH * ns, SSM_H, SSM_P, SSM_N)))
        x, qm = _merge(y_r.reshape(nb * seq, d), y_m.reshape(nb * seq, d), y_s.reshape(nb * seq, d), proj,
                       x, lw['w_br_ret'], lw['w_br_ml'], lw['w_br_ssm'], lw['w_out'], lw['norm_mem'],
                       lw['mem_wq'], (lw['ret_norm'], lw['ml_norm'], lw['ssm_norm']), layer=l, tm=512,
                       q_dtype=BF16)
        if l == 0:
            zero_states = (ret_s[0], mlc_s[0], ssm_s[0])
        x = _attn_prefill(qm, memk3, memv3, x, lw['mem_wo'], layer=l, nb=nb, seq=seq, tq=1024)
        x = _mlp(x, lw['norm_mlp'], lw['mlp_w1'], lw['mlp_w2'], norm_f, layer=l, tm=1024, tf=2048,
                 final_norm=(l == DEPTH - 1))
        states_p.append((ret_st, ml_c, ml_n, ml_m[:, 0, :ML_H], ml_cv, ssm_st, ssm_cv))
    y_prompt = x.reshape(nb, seq, d)
    stack = lambda parts, i: jnp.stack([p[i] for p in parts])
    prompt_states = tuple(stack(states_p, i) for i in range(7))
    memk_p = memk.reshape(DEPTH, nb, MEM_LEN, MEM_H, MEM_DH)
    memv_p = memv.reshape(DEPTH, nb, MEM_LEN, MEM_H, MEM_DH)

    cos_s, sin_s = _rope_tables(PAST_LEN + jnp.arange(1, dtype=jnp.int32))
    ret_state = state_ret.reshape(DEPTH * ns, RET_H, RET_DK, RET_DV)
    mlc_state = state_mlstm_C.reshape(DEPTH * ns, ML_H, ML_DH, ML_DH)
    mln_state = state_mlstm_n.reshape(DEPTH * ns, ML_H * ML_DH)
    mlm_state = state_mlstm_m.reshape(DEPTH * ns, ML_H)
    ssm_st = state_ssm.reshape(DEPTH * ns, SSM_H, SSM_P, SSM_N)
    ck = cache_mem_k.reshape(DEPTH * ns, MEM_LEN, MEM_H, MEM_DH)
    cv = cache_mem_v.reshape(DEPTH * ns, MEM_LEN, MEM_H, MEM_DH)

    ml_buf = jnp.swapaxes(state_mlstm_conv, 1, 2)
    ssm_buf = jnp.swapaxes(state_ssm_conv, 1, 2)

    x = x_sample.reshape(ns, d)
    ret_s, mlc_s, ssm_s = zero_states
    small_s = []
    for l in range(DEPTH):
        proj, small = _inproj(x, lw['norm_mix'], lw['w_main'], lw['w_small'], layer=l, tm=ns, tn=1024, out_dtype=F32)
        y_r, ret_s = _ret_decode(proj, cos_s, sin_s, lw['ret_norm'], ret_state, ret_s,
                                 layer=l)
        y_m, mlc_s, ml_n, ml_m, ml_bo = _ml_decode(proj, small, ml_buf, lw['ml_cw'], lw['ml_cb'], lw['ml_wq'],
                                                   lw['ml_wk'], lw['ml_gb'], lw['ml_norm'],
                                                   mlc_state, mln_state, mlm_state, mlc_s,
                                                   layer=l)
        y_s, ssm_s, ssm_bo = _ssd_decode(proj, small, ssm_buf, lw['ssm_cw'], lw['ssm_cb'], lw['ssm_dtb'],
                                         lw['ssm_alog'], lw['ssm_dvec'], lw['ssm_norm'], ssm_st,
                                         ssm_s, layer=l)
        x, qm = _merge(y_r, y_m, y_s, proj, x, lw['w_br_ret'], lw['w_br_ml'], lw['w_br_ssm'], lw['w_out'],
                       lw['norm_mem'], lw['mem_wq'], layer=l, tm=ns, q_dtype=F32)
        att = _attn_decode(qm.reshape(ns, MEM_H, MEM_DH), ck, cv, layer=l, bb=4)
        x = _mm(att.reshape(ns, d), lw['mem_wo'], x, layer=l, tm=ns, tn=d)
        x = _mlp(x, lw['norm_mlp'], lw['mlp_w1'], lw['mlp_w2'], norm_f, layer=l, tm=ns, tf=2048,
                 final_norm=(l == DEPTH - 1))
        small_s.append((ml_n.reshape(ns, ML_H, ML_DH), ml_m, jnp.swapaxes(ml_bo, 0, 1),
                        jnp.swapaxes(ssm_bo, 0, 1)))
    y_sample = x.reshape(ns, 1, d)
    sample_states = (ret_s.reshape(DEPTH, ns, RET_H, RET_DK, RET_DV),
                     mlc_s.reshape(DEPTH, ns, ML_H, ML_DH, ML_DH),
                     stack(small_s, 0), stack(small_s, 1), stack(small_s, 2),
                     ssm_s.reshape(DEPTH, ns, SSM_H, SSM_P, SSM_N),
                     stack(small_s, 3))

    return (y_prompt, y_sample, *prompt_states, memk_p, memv_p, *sample_states)
```

```python
import functools
import math

import jax
import jax.numpy as jnp
from jax import lax
from jax.experimental import pallas as pl
from jax.experimental.pallas import tpu as pltpu

F32 = jnp.float32
BF16 = jnp.bfloat16

D_MODEL = 1024
DEPTH = 2
PAST_LEN = 16384
CHUNK = 128
CONV_K = 4
EPS = 1e-6
RET_H, RET_DK, RET_DV = 4, 128, 256
ROPE_THETA = 10000.0
ML_H, ML_DH = 4, 256
SSM_H, SSM_P, SSM_G, SSM_N = 16, 64, 4, 128
SSM_R = SSM_H // SSM_G
MEM_LEN, MEM_H, MEM_DH = 256, 4, 256
D_FF = 4 * D_MODEL

C_Q, C_K, C_V, C_G = 0, 512, 1024, 2048
C_U, C_VM, C_OM = 3072, 4096, 5120
C_Z, C_XS, C_B, C_C = 6144, 7168, 8192, 8704
C_GR, C_GM, C_GS = 9216, 10240, 11264
N_MAIN = 12288
L_IG, L_FG, L_DT = 0, 4, 8
N_SMALL = 128

VMEM_LIMIT = 56 * 1024 * 1024
NT_DIMS = (((1,), (1,)), ((), ()))

_LOG_G = [math.log1p(-(2.0 ** (-5.0 - h))) for h in range(RET_H)]


def _cparams(n_axes):
    return pltpu.CompilerParams(dimension_semantics=("arbitrary",) * n_axes,
                                vmem_limit_bytes=VMEM_LIMIT)


def _dot(a, b):
    return jnp.dot(a, b, preferred_element_type=F32)


def _dot_nt(a, b):
    return lax.dot_general(a, b, NT_DIMS, preferred_element_type=F32)


def _rmsnorm(x, w):
    ms = jnp.mean(x * x, axis=-1, keepdims=True)
    return x * lax.rsqrt(ms + EPS) * w


def _groupnorm(x, w, center):
    if center:
        x = x - jnp.mean(x, axis=-1, keepdims=True)
    ms = jnp.mean(x * x, axis=-1, keepdims=True)
    return x * lax.rsqrt(ms + EPS) * w


def _sigmoid(x):
    return 0.5 * jnp.tanh(0.5 * x) + 0.5


def _silu(x):
    return x * _sigmoid(x)


def _softplus(x):
    return jnp.maximum(x, 0.0) + jnp.log1p(jnp.exp(-jnp.abs(x)))


def _log_sigmoid(x):
    return -_softplus(-x)


def _cumsum_rows(x, tril_b):
    hi = x.astype(BF16)
    r1 = x - hi.astype(F32)
    mid = r1.astype(BF16)
    lo = (r1 - mid.astype(F32)).astype(BF16)
    return _dot(tril_b, hi) + _dot(tril_b, mid) + _dot(tril_b, lo)


def _pad_rows(x, rows):
    return jnp.concatenate([x, jnp.zeros((rows - x.shape[0], x.shape[1]), x.dtype)], axis=0)


def _pad_rows_t(x):
    return _pad_rows(x, 128).T


def _carried_call(kernel_fn, nblk, layer, prev):
    return (lambda *refs: kernel_fn(*refs[1:]), nblk, lambda i: i, lambda i: layer * nblk + i,
            [prev], [pl.BlockSpec(memory_space=pl.ANY)], {0: 1})


def _zero_out(shape, steps):
    blk = (shape[0] // steps,) + tuple(shape[1:])
    nz = len(shape) - 1
    return pl.BlockSpec(blk, lambda i: (i,) + (0,) * nz), jax.ShapeDtypeStruct(tuple(shape), F32)


def _zero_fill(zero_refs):
    for z_ref in zero_refs:
        z_ref[...] = jnp.zeros_like(z_ref)


def _inproj_kernel(x_ref, nw_ref, w_ref, ws_ref, o_ref, os_ref, *rest):
    *zero_refs, xn_ref = rest
    _zero_fill(zero_refs)

    @pl.when(pl.program_id(1) == 0)
    def _():
        xn = _rmsnorm(x_ref[...], nw_ref[...]).astype(BF16)
        xn_ref[...] = xn
        os_ref[...] = _dot_nt(xn, ws_ref[...])

    o_ref[...] = _dot_nt(xn_ref[...], w_ref[...]).astype(o_ref.dtype)


def _lsel(layer, *block, **kw):
    return pl.BlockSpec((None,) + block, lambda *_: (layer,) + (0,) * len(block), **kw)


def _inproj(x, nw, w_main, w_small, *, layer, tm, tn, out_dtype, zero_shape=None):
    m = x.shape[0]
    z_specs, z_shapes = [], []
    if zero_shape is not None:
        n_i, n_j = m // tm, N_MAIN // tn
        per_i = max(p for p in range(1, n_j + 1) if (zero_shape[0] // n_i) % p == 0)
        blk = (zero_shape[0] // (n_i * per_i),) + tuple(zero_shape[1:])
        z_specs = [pl.BlockSpec(blk, lambda i, j: (i * per_i + jnp.minimum(j, per_i - 1), 0, 0, 0))]
        z_shapes = [jax.ShapeDtypeStruct(tuple(zero_shape), F32)]
    return pl.pallas_call(
        _inproj_kernel,
        grid=(m // tm, N_MAIN // tn),
        in_specs=[pl.BlockSpec((tm, D_MODEL), lambda i, j: (i, 0)),
                  _lsel(layer, 1, D_MODEL),
                  pl.BlockSpec((None, tn, D_MODEL), lambda i, j: (layer, j, 0)),
                  _lsel(layer, N_SMALL, D_MODEL)],
        out_specs=[pl.BlockSpec((tm, tn), lambda i, j: (i, j)),
                   pl.BlockSpec((tm, N_SMALL), lambda i, j: (i, 0))] + z_specs,
        out_shape=[jax.ShapeDtypeStruct((m, N_MAIN), out_dtype),
                   jax.ShapeDtypeStruct((m, N_SMALL), F32)] + z_shapes,
        scratch_shapes=[pltpu.VMEM((tm, D_MODEL), BF16)],
        compiler_params=_cparams(2),
        name="inproj",
    )(x, nw, w_main, w_small)


def _mm_kernel(*refs, has_res):
    if has_res:
        a_ref, w_ref, r_ref, o_ref = refs
    else:
        a_ref, w_ref, o_ref = refs
    acc = _dot(a_ref[...].astype(BF16), w_ref[...])
    if has_res:
        acc = r_ref[...] + acc
    o_ref[...] = acc.astype(o_ref.dtype)


def _mm(a, w, res=None, *, layer, tm, tn, out_dtype=F32):
    m, k = a.shape
    n = w.shape[2]
    in_specs = [pl.BlockSpec((tm, k), lambda i, j: (i, 0)),
                pl.BlockSpec((None, k, tn), lambda i, j: (layer, 0, j))]
    args = [a, w]
    if res is not None:
        in_specs.append(pl.BlockSpec((tm, tn), lambda i, j: (i, j)))
        args.append(res)
    return pl.pallas_call(
        functools.partial(_mm_kernel, has_res=res is not None),
        grid=(m // tm, n // tn),
        in_specs=in_specs,
        out_specs=pl.BlockSpec((tm, tn), lambda i, j: (i, j)),
        out_shape=jax.ShapeDtypeStruct((m, n), out_dtype),
        compiler_params=_cparams(2),
        name="mm",
    )(*args)


def _memkv_kernel(a_ref, wk_ref, wv_ref, ok_ref, ov_ref, kb_ref, vb_ref):
    a = a_ref[...].astype(BF16)
    for w_ref, o_ref, b_ref in ((wk_ref, ok_ref, kb_ref), (wv_ref, ov_ref, vb_ref)):
        acc = _dot(a, w_ref[...])
        b_ref[...] = acc.astype(BF16)
        for h in range(MEM_H):
            o_ref[:, h, :] = acc[:, h * MEM_DH:(h + 1) * MEM_DH]


def _memkv(a, wk, wv, *, tm):
    m = a.shape[0]
    n = wk.shape[2]
    return pl.pallas_call(
        _memkv_kernel,
        grid=(DEPTH, m // tm),
        in_specs=[pl.BlockSpec((tm, D_MODEL), lambda l, i: (i, 0)),
                  pl.BlockSpec((None, D_MODEL, n), lambda l, i: (l, 0, 0)),
                  pl.BlockSpec((None, D_MODEL, n), lambda l, i: (l, 0, 0))],
        out_specs=[pl.BlockSpec((None, tm, MEM_H, MEM_DH), lambda l, i: (l, i, 0, 0)),
                   pl.BlockSpec((None, tm, MEM_H, MEM_DH), lambda l, i: (l, i, 0, 0)),
                   pl.BlockSpec((None, tm, n), lambda l, i: (l, i, 0)),
                   pl.BlockSpec((None, tm, n), lambda l, i: (l, i, 0))],
        out_shape=[jax.ShapeDtypeStruct((DEPTH, m, MEM_H, MEM_DH), F32)] * 2
        + [jax.ShapeDtypeStruct((DEPTH, m, n), BF16)] * 2,
        compiler_params=_cparams(2),
        name="memkv",
    )(a, wk, wv)


class _Ctx(dict):
    __getattr__ = dict.__getitem__
    __setattr__ = dict.__setitem__


def _seq_loop(nb, per_trip, seq_phases, n_items, item_phases, finish):
    def body(i, carry):
        seqs = [_Ctx(b=i * per_trip + u) for u in range(per_trip)]
        for phase in seq_phases:
            for s in seqs:
                phase(s)
        items = []
        for s in seqs:
            s.heads = [_Ctx(seq=s, b=s.b, k=k) for k in range(n_items)]
            items.extend(s.heads)
        for phase in item_phases:
            for it in items:
                phase(it)
        for s in seqs:
            finish(s)
        return carry

    lax.fori_loop(0, nb // per_trip, body, 0)


def _ret_prefill_kernel(q_ref, k_ref, v_ref, cos_ref, sin_ref, y_ref, s_ref, *zero_refs):
    _zero_fill(zero_refs)

    @pl.when(pl.program_id(0) == 0)
    def _():
        s_ref[...] = jnp.zeros_like(s_ref)

    c = CHUNK
    nb = q_ref.shape[0]
    cosf = cos_ref[...]
    sinf = sin_ref[...]
    row = lax.broadcasted_iota(jnp.int32, (c, c), 0)
    col = lax.broadcasted_iota(jnp.int32, (c, c), 1)
    causal = row >= col
    diff = jnp.where(causal, (row - col).astype(F32), 0.0)
    rowf = row.astype(F32)
    rowf_v = lax.broadcasted_iota(jnp.int32, (c, RET_DV), 0).astype(F32)
    decay = [jnp.where(causal, jnp.exp(lg * diff), 0.0) for lg in _LOG_G]
    q_decay = [jnp.exp(lg * (rowf_v + 1.0)) for lg in _LOG_G]
    k_decay = [jnp.exp(lg * (c - 1.0 - rowf)) * (RET_DK ** -0.5) for lg in _LOG_G]

    def load(s):
        s.s_old = [s_ref[s.b, h] for h in range(RET_H)]

    def rotate(it):
        h = it.k
        sk = slice(h * RET_DK, (h + 1) * RET_DK)
        q = q_ref[it.b, :, sk].astype(F32)
        k = k_ref[it.b, :, sk].astype(F32)
        qr = q * cosf + pltpu.roll(q, RET_DK // 2, 1) * sinf
        kr = k * cosf + pltpu.roll(k, RET_DK // 2, 1) * sinf
        it.qb = qr.astype(BF16)
        it.kb = (kr * (RET_DK ** -0.5)).astype(BF16)
        it.kt = (kr * k_decay[h]).T.astype(BF16)
        it.v = v_ref[it.b, :, h * RET_DV:(h + 1) * RET_DV]

    def scores(it):
        it.sc = (_dot_nt(it.qb, it.kb) * decay[it.k]).astype(BF16)

    def mix(it):
        h = it.k
        s_old = it.seq.s_old[h]
        it.y = _dot(it.sc, it.v) + _dot(it.qb, s_old.astype(BF16)) * q_decay[h]
        it.s_new = s_old * math.exp(_LOG_G[h] * c) + _dot(it.kt, it.v)

    def emit(it):
        y_ref[it.b, :, it.k * RET_DV:(it.k + 1) * RET_DV] = it.y.astype(y_ref.dtype)

    def finish(s):
        for it in s.heads:
            s_ref[s.b, it.k] = it.s_new

    _seq_loop(nb, 4, [load], RET_H, [rotate, scores, mix, emit], finish)


def _ret_prefill(proj, cosf, sinf, *, nb, seq, zero_shape=None):
    c = CHUNK
    p3 = proj.reshape(nb, seq, N_MAIN)
    z_specs, z_shapes = ([], []) if zero_shape is None else [[v] for v in _zero_out(zero_shape, seq // c)]
    return pl.pallas_call(
        _ret_prefill_kernel,
        grid=(seq // c,),
        in_specs=[
            pl.BlockSpec((nb, c, 512), lambda t: (0, t, C_Q // 512)),
            pl.BlockSpec((nb, c, 512), lambda t: (0, t, C_K // 512)),
            pl.BlockSpec((nb, c, 1024), lambda t: (0, t, C_V // 1024)),
            pl.BlockSpec((c, RET_DK), lambda t: (t, 0)),
            pl.BlockSpec((c, RET_DK), lambda t: (t, 0))],
        out_specs=[pl.BlockSpec((nb, c, 1024), lambda t: (0, t, 0)),
                   pl.BlockSpec((nb, RET_H, RET_DK, RET_DV), lambda t: (0, 0, 0, 0))] + z_specs,
        out_shape=[jax.ShapeDtypeStruct((nb, seq, 1024), BF16),
                   jax.ShapeDtypeStruct((nb, RET_H, RET_DK, RET_DV), F32)] + z_shapes,
        compiler_params=_cparams(1),
        name="ret_prefill",
    )(p3, p3, p3, cosf, sinf)


def _shift_matrix(first_chunk):
    c = CHUNK
    r = lax.broadcasted_iota(jnp.int32, (3 * c, 2 * c), 0)
    q = lax.broadcasted_iota(jnp.int32, (3 * c, 2 * c), 1)
    k = lax.shift_right_logical(r, 7)
    t = jnp.bitwise_and(r, c - 1)
    hit = (q == c + t - (CONV_K - 1) + k) & (q >= jnp.where(first_chunk, c, 0))
    return jnp.where(hit, 1.0, 0.0).astype(BF16)


def _conv_silu(shift, x_prev, x_cur, cw, cb):
    c = CHUNK
    p = _dot(shift, jnp.concatenate([x_prev, x_cur], axis=0))
    acc = cb
    for j in range(CONV_K - 1):
        acc = acc + p[j * c:(j + 1) * c] * cw[j:j + 1, :]
    acc = acc + x_cur.astype(F32) * cw[CONV_K - 1:CONV_K, :]
    return _silu(acc)


def _ml_prefill_kernel(u_ref, up_ref, v_ref, sm_ref, cw_ref, cb_ref, wq_ref, wk_ref, gb_ref,
                       y_ref, c_ref, n_ref, m_ref, cv_ref, tail):
    c = CHUNK
    nb = u_ref.shape[0]
    first = pl.program_id(0) == 0

    @pl.when(first)
    def _():
        c_ref[...] = jnp.zeros_like(c_ref)
        n_ref[...] = jnp.zeros_like(n_ref)
        m_ref[...] = jnp.zeros_like(m_ref)

    row = lax.broadcasted_iota(jnp.int32, (c, c), 0)
    col = lax.broadcasted_iota(jnp.int32, (c, c), 1)
    causal = row >= col
    tril_b = jnp.where(causal, 1.0, 0.0).astype(BF16)
    lane_m = lax.broadcasted_iota(jnp.int32, (1, 128), 1)
    shift = _shift_matrix(first)

    def load(s):
        s.c_old = [c_ref[s.b, h] for h in range(ML_H)]
        s.n_old = n_ref[s.b]
        s.m_row = m_ref[s.b]

    def gates_phase(s):
        tail[...] = u_ref[s.b, c - 16:c, :].astype(F32)
        cv_ref[s.b] = tail[16 - (CONV_K - 1):16, :]
        s.gates = sm_ref[s.b] + gb_ref[...]
        s.b_all = _cumsum_rows(_log_sigmoid(s.gates), tril_b)
        s.b_all_t = s.b_all.T
        s.gates_t = s.gates.T

    def conv_phase(it):
        sl = slice(it.k * ML_DH, (it.k + 1) * ML_DH)
        it.uc = _conv_silu(shift, up_ref[it.b, :, sl], u_ref[it.b, :, sl], cw_ref[:, sl], cb_ref[:, sl]).astype(BF16)
        it.v = v_ref[it.b, :, sl]

    def weights_phase(it):
        h, s = it.k, it.seq
        b_col = s.b_all[:, L_FG + h:L_FG + h + 1]
        b_row = s.b_all_t[L_FG + h:L_FG + h + 1, :]
        it_col = s.gates[:, L_IG + h:L_IG + h + 1]
        it_row = s.gates_t[L_IG + h:L_IG + h + 1, :]
        m_prev = s.m_row[:, h:h + 1]
        logw = jnp.where(causal, b_col - b_row + it_row, -jnp.inf)
        inter = b_col + m_prev
        it.m_t = jnp.maximum(inter, jnp.max(logw, axis=-1, keepdims=True))
        it.dmat = jnp.exp(logw - it.m_t)
        it.w_int = jnp.exp(inter - it.m_t)
        b_end = b_col[c - 1:c, :]
        logw_s = b_end - b_col + it_col
        it.m_new = jnp.maximum(b_end + m_prev, jnp.max(logw_s, axis=0, keepdims=True))
        it.w_s = jnp.exp(logw_s - it.m_new)
        it.w_prev = jnp.exp(b_end + m_prev - it.m_new)

    def qk_phase(it):
        it.qm = _dot(it.uc, wq_ref[it.k])
        it.km = _dot(it.uc, wk_ref[it.k]) * (ML_DH ** -0.5)
        it.qb = it.qm.astype(BF16)
        it.kb = it.km.astype(BF16)
        it.kw = it.km * it.w_s
        it.kwt = it.kw.T.astype(BF16)

    def scores_phase(it):
        it.sc = _dot_nt(it.qb, it.kb) * it.dmat

    def mix_phase(it):
        c_old = it.seq.c_old[it.k]
        n_old = it.seq.n_old[it.k:it.k + 1, :]
        it.num = _dot(it.sc.astype(BF16), it.v) + _dot(it.qb, c_old.astype(BF16)) * it.w_int
        it.c_new = c_old * it.w_prev + _dot(it.kwt, it.v)
        it.n_new = n_old * it.w_prev + jnp.sum(it.kw, axis=0, keepdims=True)
        den = jnp.sum(it.sc, axis=-1, keepdims=True) + jnp.sum(it.qm * n_old, axis=-1, keepdims=True) * it.w_int
        it.den = jnp.maximum(jnp.abs(den), jnp.exp(-it.m_t))

    def emit_phase(it):
        y_ref[it.b, :, it.k * ML_DH:(it.k + 1) * ML_DH] = (it.num / it.den).astype(y_ref.dtype)

    def finish(s):
        m_row_new = s.m_row
        for it in s.heads:
            c_ref[s.b, it.k] = it.c_new
            n_ref[s.b, it.k:it.k + 1, :] = it.n_new
            m_row_new = jnp.where(lane_m == it.k, it.m_new, m_row_new)
        m_ref[s.b] = m_row_new

    _seq_loop(nb, 1, [load, gates_phase], ML_H,
              [conv_phase, weights_phase, qk_phase, scores_phase, mix_phase, emit_phase], finish)


def _ml_prefill(proj, small, cw, cb, wq, wk, gb, *, layer, nb, seq):
    c = CHUNK
    p3 = proj.reshape(nb, seq, N_MAIN)
    s3 = small.reshape(nb, seq, N_SMALL)
    w = ML_H * ML_DH
    return pl.pallas_call(
        _ml_prefill_kernel,
        grid=(seq // c,),
        in_specs=[
            pl.BlockSpec((nb, c, w), lambda t: (0, t, C_U // w)),
            pl.BlockSpec((nb, c, w), lambda t: (0, jnp.maximum(t - 1, 0), C_U // w)),
            pl.BlockSpec((nb, c, w), lambda t: (0, t, C_VM // w)),
            pl.BlockSpec((nb, c, N_SMALL), lambda t: (0, t, 0)),
            _lsel(layer, CONV_K, w),
            _lsel(layer, 1, w),
            _lsel(layer, ML_H, ML_DH, ML_DH),
            _lsel(layer, ML_H, ML_DH, ML_DH),
            _lsel(layer, 1, N_SMALL)],
        out_specs=[pl.BlockSpec((nb, c, w), lambda t: (0, t, 0)),
                   pl.BlockSpec((nb, ML_H, ML_DH, ML_DH), lambda t: (0, 0, 0, 0)),
                   pl.BlockSpec((nb, ML_H, ML_DH), lambda t: (0, 0, 0)),
                   pl.BlockSpec((nb, 1, 128), lambda t: (0, 0, 0)),
                   pl.BlockSpec((nb, CONV_K - 1, w), lambda t: (0, 0, 0))],
        out_shape=[jax.ShapeDtypeStruct((nb, seq, w), BF16),
                   jax.ShapeDtypeStruct((nb, ML_H, ML_DH, ML_DH), F32),
                   jax.ShapeDtypeStruct((nb, ML_H, ML_DH), F32),
                   jax.ShapeDtypeStruct((nb, 1, 128), F32),
                   jax.ShapeDtypeStruct((nb, CONV_K - 1, w), F32)],
        scratch_shapes=[pltpu.VMEM((16, w), F32)],
        compiler_params=_cparams(1),
        name="ml_prefill",
    )(p3, p3, p3, s3, cw, cb, wq, wk, gb)


def _block_bcast(tile, lanes):
    c = tile.shape[0]
    lane = lax.broadcasted_iota(jnp.int32, (c, 128), 1)
    cols = [jnp.broadcast_to(tile[:, l:l + 1], (c, 128)) for l in lanes]
    left = jnp.where(lane < SSM_P, cols[0], cols[1])
    right = jnp.where(lane < SSM_P, cols[2], cols[3])
    return jnp.concatenate([left, right], axis=1)


def _ssd_prefill_kernel(xs_ref, xsp_ref, bc_ref, bcp_ref, sm_ref, cw_ref, cb_ref, dtb_ref, alog_ref,
                        dv_ref, y_ref, s_ref, cv_ref, *rest):
    *zero_refs, tail = rest
    _zero_fill(zero_refs)
    c = CHUNK
    nb = xs_ref.shape[0]
    wx = SSM_H * SSM_P
    gw = SSM_R * SSM_P
    first = pl.program_id(0) == 0

    @pl.when(first)
    def _():
        s_ref[...] = jnp.zeros_like(s_ref)

    row = lax.broadcasted_iota(jnp.int32, (c, c), 0)
    col = lax.broadcasted_iota(jnp.int32, (c, c), 1)
    causal = row >= col
    tril_b = jnp.where(causal, 1.0, 0.0).astype(BF16)
    lane_blk = lax.shift_right_logical(lax.broadcasted_iota(jnp.int32, (c, gw), 1), 6)
    head_mask = [jnp.where(lane_blk == r, 1.0, 0.0).astype(BF16) for r in range(SSM_R)]
    a_row = -jnp.exp(alog_ref[...])
    shift = _shift_matrix(first)

    def conv(cur_ref, prv_ref, b, lanes, w_off):
        wl = slice(w_off + lanes.start, w_off + lanes.stop)
        return _conv_silu(shift, prv_ref[b, :, lanes], cur_ref[b, :, lanes], cw_ref[:, wl], cb_ref[:, wl])

    def load(s):
        s.s_old = [s_ref[s.b, g * SSM_R:(g + 1) * SSM_R].reshape(gw, SSM_N) for g in range(SSM_G)]

    def dt_phase(s):
        b = s.b
        tail[:, 0:wx] = xs_ref[b, c - 16:c, :].astype(F32)
        tail[:, wx:2 * wx] = bc_ref[b, c - 16:c, :].astype(F32)
        cv_ref[b] = tail[16 - (CONV_K - 1):16, :]
        s.delta = _softplus(sm_ref[b] + dtb_ref[...])
        s.cum = _cumsum_rows(s.delta * a_row, tril_b)
        s.cum_t = s.cum.T
        s.delta_t = s.delta.T
        s.ecum = jnp.exp(s.cum)
        cum_end = s.cum[c - 1:c, :]
        s.w_state = jnp.exp(cum_end - s.cum) * s.delta
        s.dec_row = jnp.exp(cum_end)

    def conv_phase(it):
        g = it.k
        it.lanes = [L_DT + g * SSM_R + r for r in range(SSM_R)]
        it.xg = conv(xs_ref, xsp_ref, it.b, slice(g * gw, (g + 1) * gw), 0)
        it.bg = conv(bc_ref, bcp_ref, it.b, slice(g * SSM_N, (g + 1) * SSM_N), wx).astype(BF16)
        it.cg = conv(bc_ref, bcp_ref, it.b, slice((SSM_G + g) * SSM_N, (SSM_G + g + 1) * SSM_N), wx).astype(BF16)

    def seg_phase(it):
        s = it.seq
        it.seg = [jnp.exp(jnp.where(causal, s.cum[:, ln:ln + 1] - s.cum_t[ln:ln + 1, :], -jnp.inf))
                  * s.delta_t[ln:ln + 1, :] for ln in it.lanes]
        xgb = it.xg.astype(BF16)
        it.x_stack = jnp.concatenate([xgb * head_mask[r] for r in range(SSM_R)], axis=0)
        it.xwt = (it.xg * _block_bcast(s.w_state, it.lanes)).T.astype(BF16)

    def cb_phase(it):
        it.cb_mat = _dot_nt(it.cg, it.bg)

    def mix_phase(it):
        s = it.seq
        s_old = s.s_old[it.k]
        m_wide = jnp.concatenate([(it.cb_mat * sg).astype(BF16) for sg in it.seg], axis=1)
        it.y = (_dot(m_wide, it.x_stack)
                + _dot_nt(it.cg, s_old.astype(BF16)) * _block_bcast(s.ecum, it.lanes))
        upd = _dot(it.xwt, it.bg)
        it.s_new = [s_old[r * SSM_P:(r + 1) * SSM_P] * s.dec_row[:, ln:ln + 1] + upd[r * SSM_P:(r + 1) * SSM_P]
                    for r, ln in enumerate(it.lanes)]

    def emit_phase(it):
        gs = slice(it.k * gw, (it.k + 1) * gw)
        y_ref[it.b, :, gs] = (it.y + dv_ref[:, gs] * it.xg).astype(y_ref.dtype)

    def finish(s):
        for it in s.heads:
            for r in range(SSM_R):
                s_ref[s.b, it.k * SSM_R + r] = it.s_new[r]

    _seq_loop(nb, 2, [load, dt_phase], SSM_G, [conv_phase, seg_phase, cb_phase, mix_phase, emit_phase], finish)


def _ssd_prefill(proj, small, cw, cb, dtb, alog, dvec, *, layer, nb, seq, zero_shape=None):
    c = CHUNK
    wx = SSM_H * SSM_P
    p3 = proj.reshape(nb, seq, N_MAIN)
    s3 = small.reshape(nb, seq, N_SMALL)
    z_specs, z_shapes = ([], []) if zero_shape is None else [[v] for v in _zero_out(zero_shape, seq // c)]
    return pl.pallas_call(
        _ssd_prefill_kernel,
        grid=(seq // c,),
        in_specs=[
            pl.BlockSpec((nb, c, wx), lambda t: (0, t, C_XS // wx)),
            pl.BlockSpec((nb, c, wx), lambda t: (0, jnp.maximum(t - 1, 0), C_XS // wx)),
            pl.BlockSpec((nb, c, wx), lambda t: (0, t, C_B // wx)),
            pl.BlockSpec((nb, c, wx), lambda t: (0, jnp.maximum(t - 1, 0), C_B // wx)),
            pl.BlockSpec((nb, c, N_SMALL), lambda t: (0, t, 0)),
            _lsel(layer, CONV_K, 2 * wx),
            _lsel(layer, 1, 2 * wx),
            _lsel(layer, 1, N_SMALL),
            _lsel(layer, 1, N_SMALL),
            _lsel(layer, 1, wx)],
        out_specs=[pl.BlockSpec((nb, c, wx), lambda t: (0, t, 0)),
                   pl.BlockSpec((nb, SSM_H, SSM_P, SSM_N), lambda t: (0, 0, 0, 0)),
                   pl.BlockSpec((nb, CONV_K - 1, 2 * wx), lambda t: (0, 0, 0))] + z_specs,
        out_shape=[jax.ShapeDtypeStruct((nb, seq, wx), BF16),
                   jax.ShapeDtypeStruct((nb, SSM_H, SSM_P, SSM_N), F32),
                   jax.ShapeDtypeStruct((nb, CONV_K - 1, 2 * wx), F32)] + z_shapes,
        scratch_shapes=[pltpu.VMEM((16, 2 * wx), F32)],
        compiler_params=_cparams(1),
        name="ssd_prefill",
    )(p3, p3, p3, p3, s3, cw, cb, dtb, alog, dvec)


MERGE_SUB = 256


def _head_norm_gate(y_ref, g_ref, n_ref, rs, n_heads, center, gate_fn, gate_first):
    wh = y_ref.shape[1] // n_heads
    parts = []
    for h in range(n_heads):
        hs = slice(h * wh, (h + 1) * wh)
        y = y_ref[rs, hs].astype(F32)
        gate = gate_fn(g_ref[rs, hs].astype(F32))
        y = _groupnorm(y * gate, n_ref[:, hs], center) if gate_first else _groupnorm(y, n_ref[:, hs], center) * gate
        parts.append(y.astype(BF16))
    return jnp.concatenate(parts, axis=1)


def _merge_kernel(yr_ref, ym_ref, ys_ref, gr_ref, gm_ref, gs_ref, x_ref, wr_ref, wm_ref, ws_ref,
                  wo_ref, nw_ref, wq_ref, *rest, fused_norm):
    if fused_norm:
        g_ref, om_ref, z_ref, nr_ref, nm_ref, ns_ref, xo_ref, qo_ref, *zero_refs = rest
    else:
        xo_ref, qo_ref, *zero_refs = rest
    _zero_fill(zero_refs)
    tm = x_ref.shape[0]
    n_sub = max(1, tm // MERGE_SUB)
    subs = [slice(i * (tm // n_sub), (i + 1) * (tm // n_sub)) for i in range(n_sub)]
    branches = ((yr_ref, gr_ref, wr_ref), (ym_ref, gm_ref, wm_ref), (ys_ref, gs_ref, ws_ref))
    if fused_norm:
        def operands(rs):
            return (_head_norm_gate(yr_ref, g_ref, nr_ref, rs, RET_H, True, _silu, False),
                    _head_norm_gate(ym_ref, om_ref, nm_ref, rs, ML_H, True, _sigmoid, False),
                    _head_norm_gate(ys_ref, z_ref, ns_ref, rs, SSM_G, False, _silu, True))
    else:
        def operands(rs):
            return tuple(y_ref[rs, :].astype(BF16) for y_ref, _, _ in branches)
    proj = [[_dot(y, w_ref[...]) for y, (_, _, w_ref) in zip(operands(rs), branches)] for rs in subs]
    merged = []
    for rs, pr in zip(subs, proj):
        gated = [_sigmoid(g_ref[rs, :].astype(F32)) * p for (_, g_ref, _), p in zip(branches, pr)]
        merged.append((gated[0] + gated[1] + gated[2]).astype(BF16))
    xns = [x_ref[rs, :] + _dot(m, wo_ref[...]) for rs, m in zip(subs, merged)]
    hqs = []
    for rs, xn in zip(subs, xns):
        xo_ref[rs, :] = xn
        hqs.append(_rmsnorm(xn, nw_ref[...]).astype(BF16))
    for rs, hq in zip(subs, hqs):
        qo_ref[rs, :] = _dot(hq, wq_ref[...]).astype(qo_ref.dtype)


def _merge(yr, ym, ys, proj, x, wr, wm, ws, wo, nw, wq, norms=None, *, layer, tm, q_dtype, zero_shape=None):
    m = x.shape[0]
    d = D_MODEL
    row = lambda i: (i, 0)
    z_specs, z_shapes = ([], []) if zero_shape is None else [[v] for v in _zero_out(zero_shape, m // tm)]
    wspec = _lsel(layer, d, d, pipeline_mode=pl.Buffered(1))
    fused = norms is not None
    extra_specs, extra_args = [], []
    if fused:
        extra_specs = [pl.BlockSpec((tm, d), lambda i: (i, C_G // d)),
                       pl.BlockSpec((tm, d), lambda i: (i, C_OM // d)),
                       pl.BlockSpec((tm, d), lambda i: (i, C_Z // d))] + [_lsel(layer, 1, d)] * 3
        extra_args = [proj, proj, proj, *norms]
    return pl.pallas_call(
        functools.partial(_merge_kernel, fused_norm=fused),
        grid=(m // tm,),
        in_specs=[pl.BlockSpec((tm, d), row), pl.BlockSpec((tm, d), row), pl.BlockSpec((tm, d), row),
                  pl.BlockSpec((tm, d), lambda i: (i, C_GR // d)),
                  pl.BlockSpec((tm, d), lambda i: (i, C_GM // d)),
                  pl.BlockSpec((tm, d), lambda i: (i, C_GS // d)),
                  pl.BlockSpec((tm, d), row),
                  wspec, wspec, wspec, wspec,
                  _lsel(layer, 1, d),
                  wspec] + extra_specs,
        out_specs=[pl.BlockSpec((tm, d), row), pl.BlockSpec((tm, d), row)] + z_specs,
        out_shape=[jax.ShapeDtypeStruct((m, d), F32), jax.ShapeDtypeStruct((m, d), q_dtype)] + z_shapes,
        compiler_params=_cparams(1),
        name="merge",
    )(yr, ym, ys, proj, proj, proj, x, wr, wm, ws, wo, nw, wq, *extra_args)


def _attn_prefill_kernel(q_ref, k_ref, v_ref, x_ref, wo_ref, o_ref):
    heads = [slice(h * MEM_DH, (h + 1) * MEM_DH) for h in range(MEM_H)]
    scores = [_dot_nt(q_ref[0, :, sl], k_ref[0, :, sl].astype(BF16)) * (MEM_DH ** -0.5) for sl in heads]
    probs = []
    for s in scores:
        p = jnp.exp(s - jnp.max(s, axis=-1, keepdims=True))
        probs.append((p / jnp.sum(p, axis=-1, keepdims=True)).astype(BF16))
    outs = [_dot(p, v_ref[0, :, sl].astype(BF16)).astype(BF16) for p, sl in zip(probs, heads)]
    o_all = jnp.concatenate(outs, axis=1)
    o_ref[0] = x_ref[0] + _dot(o_all, wo_ref[...])


def _attn_prefill(q, mk, mv, x, wo, *, layer, nb, seq, tq):
    d = D_MODEL
    q3 = q.reshape(nb, seq, d)
    x3 = x.reshape(nb, seq, d)
    out = pl.pallas_call(
        _attn_prefill_kernel,
        grid=(nb, seq // tq),
        in_specs=[pl.BlockSpec((1, tq, d), lambda b, t: (b, t, 0)),
                  pl.BlockSpec((1, MEM_LEN, d), lambda b, t: (layer * nb + b, 0, 0)),
                  pl.BlockSpec((1, MEM_LEN, d), lambda b, t: (layer * nb + b, 0, 0)),
                  pl.BlockSpec((1, tq, d), lambda b, t: (b, t, 0)),
                  _lsel(layer, d, d)],
        out_specs=pl.BlockSpec((1, tq, d), lambda b, t: (b, t, 0)),
        out_shape=jax.ShapeDtypeStruct((nb, seq, d), F32),
        compiler_params=_cparams(2),
        name="attn_prefill",
    )(q3, mk, mv, x3, wo)
    return out.reshape(nb * seq, d)


def _attn_decode_kernel(q_ref, k_ref, v_ref, o_ref):
    for j in range(q_ref.shape[0]):
        qs = q_ref[j] * (MEM_DH ** -0.5)
        s = jnp.sum(k_ref[j] * qs[None], axis=-1, keepdims=True)
        p = jnp.exp(s - jnp.max(s, axis=0, keepdims=True))
        o_ref[j] = jnp.sum(p * v_ref[j], axis=0) / jnp.sum(p, axis=0)


def _attn_decode(q, ck, cv, *, layer, bb):
    nb = q.shape[0]
    nblk = nb // bb
    blk = (bb, MEM_LEN, MEM_H, MEM_DH)
    return pl.pallas_call(
        _attn_decode_kernel,
        grid=(nblk,),
        in_specs=[pl.BlockSpec((bb, MEM_H, MEM_DH), lambda i: (i, 0, 0)),
                  pl.BlockSpec(blk, lambda i: (layer * nblk + i, 0, 0, 0)),
                  pl.BlockSpec(blk, lambda i: (layer * nblk + i, 0, 0, 0))],
        out_specs=pl.BlockSpec((bb, MEM_H, MEM_DH), lambda i: (i, 0, 0)),
        out_shape=jax.ShapeDtypeStruct((nb, MEM_H, MEM_DH), F32),
        compiler_params=_cparams(1),
        name="attn_decode",
    )(q, ck, cv)


MLP_SUB = 1024


def _mlp_kernel(x_ref, nw_ref, w1_ref, w2_ref, nf_ref, o_ref, xn_ref, acc_ref, *, final_norm):
    j = pl.program_id(1)

    @pl.when(j == 0)
    def _():
        xn_ref[...] = _rmsnorm(x_ref[...], nw_ref[...]).astype(BF16)
        acc_ref[...] = jnp.zeros_like(acc_ref)

    subs = [slice(s * MLP_SUB, (s + 1) * MLP_SUB) for s in range(w1_ref.shape[1] // MLP_SUB)]
    hids = [_dot(xn_ref[...], w1_ref[:, sl]) for sl in subs]
    hids = [jnp.square(jnp.maximum(hid, 0.0)).astype(BF16) for hid in hids]
    acc = acc_ref[...]
    for hid, sl in zip(hids, subs):
        acc = acc + _dot(hid, w2_ref[sl, :])
    acc_ref[...] = acc

    @pl.when(j == pl.num_programs(1) - 1)
    def _():
        y = x_ref[...] + acc_ref[...]
        if final_norm:
            y = _rmsnorm(y, nf_ref[...])
        o_ref[...] = y


def _mlp(x, nw, w1, w2, nf, *, layer, tm, tf, final_norm):
    m = x.shape[0]
    d = D_MODEL
    return pl.pallas_call(
        functools.partial(_mlp_kernel, final_norm=final_norm),
        grid=(m // tm, D_FF // tf),
        in_specs=[pl.BlockSpec((tm, d), lambda i, j: (i, 0)),
                  _lsel(layer, 1, d),
                  pl.BlockSpec((None, d, tf), lambda i, j: (layer, 0, j)),
                  pl.BlockSpec((None, tf, d), lambda i, j: (layer, j, 0)),
                  pl.BlockSpec((1, d), lambda i, j: (0, 0))],
        out_specs=pl.BlockSpec((tm, d), lambda i, j: (i, 0)),
        out_shape=jax.ShapeDtypeStruct((m, d), F32),
        scratch_shapes=[pltpu.VMEM((tm, d), BF16), pltpu.VMEM((tm, d), F32)],
        compiler_params=_cparams(2),
        name="mlp",
    )(x, nw, w1, w2, nf)


DEC_BB = 8


def _row_mask(x, j):
    rows = lax.broadcasted_iota(jnp.int32, x.shape, 0)
    return jnp.where(rows == j, x, 0.0).astype(BF16)


def _blockdiag_rows(v):
    n = v.shape[0]
    tiled = jnp.concatenate([v] * n, axis=1)
    rows = lax.broadcasted_iota(jnp.int32, tiled.shape, 0)
    blk = lax.shift_right_logical(lax.broadcasted_iota(jnp.int32, tiled.shape, 1), 8)
    return _pad_rows(jnp.where(rows == blk, tiled, 0.0), 128).astype(BF16)


def _ret_decode_kernel(q_ref, k_ref, v_ref, g_ref, cos_ref, sin_ref, nw_ref, s_ref, y_ref, so_ref, ybuf):
    cosf = cos_ref[...]
    sinf = sin_ref[...]
    for h in range(RET_H):
        sk = slice(h * RET_DK, (h + 1) * RET_DK)
        sv = slice(h * RET_DV, (h + 1) * RET_DV)
        g_dec = math.exp(_LOG_G[h])
        q = q_ref[:, sk]
        k = k_ref[:, sk]
        qr = q * cosf + pltpu.roll(q, RET_DK // 2, 1) * sinf
        kr = (k * cosf + pltpu.roll(k, RET_DK // 2, 1) * sinf) * (RET_DK ** -0.5)
        k_t = _pad_rows_t(kr).astype(BF16)
        q16 = _pad_rows(qr, 16).astype(BF16)
        upd = _dot(k_t, _blockdiag_rows(v_ref[:, sv]))
        s_wide = []
        for j in range(DEC_BB):
            s_new = s_ref[j, h] * g_dec + upd[:, j * RET_DV:(j + 1) * RET_DV]
            so_ref[j, h] = s_new
            s_wide.append(s_new.astype(BF16))
        y_all = _dot(q16, jnp.concatenate(s_wide, axis=1))
        for j in range(DEC_BB):
            ybuf[j:j + 1, sv] = y_all[j:j + 1, j * RET_DV:(j + 1) * RET_DV]
        y_ref[:, sv] = _groupnorm(ybuf[:, sv], nw_ref[:, sv], True) * _silu(g_ref[:, sv])


def _ret_decode(proj, cosf, sinf, nw, state, prev, *, layer):
    nb = proj.shape[0]
    bb = DEC_BB
    nblk = nb // bb
    kern, steps, ib, ob, al_args, al_specs, al_map = _carried_call(_ret_decode_kernel, nblk, layer, prev)
    sblk = (bb, RET_H, RET_DK, RET_DV)
    return pl.pallas_call(
        kern,
        grid=(steps,),
        in_specs=al_specs + [
            pl.BlockSpec((bb, 512), lambda i: (ib(i), C_Q // 512)),
            pl.BlockSpec((bb, 512), lambda i: (ib(i), C_K // 512)),
            pl.BlockSpec((bb, 1024), lambda i: (ib(i), C_V // 1024)),
            pl.BlockSpec((bb, 1024), lambda i: (ib(i), C_G // 1024)),
            pl.BlockSpec((1, RET_DK), lambda i: (0, 0)),
            pl.BlockSpec((1, RET_DK), lambda i: (0, 0)),
            _lsel(layer, 1, 1024),
            pl.BlockSpec(sblk, lambda i: (layer * nblk + ib(i), 0, 0, 0))],
        out_specs=[pl.BlockSpec((bb, 1024), lambda i: (ib(i), 0)),
                   pl.BlockSpec(sblk, lambda i: (ob(i), 0, 0, 0))],
        out_shape=[jax.ShapeDtypeStruct((nb, RET_H * RET_DV), F32),
                   jax.ShapeDtypeStruct((DEPTH * nb, RET_H, RET_DK, RET_DV), F32)],
        scratch_shapes=[pltpu.VMEM((bb, RET_H * RET_DV), F32)],
        input_output_aliases=al_map,
        compiler_params=_cparams(1),
        name="ret_decode",
    )(*al_args, proj, proj, proj, proj, cosf, sinf, nw, state)


def _conv_step(x, buf_ref, cw_ref, cb_ref, bo_ref):
    acc = cb_ref[...]
    for j in range(CONV_K - 1):
        acc = acc + buf_ref[j] * cw_ref[j:j + 1, :]
    acc = acc + x * cw_ref[CONV_K - 1:CONV_K, :]
    bo_ref[0] = buf_ref[1]
    bo_ref[1] = buf_ref[2]
    bo_ref[2] = x
    return _silu(acc)


def _ml_decode_kernel(u_ref, v_ref, o_ref, sm_ref, buf_ref, cw_ref, cb_ref, wq_ref, wk_ref, gb_ref, nw_ref,
                      c_ref, n_ref, m_ref,
                      y_ref, co_ref, no_ref, mo_ref, bo_ref, ybuf):
    uc = _conv_step(u_ref[...], buf_ref, cw_ref, cb_ref, bo_ref).astype(BF16)
    gates = sm_ref[...] + gb_ref[...]
    lane4 = lax.broadcasted_iota(jnp.int32, m_ref.shape, 1)
    m_all = m_ref[...]
    m_out = m_all
    for h in range(ML_H):
        sl = slice(h * ML_DH, (h + 1) * ML_DH)
        qm = _dot(uc[:, sl], wq_ref[h])
        km = _dot(uc[:, sl], wk_ref[h]) * (ML_DH ** -0.5)
        i_pre = gates[:, L_IG + h:L_IG + h + 1]
        log_f = _log_sigmoid(gates[:, L_FG + h:L_FG + h + 1])
        m_prev = m_all[:, h:h + 1]
        inter = log_f + m_prev
        m_new = jnp.maximum(inter, i_pre)
        w_s = jnp.exp(i_pre - m_new)
        w_prev = jnp.exp(inter - m_new)
        kw = km * w_s
        n_new = n_ref[:, sl] * w_prev + kw
        no_ref[:, sl] = n_new
        m_out = jnp.where(lane4 == h, m_new, m_out)

        k_t = _pad_rows_t(kw).astype(BF16)
        q16 = _pad_rows(qm, 16).astype(BF16)
        upd = _dot(k_t, _blockdiag_rows(v_ref[:, sl]))
        c_wide = []
        for j in range(DEC_BB):
            c_new = c_ref[j, h] * w_prev[j:j + 1, :] + upd[:, j * ML_DH:(j + 1) * ML_DH]
            co_ref[j, h] = c_new
            c_wide.append(c_new.astype(BF16))
        y_all = _dot(q16, jnp.concatenate(c_wide, axis=1))
        for j in range(DEC_BB):
            ybuf[j:j + 1, sl] = y_all[j:j + 1, j * ML_DH:(j + 1) * ML_DH]
        den = jnp.sum(qm * n_new, axis=-1, keepdims=True)
        den = jnp.maximum(jnp.abs(den), jnp.exp(-m_new))
        y_ref[:, sl] = _groupnorm(ybuf[:, sl] / den, nw_ref[:, sl], True) * _sigmoid(o_ref[:, sl])
    mo_ref[...] = m_out


def _ml_decode(proj, small, buf, cw, cb, wq, wk, gb, nw, c_state, n_state, m_state, prev, *, layer):
    nb = proj.shape[0]
    bb = DEC_BB
    nblk = nb // bb
    dh = ML_DH
    w = ML_H * ML_DH
    kern, steps, ib, ob, al_args, al_specs, al_map = _carried_call(_ml_decode_kernel, nblk, layer, prev)
    cblk = (bb, ML_H, dh, dh)
    return pl.pallas_call(
        kern,
        grid=(steps,),
        in_specs=al_specs + [
            pl.BlockSpec((bb, w), lambda i: (ib(i), C_U // w)),
            pl.BlockSpec((bb, w), lambda i: (ib(i), C_VM // w)),
            pl.BlockSpec((bb, w), lambda i: (ib(i), C_OM // w)),
            pl.BlockSpec((bb, N_SMALL), lambda i: (ib(i), 0)),
            pl.BlockSpec((None, CONV_K - 1, bb, w), lambda i: (layer, 0, ib(i), 0)),
            _lsel(layer, CONV_K, w),
            _lsel(layer, 1, w),
            _lsel(layer, ML_H, dh, dh),
            _lsel(layer, ML_H, dh, dh),
            _lsel(layer, 1, N_SMALL),
            _lsel(layer, 1, w),
            pl.BlockSpec(cblk, lambda i: (layer * nblk + ib(i), 0, 0, 0)),
            pl.BlockSpec((bb, w), lambda i: (layer * nblk + ib(i), 0)),
            pl.BlockSpec((bb, ML_H), lambda i: (layer * nblk + ib(i), 0))],
        out_specs=[pl.BlockSpec((bb, w), lambda i: (ib(i), 0)),
                   pl.BlockSpec(cblk, lambda i: (ob(i), 0, 0, 0)),
                   pl.BlockSpec((bb, w), lambda i: (ib(i), 0)),
                   pl.BlockSpec((bb, ML_H), lambda i: (ib(i), 0)),
                   pl.BlockSpec((CONV_K - 1, bb, w), lambda i: (0, ib(i), 0))],
        out_shape=[jax.ShapeDtypeStruct((nb, w), F32),
                   jax.ShapeDtypeStruct((DEPTH * nb, ML_H, dh, dh), F32),
                   jax.ShapeDtypeStruct((nb, w), F32),
                   jax.ShapeDtypeStruct((nb, ML_H), F32),
                   jax.ShapeDtypeStruct((CONV_K - 1, nb, w), F32)],
        scratch_shapes=[pltpu.VMEM((bb, w), F32)],
        input_output_aliases=al_map,
        compiler_params=_cparams(1),
        name="ml_decode",
    )(*al_args, proj, proj, proj, small, buf, cw, cb, wq, wk, gb, nw, c_state, n_state, m_state)


def _ssd_decode_kernel(z_ref, xs_ref, bc_ref, sm_ref, buf_ref, cw_ref, cb_ref, dtb_ref, alog_ref, dv_ref, nw_ref,
                       s_ref, y_ref, so_ref, bo_ref, ybuf):
    wx = SSM_H * SSM_P
    gw = SSM_R * SSM_P
    x_in = jnp.concatenate([xs_ref[...], bc_ref[...]], axis=1)
    xc = _conv_step(x_in, buf_ref, cw_ref, cb_ref, bo_ref)
    xs = xc[:, 0:wx]

    delta = _softplus(sm_ref[...] + dtb_ref[...])
    d_a = jnp.exp(delta * (-jnp.exp(alog_ref[...])))
    lane_blk = lax.shift_right_logical(lax.broadcasted_iota(jnp.int32, (DEC_BB, wx), 1), 6)
    dt_full = jnp.zeros((DEC_BB, wx), F32)
    for hh in range(SSM_H):
        dt_full = jnp.where(lane_blk == hh, delta[:, L_DT + hh:L_DT + hh + 1], dt_full)
    x_t = _pad_rows_t(xs * dt_full).astype(BF16)

    for g in range(SSM_G):
        gs = slice(g * gw, (g + 1) * gw)
        b_pad = _pad_rows(xc[:, wx + g * SSM_N:wx + (g + 1) * SSM_N], 128)
        c16 = _pad_rows(xc[:, wx + (SSM_G + g) * SSM_N:wx + (SSM_G + g + 1) * SSM_N], 16).astype(BF16)
        for j in range(DEC_BB):
            upd = _dot(x_t[gs, :], _row_mask(b_pad, j))
            parts = []
            for r in range(SSM_R):
                hh = g * SSM_R + r
                s_new = (s_ref[j, hh] * d_a[j:j + 1, L_DT + hh:L_DT + hh + 1]
                         + upd[r * SSM_P:(r + 1) * SSM_P])
                so_ref[j, hh] = s_new
                parts.append(s_new.astype(BF16))
            ybuf[j:j + 1, gs] = _dot_nt(c16, jnp.concatenate(parts, axis=0))[j:j + 1, :]
        y = ybuf[:, gs] + dv_ref[:, gs] * xs[:, gs]
        y_ref[:, gs] = _groupnorm(y * _silu(z_ref[:, gs]), nw_ref[:, gs], False)


def _ssd_decode(proj, small, buf, cw, cb, dtb, alog, dvec, nw, state, prev, *, layer):
    nb = proj.shape[0]
    bb = DEC_BB
    nblk = nb // bb
    wx = SSM_H * SSM_P
    kern, steps, ib, ob, al_args, al_specs, al_map = _carried_call(_ssd_decode_kernel, nblk, layer, prev)
    sblk = (bb, SSM_H, SSM_P, SSM_N)
    return pl.pallas_call(
        kern,
        grid=(steps,),
        in_specs=al_specs + [
            pl.BlockSpec((bb, wx), lambda i: (ib(i), C_Z // wx)),
            pl.BlockSpec((bb, wx), lambda i: (ib(i), C_XS // wx)),
            pl.BlockSpec((bb, wx), lambda i: (ib(i), C_B // wx)),
            pl.BlockSpec((bb, N_SMALL), lambda i: (ib(i), 0)),
            pl.BlockSpec((None, CONV_K - 1, bb, 2 * wx), lambda i: (layer, 0, ib(i), 0)),
            _lsel(layer, CONV_K, 2 * wx),
            _lsel(layer, 1, 2 * wx),
            _lsel(layer, 1, N_SMALL),
            _lsel(layer, 1, N_SMALL),
            _lsel(layer, 1, wx),
            _lsel(layer, 1, wx),
            pl.BlockSpec(sblk, lambda i: (layer * nblk + ib(i), 0, 0, 0))],
        out_specs=[pl.BlockSpec((bb, wx), lambda i: (ib(i), 0)),
                   pl.BlockSpec(sblk, lambda i: (ob(i), 0, 0, 0)),
                   pl.BlockSpec((CONV_K - 1, bb, 2 * wx), lambda i: (0, ib(i), 0))],
        out_shape=[jax.ShapeDtypeStruct((nb, wx), F32),
                   jax.ShapeDtypeStruct((DEPTH * nb, SSM_H, SSM_P, SSM_N), F32),
                   jax.ShapeDtypeStruct((CONV_K - 1, nb, 2 * wx), F32)],
        scratch_shapes=[pltpu.VMEM((bb, wx), F32)],
        input_output_aliases=al_map,
        compiler_params=_cparams(1),
        name="ssd_decode",
    )(*al_args, proj, proj, proj, small, buf, cw, cb, dtb, alog, dvec, nw, state)


def _rope_tables(pos):
    half = RET_DK // 2
    freqs = ROPE_THETA ** (-jnp.arange(half, dtype=F32) / half)
    ang = pos.astype(F32)[:, None] * freqs[None, :]
    cos = jnp.cos(ang)
    sin = jnp.sin(ang)
    return jnp.concatenate([cos, cos], axis=1), jnp.concatenate([-sin, sin], axis=1)


def _pad_lanes(v, offset):
    n = v.shape[1]
    return jnp.pad(v.astype(F32), ((0, 0), (offset, N_SMALL - offset - n))).reshape(DEPTH, 1, N_SMALL)


REPACK_TN = 1024
REPACK_RUNS = ((0, 6, 0), (6, 9, 2 * ML_H), (9, 12, 2 * ML_H + SSM_H))
SRC_GATE_BLK = 6144 // 128
SRC_DT_BLK = (6144 + 2 * ML_H + 3072) // 128


def _repack_kernel(a_ref, b_ref, g_ref, dt_ref, o_ref, os_ref):
    j = pl.program_id(1)

    @pl.when(j == 0)
    def _():
        row = lax.broadcasted_iota(jnp.int32, (N_SMALL, D_MODEL), 0)
        os_ref[...] = jnp.where(row < L_DT, g_ref[...],
                                jnp.where(row < L_DT + SSM_H, dt_ref[...], 0.0)).astype(BF16)

    for lo, hi, skip in REPACK_RUNS:
        @pl.when((j >= lo) & (j < hi))
        def _(skip=skip):
            src = a_ref[...] if skip == 0 else jnp.concatenate([a_ref[skip:, :], b_ref[0:skip, :]], axis=0)
            o_ref[...] = src.astype(BF16)


def _repack_w_in(w_in_t):
    tn = REPACK_TN
    return pl.pallas_call(
        _repack_kernel,
        grid=(DEPTH, N_MAIN // tn),
        in_specs=[pl.BlockSpec((None, tn, D_MODEL), lambda l, j: (l, j, 0)),
                  pl.BlockSpec((None, 128, D_MODEL), lambda l, j: (l, (j + 1) * (tn // 128), 0)),
                  pl.BlockSpec((None, 128, D_MODEL), lambda l, j: (l, SRC_GATE_BLK, 0)),
                  pl.BlockSpec((None, 128, D_MODEL), lambda l, j: (l, SRC_DT_BLK, 0))],
        out_specs=[pl.BlockSpec((None, tn, D_MODEL), lambda l, j: (l, j, 0)),
                   pl.BlockSpec((None, N_SMALL, D_MODEL), lambda l, j: (l, 0, 0))],
        out_shape=[jax.ShapeDtypeStruct((DEPTH, N_MAIN, D_MODEL), BF16),
                   jax.ShapeDtypeStruct((DEPTH, N_SMALL, D_MODEL), BF16)],
        compiler_params=_cparams(2),
        name="repack_w_in",
    )(w_in_t, w_in_t, w_in_t, w_in_t)


def _prep_weights(W):
    w_main, w_small = _repack_w_in(jnp.swapaxes(W['w_in'], 1, 2))
    row = lambda v: v.reshape(DEPTH, 1, -1).astype(F32)
    bf = lambda v: v.astype(BF16)
    return dict(
        w_main=w_main, w_small=w_small,
        norm_mix=row(W['norm_mix_w']),
        ret_norm=row(W['ret_norm_w']),
        ml_cw=W['ml_conv_w'], ml_cb=row(W['ml_conv_b']),
        ml_wq=bf(W['ml_wq']), ml_wk=bf(W['ml_wk']),
        ml_gb=_pad_lanes(W['ml_gate_b'], L_IG),
        ml_norm=row(W['ml_norm_w']),
        ssm_cw=W['ssm_conv_w'], ssm_cb=row(W['ssm_conv_b']),
        ssm_dtb=_pad_lanes(W['ssm_dt_bias'], L_DT),
        ssm_alog=_pad_lanes(W['ssm_A_log'], L_DT),
        ssm_dvec=row(jnp.repeat(W['ssm_D'], SSM_P, axis=1)),
        ssm_norm=row(W['ssm_norm_w']),
        w_br_ret=bf(W['w_br_ret']), w_br_ml=bf(W['w_br_ml']),
        w_br_ssm=bf(W['w_br_ssm']), w_out=bf(W['w_out_mix']),
        norm_mem=row(W['norm_mem_w']),
        mem_wq=bf(W['mem_wq']), mem_wo=bf(W['mem_wo']),
        norm_mlp=row(W['norm_mlp_w']),
        mlp_w1=bf(W['mlp_w1']), mlp_w2=bf(W['mlp_w2']),
    )


def kernel(x_prompt, x_sample, mem_prompt, state_ret, state_mlstm_C, state_mlstm_n, state_mlstm_m,
           state_mlstm_conv, state_ssm, state_ssm_conv, cache_mem_k, cache_mem_v,
           norm_mix_w, w_in, ret_norm_w, ml_conv_w, ml_conv_b, ml_wq, ml_wk, ml_gate_b, ml_norm_w,
           ssm_conv_w, ssm_conv_b, ssm_dt_bias, ssm_A_log, ssm_D, ssm_norm_w,
           w_br_ret, w_br_ml, w_br_ssm, w_out_mix, norm_mem_w, mem_wq, mem_wk, mem_wv, mem_wo,
           norm_mlp_w, mlp_w1, mlp_w2, norm_f_w):
    W = dict(norm_mix_w=norm_mix_w, w_in=w_in, ret_norm_w=ret_norm_w, ml_conv_w=ml_conv_w,
             ml_conv_b=ml_conv_b, ml_wq=ml_wq, ml_wk=ml_wk, ml_gate_b=ml_gate_b, ml_norm_w=ml_norm_w,
             ssm_conv_w=ssm_conv_w, ssm_conv_b=ssm_conv_b, ssm_dt_bias=ssm_dt_bias, ssm_A_log=ssm_A_log,
             ssm_D=ssm_D, ssm_norm_w=ssm_norm_w, w_br_ret=w_br_ret, w_br_ml=w_br_ml, w_br_ssm=w_br_ssm,
             w_out_mix=w_out_mix, norm_mem_w=norm_mem_w, mem_wq=mem_wq, mem_wo=mem_wo,
             norm_mlp_w=norm_mlp_w, mlp_w1=mlp_w1, mlp_w2=mlp_w2)
    lw = _prep_weights(W)
    norm_f = norm_f_w.reshape(1, -1).astype(F32)
    d = D_MODEL

    nb, seq = x_prompt.shape[0], x_prompt.shape[1]
    ns = x_sample.shape[0]
    cos_p, sin_p = _rope_tables(jnp.arange(seq, dtype=jnp.int32))
    memk, memv, memk_b, memv_b = _memkv(mem_prompt.reshape(nb * MEM_LEN, d), mem_wk.astype(BF16),
                                        mem_wv.astype(BF16), tm=1024)
    memk3 = memk_b.reshape(DEPTH * nb, MEM_LEN, d)
    memv3 = memv_b.reshape(DEPTH * nb, MEM_LEN, d)

    x = x_prompt.reshape(nb * seq, d)
    states_p = []
    for l in range(DEPTH):
        zs = (lambda shape: shape) if l == 0 else (lambda shape: None)
        proj, small, *mlc_s = _inproj(x, lw['norm_mix'], lw['w_main'], lw['w_small'], layer=l, tm=2048, tn=1024,
                                      out_dtype=BF16, zero_shape=zs((DEPTH * ns, ML_H, ML_DH, ML_DH)))
        y_r, ret_st, *ret_s = _ret_prefill(proj, cos_p, sin_p, nb=nb, seq=seq,
                                           zero_shape=zs((DEPTH * ns, RET_H, RET_DK, RET_DV)))
        y_m, ml_c, ml_n, ml_m, ml_cv = _ml_prefill(proj, small, lw['ml_cw'], lw['ml_cb'], lw['ml_wq'], lw['ml_wk'],
                                                   lw['ml_gb'], layer=l, nb=nb, seq=seq)
        y_s, ssm_st, ssm_cv, *ssm_s = _ssd_prefill(proj, small, lw['ssm_cw'], lw['ssm_cb'], lw['ssm_dtb'],
                                                   lw['ssm_alog'], lw['ssm_dvec'], layer=l, nb=nb,
                                                   seq=seq, zero_shape=zs((DEPTH * ns, SSM_H, SSM_P, SSM_N)))
        x, qm = _merge(y_r.reshape(nb * seq, d), y_m.reshape(nb * seq, d), y_s.reshape(nb * seq, d), proj,
                       x, lw['w_br_ret'], lw['w_br_ml'], lw['w_br_ssm'], lw['w_out'], lw['norm_mem'],
                       lw['mem_wq'], (lw['ret_norm'], lw['ml_norm'], lw['ssm_norm']), layer=l, tm=512,
                       q_dtype=BF16)
        if l == 0:
            zero_states = (ret_s[0], mlc_s[0], ssm_s[0])
        x = _attn_prefill(qm, memk3, memv3, x, lw['mem_wo'], layer=l, nb=nb, seq=seq, tq=1024)
        x = _mlp(x, lw['norm_mlp'], lw['mlp_w1'], lw['mlp_w2'], norm_f, layer=l, tm=1024, tf=2048,
                 final_norm=(l == DEPTH - 1))
        states_p.append((ret_st, ml_c, ml_n, ml_m[:, 0, :ML_H], ml_cv, ssm_st, ssm_cv))
    y_prompt = x.reshape(nb, seq, d)
    stack = lambda parts, i: jnp.stack([p[i] for p in parts])
    prompt_states = tuple(stack(states_p, i) for i in range(7))
    memk_p = memk.reshape(DEPTH, nb, MEM_LEN, MEM_H, MEM_DH)
    memv_p = memv.reshape(DEPTH, nb, MEM_LEN, MEM_H, MEM_DH)

    cos_s, sin_s = _rope_tables(PAST_LEN + jnp.arange(1, dtype=jnp.int32))
    ret_state = state_ret.reshape(DEPTH * ns, RET_H, RET_DK, RET_DV)
    mlc_state = state_mlstm_C.reshape(DEPTH * ns, ML_H, ML_DH, ML_DH)
    mln_state = state_mlstm_n.reshape(DEPTH * ns, ML_H * ML_DH)
    mlm_state = state_mlstm_m.reshape(DEPTH * ns, ML_H)
    ssm_st = state_ssm.reshape(DEPTH * ns, SSM_H, SSM_P, SSM_N)
    ck = cache_mem_k.reshape(DEPTH * ns, MEM_LEN, MEM_H, MEM_DH)
    cv = cache_mem_v.reshape(DEPTH * ns, MEM_LEN, MEM_H, MEM_DH)

    ml_buf = jnp.swapaxes(state_mlstm_conv, 1, 2)
    ssm_buf = jnp.swapaxes(state_ssm_conv, 1, 2)

    x = x_sample.reshape(ns, d)
    ret_s, mlc_s, ssm_s = zero_states
    small_s = []
    for l in range(DEPTH):
        proj, small = _inproj(x, lw['norm_mix'], lw['w_main'], lw['w_small'], layer=l, tm=ns, tn=1024, out_dtype=F32)
        y_r, ret_s = _ret_decode(proj, cos_s, sin_s, lw['ret_norm'], ret_state, ret_s,
                                 layer=l)
        y_m, mlc_s, ml_n, ml_m, ml_bo = _ml_decode(proj, small, ml_buf, lw['ml_cw'], lw['ml_cb'], lw['ml_wq'],
                                                   lw['ml_wk'], lw['ml_gb'], lw['ml_norm'],
                                                   mlc_state, mln_state, mlm_state, mlc_s,
                                                   layer=l)
        y_s, ssm_s, ssm_bo = _ssd_decode(proj, small, ssm_buf, lw['ssm_cw'], lw['ssm_cb'], lw['ssm_dtb'],
                                         lw['ssm_alog'], lw['ssm_dvec'], lw['ssm_norm'], ssm_st,
                                         ssm_s, layer=l)
        x, qm = _merge(y_r, y_m, y_s, proj, x, lw['w_br_ret'], lw['w_br_ml'], lw['w_br_ssm'], lw['w_out'],
                       lw['norm_mem'], lw['mem_wq'], layer=l, tm=ns, q_dtype=F32)
        att = _attn_decode(qm.reshape(ns, MEM_H, MEM_DH), ck, cv, layer=l, bb=4)
        x = _mm(att.reshape(ns, d), lw['mem_wo'], x, layer=l, tm=ns, tn=d)
        x = _mlp(x, lw['norm_mlp'], lw['mlp_w1'], lw['mlp_w2'], norm_f, layer=l, tm=ns, tf=2048,
                 final_norm=(l == DEPTH - 1))
        small_s.append((ml_n.reshape(ns, ML_H, ML_DH), ml_m, jnp.swapaxes(ml_bo, 0, 1),
                        jnp.swapaxes(ssm_bo, 0, 1)))
    y_sample = x.reshape(ns, 1, d)
    sample_states = (ret_s.reshape(DEPTH, ns, RET_H, RET_DK, RET_DV),
                     mlc_s.reshape(DEPTH, ns, ML_H, ML_DH, ML_DH),
                     stack(small_s, 0), stack(small_s, 1), stack(small_s, 2),
                     ssm_s.reshape(DEPTH, ns, SSM_H, SSM_P, SSM_N),
                     stack(small_s, 3))

    return (y_prompt, y_sample, *prompt_states, memk_p, memv_p, *sample_states)
```

```python
import functools
import math

import jax
import jax.numpy as jnp
from jax import lax
from jax.experimental import pallas as pl
from jax.experimental.pallas import tpu as pltpu

F32 = jnp.float32
BF16 = jnp.bfloat16

D_MODEL = 1024
DEPTH = 2
PAST_LEN = 16384
CHUNK = 128
CONV_K = 4
EPS = 1e-6
RET_H, RET_DK, RET_DV = 4, 128, 256
ROPE_THETA = 10000.0
ML_H, ML_DH = 4, 256
SSM_H, SSM_P, SSM_G, SSM_N = 16, 64, 4, 128
SSM_R = SSM_H // SSM_G
MEM_LEN, MEM_H, MEM_DH = 256, 4, 256
D_FF = 4 * D_MODEL

C_Q, C_K, C_V, C_G = 0, 512, 1024, 2048
C_U, C_VM, C_OM = 3072, 4096, 5120
C_Z, C_XS, C_B, C_C = 6144, 7168, 8192, 8704
C_GR, C_GM, C_GS = 9216, 10240, 11264
N_MAIN = 12288
L_IG, L_FG, L_DT = 0, 4, 8
N_SMALL = 128

VMEM_LIMIT = 56 * 1024 * 1024
NT_DIMS = (((1,), (1,)), ((), ()))

_LOG_G = [math.log1p(-(2.0 ** (-5.0 - h))) for h in range(RET_H)]


def _cparams(n_axes):
    return pltpu.CompilerParams(dimension_semantics=("arbitrary",) * n_axes,
                                vmem_limit_bytes=VMEM_LIMIT)


def _dot(a, b):
    return jnp.dot(a, b, preferred_element_type=F32)


def _dot_nt(a, b):
    return lax.dot_general(a, b, NT_DIMS, preferred_element_type=F32)


def _rmsnorm(x, w):
    ms = jnp.mean(x * x, axis=-1, keepdims=True)
    return x * lax.rsqrt(ms + EPS) * w


def _groupnorm(x, w, center):
    if center:
        x = x - jnp.mean(x, axis=-1, keepdims=True)
    ms = jnp.mean(x * x, axis=-1, keepdims=True)
    return x * lax.rsqrt(ms + EPS) * w


def _sigmoid(x):
    return 0.5 * jnp.tanh(0.5 * x) + 0.5


def _silu(x):
    return x * _sigmoid(x)


def _softplus(x):
    return jnp.maximum(x, 0.0) + jnp.log1p(jnp.exp(-jnp.abs(x)))


def _log_sigmoid(x):
    return -_softplus(-x)


def _cumsum_rows(x, tril_b):
    hi = x.astype(BF16)
    r1 = x - hi.astype(F32)
    mid = r1.astype(BF16)
    lo = (r1 - mid.astype(F32)).astype(BF16)
    return _dot(tril_b, hi) + _dot(tril_b, mid) + _dot(tril_b, lo)


def _pad_rows(x, rows):
    return jnp.concatenate([x, jnp.zeros((rows - x.shape[0], x.shape[1]), x.dtype)], axis=0)


def _pad_rows_t(x):
    return _pad_rows(x, 128).T


def _carried_call(kernel_fn, nblk, layer, prev):
    return (lambda *refs: kernel_fn(*refs[1:]), nblk, lambda i: i, lambda i: layer * nblk + i,
            [prev], [pl.BlockSpec(memory_space=pl.ANY)], {0: 1})


def _zero_out(shape, steps):
    blk = (shape[0] // steps,) + tuple(shape[1:])
    nz = len(shape) - 1
    return pl.BlockSpec(blk, lambda i: (i,) + (0,) * nz), jax.ShapeDtypeStruct(tuple(shape), F32)


def _zero_fill(zero_refs):
    for z_ref in zero_refs:
        z_ref[...] = jnp.zeros_like(z_ref)


def _inproj_kernel(x_ref, nw_ref, w_ref, ws_ref, o_ref, os_ref, *rest):
    *zero_refs, xn_ref = rest
    _zero_fill(zero_refs)

    @pl.when(pl.program_id(1) == 0)
    def _():
        xn = _rmsnorm(x_ref[...], nw_ref[...]).astype(BF16)
        xn_ref[...] = xn
        os_ref[...] = _dot_nt(xn, ws_ref[...])

    o_ref[...] = _dot_nt(xn_ref[...], w_ref[...]).astype(o_ref.dtype)


def _lsel(layer, *block, **kw):
    return pl.BlockSpec((None,) + block, lambda *_: (layer,) + (0,) * len(block), **kw)


def _inproj(x, nw, w_main, w_small, *, layer, tm, tn, out_dtype, zero_shape=None):
    m = x.shape[0]
    z_specs, z_shapes = [], []
    if zero_shape is not None:
        n_i, n_j = m // tm, N_MAIN // tn
        per_i = max(p for p in range(1, n_j + 1) if (zero_shape[0] // n_i) % p == 0)
        blk = (zero_shape[0] // (n_i * per_i),) + tuple(zero_shape[1:])
        z_specs = [pl.BlockSpec(blk, lambda i, j: (i * per_i + jnp.minimum(j, per_i - 1), 0, 0, 0))]
        z_shapes = [jax.ShapeDtypeStruct(tuple(zero_shape), F32)]
    return pl.pallas_call(
        _inproj_kernel,
        grid=(m // tm, N_MAIN // tn),
        in_specs=[pl.BlockSpec((tm, D_MODEL), lambda i, j: (i, 0)),
                  _lsel(layer, 1, D_MODEL),
                  pl.BlockSpec((None, tn, D_MODEL), lambda i, j: (layer, j, 0)),
                  _lsel(layer, N_SMALL, D_MODEL)],
        out_specs=[pl.BlockSpec((tm, tn), lambda i, j: (i, j)),
                   pl.BlockSpec((tm, N_SMALL), lambda i, j: (i, 0))] + z_specs,
        out_shape=[jax.ShapeDtypeStruct((m, N_MAIN), out_dtype),
                   jax.ShapeDtypeStruct((m, N_SMALL), F32)] + z_shapes,
        scratch_shapes=[pltpu.VMEM((tm, D_MODEL), BF16)],
        compiler_params=_cparams(2),
        name="inproj",
    )(x, nw, w_main, w_small)


def _mm_kernel(*refs, has_res):
    if has_res:
        a_ref, w_ref, r_ref, o_ref = refs
    else:
        a_ref, w_ref, o_ref = refs
    acc = _dot(a_ref[...].astype(BF16), w_ref[...])
    if has_res:
        acc = r_ref[...] + acc
    o_ref[...] = acc.astype(o_ref.dtype)


def _mm(a, w, res=None, *, layer, tm, tn, out_dtype=F32):
    m, k = a.shape
    n = w.shape[2]
    in_specs = [pl.BlockSpec((tm, k), lambda i, j: (i, 0)),
                pl.BlockSpec((None, k, tn), lambda i, j: (layer, 0, j))]
    args = [a, w]
    if res is not None:
        in_specs.append(pl.BlockSpec((tm, tn), lambda i, j: (i, j)))
        args.append(res)
    return pl.pallas_call(
        functools.partial(_mm_kernel, has_res=res is not None),
        grid=(m // tm, n // tn),
        in_specs=in_specs,
        out_specs=pl.BlockSpec((tm, tn), lambda i, j: (i, j)),
        out_shape=jax.ShapeDtypeStruct((m, n), out_dtype),
        compiler_params=_cparams(2),
        name="mm",
    )(*args)


def _memkv_kernel(a_ref, wk_ref, wv_ref, ok_ref, ov_ref, kb_ref, vb_ref):
    a = a_ref[...].astype(BF16)
    for w_ref, o_ref, b_ref in ((wk_ref, ok_ref, kb_ref), (wv_ref, ov_ref, vb_ref)):
        acc = _dot(a, w_ref[...])
        b_ref[...] = acc.astype(BF16)
        for h in range(MEM_H):
            o_ref[:, h, :] = acc[:, h * MEM_DH:(h + 1) * MEM_DH]


def _memkv(a, wk, wv, *, tm):
    m = a.shape[0]
    n = wk.shape[2]
    return pl.pallas_call(
        _memkv_kernel,
        grid=(DEPTH, m // tm),
        in_specs=[pl.BlockSpec((tm, D_MODEL), lambda l, i: (i, 0)),
                  pl.BlockSpec((None, D_MODEL, n), lambda l, i: (l, 0, 0)),
                  pl.BlockSpec((None, D_MODEL, n), lambda l, i: (l, 0, 0))],
        out_specs=[pl.BlockSpec((None, tm, MEM_H, MEM_DH), lambda l, i: (l, i, 0, 0)),
                   pl.BlockSpec((None, tm, MEM_H, MEM_DH), lambda l, i: (l, i, 0, 0)),
                   pl.BlockSpec((None, tm, n), lambda l, i: (l, i, 0)),
                   pl.BlockSpec((None, tm, n), lambda l, i: (l, i, 0))],
        out_shape=[jax.ShapeDtypeStruct((DEPTH, m, MEM_H, MEM_DH), F32)] * 2
        + [jax.ShapeDtypeStruct((DEPTH, m, n), BF16)] * 2,
        compiler_params=_cparams(2),
        name="memkv",
    )(a, wk, wv)


class _Ctx(dict):
    __getattr__ = dict.__getitem__
    __setattr__ = dict.__setitem__


def _seq_loop(nb, per_trip, seq_phases, n_items, item_phases, finish):
    def body(i, carry):
        seqs = [_Ctx(b=i * per_trip + u) for u in range(per_trip)]
        for phase in seq_phases:
            for s in seqs:
                phase(s)
        items = []
        for s in seqs:
            s.heads = [_Ctx(seq=s, b=s.b, k=k) for k in range(n_items)]
            items.extend(s.heads)
        for phase in item_phases:
            for it in items:
                phase(it)
        for s in seqs:
            finish(s)
        return carry

    lax.fori_loop(0, nb // per_trip, body, 0)


def _ret_prefill_kernel(q_ref, k_ref, v_ref, cos_ref, sin_ref, y_ref, s_ref, *zero_refs):
    _zero_fill(zero_refs)

    @pl.when(pl.program_id(0) == 0)
    def _():
        s_ref[...] = jnp.zeros_like(s_ref)

    c = CHUNK
    nb = q_ref.shape[0]
    cosf = cos_ref[...]
    sinf = sin_ref[...]
    row = lax.broadcasted_iota(jnp.int32, (c, c), 0)
    col = lax.broadcasted_iota(jnp.int32, (c, c), 1)
    causal = row >= col
    diff = jnp.where(causal, (row - col).astype(F32), 0.0)
    rowf = row.astype(F32)
    rowf_v = lax.broadcasted_iota(jnp.int32, (c, RET_DV), 0).astype(F32)
    decay = [jnp.where(causal, jnp.exp(lg * diff), 0.0) for lg in _LOG_G]
    q_decay = [jnp.exp(lg * (rowf_v + 1.0)) for lg in _LOG_G]
    k_decay = [jnp.exp(lg * (c - 1.0 - rowf)) * (RET_DK ** -0.5) for lg in _LOG_G]

    def load(s):
        s.s_old = [s_ref[s.b, h] for h in range(RET_H)]

    def rotate(it):
        h = it.k
        sk = slice(h * RET_DK, (h + 1) * RET_DK)
        q = q_ref[it.b, :, sk].astype(F32)
        k = k_ref[it.b, :, sk].astype(F32)
        qr = q * cosf + pltpu.roll(q, RET_DK // 2, 1) * sinf
        kr = k * cosf + pltpu.roll(k, RET_DK // 2, 1) * sinf
        it.qb = qr.astype(BF16)
        it.kb = (kr * (RET_DK ** -0.5)).astype(BF16)
        it.kt = (kr * k_decay[h]).T.astype(BF16)
        it.v = v_ref[it.b, :, h * RET_DV:(h + 1) * RET_DV]

    def scores(it):
        it.sc = (_dot_nt(it.qb, it.kb) * decay[it.k]).astype(BF16)

    def mix(it):
        h = it.k
        s_old = it.seq.s_old[h]
        it.y = _dot(it.sc, it.v) + _dot(it.qb, s_old.astype(BF16)) * q_decay[h]
        it.s_new = s_old * math.exp(_LOG_G[h] * c) + _dot(it.kt, it.v)

    def emit(it):
        y_ref[it.b, :, it.k * RET_DV:(it.k + 1) * RET_DV] = it.y.astype(y_ref.dtype)

    def finish(s):
        for it in s.heads:
            s_ref[s.b, it.k] = it.s_new

    _seq_loop(nb, 4, [load], RET_H, [rotate, scores, mix, emit], finish)


def _ret_prefill(proj, cosf, sinf, *, nb, seq, zero_shape=None):
    c = CHUNK
    p3 = proj.reshape(nb, seq, N_MAIN)
    z_specs, z_shapes = ([], []) if zero_shape is None else [[v] for v in _zero_out(zero_shape, seq // c)]
    return pl.pallas_call(
        _ret_prefill_kernel,
        grid=(seq // c,),
        in_specs=[
            pl.BlockSpec((nb, c, 512), lambda t: (0, t, C_Q // 512)),
            pl.BlockSpec((nb, c, 512), lambda t: (0, t, C_K // 512)),
            pl.BlockSpec((nb, c, 1024), lambda t: (0, t, C_V // 1024)),
            pl.BlockSpec((c, RET_DK), lambda t: (t, 0)),
            pl.BlockSpec((c, RET_DK), lambda t: (t, 0))],
        out_specs=[pl.BlockSpec((nb, c, 1024), lambda t: (0, t, 0)),
                   pl.BlockSpec((nb, RET_H, RET_DK, RET_DV), lambda t: (0, 0, 0, 0))] + z_specs,
        out_shape=[jax.ShapeDtypeStruct((nb, seq, 1024), BF16),
                   jax.ShapeDtypeStruct((nb, RET_H, RET_DK, RET_DV), F32)] + z_shapes,
        compiler_params=_cparams(1),
        name="ret_prefill",
    )(p3, p3, p3, cosf, sinf)


def _shift_matrix(first_chunk):
    c = CHUNK
    r = lax.broadcasted_iota(jnp.int32, (3 * c, 2 * c), 0)
    q = lax.broadcasted_iota(jnp.int32, (3 * c, 2 * c), 1)
    k = lax.shift_right_logical(r, 7)
    t = jnp.bitwise_and(r, c - 1)
    hit = (q == c + t - (CONV_K - 1) + k) & (q >= jnp.where(first_chunk, c, 0))
    return jnp.where(hit, 1.0, 0.0).astype(BF16)


def _conv_silu(shift, x_prev, x_cur, cw, cb):
    c = CHUNK
    p = _dot(shift, jnp.concatenate([x_prev, x_cur], axis=0))
    acc = cb
    for j in range(CONV_K - 1):
        acc = acc + p[j * c:(j + 1) * c] * cw[j:j + 1, :]
    acc = acc + x_cur.astype(F32) * cw[CONV_K - 1:CONV_K, :]
    return _silu(acc)


def _ml_prefill_kernel(u_ref, up_ref, v_ref, sm_ref, cw_ref, cb_ref, wq_ref, wk_ref, gb_ref,
                       y_ref, c_ref, n_ref, m_ref, cv_ref, tail):
    c = CHUNK
    nb = u_ref.shape[0]
    first = pl.program_id(0) == 0

    @pl.when(first)
    def _():
        c_ref[...] = jnp.zeros_like(c_ref)
        n_ref[...] = jnp.zeros_like(n_ref)
        m_ref[...] = jnp.zeros_like(m_ref)

    row = lax.broadcasted_iota(jnp.int32, (c, c), 0)
    col = lax.broadcasted_iota(jnp.int32, (c, c), 1)
    causal = row >= col
    tril_b = jnp.where(causal, 1.0, 0.0).astype(BF16)
    lane_m = lax.broadcasted_iota(jnp.int32, (1, 128), 1)
    shift = _shift_matrix(first)

    def load(s):
        s.c_old = [c_ref[s.b, h] for h in range(ML_H)]
        s.n_old = n_ref[s.b]
        s.m_row = m_ref[s.b]

    def gates_phase(s):
        tail[...] = u_ref[s.b, c - 16:c, :].astype(F32)
        cv_ref[s.b] = tail[16 - (CONV_K - 1):16, :]
        s.gates = sm_ref[s.b] + gb_ref[...]
        s.b_all = _cumsum_rows(_log_sigmoid(s.gates), tril_b)
        s.b_all_t = s.b_all.T
        s.gates_t = s.gates.T

    def conv_phase(it):
        sl = slice(it.k * ML_DH, (it.k + 1) * ML_DH)
        it.uc = _conv_silu(shift, up_ref[it.b, :, sl], u_ref[it.b, :, sl], cw_ref[:, sl], cb_ref[:, sl]).astype(BF16)
        it.v = v_ref[it.b, :, sl]

    def weights_phase(it):
        h, s = it.k, it.seq
        b_col = s.b_all[:, L_FG + h:L_FG + h + 1]
        b_row = s.b_all_t[L_FG + h:L_FG + h + 1, :]
        it_col = s.gates[:, L_IG + h:L_IG + h + 1]
        it_row = s.gates_t[L_IG + h:L_IG + h + 1, :]
        m_prev = s.m_row[:, h:h + 1]
        logw = jnp.where(causal, b_col - b_row + it_row, -jnp.inf)
        inter = b_col + m_prev
        it.m_t = jnp.maximum(inter, jnp.max(logw, axis=-1, keepdims=True))
        it.dmat = jnp.exp(logw - it.m_t)
        it.w_int = jnp.exp(inter - it.m_t)
        b_end = b_col[c - 1:c, :]
        logw_s = b_end - b_col + it_col
        it.m_new = jnp.maximum(b_end + m_prev, jnp.max(logw_s, axis=0, keepdims=True))
        it.w_s = jnp.exp(logw_s - it.m_new)
        it.w_prev = jnp.exp(b_end + m_prev - it.m_new)

    def qk_phase(it):
        it.qm = _dot(it.uc, wq_ref[it.k])
        it.km = _dot(it.uc, wk_ref[it.k]) * (ML_DH ** -0.5)
        it.qb = it.qm.astype(BF16)
        it.kb = it.km.astype(BF16)
        it.kw = it.km * it.w_s
        it.kwt = it.kw.T.astype(BF16)

    def scores_phase(it):
        it.sc = _dot_nt(it.qb, it.kb) * it.dmat

    def mix_phase(it):
        c_old = it.seq.c_old[it.k]
        n_old = it.seq.n_old[it.k:it.k + 1, :]
        it.num = _dot(it.sc.astype(BF16), it.v) + _dot(it.qb, c_old.astype(BF16)) * it.w_int
        it.c_new = c_old * it.w_prev + _dot(it.kwt, it.v)
        it.n_new = n_old * it.w_prev + jnp.sum(it.kw, axis=0, keepdims=True)
        den = jnp.sum(it.sc, axis=-1, keepdims=True) + jnp.sum(it.qm * n_old, axis=-1, keepdims=True) * it.w_int
        it.den = jnp.maximum(jnp.abs(den), jnp.exp(-it.m_t))

    def emit_phase(it):
        y_ref[it.b, :, it.k * ML_DH:(it.k + 1) * ML_DH] = (it.num / it.den).astype(y_ref.dtype)

    def finish(s):
        m_row_new = s.m_row
        for it in s.heads:
            c_ref[s.b, it.k] = it.c_new
            n_ref[s.b, it.k:it.k + 1, :] = it.n_new
            m_row_new = jnp.where(lane_m == it.k, it.m_new, m_row_new)
        m_ref[s.b] = m_row_new

    _seq_loop(nb, 1, [load, gates_phase], ML_H,
              [conv_phase, weights_phase, qk_phase, scores_phase, mix_phase, emit_phase], finish)


def _ml_prefill(proj, small, cw, cb, wq, wk, gb, *, layer, nb, seq):
    c = CHUNK
    p3 = proj.reshape(nb, seq, N_MAIN)
    s3 = small.reshape(nb, seq, N_SMALL)
    w = ML_H * ML_DH
    return pl.pallas_call(
        _ml_prefill_kernel,
        grid=(seq // c,),
        in_specs=[
            pl.BlockSpec((nb, c, w), lambda t: (0, t, C_U // w)),
            pl.BlockSpec((nb, c, w), lambda t: (0, jnp.maximum(t - 1, 0), C_U // w)),
            pl.BlockSpec((nb, c, w), lambda t: (0, t, C_VM // w)),
            pl.BlockSpec((nb, c, N_SMALL), lambda t: (0, t, 0)),
            _lsel(layer, CONV_K, w),
            _lsel(layer, 1, w),
            _lsel(layer, ML_H, ML_DH, ML_DH),
            _lsel(layer, ML_H, ML_DH, ML_DH),
            _lsel(layer, 1, N_SMALL)],
        out_specs=[pl.BlockSpec((nb, c, w), lambda t: (0, t, 0)),
                   pl.BlockSpec((nb, ML_H, ML_DH, ML_DH), lambda t: (0, 0, 0, 0)),
                   pl.BlockSpec((nb, ML_H, ML_DH), lambda t: (0, 0, 0)),
                   pl.BlockSpec((nb, 1, 128), lambda t: (0, 0, 0)),
                   pl.BlockSpec((nb, CONV_K - 1, w), lambda t: (0, 0, 0))],
        out_shape=[jax.ShapeDtypeStruct((nb, seq, w), BF16),
                   jax.ShapeDtypeStruct((nb, ML_H, ML_DH, ML_DH), F32),
                   jax.ShapeDtypeStruct((nb, ML_H, ML_DH), F32),
                   jax.ShapeDtypeStruct((nb, 1, 128), F32),
                   jax.ShapeDtypeStruct((nb, CONV_K - 1, w), F32)],
        scratch_shapes=[pltpu.VMEM((16, w), F32)],
        compiler_params=_cparams(1),
        name="ml_prefill",
    )(p3, p3, p3, s3, cw, cb, wq, wk, gb)


def _block_bcast(tile, lanes):
    c = tile.shape[0]
    lane = lax.broadcasted_iota(jnp.int32, (c, 128), 1)
    cols = [jnp.broadcast_to(tile[:, l:l + 1], (c, 128)) for l in lanes]
    left = jnp.where(lane < SSM_P, cols[0], cols[1])
    right = jnp.where(lane < SSM_P, cols[2], cols[3])
    return jnp.concatenate([left, right], axis=1)


def _ssd_prefill_kernel(xs_ref, xsp_ref, bc_ref, bcp_ref, sm_ref, cw_ref, cb_ref, dtb_ref, alog_ref,
                        dv_ref, y_ref, s_ref, cv_ref, *rest):
    *zero_refs, tail = rest
    _zero_fill(zero_refs)
    c = CHUNK
    nb = xs_ref.shape[0]
    wx = SSM_H * SSM_P
    gw = SSM_R * SSM_P
    first = pl.program_id(0) == 0

    @pl.when(first)
    def _():
        s_ref[...] = jnp.zeros_like(s_ref)

    row = lax.broadcasted_iota(jnp.int32, (c, c), 0)
    col = lax.broadcasted_iota(jnp.int32, (c, c), 1)
    causal = row >= col
    tril_b = jnp.where(causal, 1.0, 0.0).astype(BF16)
    lane_blk = lax.shift_right_logical(lax.broadcasted_iota(jnp.int32, (c, gw), 1), 6)
    head_mask = [jnp.where(lane_blk == r, 1.0, 0.0).astype(BF16) for r in range(SSM_R)]
    a_row = -jnp.exp(alog_ref[...])
    shift = _shift_matrix(first)

    def conv(cur_ref, prv_ref, b, lanes, w_off):
        wl = slice(w_off + lanes.start, w_off + lanes.stop)
        return _conv_silu(shift, prv_ref[b, :, lanes], cur_ref[b, :, lanes], cw_ref[:, wl], cb_ref[:, wl])

    def load(s):
        s.s_old = [s_ref[s.b, g * SSM_R:(g + 1) * SSM_R].reshape(gw, SSM_N) for g in range(SSM_G)]

    def dt_phase(s):
        b = s.b
        tail[:, 0:wx] = xs_ref[b, c - 16:c, :].astype(F32)
        tail[:, wx:2 * wx] = bc_ref[b, c - 16:c, :].astype(F32)
        cv_ref[b] = tail[16 - (CONV_K - 1):16, :]
        s.delta = _softplus(sm_ref[b] + dtb_ref[...])
        s.cum = _cumsum_rows(s.delta * a_row, tril_b)
        s.cum_t = s.cum.T
        s.delta_t = s.delta.T
        s.ecum = jnp.exp(s.cum)
        cum_end = s.cum[c - 1:c, :]
        s.w_state = jnp.exp(cum_end - s.cum) * s.delta
        s.dec_row = jnp.exp(cum_end)

    def conv_phase(it):
        g = it.k
        it.lanes = [L_DT + g * SSM_R + r for r in range(SSM_R)]
        it.xg = conv(xs_ref, xsp_ref, it.b, slice(g * gw, (g + 1) * gw), 0)
        it.bg = conv(bc_ref, bcp_ref, it.b, slice(g * SSM_N, (g + 1) * SSM_N), wx).astype(BF16)
        it.cg = conv(bc_ref, bcp_ref, it.b, slice((SSM_G + g) * SSM_N, (SSM_G + g + 1) * SSM_N), wx).astype(BF16)

    def seg_phase(it):
        s = it.seq
        it.seg = [jnp.exp(jnp.where(causal, s.cum[:, ln:ln + 1] - s.cum_t[ln:ln + 1, :], -jnp.inf))
                  * s.delta_t[ln:ln + 1, :] for ln in it.lanes]
        xgb = it.xg.astype(BF16)
        it.x_stack = jnp.concatenate([xgb * head_mask[r] for r in range(SSM_R)], axis=0)
        it.xwt = (it.xg * _block_bcast(s.w_state, it.lanes)).T.astype(BF16)

    def cb_phase(it):
        it.cb_mat = _dot_nt(it.cg, it.bg)

    def mix_phase(it):
        s = it.seq
        s_old = s.s_old[it.k]
        m_wide = jnp.concatenate([(it.cb_mat * sg).astype(BF16) for sg in it.seg], axis=1)
        it.y = (_dot(m_wide, it.x_stack)
                + _dot_nt(it.cg, s_old.astype(BF16)) * _block_bcast(s.ecum, it.lanes))
        upd = _dot(it.xwt, it.bg)
        it.s_new = [s_old[r * SSM_P:(r + 1) * SSM_P] * s.dec_row[:, ln:ln + 1] + upd[r * SSM_P:(r + 1) * SSM_P]
                    for r, ln in enumerate(it.lanes)]

    def emit_phase(it):
        gs = slice(it.k * gw, (it.k + 1) * gw)
        y_ref[it.b, :, gs] = (it.y + dv_ref[:, gs] * it.xg).astype(y_ref.dtype)

    def finish(s):
        for it in s.heads:
            for r in range(SSM_R):
                s_ref[s.b, it.k * SSM_R + r] = it.s_new[r]

    _seq_loop(nb, 2, [load, dt_phase], SSM_G, [conv_phase, seg_phase, cb_phase, mix_phase, emit_phase], finish)


def _ssd_prefill(proj, small, cw, cb, dtb, alog, dvec, *, layer, nb, seq, zero_shape=None):
    c = CHUNK
    wx = SSM_H * SSM_P
    p3 = proj.reshape(nb, seq, N_MAIN)
    s3 = small.reshape(nb, seq, N_SMALL)
    z_specs, z_shapes = ([], []) if zero_shape is None else [[v] for v in _zero_out(zero_shape, seq // c)]
    return pl.pallas_call(
        _ssd_prefill_kernel,
        grid=(seq // c,),
        in_specs=[
            pl.BlockSpec((nb, c, wx), lambda t: (0, t, C_XS // wx)),
            pl.BlockSpec((nb, c, wx), lambda t: (0, jnp.maximum(t - 1, 0), C_XS // wx)),
            pl.BlockSpec((nb, c, wx), lambda t: (0, t, C_B // wx)),
            pl.BlockSpec((nb, c, wx), lambda t: (0, jnp.maximum(t - 1, 0), C_B // wx)),
            pl.BlockSpec((nb, c, N_SMALL), lambda t: (0, t, 0)),
            _lsel(layer, CONV_K, 2 * wx),
            _lsel(layer, 1, 2 * wx),
            _lsel(layer, 1, N_SMALL),
            _lsel(layer, 1, N_SMALL),
            _lsel(layer, 1, wx)],
        out_specs=[pl.BlockSpec((nb, c, wx), lambda t: (0, t, 0)),
                   pl.BlockSpec((nb, SSM_H, SSM_P, SSM_N), lambda t: (0, 0, 0, 0)),
                   pl.BlockSpec((nb, CONV_K - 1, 2 * wx), lambda t: (0, 0, 0))] + z_specs,
        out_shape=[jax.ShapeDtypeStruct((nb, seq, wx), BF16),
                   jax.ShapeDtypeStruct((nb, SSM_H, SSM_P, SSM_N), F32),
                   jax.ShapeDtypeStruct((nb, CONV_K - 1, 2 * wx), F32)] + z_shapes,
        scratch_shapes=[pltpu.VMEM((16, 2 * wx), F32)],
        compiler_params=_cparams(1),
        name="ssd_prefill",
    )(p3, p3, p3, p3, s3, cw, cb, dtb, alog, dvec)


MERGE_SUB = 128


def _head_norm_gate(y_ref, g_ref, n_ref, rs, n_heads, center, gate_fn, gate_first):
    wh = y_ref.shape[1] // n_heads
    parts = []
    for h in range(n_heads):
        hs = slice(h * wh, (h + 1) * wh)
        y = y_ref[rs, hs].astype(F32)
        gate = gate_fn(g_ref[rs, hs].astype(F32))
        y = _groupnorm(y * gate, n_ref[:, hs], center) if gate_first else _groupnorm(y, n_ref[:, hs], center) * gate
        parts.append(y.astype(BF16))
    return jnp.concatenate(parts, axis=1)


def _merge_kernel(yr_ref, ym_ref, ys_ref, gr_ref, gm_ref, gs_ref, x_ref, wr_ref, wm_ref, ws_ref,
                  wo_ref, nw_ref, wq_ref, *rest, fused_norm):
    if fused_norm:
        g_ref, om_ref, z_ref, nr_ref, nm_ref, ns_ref, xo_ref, qo_ref, *zero_refs = rest
    else:
        xo_ref, qo_ref, *zero_refs = rest
    _zero_fill(zero_refs)
    tm = x_ref.shape[0]
    n_sub = max(1, tm // MERGE_SUB)
    subs = [slice(i * (tm // n_sub), (i + 1) * (tm // n_sub)) for i in range(n_sub)]
    branches = ((yr_ref, gr_ref, wr_ref), (ym_ref, gm_ref, wm_ref), (ys_ref, gs_ref, ws_ref))
    if fused_norm:
        def operands(rs):
            return (_head_norm_gate(yr_ref, g_ref, nr_ref, rs, RET_H, True, _silu, False),
                    _head_norm_gate(ym_ref, om_ref, nm_ref, rs, ML_H, True, _sigmoid, False),
                    _head_norm_gate(ys_ref, z_ref, ns_ref, rs, SSM_G, False, _silu, True))
    else:
        def operands(rs):
            return tuple(y_ref[rs, :].astype(BF16) for y_ref, _, _ in branches)
    proj = [[_dot(y, w_ref[...]) for y, (_, _, w_ref) in zip(operands(rs), branches)] for rs in subs]
    merged = []
    for rs, pr in zip(subs, proj):
        gated = [_sigmoid(g_ref[rs, :].astype(F32)) * p for (_, g_ref, _), p in zip(branches, pr)]
        merged.append((gated[0] + gated[1] + gated[2]).astype(BF16))
    xns = [x_ref[rs, :] + _dot(m, wo_ref[...]) for rs, m in zip(subs, merged)]
    hqs = []
    for rs, xn in zip(subs, xns):
        xo_ref[rs, :] = xn
        hqs.append(_rmsnorm(xn, nw_ref[...]).astype(BF16))
    for rs, hq in zip(subs, hqs):
        qo_ref[rs, :] = _dot(hq, wq_ref[...]).astype(qo_ref.dtype)


def _merge(yr, ym, ys, proj, x, wr, wm, ws, wo, nw, wq, norms=None, *, layer, tm, q_dtype, zero_shape=None):
    m = x.shape[0]
    d = D_MODEL
    row = lambda i: (i, 0)
    z_specs, z_shapes = ([], []) if zero_shape is None else [[v] for v in _zero_out(zero_shape, m // tm)]
    wspec = _lsel(layer, d, d, pipeline_mode=pl.Buffered(1))
    fused = norms is not None
    extra_specs, extra_args = [], []
    if fused:
        extra_specs = [pl.BlockSpec((tm, d), lambda i: (i, C_G // d)),
                       pl.BlockSpec((tm, d), lambda i: (i, C_OM // d)),
                       pl.BlockSpec((tm, d), lambda i: (i, C_Z // d))] + [_lsel(layer, 1, d)] * 3
        extra_args = [proj, proj, proj, *norms]
    return pl.pallas_call(
        functools.partial(_merge_kernel, fused_norm=fused),
        grid=(m // tm,),
        in_specs=[pl.BlockSpec((tm, d), row), pl.BlockSpec((tm, d), row), pl.BlockSpec((tm, d), row),
                  pl.BlockSpec((tm, d), lambda i: (i, C_GR // d)),
                  pl.BlockSpec((tm, d), lambda i: (i, C_GM // d)),
                  pl.BlockSpec((tm, d), lambda i: (i, C_GS // d)),
                  pl.BlockSpec((tm, d), row),
                  wspec, wspec, wspec, wspec,
                  _lsel(layer, 1, d),
                  wspec] + extra_specs,
        out_specs=[pl.BlockSpec((tm, d), row), pl.BlockSpec((tm, d), row)] + z_specs,
        out_shape=[jax.ShapeDtypeStruct((m, d), F32), jax.ShapeDtypeStruct((m, d), q_dtype)] + z_shapes,
        compiler_params=_cparams(1),
        name="merge",
    )(yr, ym, ys, proj, proj, proj, x, wr, wm, ws, wo, nw, wq, *extra_args)


def _attn_prefill_kernel(q_ref, k_ref, v_ref, x_ref, wo_ref, o_ref):
    heads = [slice(h * MEM_DH, (h + 1) * MEM_DH) for h in range(MEM_H)]
    scores = [_dot_nt(q_ref[0, :, sl], k_ref[0, :, sl].astype(BF16)) * (MEM_DH ** -0.5) for sl in heads]
    probs = []
    for s in scores:
        p = jnp.exp(s - jnp.max(s, axis=-1, keepdims=True))
        probs.append((p / jnp.sum(p, axis=-1, keepdims=True)).astype(BF16))
    outs = [_dot(p, v_ref[0, :, sl].astype(BF16)).astype(BF16) for p, sl in zip(probs, heads)]
    o_all = jnp.concatenate(outs, axis=1)
    o_ref[0] = x_ref[0] + _dot(o_all, wo_ref[...])


def _attn_prefill(q, mk, mv, x, wo, *, layer, nb, seq, tq):
    d = D_MODEL
    q3 = q.reshape(nb, seq, d)
    x3 = x.reshape(nb, seq, d)
    out = pl.pallas_call(
        _attn_prefill_kernel,
        grid=(nb, seq // tq),
        in_specs=[pl.BlockSpec((1, tq, d), lambda b, t: (b, t, 0)),
                  pl.BlockSpec((1, MEM_LEN, d), lambda b, t: (layer * nb + b, 0, 0)),
                  pl.BlockSpec((1, MEM_LEN, d), lambda b, t: (layer * nb + b, 0, 0)),
                  pl.BlockSpec((1, tq, d), lambda b, t: (b, t, 0)),
                  _lsel(layer, d, d)],
        out_specs=pl.BlockSpec((1, tq, d), lambda b, t: (b, t, 0)),
        out_shape=jax.ShapeDtypeStruct((nb, seq, d), F32),
        compiler_params=_cparams(2),
        name="attn_prefill",
    )(q3, mk, mv, x3, wo)
    return out.reshape(nb * seq, d)


def _attn_decode_kernel(q_ref, k_ref, v_ref, o_ref):
    for j in range(q_ref.shape[0]):
        qs = q_ref[j] * (MEM_DH ** -0.5)
        s = jnp.sum(k_ref[j] * qs[None], axis=-1, keepdims=True)
        p = jnp.exp(s - jnp.max(s, axis=0, keepdims=True))
        o_ref[j] = jnp.sum(p * v_ref[j], axis=0) / jnp.sum(p, axis=0)


def _attn_decode(q, ck, cv, *, layer, bb):
    nb = q.shape[0]
    nblk = nb // bb
    blk = (bb, MEM_LEN, MEM_H, MEM_DH)
    return pl.pallas_call(
        _attn_decode_kernel,
        grid=(nblk,),
        in_specs=[pl.BlockSpec((bb, MEM_H, MEM_DH), lambda i: (i, 0, 0)),
                  pl.BlockSpec(blk, lambda i: (layer * nblk + i, 0, 0, 0)),
                  pl.BlockSpec(blk, lambda i: (layer * nblk + i, 0, 0, 0))],
        out_specs=pl.BlockSpec((bb, MEM_H, MEM_DH), lambda i: (i, 0, 0)),
        out_shape=jax.ShapeDtypeStruct((nb, MEM_H, MEM_DH), F32),
        compiler_params=_cparams(1),
        name="attn_decode",
    )(q, ck, cv)


MLP_SUB = 1024


def _mlp_kernel(x_ref, nw_ref, w1_ref, w2_ref, nf_ref, o_ref, xn_ref, acc_ref, *, final_norm):
    j = pl.program_id(1)

    @pl.when(j == 0)
    def _():
        xn_ref[...] = _rmsnorm(x_ref[...], nw_ref[...]).astype(BF16)
        acc_ref[...] = jnp.zeros_like(acc_ref)

    subs = [slice(s * MLP_SUB, (s + 1) * MLP_SUB) for s in range(w1_ref.shape[1] // MLP_SUB)]
    hids = [_dot(xn_ref[...], w1_ref[:, sl]) for sl in subs]
    hids = [jnp.square(jnp.maximum(hid, 0.0)).astype(BF16) for hid in hids]
    acc = acc_ref[...]
    for hid, sl in zip(hids, subs):
        acc = acc + _dot(hid, w2_ref[sl, :])
    acc_ref[...] = acc

    @pl.when(j == pl.num_programs(1) - 1)
    def _():
        y = x_ref[...] + acc_ref[...]
        if final_norm:
            y = _rmsnorm(y, nf_ref[...])
        o_ref[...] = y


def _mlp(x, nw, w1, w2, nf, *, layer, tm, tf, final_norm):
    m = x.shape[0]
    d = D_MODEL
    return pl.pallas_call(
        functools.partial(_mlp_kernel, final_norm=final_norm),
        grid=(m // tm, D_FF // tf),
        in_specs=[pl.BlockSpec((tm, d), lambda i, j: (i, 0)),
                  _lsel(layer, 1, d),
                  pl.BlockSpec((None, d, tf), lambda i, j: (layer, 0, j)),
                  pl.BlockSpec((None, tf, d), lambda i, j: (layer, j, 0)),
                  pl.BlockSpec((1, d), lambda i, j: (0, 0))],
        out_specs=pl.BlockSpec((tm, d), lambda i, j: (i, 0)),
        out_shape=jax.ShapeDtypeStruct((m, d), F32),
        scratch_shapes=[pltpu.VMEM((tm, d), BF16), pltpu.VMEM((tm, d), F32)],
        compiler_params=_cparams(2),
        name="mlp",
    )(x, nw, w1, w2, nf)


DEC_BB = 8


def _row_mask(x, j):
    rows = lax.broadcasted_iota(jnp.int32, x.shape, 0)
    return jnp.where(rows == j, x, 0.0).astype(BF16)


def _blockdiag_rows(v):
    n = v.shape[0]
    tiled = jnp.concatenate([v] * n, axis=1)
    rows = lax.broadcasted_iota(jnp.int32, tiled.shape, 0)
    blk = lax.shift_right_logical(lax.broadcasted_iota(jnp.int32, tiled.shape, 1), 8)
    return _pad_rows(jnp.where(rows == blk, tiled, 0.0), 128).astype(BF16)


def _ret_decode_kernel(q_ref, k_ref, v_ref, g_ref, cos_ref, sin_ref, nw_ref, s_ref, y_ref, so_ref, ybuf):
    cosf = cos_ref[...]
    sinf = sin_ref[...]
    for h in range(RET_H):
        sk = slice(h * RET_DK, (h + 1) * RET_DK)
        sv = slice(h * RET_DV, (h + 1) * RET_DV)
        g_dec = math.exp(_LOG_G[h])
        q = q_ref[:, sk]
        k = k_ref[:, sk]
        qr = q * cosf + pltpu.roll(q, RET_DK // 2, 1) * sinf
        kr = (k * cosf + pltpu.roll(k, RET_DK // 2, 1) * sinf) * (RET_DK ** -0.5)
        k_t = _pad_rows_t(kr).astype(BF16)
        q16 = _pad_rows(qr, 16).astype(BF16)
        upd = _dot(k_t, _blockdiag_rows(v_ref[:, sv]))
        s_wide = []
        for j in range(DEC_BB):
            s_new = s_ref[j, h] * g_dec + upd[:, j * RET_DV:(j + 1) * RET_DV]
            so_ref[j, h] = s_new
            s_wide.append(s_new.astype(BF16))
        y_all = _dot(q16, jnp.concatenate(s_wide, axis=1))
        for j in range(DEC_BB):
            ybuf[j:j + 1, sv] = y_all[j:j + 1, j * RET_DV:(j + 1) * RET_DV]
        y_ref[:, sv] = _groupnorm(ybuf[:, sv], nw_ref[:, sv], True) * _silu(g_ref[:, sv])


def _ret_decode(proj, cosf, sinf, nw, state, prev, *, layer):
    nb = proj.shape[0]
    bb = DEC_BB
    nblk = nb // bb
    kern, steps, ib, ob, al_args, al_specs, al_map = _carried_call(_ret_decode_kernel, nblk, layer, prev)
    sblk = (bb, RET_H, RET_DK, RET_DV)
    return pl.pallas_call(
        kern,
        grid=(steps,),
        in_specs=al_specs + [
            pl.BlockSpec((bb, 512), lambda i: (ib(i), C_Q // 512)),
            pl.BlockSpec((bb, 512), lambda i: (ib(i), C_K // 512)),
            pl.BlockSpec((bb, 1024), lambda i: (ib(i), C_V // 1024)),
            pl.BlockSpec((bb, 1024), lambda i: (ib(i), C_G // 1024)),
            pl.BlockSpec((1, RET_DK), lambda i: (0, 0)),
            pl.BlockSpec((1, RET_DK), lambda i: (0, 0)),
            _lsel(layer, 1, 1024),
            pl.BlockSpec(sblk, lambda i: (layer * nblk + ib(i), 0, 0, 0))],
        out_specs=[pl.BlockSpec((bb, 1024), lambda i: (ib(i), 0)),
                   pl.BlockSpec(sblk, lambda i: (ob(i), 0, 0, 0))],
        out_shape=[jax.ShapeDtypeStruct((nb, RET_H * RET_DV), F32),
                   jax.ShapeDtypeStruct((DEPTH * nb, RET_H, RET_DK, RET_DV), F32)],
        scratch_shapes=[pltpu.VMEM((bb, RET_H * RET_DV), F32)],
        input_output_aliases=al_map,
        compiler_params=_cparams(1),
        name="ret_decode",
    )(*al_args, proj, proj, proj, proj, cosf, sinf, nw, state)


def _conv_step(x, buf_ref, cw_ref, cb_ref, bo_ref):
    acc = cb_ref[...]
    for j in range(CONV_K - 1):
        acc = acc + buf_ref[j] * cw_ref[j:j + 1, :]
    acc = acc + x * cw_ref[CONV_K - 1:CONV_K, :]
    bo_ref[0] = buf_ref[1]
    bo_ref[1] = buf_ref[2]
    bo_ref[2] = x
    return _silu(acc)


def _ml_decode_kernel(u_ref, v_ref, o_ref, sm_ref, buf_ref, cw_ref, cb_ref, wq_ref, wk_ref, gb_ref, nw_ref,
                      c_ref, n_ref, m_ref,
                      y_ref, co_ref, no_ref, mo_ref, bo_ref, ybuf):
    uc = _conv_step(u_ref[...], buf_ref, cw_ref, cb_ref, bo_ref).astype(BF16)
    gates = sm_ref[...] + gb_ref[...]
    lane4 = lax.broadcasted_iota(jnp.int32, m_ref.shape, 1)
    m_all = m_ref[...]
    m_out = m_all
    for h in range(ML_H):
        sl = slice(h * ML_DH, (h + 1) * ML_DH)
        qm = _dot(uc[:, sl], wq_ref[h])
        km = _dot(uc[:, sl], wk_ref[h]) * (ML_DH ** -0.5)
        i_pre = gates[:, L_IG + h:L_IG + h + 1]
        log_f = _log_sigmoid(gates[:, L_FG + h:L_FG + h + 1])
        m_prev = m_all[:, h:h + 1]
        inter = log_f + m_prev
        m_new = jnp.maximum(inter, i_pre)
        w_s = jnp.exp(i_pre - m_new)
        w_prev = jnp.exp(inter - m_new)
        kw = km * w_s
        n_new = n_ref[:, sl] * w_prev + kw
        no_ref[:, sl] = n_new
        m_out = jnp.where(lane4 == h, m_new, m_out)

        k_t = _pad_rows_t(kw).astype(BF16)
        q16 = _pad_rows(qm, 16).astype(BF16)
        upd = _dot(k_t, _blockdiag_rows(v_ref[:, sl]))
        c_wide = []
        for j in range(DEC_BB):
            c_new = c_ref[j, h] * w_prev[j:j + 1, :] + upd[:, j * ML_DH:(j + 1) * ML_DH]
            co_ref[j, h] = c_new
            c_wide.append(c_new.astype(BF16))
        y_all = _dot(q16, jnp.concatenate(c_wide, axis=1))
        for j in range(DEC_BB):
            ybuf[j:j + 1, sl] = y_all[j:j + 1, j * ML_DH:(j + 1) * ML_DH]
        den = jnp.sum(qm * n_new, axis=-1, keepdims=True)
        den = jnp.maximum(jnp.abs(den), jnp.exp(-m_new))
        y_ref[:, sl] = _groupnorm(ybuf[:, sl] / den, nw_ref[:, sl], True) * _sigmoid(o_ref[:, sl])
    mo_ref[...] = m_out


def _ml_decode(proj, small, buf, cw, cb, wq, wk, gb, nw, c_state, n_state, m_state, prev, *, layer):
    nb = proj.shape[0]
    bb = DEC_BB
    nblk = nb // bb
    dh = ML_DH
    w = ML_H * ML_DH
    kern, steps, ib, ob, al_args, al_specs, al_map = _carried_call(_ml_decode_kernel, nblk, layer, prev)
    cblk = (bb, ML_H, dh, dh)
    return pl.pallas_call(
        kern,
        grid=(steps,),
        in_specs=al_specs + [
            pl.BlockSpec((bb, w), lambda i: (ib(i), C_U // w)),
            pl.BlockSpec((bb, w), lambda i: (ib(i), C_VM // w)),
            pl.BlockSpec((bb, w), lambda i: (ib(i), C_OM // w)),
            pl.BlockSpec((bb, N_SMALL), lambda i: (ib(i), 0)),
            pl.BlockSpec((None, CONV_K - 1, bb, w), lambda i: (layer, 0, ib(i), 0)),
            _lsel(layer, CONV_K, w),
            _lsel(layer, 1, w),
            _lsel(layer, ML_H, dh, dh),
            _lsel(layer, ML_H, dh, dh),
            _lsel(layer, 1, N_SMALL),
            _lsel(layer, 1, w),
            pl.BlockSpec(cblk, lambda i: (layer * nblk + ib(i), 0, 0, 0)),
            pl.BlockSpec((bb, w), lambda i: (layer * nblk + ib(i), 0)),
            pl.BlockSpec((bb, ML_H), lambda i: (layer * nblk + ib(i), 0))],
        out_specs=[pl.BlockSpec((bb, w), lambda i: (ib(i), 0)),
                   pl.BlockSpec(cblk, lambda i: (ob(i), 0, 0, 0)),
                   pl.BlockSpec((bb, w), lambda i: (ib(i), 0)),
                   pl.BlockSpec((bb, ML_H), lambda i: (ib(i), 0)),
                   pl.BlockSpec((CONV_K - 1, bb, w), lambda i: (0, ib(i), 0))],
        out_shape=[jax.ShapeDtypeStruct((nb, w), F32),
                   jax.ShapeDtypeStruct((DEPTH * nb, ML_H, dh, dh), F32),
                   jax.ShapeDtypeStruct((nb, w), F32),
                   jax.ShapeDtypeStruct((nb, ML_H), F32),
                   jax.ShapeDtypeStruct((CONV_K - 1, nb, w), F32)],
        scratch_shapes=[pltpu.VMEM((bb, w), F32)],
        input_output_aliases=al_map,
        compiler_params=_cparams(1),
        name="ml_decode",
    )(*al_args, proj, proj, proj, small, buf, cw, cb, wq, wk, gb, nw, c_state, n_state, m_state)


def _ssd_decode_kernel(z_ref, xs_ref, bc_ref, sm_ref, buf_ref, cw_ref, cb_ref, dtb_ref, alog_ref, dv_ref, nw_ref,
                       s_ref, y_ref, so_ref, bo_ref, ybuf):
    wx = SSM_H * SSM_P
    gw = SSM_R * SSM_P
    x_in = jnp.concatenate([xs_ref[...], bc_ref[...]], axis=1)
    xc = _conv_step(x_in, buf_ref, cw_ref, cb_ref, bo_ref)
    xs = xc[:, 0:wx]

    delta = _softplus(sm_ref[...] + dtb_ref[...])
    d_a = jnp.exp(delta * (-jnp.exp(alog_ref[...])))
    lane_blk = lax.shift_right_logical(lax.broadcasted_iota(jnp.int32, (DEC_BB, wx), 1), 6)
    dt_full = jnp.zeros((DEC_BB, wx), F32)
    for hh in range(SSM_H):
        dt_full = jnp.where(lane_blk == hh, delta[:, L_DT + hh:L_DT + hh + 1], dt_full)
    x_t = _pad_rows_t(xs * dt_full).astype(BF16)

    for g in range(SSM_G):
        gs = slice(g * gw, (g + 1) * gw)
        b_pad = _pad_rows(xc[:, wx + g * SSM_N:wx + (g + 1) * SSM_N], 128)
        c16 = _pad_rows(xc[:, wx + (SSM_G + g) * SSM_N:wx + (SSM_G + g + 1) * SSM_N], 16).astype(BF16)
        for j in range(DEC_BB):
            upd = _dot(x_t[gs, :], _row_mask(b_pad, j))
            parts = []
            for r in range(SSM_R):
                hh = g * SSM_R + r
                s_new = (s_ref[j, hh] * d_a[j:j + 1, L_DT + hh:L_DT + hh + 1]
                         + upd[r * SSM_P:(r + 1) * SSM_P])
                so_ref[j, hh] = s_new
                parts.append(s_new.astype(BF16))
            ybuf[j:j + 1, gs] = _dot_nt(c16, jnp.concatenate(parts, axis=0))[j:j + 1, :]
        y = ybuf[:, gs] + dv_ref[:, gs] * xs[:, gs]
        y_ref[:, gs] = _groupnorm(y * _silu(z_ref[:, gs]), nw_ref[:, gs], False)


def _ssd_decode(proj, small, buf, cw, cb, dtb, alog, dvec, nw, state, prev, *, layer):
    nb = proj.shape[0]
    bb = DEC_BB
    nblk = nb // bb
    wx = SSM_H * SSM_P
    kern, steps, ib, ob, al_args, al_specs, al_map = _carried_call(_ssd_decode_kernel, nblk, layer, prev)
    sblk = (bb, SSM_H, SSM_P, SSM_N)
    return pl.pallas_call(
        kern,
        grid=(steps,),
        in_specs=al_specs + [
            pl.BlockSpec((bb, wx), lambda i: (ib(i), C_Z // wx)),
            pl.BlockSpec((bb, wx), lambda i: (ib(i), C_XS // wx)),
            pl.BlockSpec((bb, wx), lambda i: (ib(i), C_B // wx)),
            pl.BlockSpec((bb, N_SMALL), lambda i: (ib(i), 0)),
            pl.BlockSpec((None, CONV_K - 1, bb, 2 * wx), lambda i: (layer, 0, ib(i), 0)),
            _lsel(layer, CONV_K, 2 * wx),
            _lsel(layer, 1, 2 * wx),
            _lsel(layer, 1, N_SMALL),
            _lsel(layer, 1, N_SMALL),
            _lsel(layer, 1, wx),
            _lsel(layer, 1, wx),
            pl.BlockSpec(sblk, lambda i: (layer * nblk + ib(i), 0, 0, 0))],
        out_specs=[pl.BlockSpec((bb, wx), lambda i: (ib(i), 0)),
                   pl.BlockSpec(sblk, lambda i: (ob(i), 0, 0, 0)),
                   pl.BlockSpec((CONV_K - 1, bb, 2 * wx), lambda i: (0, ib(i), 0))],
        out_shape=[jax.ShapeDtypeStruct((nb, wx), F32),
                   jax.ShapeDtypeStruct((DEPTH * nb, SSM_H, SSM_P, SSM_N), F32),
                   jax.ShapeDtypeStruct((CONV_K - 1, nb, 2 * wx), F32)],
        scratch_shapes=[pltpu.VMEM((bb, wx), F32)],
        input_output_aliases=al_map,
        compiler_params=_cparams(1),
        name="ssd_decode",
    )(*al_args, proj, proj, proj, small, buf, cw, cb, dtb, alog, dvec, nw, state)


def _rope_tables(pos):
    half = RET_DK // 2
    freqs = ROPE_THETA ** (-jnp.arange(half, dtype=F32) / half)
    ang = pos.astype(F32)[:, None] * freqs[None, :]
    cos = jnp.cos(ang)
    sin = jnp.sin(ang)
    return jnp.concatenate([cos, cos], axis=1), jnp.concatenate([-sin, sin], axis=1)


def _pad_lanes(v, offset):
    n = v.shape[1]
    return jnp.pad(v.astype(F32), ((0, 0), (offset, N_SMALL - offset - n))).reshape(DEPTH, 1, N_SMALL)


REPACK_TN = 1024
REPACK_RUNS = ((0, 6, 0), (6, 9, 2 * ML_H), (9, 12, 2 * ML_H + SSM_H))
SRC_GATE_BLK = 6144 // 128
SRC_DT_BLK = (6144 + 2 * ML_H + 3072) // 128


def _repack_kernel(a_ref, b_ref, g_ref, dt_ref, o_ref, os_ref):
    j = pl.program_id(1)

    @pl.when(j == 0)
    def _():
        row = lax.broadcasted_iota(jnp.int32, (N_SMALL, D_MODEL), 0)
        os_ref[...] = jnp.where(row < L_DT, g_ref[...],
                                jnp.where(row < L_DT + SSM_H, dt_ref[...], 0.0)).astype(BF16)

    for lo, hi, skip in REPACK_RUNS:
        @pl.when((j >= lo) & (j < hi))
        def _(skip=skip):
            src = a_ref[...] if skip == 0 else jnp.concatenate([a_ref[skip:, :], b_ref[0:skip, :]], axis=0)
            o_ref[...] = src.astype(BF16)


def _repack_w_in(w_in_t):
    tn = REPACK_TN
    return pl.pallas_call(
        _repack_kernel,
        grid=(DEPTH, N_MAIN // tn),
        in_specs=[pl.BlockSpec((None, tn, D_MODEL), lambda l, j: (l, j, 0)),
                  pl.BlockSpec((None, 128, D_MODEL), lambda l, j: (l, (j + 1) * (tn // 128), 0)),
                  pl.BlockSpec((None, 128, D_MODEL), lambda l, j: (l, SRC_GATE_BLK, 0)),
                  pl.BlockSpec((None, 128, D_MODEL), lambda l, j: (l, SRC_DT_BLK, 0))],
        out_specs=[pl.BlockSpec((None, tn, D_MODEL), lambda l, j: (l, j, 0)),
                   pl.BlockSpec((None, N_SMALL, D_MODEL), lambda l, j: (l, 0, 0))],
        out_shape=[jax.ShapeDtypeStruct((DEPTH, N_MAIN, D_MODEL), BF16),
                   jax.ShapeDtypeStruct((DEPTH, N_SMALL, D_MODEL), BF16)],
        compiler_params=_cparams(2),
        name="repack_w_in",
    )(w_in_t, w_in_t, w_in_t, w_in_t)


def _prep_weights(W):
    w_main, w_small = _repack_w_in(jnp.swapaxes(W['w_in'], 1, 2))
    row = lambda v: v.reshape(DEPTH, 1, -1).astype(F32)
    bf = lambda v: v.astype(BF16)
    return dict(
        w_main=w_main, w_small=w_small,
        norm_mix=row(W['norm_mix_w']),
        ret_norm=row(W['ret_norm_w']),
        ml_cw=W['ml_conv_w'], ml_cb=row(W['ml_conv_b']),
        ml_wq=bf(W['ml_wq']), ml_wk=bf(W['ml_wk']),
        ml_gb=_pad_lanes(W['ml_gate_b'], L_IG),
        ml_norm=row(W['ml_norm_w']),
        ssm_cw=W['ssm_conv_w'], ssm_cb=row(W['ssm_conv_b']),
        ssm_dtb=_pad_lanes(W['ssm_dt_bias'], L_DT),
        ssm_alog=_pad_lanes(W['ssm_A_log'], L_DT),
        ssm_dvec=row(jnp.repeat(W['ssm_D'], SSM_P, axis=1)),
        ssm_norm=row(W['ssm_norm_w']),
        w_br_ret=bf(W['w_br_ret']), w_br_ml=bf(W['w_br_ml']),
        w_br_ssm=bf(W['w_br_ssm']), w_out=bf(W['w_out_mix']),
        norm_mem=row(W['norm_mem_w']),
        mem_wq=bf(W['mem_wq']), mem_wo=bf(W['mem_wo']),
        norm_mlp=row(W['norm_mlp_w']),
        mlp_w1=bf(W['mlp_w1']), mlp_w2=bf(W['mlp_w2']),
    )


def kernel(x_prompt, x_sample, mem_prompt, state_ret, state_mlstm_C, state_mlstm_n, state_mlstm_m,
           state_mlstm_conv, state_ssm, state_ssm_conv, cache_mem_k, cache_mem_v,
           norm_mix_w, w_in, ret_norm_w, ml_conv_w, ml_conv_b, ml_wq, ml_wk, ml_gate_b, ml_norm_w,
           ssm_conv_w, ssm_conv_b, ssm_dt_bias, ssm_A_log, ssm_D, ssm_norm_w,
           w_br_ret, w_br_ml, w_br_ssm, w_out_mix, norm_mem_w, mem_wq, mem_wk, mem_wv, mem_wo,
           norm_mlp_w, mlp_w1, mlp_w2, norm_f_w):
    W = dict(norm_mix_w=norm_mix_w, w_in=w_in, ret_norm_w=ret_norm_w, ml_conv_w=ml_conv_w,
             ml_conv_b=ml_conv_b, ml_wq=ml_wq, ml_wk=ml_wk, ml_gate_b=ml_gate_b, ml_norm_w=ml_norm_w,
             ssm_conv_w=ssm_conv_w, ssm_conv_b=ssm_conv_b, ssm_dt_bias=ssm_dt_bias, ssm_A_log=ssm_A_log,
             ssm_D=ssm_D, ssm_norm_w=ssm_norm_w, w_br_ret=w_br_ret, w_br_ml=w_br_ml, w_br_ssm=w_br_ssm,
             w_out_mix=w_out_mix, norm_mem_w=norm_mem_w, mem_wq=mem_wq, mem_wo=mem_wo,
             norm_mlp_w=norm_mlp_w, mlp_w1=mlp_w1, mlp_w2=mlp_w2)
    lw = _prep_weights(W)
    norm_f = norm_f_w.reshape(1, -1).astype(F32)
    d = D_MODEL

    nb, seq = x_prompt.shape[0], x_prompt.shape[1]
    ns = x_sample.shape[0]
    cos_p, sin_p = _rope_tables(jnp.arange(seq, dtype=jnp.int32))
    memk, memv, memk_b, memv_b = _memkv(mem_prompt.reshape(nb * MEM_LEN, d), mem_wk.astype(BF16),
                                        mem_wv.astype(BF16), tm=1024)
    memk3 = memk_b.reshape(DEPTH * nb, MEM_LEN, d)
    memv3 = memv_b.reshape(DEPTH * nb, MEM_LEN, d)

    x = x_prompt.reshape(nb * seq, d)
    states_p = []
    for l in range(DEPTH):
        zs = (lambda shape: shape) if l == 0 else (lambda shape: None)
        proj, small, *mlc_s = _inproj(x, lw['norm_mix'], lw['w_main'], lw['w_small'], layer=l, tm=2048, tn=1024,
                                      out_dtype=BF16, zero_shape=zs((DEPTH * ns, ML_H, ML_DH, ML_DH)))
        y_r, ret_st, *ret_s = _ret_prefill(proj, cos_p, sin_p, nb=nb, seq=seq,
                                           zero_shape=zs((DEPTH * ns, RET_H, RET_DK, RET_DV)))
        y_m, ml_c, ml_n, ml_m, ml_cv = _ml_prefill(proj, small, lw['ml_cw'], lw['ml_cb'], lw['ml_wq'], lw['ml_wk'],
                                                   lw['ml_gb'], layer=l, nb=nb, seq=seq)
        y_s, ssm_st, ssm_cv, *ssm_s = _ssd_prefill(proj, small, lw['ssm_cw'], lw['ssm_cb'], lw['ssm_dtb'],
                                                   lw['ssm_alog'], lw['ssm_dvec'], layer=l, nb=nb,
                                                   seq=seq, zero_shape=zs((DEPTH * ns, SSM_H, SSM_P, SSM_N)))
        x, qm = _merge(y_r.reshape(nb * seq, d), y_m.reshape(nb * seq, d), y_s.reshape(nb * seq, d), proj,
                       x, lw['w_br_ret'], lw['w_br_ml'], lw['w_br_ssm'], lw['w_out'], lw['norm_mem'],
                       lw['mem_wq'], (lw['ret_norm'], lw['ml_norm'], lw['ssm_norm']), layer=l, tm=512,
                       q_dtype=BF16)
        if l == 0:
            zero_states = (ret_s[0], mlc_s[0], ssm_s[0])
        x = _attn_prefill(qm, memk3, memv3, x, lw['mem_wo'], layer=l, nb=nb, seq=seq, tq=1024)
        x = _mlp(x, lw['norm_mlp'], lw['mlp_w1'], lw['mlp_w2'], norm_f, layer=l, tm=1024, tf=2048,
                 final_norm=(l == DEPTH - 1))
        states_p.append((ret_st, ml_c, ml_n, ml_m[:, 0, :ML_H], ml_cv, ssm_st, ssm_cv))
    y_prompt = x.reshape(nb, seq, d)
    stack = lambda parts, i: jnp.stack([p[i] for p in parts])
    prompt_states = tuple(stack(states_p, i) for i in range(7))
    memk_p = memk.reshape(DEPTH, nb, MEM_LEN, MEM_H, MEM_DH)
    memv_p = memv.reshape(DEPTH, nb, MEM_LEN, MEM_H, MEM_DH)

    cos_s, sin_s = _rope_tables(PAST_LEN + jnp.arange(1, dtype=jnp.int32))
    ret_state = state_ret.reshape(DEPTH * ns, RET_H, RET_DK, RET_DV)
    mlc_state = state_mlstm_C.reshape(DEPTH * ns, ML_H, ML_DH, ML_DH)
    mln_state = state_mlstm_n.reshape(DEPTH * ns, ML_H * ML_DH)
    mlm_state = state_mlstm_m.reshape(DEPTH * ns, ML_H)
    ssm_st = state_ssm.reshape(DEPTH * ns, SSM_H, SSM_P, SSM_N)
    ck = cache_mem_k.reshape(DEPTH * ns, MEM_LEN, MEM_H, MEM_DH)
    cv = cache_mem_v.reshape(DEPTH * ns, MEM_LEN, MEM_H, MEM_DH)

    ml_buf = jnp.swapaxes(state_mlstm_conv, 1, 2)
    ssm_buf = jnp.swapaxes(state_ssm_conv, 1, 2)

    x = x_sample.reshape(ns, d)
    ret_s, mlc_s, ssm_s = zero_states
    small_s = []
    for l in range(DEPTH):
        proj, small = _inproj(x, lw['norm_mix'], lw['w_main'], lw['w_small'], layer=l, tm=ns, tn=1024, out_dtype=F32)
        y_r, ret_s = _ret_decode(proj, cos_s, sin_s, lw['ret_norm'], ret_state, ret_s,
                                 layer=l)
        y_m, mlc_s, ml_n, ml_m, ml_bo = _ml_decode(proj, small, ml_buf, lw['ml_cw'], lw['ml_cb'], lw['ml_wq'],
                                                   lw['ml_wk'], lw['ml_gb'], lw['ml_norm'],
                                                   mlc_state, mln_state, mlm_state, mlc_s,
                                                   layer=l)
        y_s, ssm_s, ssm_bo = _ssd_decode(proj, small, ssm_buf, lw['ssm_cw'], lw['ssm_cb'], lw['ssm_dtb'],
                                         lw['ssm_alog'], lw['ssm_dvec'], lw['ssm_norm'], ssm_st,
                                         ssm_s, layer=l)
        x, qm = _merge(y_r, y_m, y_s, proj, x, lw['w_br_ret'], lw['w_br_ml'], lw['w_br_ssm'], lw['w_out'],
                       lw['norm_mem'], lw['mem_wq'], layer=l, tm=ns, q_dtype=F32)
        att = _attn_decode(qm.reshape(ns, MEM_H, MEM_DH), ck, cv, layer=l, bb=4)
        x = _mm(att.reshape(ns, d), lw['mem_wo'], x, layer=l, tm=ns, tn=d)
        x = _mlp(x, lw['norm_mlp'], lw['mlp_w1'], lw['mlp_w2'], norm_f, layer=l, tm=ns, tf=2048,
                 final_norm=(l == DEPTH - 1))
        small_s.append((ml_n.reshape(ns, ML_H, ML_DH), ml_m, jnp.swapaxes(ml_bo, 0, 1),
                        jnp.swapaxes(ssm_bo, 0, 1)))
    y_sample = x.reshape(ns, 1, d)
    sample_states = (ret_s.reshape(DEPTH, ns, RET_H, RET_DK, RET_DV),
                     mlc_s.reshape(DEPTH, ns, ML_H, ML_DH, ML_DH),
                     stack(small_s, 0), stack(small_s, 1), stack(small_s, 2),
                     ssm_s.reshape(DEPTH, ns, SSM_H, SSM_P, SSM_N),
                     stack(small_s, 3))

    return (y_prompt, y_sample, *prompt_states, memk_p, memv_p, *sample_states)
```
